```python
import math
import jax
import jax.numpy as jnp
from jax import lax
import numpy as np

D_MODEL = 1024
BATCH = 8
SEQ = 4096
DEPTH = 2
DEC_BATCH = 128
DEC_SEQ = 4
PAST_LEN = 16384
PAGE_SIZE = 128

N_EVEN = (DEPTH + 1) // 2
N_ODD = DEPTH // 2

D_FF = 2816
A_CHUNK = 128
A_HEADS = 8
A_WIDTH = D_MODEL
A_HEAD_DIM = A_WIDTH // A_HEADS
B_HEADS = 16
B_HEAD_DIM = 64
B_INNER = B_HEADS * B_HEAD_DIM
B_GROUPS = 2
B_STATE = 128
B_CONV = 4
B_CHUNK = 128
B_CONV_DIM = B_INNER + 2 * B_GROUPS * B_STATE
C_WIDTH = D_MODEL
C_WINDOWS = (2, 4, 8, 16)
C_GROUPS = len(C_WINDOWS)
C_GROUP_DIM = C_WIDTH // C_GROUPS
C_STATE_LEN = max(C_WINDOWS) - 1
D_Q_HEADS = 16
D_KV_HEADS = 4
D_HEAD_DIM = 64
D_WINDOW = 128
REL_BUCKETS = 32
REL_MAX_DIST = 128

EVEN_IN = 2 * A_WIDTH + B_INNER + B_CONV_DIM + B_HEADS
EVEN_MIX = A_WIDTH + B_INNER
ODD_IN = C_WIDTH + (D_Q_HEADS + 2 * D_KV_HEADS) * D_HEAD_DIM
ODD_MIX = C_WIDTH + D_Q_HEADS * D_HEAD_DIM
EPS = 1e-6
NEG = -1e30

kernel_name = 'hybrid_gmlp_ssd_pool_swa_decoder_step'


def rms_norm(x, g):
    xf = x.astype(jnp.float32)
    y = xf * lax.rsqrt(jnp.mean(xf * xf, axis=-1, keepdims=True) + EPS)
    return (y * g.astype(jnp.float32)).astype(x.dtype)


def layer_norm(x, g, b):
    xf = x.astype(jnp.float32)
    mu = jnp.mean(xf, axis=-1, keepdims=True)
    var = jnp.mean(jnp.square(xf - mu), axis=-1, keepdims=True)
    y = (xf - mu) * lax.rsqrt(var + EPS) * g.astype(jnp.float32) + b.astype(jnp.float32)
    return y.astype(x.dtype)


def swiglu(x, w_gu, w_down):
    gate, up = jnp.split(x @ w_gu, 2, axis=-1)
    return (jax.nn.silu(gate) * up) @ w_down


def macaron_half(h, g, w_gu, w_down):
    return h + 0.5 * swiglu(rms_norm(h, g), w_gu, w_down)


def t5_bucket(dist):
    n = np.maximum(dist, 0)
    max_exact = REL_BUCKETS // 2
    n_safe = np.maximum(n, 1).astype(np.float32)
    scale = np.float32((REL_BUCKETS - max_exact) / math.log(REL_MAX_DIST / max_exact))
    large = max_exact + (np.log(n_safe / max_exact) * scale).astype(np.int32)
    large = np.minimum(large, REL_BUCKETS - 1)
    return np.where(n < max_exact, n, large).astype(np.int32)


def band_bias(rel_table, q_pos, k_pos):
    bucket = t5_bucket(q_pos[:, None] - k_pos[None, :])
    return jnp.transpose(rel_table[bucket].astype(jnp.float32), (2, 0, 1))


def mixer_a(proj, ln_g, ln_b, w_s, b_s, n_chunk):
    b, L, _ = proj.shape
    u, v = jnp.split(jax.nn.gelu(proj), 2, axis=-1)
    v = layer_norm(v, ln_g, ln_b)
    shp = (b, L // n_chunk, n_chunk, A_HEADS, A_HEAD_DIM)
    causal = np.tril(np.ones((n_chunk, n_chunk), dtype=bool))
    w = jnp.where(causal, w_s[:, :n_chunk, :n_chunk], 0)
    gate = jnp.einsum('hij,bcjhe->bcihe', w, v.reshape(shp)) + b_s[:, :n_chunk].T[None, None, :, :, None]
    y = u.reshape(shp) * gate
    return y.reshape(b, L, A_WIDTH), v


def short_conv(xbc, conv_state, w, bias):
    L = xbc.shape[1]
    ext = jnp.concatenate([conv_state.astype(xbc.dtype), xbc], axis=1)
    out = bias
    for tap in range(B_CONV):
        out = out + ext[:, tap:tap + L] * w[tap]
    return jax.nn.silu(out), ext[:, ext.shape[1] - (B_CONV - 1):]


def ssd_scan(x, dt, a, bm, cm, init_state, chunk):
    f32 = jnp.float32
    b, L, H, P = x.shape
    G, N = bm.shape[2], bm.shape[3]
    R = H // G
    nc = L // chunk
    xd = (x.astype(f32) * dt[..., None]).reshape(b, nc, chunk, G, R, P)
    da = (dt * a).reshape(b, nc, chunk, G, R)
    bmc = bm.astype(f32).reshape(b, nc, chunk, G, N)
    cmc = cm.astype(f32).reshape(b, nc, chunk, G, N)
    a_cum = jnp.cumsum(da, axis=2)
    seg = a_cum[:, :, :, None] - a_cum[:, :, None, :]
    causal = np.tril(np.ones((chunk, chunk), dtype=bool))[None, None, :, :, None, None]
    lmat = jnp.exp(jnp.where(causal, seg, -jnp.inf))
    cb = jnp.einsum('bcign,bcjgn->bcijg', cmc, bmc)
    y_diag = jnp.einsum('bcijg,bcijgr,bcjgrp->bcigrp', cb, lmat, xd)
    decay = jnp.exp(a_cum[:, :, -1:] - a_cum)
    states = jnp.einsum('bcjgn,bcjgr,bcjgrp->bcgrpn', bmc, decay, xd)
    chunk_decay = jnp.exp(a_cum[:, :, -1])
    s0 = init_state.astype(f32).reshape(b, G, R, P, N)

    def step(s, inp):
        st, dec = inp
        return s * dec[..., None, None] + st, s

    final, prev = lax.scan(step, s0, (jnp.moveaxis(states, 1, 0), jnp.moveaxis(chunk_decay, 1, 0)))
    prev = jnp.moveaxis(prev, 0, 1)
    y_off = jnp.einsum('bcign,bcgrpn,bcigr->bcigrp', cmc, prev, jnp.exp(a_cum))
    y = (y_diag + y_off).reshape(b, L, H, P)
    return y, final.reshape(b, H, P, N)


def mixer_b(proj, conv_state, ssm_state, conv_w, conv_b, dt_bias, a_log, d_skip, norm_g, chunk):
    f32 = jnp.float32
    b, L, _ = proj.shape
    z = proj[..., :B_INNER]
    xbc, new_conv = short_conv(proj[..., B_INNER:B_INNER + B_CONV_DIM], conv_state, conv_w, conv_b)
    dt_raw = proj[..., B_INNER + B_CONV_DIM:]
    gn = B_GROUPS * B_STATE
    xs = xbc[..., :B_INNER].reshape(b, L, B_HEADS, B_HEAD_DIM)
    bm = xbc[..., B_INNER:B_INNER + gn].reshape(b, L, B_GROUPS, B_STATE)
    cm = xbc[..., B_INNER + gn:].reshape(b, L, B_GROUPS, B_STATE)
    dt = jax.nn.softplus(dt_raw.astype(f32) + dt_bias.astype(f32))
    a = -jnp.exp(a_log.astype(f32))
    y, new_ssm = ssd_scan(xs, dt, a, bm, cm, ssm_state, chunk)
    y = y + xs.astype(f32) * d_skip.astype(f32)[:, None]
    y = y.reshape(b, L, B_INNER) * jax.nn.silu(z.astype(f32))
    yg = y.reshape(b, L, B_GROUPS, B_INNER // B_GROUPS)
    yg = yg * lax.rsqrt(jnp.mean(yg * yg, axis=-1, keepdims=True) + EPS)
    y = yg.reshape(b, L, B_INNER) * norm_g.astype(f32)
    return y.astype(proj.dtype), new_conv, new_ssm.astype(ssm_state.dtype)


def mixer_c(c_in, pool_state, start_pos, lin_w, scale):
    f32 = jnp.float32
    b, L, _ = c_in.shape
    p0 = pool_state.shape[1]
    ext = jnp.concatenate([pool_state.astype(f32), c_in.astype(f32)], axis=1)
    csum = jnp.concatenate([jnp.zeros((b, 1, C_WIDTH), f32), jnp.cumsum(ext, axis=1)], axis=1)
    hi = np.arange(p0, p0 + L) + 1
    pos = start_pos + np.arange(L)
    cur = ext[:, p0:]
    outs = []
    for gi, win in enumerate(C_WINDOWS):
        sl = slice(gi * C_GROUP_DIM, (gi + 1) * C_GROUP_DIM)
        lo = np.maximum(hi - win, 0)
        count = np.minimum(pos + 1, win).astype(np.float32)[None, :, None]
        pooled = (csum[:, hi, sl] - csum[:, lo, sl]) / count - cur[..., sl]
        outs.append(jnp.einsum('blc,cd->bld', pooled, lin_w[gi].astype(f32)))
    y = jnp.concatenate(outs, axis=-1) * scale.astype(f32)
    return y.astype(c_in.dtype), ext[:, ext.shape[1] - C_STATE_LEN:].astype(c_in.dtype)


def d_qkv(proj, q_norm, k_norm):
    b, L, _ = proj.shape
    qw = D_Q_HEADS * D_HEAD_DIM
    kw = D_KV_HEADS * D_HEAD_DIM
    q = proj[..., :qw].reshape(b, L, D_KV_HEADS, D_Q_HEADS // D_KV_HEADS, D_HEAD_DIM)
    k = proj[..., qw:qw + kw].reshape(b, L, D_KV_HEADS, D_HEAD_DIM)
    v = proj[..., qw + kw:].reshape(b, L, D_KV_HEADS, D_HEAD_DIM)
    return rms_norm(q, q_norm), rms_norm(k, k_norm), v


def sink_attention(q, k, v, bias, valid, sinks):
    f32 = jnp.float32
    grp, lq, lk = q.shape[3], q.shape[1], k.shape[1]
    s = jnp.einsum('bqhgd,bkhd->bhgqk', q.astype(f32), k.astype(f32)) * (D_HEAD_DIM ** -0.5)
    s = jnp.where(valid, s + bias.reshape(D_KV_HEADS, grp, lq, lk), NEG)
    sink = sinks.astype(f32).reshape(1, D_KV_HEADS, grp, 1, 1)
    m = jnp.maximum(jnp.max(s, axis=-1, keepdims=True), sink)
    p = jnp.exp(s - m)
    denom = jnp.sum(p, axis=-1, keepdims=True) + jnp.exp(sink - m)
    o = jnp.einsum('bhgqk,bkhd->bqhgd', p / denom, v.astype(f32))
    return o.astype(v.dtype)


def swa_prompt(q, k, v, rel_table, sinks):
    b, S = q.shape[0], q.shape[1]
    W = D_WINDOW
    kp = jnp.pad(k, ((0, 0), (W, 0), (0, 0), (0, 0)))
    vp = jnp.pad(v, ((0, 0), (W, 0), (0, 0), (0, 0)))
    r = np.arange(W) + W
    c = np.arange(2 * W)
    bias = band_bias(rel_table, r, c)
    dist = r[:, None] - c[None, :]
    band = (dist >= 0) & (dist < W)

    def block(i):
        start = i * W
        qb = lax.dynamic_slice_in_dim(q, start, W, axis=1)
        kb = lax.dynamic_slice_in_dim(kp, start, 2 * W, axis=1)
        vb = lax.dynamic_slice_in_dim(vp, start, 2 * W, axis=1)
        valid = band & ((start - W + c) >= 0)[None, :]
        return sink_attention(qb, kb, vb, bias, valid, sinks)

    o = lax.map(block, jnp.arange(S // W))
    return jnp.moveaxis(o, 0, 1).reshape(b, S, D_Q_HEADS * D_HEAD_DIM)


def swa_sample(q, k, v, k_buf, v_buf, rel_table, sinks, start_pos):
    b, L = q.shape[0], q.shape[1]
    nbuf = k_buf.shape[1]
    kk = jnp.concatenate([k_buf.astype(k.dtype), k], axis=1)
    vv = jnp.concatenate([v_buf.astype(v.dtype), v], axis=1)
    q_pos = start_pos + np.arange(L)
    k_pos = start_pos - nbuf + np.arange(nbuf + L)
    bias = band_bias(rel_table, q_pos, k_pos)
    dist = q_pos[:, None] - k_pos[None, :]
    valid = (dist >= 0) & (dist < D_WINDOW) & (k_pos >= 0)[None, :]
    o = sink_attention(q, kk, vv, bias, valid, sinks)
    return o.reshape(b, L, D_Q_HEADS * D_HEAD_DIM), kk[:, L:], vv[:, L:]


def even_mixer(xn, w_in, w_out, a_ln_g, a_ln_b, a_w_s, a_b_s, conv_w, conv_b, dt_bias, a_log, d_skip,
               b_norm_g, conv_state, ssm_state, a_chunk, b_chunk):
    proj = xn @ w_in
    ya, v_rows = mixer_a(proj[..., :2 * A_WIDTH], a_ln_g, a_ln_b, a_w_s, a_b_s, a_chunk)
    yb, new_conv, new_ssm = mixer_b(proj[..., 2 * A_WIDTH:], conv_state, ssm_state, conv_w, conv_b,
                                    dt_bias, a_log, d_skip, b_norm_g, b_chunk)
    y = jnp.concatenate([ya, yb.astype(ya.dtype)], axis=-1) @ w_out
    return y, v_rows, new_conv, new_ssm


def odd_mixer(xn, w_in, w_out, c_lin_w, c_scale, q_norm, k_norm, sinks, rel_table, pool_state, k_buf,
              v_buf, start_pos):
    proj = xn @ w_in
    yc, new_pool = mixer_c(proj[..., :C_WIDTH], pool_state, start_pos, c_lin_w, c_scale)
    q, k, v = d_qkv(proj[..., C_WIDTH:], q_norm, k_norm)
    if k_buf is None:
        yd = swa_prompt(q, k, v, rel_table, sinks)
        new_k = k[:, k.shape[1] - D_WINDOW:]
        new_v = v[:, v.shape[1] - D_WINDOW:]
    else:
        yd, new_k, new_v = swa_sample(q, k, v, k_buf, v_buf, rel_table, sinks, start_pos)
    y = jnp.concatenate([yc, yd.astype(yc.dtype)], axis=-1) @ w_out
    return y, new_pool, new_k, new_v


def setup_inputs(seed: int = 0) -> dict:
    key = jax.random.key(seed)
    it = iter(list(jax.random.split(key, 40)))
    f32 = jnp.float32

    def nrm(shape, scale):
        return jax.random.normal(next(it), shape, f32) * scale

    def gain(shape):
        return 1.0 + 0.1 * jax.random.normal(next(it), shape, f32)

    win_buf = min(D_WINDOW, PAST_LEN)
    inp = {}
    inp['x_prompt'] = nrm((BATCH, SEQ, D_MODEL), 1.0)
    inp['x_sample'] = nrm((DEC_BATCH, DEC_SEQ, D_MODEL), 1.0)
    inp['state_ssm'] = nrm((N_EVEN, DEC_BATCH, B_HEADS, B_HEAD_DIM, B_STATE), 0.5)
    inp['state_conv'] = nrm((N_EVEN, DEC_BATCH, B_CONV - 1, B_CONV_DIM), 1.0)
    inp['state_pool'] = nrm((N_ODD, DEC_BATCH, C_STATE_LEN, C_WIDTH), 1.0)
    inp['cache_k_win'] = nrm((N_ODD, DEC_BATCH, win_buf, D_KV_HEADS, D_HEAD_DIM), 1.0)
    inp['cache_v_win'] = nrm((N_ODD, DEC_BATCH, win_buf, D_KV_HEADS, D_HEAD_DIM), 1.0)
    inp['ffn1_norm'] = gain((DEPTH, D_MODEL))
    inp['ffn1_w_gu'] = nrm((DEPTH, D_MODEL, 2 * D_FF), D_MODEL ** -0.5)
    inp['ffn1_w_down'] = nrm((DEPTH, D_FF, D_MODEL), D_FF ** -0.5)
    inp['mix_norm'] = gain((DEPTH, D_MODEL))
    inp['ffn2_norm'] = gain((DEPTH, D_MODEL))
    inp['ffn2_w_gu'] = nrm((DEPTH, D_MODEL, 2 * D_FF), D_MODEL ** -0.5)
    inp['ffn2_w_down'] = nrm((DEPTH, D_FF, D_MODEL), D_FF ** -0.5)
    inp['ev_w_in'] = nrm((N_EVEN, D_MODEL, EVEN_IN), D_MODEL ** -0.5)
    inp['ev_w_out'] = nrm((N_EVEN, EVEN_MIX, D_MODEL), EVEN_MIX ** -0.5)
    inp['a_ln_g'] = gain((N_EVEN, A_WIDTH))
    inp['a_ln_b'] = nrm((N_EVEN, A_WIDTH), 0.02)
    inp['a_w_s'] = nrm((N_EVEN, A_HEADS, A_CHUNK, A_CHUNK), A_CHUNK ** -0.5)
    inp['a_b_s'] = gain((N_EVEN, A_HEADS, A_CHUNK))
    inp['b_conv_w'] = nrm((N_EVEN, B_CONV, B_CONV_DIM), B_CONV ** -0.5)
    inp['b_conv_b'] = nrm((N_EVEN, B_CONV_DIM), 0.02)
    dt0 = jnp.exp(jax.random.uniform(next(it), (N_EVEN, B_HEADS), f32, math.log(1e-3), math.log(1e-1)))
    inp['b_dt_bias'] = dt0 + jnp.log(-jnp.expm1(-dt0))
    inp['b_a_log'] = jnp.log(jax.random.uniform(next(it), (N_EVEN, B_HEADS), f32, 1.0, 16.0))
    inp['b_d_skip'] = gain((N_EVEN, B_HEADS))
    inp['b_norm_g'] = gain((N_EVEN, B_INNER))
    inp['od_w_in'] = nrm((N_ODD, D_MODEL, ODD_IN), D_MODEL ** -0.5)
    inp['od_w_out'] = nrm((N_ODD, ODD_MIX, D_MODEL), ODD_MIX ** -0.5)
    inp['c_lin_w'] = nrm((N_ODD, C_GROUPS, C_GROUP_DIM, C_GROUP_DIM), C_GROUP_DIM ** -0.5)
    inp['c_scale'] = gain((N_ODD, C_WIDTH))
    inp['d_q_norm'] = gain((N_ODD, D_HEAD_DIM))
    inp['d_k_norm'] = gain((N_ODD, D_HEAD_DIM))
    inp['d_sinks'] = nrm((N_ODD, D_Q_HEADS), 0.5)
    inp['rel_bias_table'] = nrm((REL_BUCKETS, D_Q_HEADS), 0.5)
    return inp


def reference(x_prompt, x_sample, state_ssm, state_conv, state_pool, cache_k_win, cache_v_win,
              ffn1_norm, ffn1_w_gu, ffn1_w_down, mix_norm, ffn2_norm, ffn2_w_gu, ffn2_w_down,
              ev_w_in, ev_w_out, a_ln_g, a_ln_b, a_w_s, a_b_s, b_conv_w, b_conv_b, b_dt_bias, b_a_log,
              b_d_skip, b_norm_g, od_w_in, od_w_out, c_lin_w, c_scale, d_q_norm, d_k_norm, d_sinks,
              rel_bias_table):
    hp, hs = x_prompt, x_sample
    bp = x_prompt.shape[0]
    new_a_v_s, new_ssm_p, new_ssm_s, new_conv_p, new_conv_s = [], [], [], [], []
    new_pool_p, new_pool_s, new_k_p, new_k_s, new_v_p, new_v_s = [], [], [], [], [], []
    for layer in range(DEPTH):
        i = layer // 2
        hp = macaron_half(hp, ffn1_norm[layer], ffn1_w_gu[layer], ffn1_w_down[layer])
        hs = macaron_half(hs, ffn1_norm[layer], ffn1_w_gu[layer], ffn1_w_down[layer])
        xp = rms_norm(hp, mix_norm[layer])
        xs = rms_norm(hs, mix_norm[layer])
        if layer % 2 == 0:
            ev = (ev_w_in[i], ev_w_out[i], a_ln_g[i], a_ln_b[i], a_w_s[i], a_b_s[i], b_conv_w[i],
                  b_conv_b[i], b_dt_bias[i], b_a_log[i], b_d_skip[i], b_norm_g[i])
            yp, _, conv_p, ssm_p = even_mixer(
                xp, *ev, jnp.zeros((bp, B_CONV - 1, B_CONV_DIM), xp.dtype),
                jnp.zeros((bp, B_HEADS, B_HEAD_DIM, B_STATE), jnp.float32), A_CHUNK, B_CHUNK)
            ys, v_rows_s, conv_s, ssm_s = even_mixer(
                xs, *ev, state_conv[i], state_ssm[i], xs.shape[1], xs.shape[1])
            new_a_v_s.append(v_rows_s)
            new_ssm_p.append(ssm_p)
            new_ssm_s.append(ssm_s)
            new_conv_p.append(conv_p)
            new_conv_s.append(conv_s)
        else:
            od = (od_w_in[i], od_w_out[i], c_lin_w[i], c_scale[i], d_q_norm[i], d_k_norm[i], d_sinks[i],
                  rel_bias_table)
            yp, pool_p, k_p, v_p = odd_mixer(
                xp, *od, jnp.zeros((bp, 0, C_WIDTH), xp.dtype), None, None, 0)
            ys, pool_s, k_s, v_s = odd_mixer(
                xs, *od, state_pool[i], cache_k_win[i], cache_v_win[i], PAST_LEN)
            new_pool_p.append(pool_p)
            new_pool_s.append(pool_s)
            new_k_p.append(k_p)
            new_k_s.append(k_s)
            new_v_p.append(v_p)
            new_v_s.append(v_s)
        hp = hp + yp
        hs = hs + ys
        hp = macaron_half(hp, ffn2_norm[layer], ffn2_w_gu[layer], ffn2_w_down[layer])
        hs = macaron_half(hs, ffn2_norm[layer], ffn2_w_gu[layer], ffn2_w_down[layer])
    return (hp, hs, jnp.stack(new_a_v_s), jnp.stack(new_ssm_p), jnp.stack(new_ssm_s),
            jnp.stack(new_conv_p), jnp.stack(new_conv_s), jnp.stack(new_pool_p), jnp.stack(new_pool_s),
            jnp.stack(new_k_p), jnp.stack(new_k_s), jnp.stack(new_v_p), jnp.stack(new_v_s))
```

```python
import functools
import math

import numpy as np
import jax
import jax.numpy as jnp
from jax import lax
from jax.experimental import pallas as pl
from jax.experimental.pallas import tpu as pltpu

F32 = jnp.float32
BF16 = jnp.bfloat16

EPS = 1e-6
NEG = -1e30

LANES = 128
SUBLANES = 8
VMEM_BYTES_V7X = 64 * 1024 * 1024
VMEM_LIMIT = VMEM_BYTES_V7X - 8 * 1024 * 1024

A_HEADS = 8
B_HEADS = 16
B_HEAD_DIM = 64
B_GROUPS = 2
B_STATE = 128
B_CONV = 4
CHUNK = 128
C_WINDOWS = (2, 4, 8, 16)
C_HALO = 16
D_Q_HEADS = 16
D_KV_HEADS = 4
D_HEAD_DIM = 64
D_GROUP = D_Q_HEADS // D_KV_HEADS
REL_BUCKETS = 32
REL_MAX_DIST = 128


def _rms(x, g):
    ms = jnp.mean(x * x, axis=-1, keepdims=True)
    return x * lax.rsqrt(ms + EPS) * g


def _sigmoid(x):
    return 1.0 / (1.0 + jnp.exp(-x))


def _silu(x):
    return x * _sigmoid(x)


def _gelu_tanh(x):
    c = math.sqrt(2.0 / math.pi)
    return x * (0.5 * (1.0 + jnp.tanh(c * (x + 0.044715 * (x * x * x)))))


def _softplus(x):
    return jnp.maximum(x, 0.0) + jnp.log1p(jnp.exp(-jnp.abs(x)))


def _split3(x):
    hi = x.astype(BF16)
    r1 = x - hi.astype(F32)
    mid = r1.astype(BF16)
    lo = (r1 - mid.astype(F32)).astype(BF16)
    return hi, mid, lo


def _dot(a, b):
    return jnp.dot(a, b, preferred_element_type=F32)


def _dot_nt(a, b):
    return lax.dot_general(a, b, (((1,), (1,)), ((), ())), preferred_element_type=F32)


def _dot_tn(a, b):
    return lax.dot_general(a, b, (((0,), (0,)), ((), ())), preferred_element_type=F32)


def _expand_heads(m, n_pairs):
    rows = m.shape[0]
    lane = lax.broadcasted_iota(jnp.int32, (rows, LANES), 1)
    first = lane < B_HEAD_DIM
    parts = []
    for p in range(n_pairs):
        a = jnp.broadcast_to(m[:, 2 * p:2 * p + 1], (rows, LANES))
        b = jnp.broadcast_to(m[:, 2 * p + 1:2 * p + 2], (rows, LANES))
        parts.append(jnp.where(first, a, b))
    return jnp.concatenate(parts, axis=1)


def _head_sumsq(x, ones_bd):
    xx = x * x
    hi = xx.astype(BF16)
    lo = (xx - hi.astype(F32)).astype(BF16)
    outs = []
    for c in range(x.shape[1] // 256):
        sl = slice(c * 256, (c + 1) * 256)
        outs.append(_dot(hi[:, sl], ones_bd) + _dot(lo[:, sl], ones_bd))
    return jnp.concatenate(outs, axis=1) if len(outs) > 1 else outs[0]


def _ffn_kernel(x_ref, g_ref, wgu_ref, wd_ref, o_ref, *, d_ff):
    x = x_ref[...]
    xn = _rms(x, g_ref[...]).astype(BF16)
    gu = _dot(xn, wgu_ref[...])
    gate = gu[:, :d_ff]
    up = gu[:, d_ff:]
    act = (_silu(gate) * up).astype(BF16)
    o_ref[...] = x + 0.5 * _dot(act, wd_ref[...])


def _resident(shape):
    nd = len(shape)
    return pl.BlockSpec(shape, lambda *_: (0,) * nd, pipeline_mode=pl.Buffered(1))


def _ffn(x2d, g, wgu, wd, tm):
    m, d = x2d.shape
    d_ff = wd.shape[0]
    assert m % tm == 0
    return pl.pallas_call(
        functools.partial(_ffn_kernel, d_ff=d_ff),
        out_shape=jax.ShapeDtypeStruct((m, d), F32),
        grid=(m // tm,),
        in_specs=[
            pl.BlockSpec((tm, d), lambda i: (i, 0)),
            _resident((1, d)),
            _resident(wgu.shape),
            _resident(wd.shape),
        ],
        out_specs=pl.BlockSpec((tm, d), lambda i: (i, 0)),
        compiler_params=pltpu.CompilerParams(
            dimension_semantics=("arbitrary",), vmem_limit_bytes=VMEM_LIMIT),
        name="ffn",
    )(x2d, g, wgu, wd)


def _even_prompt_kernel(h_ref, g_ref, wa_ref, wz_ref, wxbc_ref, wdt_ref, wout_ref,
                        lng_ref, lnb_ref, ws_ref, bst_ref, convw_ref, convb_ref,
                        dtb_ref, alog_ref, dskip_ref, normg_ref,
                        o_ref, conv_out_ref, ssm_out_ref,
                        ext_ref, st_ref, *, tile, width_a, inner):
    s = pl.program_id(1)
    n_chunks = tile // CHUNK
    halo = SUBLANES

    @pl.when(s == 0)
    def _():
        ext_ref[0:halo, :] = jnp.zeros((halo, ext_ref.shape[1]), F32)
        st_ref[...] = jnp.zeros(st_ref.shape, F32)

    x = h_ref[...]
    xn = _rms(x, g_ref[...]).astype(BF16)

    row = lax.broadcasted_iota(jnp.int32, (CHUNK, CHUNK), 0)
    col = lax.broadcasted_iota(jnp.int32, (CHUNK, CHUNK), 1)
    causal = row >= col
    lane = lax.broadcasted_iota(jnp.int32, (CHUNK, LANES), 1)
    first_half = lane < B_HEAD_DIM

    ga = _gelu_tanh(_dot(xn, wa_ref[...]))
    u = ga[:, :width_a]
    v = ga[:, width_a:]
    mu = jnp.mean(v, axis=-1, keepdims=True)
    vc = v - mu
    var = jnp.mean(vc * vc, axis=-1, keepdims=True)
    v = vc * lax.rsqrt(var + EPS) * lng_ref[...] + lnb_ref[...]
    vb = v.astype(BF16)
    head_w = width_a // A_HEADS
    gate_cols = []
    for hh in range(A_HEADS):
        w = jnp.where(causal, ws_ref[hh], 0.0).astype(BF16)
        rhs = jnp.concatenate(
            [vb[c * CHUNK:(c + 1) * CHUNK, hh * head_w:(hh + 1) * head_w] for c in range(n_chunks)], axis=1)
        out = _dot(w, rhs)
        bias = jnp.broadcast_to(bst_ref[:, hh:hh + 1], (CHUNK, head_w))
        gate_cols.append(jnp.concatenate(
            [out[:, c * head_w:(c + 1) * head_w] + bias for c in range(n_chunks)], axis=0))
    ya = u * jnp.concatenate(gate_cols, axis=1)

    z = _dot(xn, wz_ref[...])
    xbc_raw = _dot(xn, wxbc_ref[...])
    dt = _softplus(_dot(xn, wdt_ref[...]) + dtb_ref[...])
    ext_ref[halo:halo + tile, :] = xbc_raw
    conv = convb_ref[...] + xbc_raw * convw_ref[B_CONV - 1:B_CONV, :]
    for k in range(1, B_CONV):
        conv = conv + ext_ref[halo - k:halo - k + tile, :] * convw_ref[B_CONV - 1 - k:B_CONV - k, :]
    tail = ext_ref[tile:tile + halo, :]
    ext_ref[0:halo, :] = tail
    conv_out_ref[...] = tail
    xbc = _silu(conv)
    gn = B_GROUPS * B_STATE
    xs = xbc[:, :inner]
    bm = xbc[:, inner:inner + gn]
    cm = xbc[:, inner + gn:]
    a_neg = -jnp.exp(alog_ref[...])
    da = dt * a_neg

    n_pairs = B_HEADS // 2
    heads_per_group = B_HEADS // B_GROUPS
    gw = heads_per_group * B_HEAD_DIM
    tril_ones = jnp.where(causal, 1.0, 0.0).astype(BF16)
    y_rows = []
    for c in range(n_chunks):
        rs = slice(c * CHUNK, (c + 1) * CHUNK)
        d_hi, d_mid, d_lo = _split3(da[rs])
        acum = _dot(tril_ones, d_hi) + _dot(tril_ones, d_mid) + _dot(tril_ones, d_lo)
        acum_t = acum.T
        a_last = acum[CHUNK - 1:CHUNK, :]
        dt_c = dt[rs]
        xs_c = xs[rs]
        xd = xs_c * _expand_heads(dt_c, n_pairs)
        xdw = xs_c * _expand_heads(dt_c * jnp.exp(a_last - acum), n_pairs)
        e_acum = _expand_heads(jnp.exp(acum), n_pairs)
        chunk_decay = _expand_heads(jnp.exp(a_last), n_pairs)
        bmc = bm[rs].astype(BF16)
        cmc = cm[rs].astype(BF16)
        cb = [_dot_nt(cmc[:, g * B_STATE:(g + 1) * B_STATE], bmc[:, g * B_STATE:(g + 1) * B_STATE])
              for g in range(B_GROUPS)]
        y_parts = []
        for p in range(n_pairs):
            g = (2 * p) // heads_per_group
            ms = []
            for hh in (2 * p, 2 * p + 1):
                seg = jnp.broadcast_to(acum[:, hh:hh + 1], (CHUNK, CHUNK)) - \
                    jnp.broadcast_to(acum_t[hh:hh + 1, :], (CHUNK, CHUNK))
                lmat = jnp.where(causal, jnp.exp(seg), 0.0)
                ms.append((cb[g] * lmat).astype(BF16))
            lhs = jnp.concatenate(ms, axis=1)
            xd_p = xd[:, p * LANES:(p + 1) * LANES]
            rhs = jnp.concatenate([jnp.where(first_half, xd_p, 0.0),
                                   jnp.where(first_half, 0.0, xd_p)], axis=0).astype(BF16)
            y_parts.append(_dot(lhs, rhs))
        y_diag = jnp.concatenate(y_parts, axis=1)
        st_prev = st_ref[...]
        stb = st_prev.astype(BF16)
        y_off = jnp.concatenate(
            [_dot(cmc[:, g * B_STATE:(g + 1) * B_STATE], stb[:, g * gw:(g + 1) * gw]) for g in range(B_GROUPS)],
            axis=1)
        xdwb = xdw.astype(BF16)
        st_add = jnp.concatenate(
            [_dot_tn(bmc[:, g * B_STATE:(g + 1) * B_STATE], xdwb[:, g * gw:(g + 1) * gw]) for g in range(B_GROUPS)],
            axis=1)
        st_ref[...] = st_prev * chunk_decay + st_add
        y_rows.append(y_diag + y_off * e_acum + xs_c * dskip_ref[...])
    y = jnp.concatenate(y_rows, axis=0) if n_chunks > 1 else y_rows[0]
    y = y * _silu(z)
    half = inner // B_GROUPS
    yn = []
    for g in range(B_GROUPS):
        yg = y[:, g * half:(g + 1) * half]
        yn.append(yg * lax.rsqrt(jnp.mean(yg * yg, axis=-1, keepdims=True) + EPS))
    yb = jnp.concatenate(yn, axis=1) * normg_ref[...]

    mix = jnp.concatenate([ya, yb], axis=1).astype(BF16)
    o_ref[...] = x + _dot(mix, wout_ref[...])

    @pl.when(s == pl.num_programs(1) - 1)
    def _():
        ssm_out_ref[...] = st_ref[...].T


def _even_prompt(h, p, tile):
    b, seq, d = h.shape
    width_a = p["wa"].shape[1] // 2
    inner = p["wz"].shape[1]
    conv_dim = p["wxbc"].shape[1]
    assert seq % tile == 0 and tile % CHUNK == 0
    small = ["lng", "lnb", "ws", "bst", "convw", "convb", "dtb", "alog", "dskip", "normg"]
    out, conv_tail, ssm = pl.pallas_call(
        functools.partial(_even_prompt_kernel, tile=tile, width_a=width_a, inner=inner),
        out_shape=(jax.ShapeDtypeStruct((b, seq, d), F32),
                   jax.ShapeDtypeStruct((b, SUBLANES, conv_dim), F32),
                   jax.ShapeDtypeStruct((b, inner, B_STATE), F32)),
        grid=(b, seq // tile),
        in_specs=[pl.BlockSpec((None, tile, d), lambda i, j: (i, j, 0)),
                  _resident((1, d))]
        + [_resident(p[k].shape) for k in ("wa", "wz", "wxbc", "wdt", "wout")]
        + [_resident(p[k].shape) for k in small],
        out_specs=(pl.BlockSpec((None, tile, d), lambda i, j: (i, j, 0)),
                   pl.BlockSpec((None, SUBLANES, conv_dim), lambda i, j: (i, 0, 0)),
                   pl.BlockSpec((None, inner, B_STATE), lambda i, j: (i, 0, 0))),
        scratch_shapes=[pltpu.VMEM((tile + SUBLANES, conv_dim), F32),
                        pltpu.VMEM((B_STATE, inner), F32)],
        compiler_params=pltpu.CompilerParams(
            dimension_semantics=("arbitrary", "arbitrary"), vmem_limit_bytes=VMEM_LIMIT),
        name="even_prompt",
    )(h, p["g"], p["wa"], p["wz"], p["wxbc"], p["wdt"], p["wout"], *[p[k] for k in small])
    return out, conv_tail[:, SUBLANES - (B_CONV - 1):, :], ssm


def _t5_bucket(dist):
    n = np.maximum(dist, 0)
    max_exact = REL_BUCKETS // 2
    n_safe = np.maximum(n, 1).astype(np.float32)
    scale = np.float32((REL_BUCKETS - max_exact) / math.log(REL_MAX_DIST / max_exact))
    large = max_exact + (np.log(n_safe / max_exact) * scale).astype(np.int32)
    large = np.minimum(large, REL_BUCKETS - 1)
    return np.where(n < max_exact, n, large).astype(np.int32)


def _fill_bias(bias_ref, bucket_ref, rel_ref):
    bucket = bucket_ref[...]
    for hh in range(D_Q_HEADS):
        acc = jnp.full(bucket.shape, NEG, F32)
        for bkt in range(REL_BUCKETS):
            acc = jnp.where(bucket == bkt, rel_ref[bkt, hh], acc)
        bias_ref[hh] = acc


def _group_attention(qg, kk, vv, bias_ref, sinks_ref, grp, extra_mask):
    lq = qg.shape[0]
    lane_kv = lax.broadcasted_iota(jnp.int32, (lq, D_KV_HEADS * D_HEAD_DIM), 1) // D_HEAD_DIM
    zero = jnp.zeros_like(qg)
    lhs = jnp.concatenate([jnp.where(lane_kv == kv, qg, zero) for kv in range(D_KV_HEADS)], axis=0)
    sc = _dot_nt(lhs, kk)
    probs = []
    for kv in range(D_KV_HEADS):
        hh = kv * D_GROUP + grp
        s_h = sc[kv * lq:(kv + 1) * lq] + bias_ref[hh]
        if extra_mask is not None:
            s_h = s_h + extra_mask
        sink = sinks_ref[hh]
        m = jnp.maximum(jnp.max(s_h, axis=-1, keepdims=True), sink)
        pexp = jnp.exp(s_h - m)
        denom = jnp.sum(pexp, axis=-1, keepdims=True) + jnp.exp(sink - m)
        probs.append((pexp / denom).astype(BF16))
    ov = _dot(jnp.concatenate(probs, axis=0), vv)
    out = jnp.zeros((lq, D_KV_HEADS * D_HEAD_DIM), F32)
    for kv in range(D_KV_HEADS):
        out = out + jnp.where(lane_kv == kv, ov[kv * lq:(kv + 1) * lq], 0.0)
    return out


def _odd_prompt_kernel(h_ref, g_ref, wc_ref, wq_ref, wk_ref, wv_ref, wout_ref,
                       linw_ref, cscale_ref, qn_ref, kn_ref, onesbd_ref, bucket_ref,
                       sinks_ref, rel_ref,
                       o_ref, pool_out_ref, k_out_ref, v_out_ref,
                       extc_ref, kprev_ref, vprev_ref, bias_ref, *, tile, width_c):
    b = pl.program_id(0)
    s = pl.program_id(1)
    n_blocks = tile // CHUNK

    @pl.when((b == 0) & (s == 0))
    def _():
        _fill_bias(bias_ref, bucket_ref, rel_ref)

    @pl.when(s == 0)
    def _():
        extc_ref[0:C_HALO, :] = jnp.zeros((C_HALO, width_c), F32)
        kprev_ref[...] = jnp.zeros(kprev_ref.shape, F32)
        vprev_ref[...] = jnp.zeros(vprev_ref.shape, F32)

    x = h_ref[...]
    xn = _rms(x, g_ref[...]).astype(BF16)

    c_in = _dot(xn, wc_ref[...])
    extc_ref[C_HALO:C_HALO + tile, :] = c_in
    e = extc_ref[...]
    tail = extc_ref[tile:tile + C_HALO, :]
    extc_ref[0:C_HALO, :] = tail
    pool_out_ref[...] = tail
    pos = (s * tile + lax.broadcasted_iota(jnp.int32, (tile, 1), 0) + 1).astype(F32)
    gdim = width_c // len(C_WINDOWS)
    run = e
    shift = 1
    yc = []
    for gi, win in enumerate(C_WINDOWS):
        while shift < win:
            run = run + pltpu.roll(run, shift, 0)
            shift *= 2
        cnt = jnp.minimum(pos, float(win))
        sl = slice(gi * gdim, (gi + 1) * gdim)
        pooled = run[C_HALO:, sl] / cnt - c_in[:, sl]
        yc.append(_dot(pooled.astype(BF16), linw_ref[gi]))
    yc = jnp.concatenate(yc, axis=1) * cscale_ref[...]

    q = _dot(xn, wq_ref[...])
    k = _dot(xn, wk_ref[...])
    v = _dot(xn, wv_ref[...])
    ones_bd = onesbd_ref[...]
    inv_d = 1.0 / D_HEAD_DIM
    qn = q * lax.rsqrt(_head_sumsq(q, ones_bd) * inv_d + EPS) * qn_ref[...]
    kn = k * lax.rsqrt(_head_sumsq(k, ones_bd) * inv_d + EPS) * kn_ref[...]
    qs = (qn * (D_HEAD_DIM ** -0.5)).astype(BF16)
    kb = kn.astype(BF16)
    vb = v.astype(BF16)
    key_lane = lax.broadcasted_iota(jnp.int32, (CHUNK, 2 * CHUNK), 1)
    first_mask = jnp.where(key_lane < CHUNK, jnp.where(s == 0, NEG, 0.0), 0.0)
    gw = D_KV_HEADS * D_HEAD_DIM
    o_rows = []
    for blk in range(n_blocks):
        rs = slice(blk * CHUNK, (blk + 1) * CHUNK)
        if blk == 0:
            k_prev, v_prev = kprev_ref[...].astype(BF16), vprev_ref[...].astype(BF16)
        else:
            k_prev, v_prev = kb[(blk - 1) * CHUNK:blk * CHUNK], vb[(blk - 1) * CHUNK:blk * CHUNK]
        kk = jnp.concatenate([k_prev, kb[rs]], axis=0)
        vv = jnp.concatenate([v_prev, vb[rs]], axis=0)
        o_rows.append(jnp.concatenate(
            [_group_attention(qs[rs, grp * gw:(grp + 1) * gw], kk, vv, bias_ref, sinks_ref, grp,
                              first_mask if blk == 0 else None)
             for grp in range(D_GROUP)], axis=1))
    kprev_ref[...] = kn[tile - CHUNK:]
    vprev_ref[...] = v[tile - CHUNK:]
    yd = jnp.concatenate(o_rows, axis=0) if n_blocks > 1 else o_rows[0]

    mix = jnp.concatenate([yc, yd], axis=1).astype(BF16)
    o_ref[...] = x + _dot(mix, wout_ref[...])

    @pl.when(s == pl.num_programs(1) - 1)
    def _():
        k_out_ref[...] = kn[tile - CHUNK:]
        v_out_ref[...] = v[tile - CHUNK:]


def _odd_prompt(h, p, tile):
    b, seq, d = h.shape
    width_c = p["wc"].shape[1]
    kvw = p["wk"].shape[1]
    assert seq % tile == 0 and tile % CHUNK == 0
    r = np.arange(CHUNK) + CHUNK
    c = np.arange(2 * CHUNK)
    dist = r[:, None] - c[None, :]
    bucket = np.where((dist >= 0) & (dist < CHUNK), _t5_bucket(dist), -1).astype(np.int32)
    vm = ["linw", "cscale", "qn", "kn", "onesbd"]
    smem = pl.BlockSpec(memory_space=pltpu.SMEM)
    out, pool_tail, k_win, v_win = pl.pallas_call(
        functools.partial(_odd_prompt_kernel, tile=tile, width_c=width_c),
        out_shape=(jax.ShapeDtypeStruct((b, seq, d), F32),
                   jax.ShapeDtypeStruct((b, C_HALO, width_c), F32),
                   jax.ShapeDtypeStruct((b, CHUNK, kvw), F32),
                   jax.ShapeDtypeStruct((b, CHUNK, kvw), F32)),
        grid=(b, seq // tile),
        in_specs=[pl.BlockSpec((None, tile, d), lambda i, j: (i, j, 0)),
                  _resident((1, d))]
        + [_resident(p[k].shape) for k in ("wc", "wq", "wk", "wv", "wout")]
        + [_resident(p[k].shape) for k in vm]
        + [_resident(bucket.shape), smem, smem],
        out_specs=(pl.BlockSpec((None, tile, d), lambda i, j: (i, j, 0)),
                   pl.BlockSpec((None, C_HALO, width_c), lambda i, j: (i, 0, 0)),
                   pl.BlockSpec((None, CHUNK, kvw), lambda i, j: (i, 0, 0)),
                   pl.BlockSpec((None, CHUNK, kvw), lambda i, j: (i, 0, 0))),
        scratch_shapes=[pltpu.VMEM((tile + C_HALO, width_c), F32),
                        pltpu.VMEM((CHUNK, kvw), F32),
                        pltpu.VMEM((CHUNK, kvw), F32),
                        pltpu.VMEM((D_Q_HEADS, CHUNK, 2 * CHUNK), F32)],
        compiler_params=pltpu.CompilerParams(
            dimension_semantics=("arbitrary", "arbitrary"), vmem_limit_bytes=VMEM_LIMIT),
        name="odd_prompt",
    )(h, p["g"], p["wc"], p["wq"], p["wk"], p["wv"], p["wout"], *[p[k] for k in vm],
      jnp.asarray(bucket), p["sinks"], p["rel"])
    return out, pool_tail[:, C_HALO - (max(C_WINDOWS) - 1):, :], k_win, v_win


def _steps(x, n, width):
    return [x[:, t * width:(t + 1) * width] for t in range(n)]


def _stack_steps(ref, n, width):
    x = ref[...]
    return jnp.concatenate(_steps(x, n, width), axis=0)


def _even_sample_front_kernel(hs_ref, g_ref, wa_ref, wz_ref, wxbc_ref, wdt_ref,
                              lng_ref, lnb_ref, wts_ref, bts_ref, convw_ref, convb_ref,
                              dtb_ref, alog_ref, dskip_ref, cs_ref,
                              v_out, ya_out, ypart_out, eacum_out, z_out, conv_out,
                              cgt_out, xdw_out, bs_out, dec_out, *, steps, d, width_a, inner, conv_dim):
    bt = hs_ref.shape[0]
    xn = _rms(_stack_steps(hs_ref, steps, d), g_ref[...]).astype(BF16)
    blk = lambda a, t: a[t * bt:(t + 1) * bt]

    ga = _gelu_tanh(_dot(xn, wa_ref[...]))
    u = ga[:, :width_a]
    v = ga[:, width_a:]
    mu = jnp.mean(v, axis=-1, keepdims=True)
    vc = v - mu
    var = jnp.mean(vc * vc, axis=-1, keepdims=True)
    v = vc * lax.rsqrt(var + EPS) * lng_ref[...] + lnb_ref[...]
    for t in range(steps):
        v_out[:, t * width_a:(t + 1) * width_a] = blk(v, t)
        gate = bts_ref[t:t + 1, :]
        for s in range(t + 1):
            gate = gate + wts_ref[t * steps + s:t * steps + s + 1, :] * blk(v, s)
        ya_out[:, t * width_a:(t + 1) * width_a] = blk(u, t) * gate

    z = _dot(xn, wz_ref[...])
    for t in range(steps):
        z_out[:, t * inner:(t + 1) * inner] = blk(z, t)
    raw = _dot(xn, wxbc_ref[...])
    dt = _softplus(_dot(xn, wdt_ref[...]) + dtb_ref[...])
    ext = _steps(cs_ref[...], B_CONV - 1, conv_dim) + [blk(raw, t) for t in range(steps)]
    for k in range(B_CONV - 1):
        conv_out[:, k * conv_dim:(k + 1) * conv_dim] = ext[len(ext) - (B_CONV - 1) + k]
    gn = B_GROUPS * B_STATE
    n_pairs = B_HEADS // 2
    a_neg = -jnp.exp(alog_ref[...])
    xs, bm, cm, dts, acum = [], [], [], [], []
    for t in range(steps):
        conv = convb_ref[...]
        for tap in range(B_CONV):
            conv = conv + ext[t + tap] * convw_ref[tap:tap + 1, :]
        xbc = _silu(conv)
        xs.append(xbc[:, :inner])
        bm.append(xbc[:, inner:inner + gn])
        cm.append(xbc[:, inner + gn:])
        dts.append(blk(dt, t))
        da = dts[t] * a_neg
        acum.append(da if t == 0 else acum[t - 1] + da)
    lane = lax.broadcasted_iota(jnp.int32, (bt, LANES), 1)
    group0 = lane < (B_HEADS // B_GROUPS)
    dec_out[...] = jnp.exp(acum[steps - 1])
    pad_rows = SUBLANES - steps
    xdw_out[:, steps * inner:] = jnp.zeros((bt, pad_rows * inner), F32)
    bs_out[:, steps * gn:] = jnp.zeros((bt, pad_rows * gn), F32)
    xd = []
    for t in range(steps):
        xd.append(xs[t] * _expand_heads(dts[t], n_pairs))
        eacum_out[:, t * inner:(t + 1) * inner] = _expand_heads(jnp.exp(acum[t]), n_pairs)
        xdw_out[:, t * inner:(t + 1) * inner] = xs[t] * _expand_heads(
            dts[t] * jnp.exp(acum[steps - 1] - acum[t]), n_pairs)
        bs_out[:, t * gn:(t + 1) * gn] = bm[t]
        for g in range(B_GROUPS):
            r = g * steps + t
            cgt_out[:, r * B_STATE:(r + 1) * B_STATE] = cm[t][:, g * B_STATE:(g + 1) * B_STATE]
    for t in range(steps):
        y = xs[t] * dskip_ref[...]
        for s in range(t + 1):
            cb = [jnp.sum(cm[t][:, g * B_STATE:(g + 1) * B_STATE] * bm[s][:, g * B_STATE:(g + 1) * B_STATE],
                          axis=-1, keepdims=True) for g in range(B_GROUPS)]
            coef = jnp.where(group0, cb[0], cb[1]) * jnp.exp(acum[t] - acum[s])
            y = y + _expand_heads(coef, n_pairs) * xd[s]
        ypart_out[:, t * inner:(t + 1) * inner] = y


def _even_sample_state_kernel(s0_ref, cgt_ref, xdw_ref, bs_ref, dec_ref, yoff_ref, snew_ref, *, bb):
    step = pl.program_id(0)
    gw = (B_HEADS // B_GROUPS) * B_HEAD_DIM

    def body(bi, carry):
        s0 = s0_ref[bi]
        c8 = cgt_ref[bi]
        c16 = jnp.concatenate([c8, jnp.zeros_like(c8)], axis=0).astype(BF16)
        yoff_ref[bi] = _dot_nt(c16, s0.astype(BF16))[:SUBLANES]
        x8 = xdw_ref[bi]
        b8 = bs_ref[bi]
        x16 = jnp.concatenate([x8, jnp.zeros_like(x8)], axis=0).astype(BF16)
        b16 = jnp.concatenate([b8, jnp.zeros_like(b8)], axis=0).astype(BF16)
        for g in range(B_GROUPS):
            add = _dot_tn(x16[:, g * gw:(g + 1) * gw], b16[:, g * B_STATE:(g + 1) * B_STATE])
            for hl in range(B_HEADS // B_GROUPS):
                hh = g * (B_HEADS // B_GROUPS) + hl
                rs = slice(hh * B_HEAD_DIM, (hh + 1) * B_HEAD_DIM)
                snew_ref[bi, rs, :] = s0[rs] * dec_ref[step * bb + bi, hh] + \
                    add[hl * B_HEAD_DIM:(hl + 1) * B_HEAD_DIM]
        return carry

    lax.fori_loop(0, bb, body, 0)


def _even_sample_back_kernel(hs_ref, ya_ref, ypart_ref, eacum_ref, z_ref, yoff_ref, normg_ref, wout_ref,
                             o_ref, *, steps, d, inner):
    bt = hs_ref.shape[0]
    half = inner // B_GROUPS
    mixes = []
    for t in range(steps):
        sl = slice(t * inner, (t + 1) * inner)
        yoff = jnp.concatenate(
            [yoff_ref[:, (g * steps + t) * inner + g * half:(g * steps + t) * inner + (g + 1) * half]
             for g in range(B_GROUPS)], axis=1)
        y = (ypart_ref[:, sl] + yoff * eacum_ref[:, sl]) * _silu(z_ref[:, sl])
        yn = []
        for g in range(B_GROUPS):
            yg = y[:, g * half:(g + 1) * half]
            yn.append(yg * lax.rsqrt(jnp.mean(yg * yg, axis=-1, keepdims=True) + EPS))
        yb = jnp.concatenate(yn, axis=1) * normg_ref[...]
        mixes.append(jnp.concatenate([ya_ref[:, t * d:(t + 1) * d], yb], axis=1))
    out = _dot(jnp.concatenate(mixes, axis=0).astype(BF16), wout_ref[...])
    for t in range(steps):
        o_ref[:, t * d:(t + 1) * d] = hs_ref[:, t * d:(t + 1) * d] + out[t * bt:(t + 1) * bt]


def _row_tiled(width, bt):
    return pl.BlockSpec((bt, width), lambda i: (i, 0))


def _even_sample(hs2, state_conv2, state_ssm3, p, steps, bt, bb):
    nb, _ = hs2.shape
    d = p["wa"].shape[0]
    width_a = p["wa"].shape[1] // 2
    inner = p["wz"].shape[1]
    conv_dim = p["wxbc"].shape[1]
    gn = B_GROUPS * B_STATE
    assert nb % bt == 0 and nb % bb == 0 and steps <= SUBLANES
    params = pltpu.CompilerParams(dimension_semantics=("arbitrary",), vmem_limit_bytes=VMEM_LIMIT)
    small = ["lng", "lnb", "wts", "bts", "convw", "convb", "dtb", "alog", "dskip"]
    widths = dict(v=steps * width_a, ya=steps * width_a, ypart=steps * inner, eacum=steps * inner,
                  z=steps * inner, conv=(B_CONV - 1) * conv_dim, cgt=SUBLANES * B_STATE,
                  xdw=SUBLANES * inner, bs=SUBLANES * gn, dec=LANES)
    front = pl.pallas_call(
        functools.partial(_even_sample_front_kernel, steps=steps, d=d, width_a=width_a, inner=inner,
                          conv_dim=conv_dim),
        out_shape=tuple(jax.ShapeDtypeStruct((nb, w), F32) for w in widths.values()),
        grid=(nb // bt,),
        in_specs=[_row_tiled(steps * d, bt), _resident((1, d))]
        + [_resident(p[k].shape) for k in ("wa", "wz", "wxbc", "wdt")]
        + [_resident(p[k].shape) for k in small]
        + [_row_tiled((B_CONV - 1) * conv_dim, bt)],
        out_specs=tuple(_row_tiled(w, bt) for w in widths.values()),
        compiler_params=params,
        name="even_sample_front",
    )(hs2, p["g"], p["wa"], p["wz"], p["wxbc"], p["wdt"], *[p[k] for k in small], state_conv2)
    v_rows, ya, ypart, eacum, z, new_conv, cgt, xdw, bs, dec = front

    hp = state_ssm3.shape[1]
    tile3 = lambda rows, width: pl.BlockSpec((bb, rows, width), lambda i: (i, 0, 0))
    yoff, new_ssm = pl.pallas_call(
        functools.partial(_even_sample_state_kernel, bb=bb),
        out_shape=(jax.ShapeDtypeStruct((nb, SUBLANES, hp), F32),
                   jax.ShapeDtypeStruct(state_ssm3.shape, F32)),
        grid=(nb // bb,),
        in_specs=[tile3(hp, B_STATE), tile3(SUBLANES, B_STATE), tile3(SUBLANES, inner), tile3(SUBLANES, gn),
                  pl.BlockSpec(memory_space=pltpu.SMEM)],
        out_specs=(tile3(SUBLANES, hp), tile3(hp, B_STATE)),
        compiler_params=params,
        name="even_sample_state",
    )(state_ssm3, cgt.reshape(nb, SUBLANES, B_STATE), xdw.reshape(nb, SUBLANES, inner),
      bs.reshape(nb, SUBLANES, gn), dec[:, :B_HEADS])

    out = pl.pallas_call(
        functools.partial(_even_sample_back_kernel, steps=steps, d=d, inner=inner),
        out_shape=jax.ShapeDtypeStruct(hs2.shape, F32),
        grid=(nb // bt,),
        in_specs=[_row_tiled(steps * d, bt), _row_tiled(steps * width_a, bt), _row_tiled(steps * inner, bt),
                  _row_tiled(steps * inner, bt), _row_tiled(steps * inner, bt), _row_tiled(SUBLANES * hp, bt),
                  _resident((1, inner)), _resident(p["wout"].shape)],
        out_specs=_row_tiled(steps * d, bt),
        compiler_params=params,
        name="even_sample_back",
    )(hs2, ya, ypart, eacum, z, yoff.reshape(nb, SUBLANES * hp), p["normg"], p["wout"])
    return out, v_rows, new_conv, new_ssm


def _odd_sample_front_kernel(hs_ref, g_ref, wc_ref, wq_ref, wk_ref, wv_ref, linw_ref, cscale_ref,
                             qn_ref, kn_ref, onesbd_ref, ps_ref,
                             yc_out, pool_out, q_out, knew_out, vnew_out, *, steps, d, width_c, past_len):
    bt = hs_ref.shape[0]
    xn = _rms(_stack_steps(hs_ref, steps, d), g_ref[...]).astype(BF16)
    blk = lambda a, t: a[t * bt:(t + 1) * bt]
    c_in = _dot(xn, wc_ref[...])
    n_state = max(C_WINDOWS) - 1
    ext = _steps(ps_ref[...], n_state, width_c) + [blk(c_in, t) for t in range(steps)]
    for j in range(n_state):
        pool_out[:, j * width_c:(j + 1) * width_c] = ext[len(ext) - n_state + j]
    gdim = width_c // len(C_WINDOWS)
    yc_cols = []
    for gi, win in enumerate(C_WINDOWS):
        sl = slice(gi * gdim, (gi + 1) * gdim)
        pooled = []
        for t in range(steps):
            hi = n_state + t
            lo = max(hi - win + 1, 0)
            acc = ext[lo][:, sl]
            for j in range(lo + 1, hi + 1):
                acc = acc + ext[j][:, sl]
            count = float(min(past_len + t + 1, win))
            pooled.append(acc / count - ext[hi][:, sl])
        yc_cols.append(_dot(jnp.concatenate(pooled, axis=0).astype(BF16), linw_ref[gi]))
    yc = jnp.concatenate(yc_cols, axis=1) * cscale_ref[...]
    for t in range(steps):
        yc_out[:, t * width_c:(t + 1) * width_c] = blk(yc, t)

    q = _dot(xn, wq_ref[...])
    k = _dot(xn, wk_ref[...])
    v = _dot(xn, wv_ref[...])
    ones_bd = onesbd_ref[...]
    inv_d = 1.0 / D_HEAD_DIM
    qn = q * lax.rsqrt(_head_sumsq(q, ones_bd) * inv_d + EPS) * qn_ref[...] * (D_HEAD_DIM ** -0.5)
    kn = k * lax.rsqrt(_head_sumsq(k, ones_bd) * inv_d + EPS) * kn_ref[...]
    qw = q.shape[1]
    kw = k.shape[1]
    pad = SUBLANES - steps
    q_out[:, steps * qw:] = jnp.zeros((bt, pad * qw), F32)
    knew_out[:, :pad * kw] = jnp.zeros((bt, pad * kw), F32)
    vnew_out[:, :pad * kw] = jnp.zeros((bt, pad * kw), F32)
    for t in range(steps):
        q_out[:, t * qw:(t + 1) * qw] = blk(qn, t)
        knew_out[:, (pad + t) * kw:(pad + t + 1) * kw] = blk(kn, t)
        vnew_out[:, (pad + t) * kw:(pad + t + 1) * kw] = blk(v, t)


SINK_BUCKET = REL_BUCKETS


def _odd_sample_attn_kernel(q_ref, knew_ref, vnew_ref, ck_ref, cv_ref, bucket_ref, sinks_ref, rel_ref,
                            o_ref, kout_ref, vout_ref, bias_ref, kext_ref, vext_ref, *, bb, steps):
    win = ck_ref.shape[1]
    kvw = ck_ref.shape[2]
    n_keys = kext_ref.shape[0]
    tile16 = 2 * SUBLANES

    @pl.when(pl.program_id(0) == 0)
    def _():
        bucket = bucket_ref[...]
        for hh in range(D_Q_HEADS):
            acc = jnp.full(bucket.shape, NEG, F32)
            for bkt in range(REL_BUCKETS):
                acc = jnp.where(bucket == bkt, rel_ref[bkt, hh], acc)
            acc = jnp.where(bucket == SINK_BUCKET, sinks_ref[hh], acc)
            bias_ref[hh * SUBLANES:(hh + 1) * SUBLANES, :] = acc
        kext_ref[win + tile16:, :] = jnp.zeros((n_keys - win - tile16, kvw), BF16)
        vext_ref[win + tile16:, :] = jnp.zeros((n_keys - win - tile16, kvw), BF16)

    sub = lax.broadcasted_iota(jnp.int32, (SUBLANES, kvw), 0)
    new_rows = sub >= SUBLANES - steps
    lane_kv = lax.broadcasted_iota(jnp.int32, (SUBLANES, kvw), 1) // D_HEAD_DIM
    gw = D_KV_HEADS * D_HEAD_DIM

    def shift_in(cache, new8, out_ref, bi):
        rolled = pltpu.roll(cache, win - steps, 0)
        out_ref[bi, 0:win - SUBLANES, :] = rolled[:win - SUBLANES]
        out_ref[bi, win - SUBLANES:, :] = jnp.where(new_rows, new8, rolled[win - SUBLANES:])

    def body(bi, carry):
        ck = ck_ref[bi]
        cv = cv_ref[bi]
        k8 = knew_ref[bi]
        v8 = vnew_ref[bi]
        shift_in(ck, k8, kout_ref, bi)
        shift_in(cv, v8, vout_ref, bi)
        kext_ref[0:win, :] = ck.astype(BF16)
        vext_ref[0:win, :] = cv.astype(BF16)
        kext_ref[win:win + tile16, :] = jnp.concatenate([k8, jnp.zeros_like(k8)], axis=0).astype(BF16)
        vext_ref[win:win + tile16, :] = jnp.concatenate([v8, jnp.zeros_like(v8)], axis=0).astype(BF16)
        q8 = q_ref[bi]
        pieces = []
        for kv in range(D_KV_HEADS):
            for grp in range(D_GROUP):
                qg = q8[:, grp * gw:(grp + 1) * gw]
                pieces.append(jnp.where(lane_kv == kv, qg, 0.0))
        lhs = jnp.concatenate(pieces, axis=0).astype(BF16)
        sc = _dot_nt(lhs, kext_ref[...]) + bias_ref[...]
        m = jnp.max(sc, axis=-1, keepdims=True)
        pexp = jnp.exp(sc - m)
        probs = (pexp / jnp.sum(pexp, axis=-1, keepdims=True)).astype(BF16)
        ov = _dot(probs, vext_ref[...])
        outs = []
        for grp in range(D_GROUP):
            acc = jnp.zeros((SUBLANES, gw), F32)
            for kv in range(D_KV_HEADS):
                r0 = (kv * D_GROUP + grp) * SUBLANES
                acc = acc + jnp.where(lane_kv == kv, ov[r0:r0 + SUBLANES], 0.0)
            outs.append(acc)
        o_ref[bi] = jnp.concatenate(outs, axis=1)
        return carry

    lax.fori_loop(0, bb, body, 0)


def _odd_sample_back_kernel(hs_ref, yc_ref, o_ref_in, wout_ref, out_ref, *, steps, d, width_c, qw):
    bt = hs_ref.shape[0]
    mix = jnp.concatenate(
        [jnp.concatenate([yc_ref[:, t * width_c:(t + 1) * width_c], o_ref_in[:, t * qw:(t + 1) * qw]], axis=1)
         for t in range(steps)], axis=0).astype(BF16)
    out = _dot(mix, wout_ref[...])
    for t in range(steps):
        out_ref[:, t * d:(t + 1) * d] = hs_ref[:, t * d:(t + 1) * d] + out[t * bt:(t + 1) * bt]


def _odd_sample(hs2, state_pool2, cache_k3, cache_v3, p, steps, past_len, bt, bb):
    nb, _ = hs2.shape
    d = p["wc"].shape[0]
    width_c = p["wc"].shape[1]
    qw = p["wq"].shape[1]
    kw = p["wk"].shape[1]
    win = cache_k3.shape[1]
    n_state = max(C_WINDOWS) - 1
    assert nb % bt == 0 and nb % bb == 0 and steps <= SUBLANES and win == CHUNK
    params = pltpu.CompilerParams(dimension_semantics=("arbitrary",), vmem_limit_bytes=VMEM_LIMIT)
    vm = ["linw", "cscale", "qn", "kn", "onesbd"]
    widths = dict(yc=steps * width_c, pool=n_state * width_c, q=SUBLANES * qw, knew=SUBLANES * kw,
                  vnew=SUBLANES * kw)
    yc, new_pool, q8, knew8, vnew8 = pl.pallas_call(
        functools.partial(_odd_sample_front_kernel, steps=steps, d=d, width_c=width_c, past_len=past_len),
        out_shape=tuple(jax.ShapeDtypeStruct((nb, w), F32) for w in widths.values()),
        grid=(nb // bt,),
        in_specs=[_row_tiled(steps * d, bt), _resident((1, d))]
        + [_resident(p[k].shape) for k in ("wc", "wq", "wk", "wv")]
        + [_resident(p[k].shape) for k in vm]
        + [_row_tiled(n_state * width_c, bt)],
        out_specs=tuple(_row_tiled(w, bt) for w in widths.values()),
        compiler_params=params,
        name="odd_sample_front",
    )(hs2, p["g"], p["wc"], p["wq"], p["wk"], p["wv"], *[p[k] for k in vm], state_pool2)

    n_keys = 2 * CHUNK
    pad = SUBLANES - steps
    bucket = np.full((SUBLANES, n_keys), -1, np.int32)
    for t in range(steps):
        q_pos = past_len + t
        k_pos = np.full(n_keys, -10 ** 9, np.int64)
        k_pos[:win] = past_len - win + np.arange(win)
        k_pos[win + pad:win + SUBLANES] = past_len + np.arange(steps)
        dist = q_pos - k_pos
        ok = (dist >= 0) & (dist < CHUNK) & (k_pos >= 0)
        bucket[t] = np.where(ok, _t5_bucket(np.where(ok, dist, 0)), -1)
    bucket[:, n_keys - 1] = SINK_BUCKET
    smem = pl.BlockSpec(memory_space=pltpu.SMEM)
    tile3 = lambda rows, width: pl.BlockSpec((bb, rows, width), lambda i: (i, 0, 0))
    o8, new_k, new_v = pl.pallas_call(
        functools.partial(_odd_sample_attn_kernel, bb=bb, steps=steps),
        out_shape=(jax.ShapeDtypeStruct((nb, SUBLANES, qw), F32),
                   jax.ShapeDtypeStruct(cache_k3.shape, F32),
                   jax.ShapeDtypeStruct(cache_v3.shape, F32)),
        grid=(nb // bb,),
        in_specs=[tile3(SUBLANES, qw), tile3(SUBLANES, kw), tile3(SUBLANES, kw), tile3(win, kw), tile3(win, kw),
                  _resident(bucket.shape), smem, smem],
        out_specs=(tile3(SUBLANES, qw), tile3(win, kw), tile3(win, kw)),
        scratch_shapes=[pltpu.VMEM((D_Q_HEADS * SUBLANES, n_keys), F32),
                        pltpu.VMEM((n_keys, kw), BF16),
                        pltpu.VMEM((n_keys, kw), BF16)],
        compiler_params=params,
        name="odd_sample_attn",
    )(q8.reshape(nb, SUBLANES, qw), knew8.reshape(nb, SUBLANES, kw), vnew8.reshape(nb, SUBLANES, kw),
      cache_k3, cache_v3, jnp.asarray(bucket), p["sinks"], p["rel"])

    out = pl.pallas_call(
        functools.partial(_odd_sample_back_kernel, steps=steps, d=d, width_c=width_c, qw=qw),
        out_shape=jax.ShapeDtypeStruct(hs2.shape, F32),
        grid=(nb // bt,),
        in_specs=[_row_tiled(steps * d, bt), _row_tiled(steps * width_c, bt), _row_tiled(SUBLANES * qw, bt),
                  _resident(p["wout"].shape)],
        out_specs=_row_tiled(steps * d, bt),
        compiler_params=params,
        name="odd_sample_back",
    )(hs2, yc, o8.reshape(nb, SUBLANES * qw), p["wout"])
    return out, new_pool, new_k, new_v


def _row(v):
    return v.reshape(1, -1).astype(F32)


def _pad_lanes(m, width=LANES):
    return jnp.pad(m, ((0, 0), (0, width - m.shape[1])))


def _prep_even(mix_norm, w_in, w_out, ln_g, ln_b, w_s, b_s, conv_w, conv_b, dt_bias, a_log, d_skip, norm_g):
    width_a = ln_g.shape[0]
    inner = norm_g.shape[0]
    conv_dim = conv_b.shape[0]
    o1 = 2 * width_a
    o2 = o1 + inner
    o3 = o2 + conv_dim
    return dict(
        g=_row(mix_norm),
        wa=w_in[:, :o1].astype(BF16),
        wz=w_in[:, o1:o2].astype(BF16),
        wxbc=w_in[:, o2:o3].astype(BF16),
        wdt=_pad_lanes(w_in[:, o3:]).astype(BF16),
        wout=w_out.astype(BF16),
        lng=_row(ln_g), lnb=_row(ln_b), ws=w_s, bst=b_s.T,
        convw=conv_w, convb=_row(conv_b),
        dtb=_pad_lanes(_row(dt_bias)), alog=_pad_lanes(_row(a_log)),
        dskip=_row(jnp.repeat(d_skip, B_HEAD_DIM)), normg=_row(norm_g),
    )


def _prep_odd(mix_norm, w_in, w_out, lin_w, c_scale, q_norm, k_norm, sinks, rel_table):
    d = w_in.shape[0]
    width_c = c_scale.shape[0]
    qw = D_Q_HEADS * D_HEAD_DIM
    kw = D_KV_HEADS * D_HEAD_DIM
    wq = w_in[:, width_c:width_c + qw].reshape(d, D_KV_HEADS, D_GROUP, D_HEAD_DIM)
    wq = wq.transpose(0, 2, 1, 3).reshape(d, qw)
    wo_d = w_out[width_c:].reshape(D_KV_HEADS, D_GROUP, D_HEAD_DIM, -1).transpose(1, 0, 2, 3).reshape(qw, -1)
    ones_bd = np.kron(np.eye(256 // D_HEAD_DIM), np.ones((D_HEAD_DIM, D_HEAD_DIM))).astype(np.float32)
    return dict(
        g=_row(mix_norm),
        wc=w_in[:, :width_c].astype(BF16),
        wq=wq.astype(BF16),
        wk=w_in[:, width_c + qw:width_c + qw + kw].astype(BF16),
        wv=w_in[:, width_c + qw + kw:].astype(BF16),
        wout=jnp.concatenate([w_out[:width_c], wo_d], axis=0).astype(BF16),
        linw=lin_w.astype(BF16), cscale=_row(c_scale),
        qn=_row(jnp.tile(q_norm, D_Q_HEADS)), kn=_row(jnp.tile(k_norm, D_KV_HEADS)),
        onesbd=jnp.asarray(ones_bd, BF16),
        sinks=sinks.astype(F32), rel=rel_table.astype(F32),
    )


def _prep_even_sample(w_s, b_s, steps):
    head_w = CHUNK
    w = jnp.transpose(w_s[:, :steps, :steps], (1, 2, 0)).reshape(steps * steps, A_HEADS)
    b = b_s[:, :steps].T
    return dict(wts=jnp.repeat(w, head_w, axis=1), bts=jnp.repeat(b, head_w, axis=1))


PAST_LEN = 16384
FFN_TILE = 512
MIXER_TILE = 256
SAMPLE_ROW_TILE = 32
SAMPLE_SEQ_TILE = 8


def kernel(x_prompt, x_sample, state_ssm, state_conv, state_pool, cache_k_win, cache_v_win,
           ffn1_norm, ffn1_w_gu, ffn1_w_down, mix_norm, ffn2_norm, ffn2_w_gu, ffn2_w_down,
           ev_w_in, ev_w_out, a_ln_g, a_ln_b, a_w_s, a_b_s, b_conv_w, b_conv_b, b_dt_bias, b_a_log,
           b_d_skip, b_norm_g, od_w_in, od_w_out, c_lin_w, c_scale, d_q_norm, d_k_norm, d_sinks,
           rel_bias_table):
    bp, seq, d = x_prompt.shape
    bs, steps, _ = x_sample.shape
    past_len = PAST_LEN
    hp = x_prompt
    hs = x_sample
    depth = ffn1_norm.shape[0]
    names = ("a_v_s", "ssm_p", "ssm_s", "conv_p", "conv_s", "pool_p", "pool_s", "k_p", "k_s", "v_p", "v_s")
    outs = {k: [] for k in names}

    def macaron(h_p, h_s, norm, w_gu, w_down):
        g = _row(norm)
        wgu = w_gu.astype(BF16)
        wd = w_down.astype(BF16)
        h_p = _ffn(h_p.reshape(bp * seq, d), g, wgu, wd, FFN_TILE).reshape(bp, seq, d)
        h_s = _ffn(h_s.reshape(bs * steps, d), g, wgu, wd, min(FFN_TILE, bs * steps)).reshape(bs, steps, d)
        return h_p, h_s

    for layer in range(depth):
        i = layer // 2
        hp, hs = macaron(hp, hs, ffn1_norm[layer], ffn1_w_gu[layer], ffn1_w_down[layer])
        hs2 = hs.reshape(bs, steps * d)
        if layer % 2 == 0:
            p = _prep_even(mix_norm[layer], ev_w_in[i], ev_w_out[i], a_ln_g[i], a_ln_b[i], a_w_s[i], a_b_s[i],
                           b_conv_w[i], b_conv_b[i], b_dt_bias[i], b_a_log[i], b_d_skip[i], b_norm_g[i])
            p.update(_prep_even_sample(a_w_s[i], a_b_s[i], steps))
            hp, conv_p, ssm_p = _even_prompt(hp, p, MIXER_TILE)
            hs2, v_rows, conv_s, ssm_s = _even_sample(
                hs2, state_conv[i].reshape(bs, -1), state_ssm[i].reshape(bs, B_HEADS * B_HEAD_DIM, B_STATE),
                p, steps, SAMPLE_ROW_TILE, SAMPLE_SEQ_TILE)
            outs["a_v_s"].append(v_rows.reshape(bs, steps, -1))
            outs["conv_p"].append(conv_p)
            outs["conv_s"].append(conv_s.reshape(state_conv[i].shape))
            outs["ssm_p"].append(ssm_p.reshape(bp, B_HEADS, B_HEAD_DIM, B_STATE))
            outs["ssm_s"].append(ssm_s.reshape(state_ssm[i].shape))
        else:
            p = _prep_odd(mix_norm[layer], od_w_in[i], od_w_out[i], c_lin_w[i], c_scale[i], d_q_norm[i],
                          d_k_norm[i], d_sinks[i], rel_bias_table)
            hp, pool_p, k_p, v_p = _odd_prompt(hp, p, MIXER_TILE)
            kv_shape = cache_k_win[i].shape
            hs2, pool_s, k_s, v_s = _odd_sample(
                hs2, state_pool[i].reshape(bs, -1), cache_k_win[i].reshape(bs, kv_shape[1], -1),
                cache_v_win[i].reshape(bs, kv_shape[1], -1), p, steps, past_len, SAMPLE_ROW_TILE, SAMPLE_SEQ_TILE)
            outs["pool_p"].append(pool_p)
            outs["pool_s"].append(pool_s.reshape(state_pool[i].shape))
            outs["k_p"].append(k_p.reshape(bp, CHUNK, D_KV_HEADS, D_HEAD_DIM))
            outs["v_p"].append(v_p.reshape(bp, CHUNK, D_KV_HEADS, D_HEAD_DIM))
            outs["k_s"].append(k_s.reshape(kv_shape))
            outs["v_s"].append(v_s.reshape(kv_shape))
        hs = hs2.reshape(bs, steps, d)
        hp, hs = macaron(hp, hs, ffn2_norm[layer], ffn2_w_gu[layer], ffn2_w_down[layer])
    return (hp, hs) + tuple(jnp.stack(outs[k]) for k in names)
```

```python
import functools
import math

import numpy as np
import jax
import jax.numpy as jnp
from jax import lax
from jax.experimental import pallas as pl
from jax.experimental.pallas import tpu as pltpu

F32 = jnp.float32
BF16 = jnp.bfloat16

EPS = 1e-6
NEG = -1e30

LANES = 128
SUBLANES = 8
VMEM_BYTES_V7X = 64 * 1024 * 1024
VMEM_LIMIT = VMEM_BYTES_V7X - 8 * 1024 * 1024

A_HEADS = 8
B_HEADS = 16
B_HEAD_DIM = 64
B_GROUPS = 2
B_STATE = 128
B_CONV = 4
CHUNK = 128
C_WINDOWS = (2, 4, 8, 16)
C_HALO = 16
D_Q_HEADS = 16
D_KV_HEADS = 4
D_HEAD_DIM = 64
D_GROUP = D_Q_HEADS // D_KV_HEADS
REL_BUCKETS = 32
REL_MAX_DIST = 128


def _rms(x, g):
    ms = jnp.mean(x * x, axis=-1, keepdims=True)
    return x * lax.rsqrt(ms + EPS) * g


def _sigmoid(x):
    return 1.0 / (1.0 + jnp.exp(-x))


def _silu(x):
    return x * _sigmoid(x)


def _gelu_tanh(x):
    c = math.sqrt(2.0 / math.pi)
    return x * (0.5 * (1.0 + jnp.tanh(c * (x + 0.044715 * (x * x * x)))))


def _softplus(x):
    return jnp.maximum(x, 0.0) + jnp.log1p(jnp.exp(-jnp.abs(x)))


def _split3(x):
    hi = x.astype(BF16)
    r1 = x - hi.astype(F32)
    mid = r1.astype(BF16)
    lo = (r1 - mid.astype(F32)).astype(BF16)
    return hi, mid, lo


def _split2_lanes(x):
    hi = x.astype(BF16)
    lo = (x - hi.astype(F32)).astype(BF16)
    return jnp.concatenate([hi, lo], axis=1)


def _dot(a, b):
    return jnp.dot(a, b, preferred_element_type=F32)


def _dot_nt(a, b):
    return lax.dot_general(a, b, (((1,), (1,)), ((), ())), preferred_element_type=F32)


def _dot_tn(a, b):
    return lax.dot_general(a, b, (((0,), (0,)), ((), ())), preferred_element_type=F32)


def _expand_heads(m, n_pairs):
    rows = m.shape[0]
    lane = lax.broadcasted_iota(jnp.int32, (rows, LANES), 1)
    first = lane < B_HEAD_DIM
    parts = []
    for p in range(n_pairs):
        a = jnp.broadcast_to(m[:, 2 * p:2 * p + 1], (rows, LANES))
        b = jnp.broadcast_to(m[:, 2 * p + 1:2 * p + 2], (rows, LANES))
        parts.append(jnp.where(first, a, b))
    return jnp.concatenate(parts, axis=1)


def _head_sumsq(x, ones_bd):
    xx = x * x
    hi = xx.astype(BF16)
    lo = (xx - hi.astype(F32)).astype(BF16)
    outs = []
    for c in range(x.shape[1] // 256):
        sl = slice(c * 256, (c + 1) * 256)
        outs.append(_dot(hi[:, sl], ones_bd) + _dot(lo[:, sl], ones_bd))
    return jnp.concatenate(outs, axis=1) if len(outs) > 1 else outs[0]


def _ffn_kernel(x_ref, g_ref, wgu_ref, wd_ref, o_ref, *, d_ff):
    x = x_ref[...]
    xn = _rms(x, g_ref[...]).astype(BF16)
    gu = _dot(xn, wgu_ref[...])
    gate = gu[:, :d_ff]
    up = gu[:, d_ff:]
    act = (_silu(gate) * up).astype(BF16)
    o_ref[...] = x + 0.5 * _dot(act, wd_ref[...])


def _resident(shape):
    nd = len(shape)
    return pl.BlockSpec(shape, lambda *_: (0,) * nd, pipeline_mode=pl.Buffered(1))


def _ffn(x2d, g, wgu, wd, tm):
    m, d = x2d.shape
    d_ff = wd.shape[0]
    assert m % tm == 0
    return pl.pallas_call(
        functools.partial(_ffn_kernel, d_ff=d_ff),
        out_shape=jax.ShapeDtypeStruct((m, d), F32),
        grid=(m // tm,),
        in_specs=[
            pl.BlockSpec((tm, d), lambda i: (i, 0)),
            _resident((1, d)),
            _resident(wgu.shape),
            _resident(wd.shape),
        ],
        out_specs=pl.BlockSpec((tm, d), lambda i: (i, 0)),
        compiler_params=pltpu.CompilerParams(
            dimension_semantics=("arbitrary",), vmem_limit_bytes=VMEM_LIMIT),
        name="ffn",
    )(x2d, g, wgu, wd)


def _even_prompt_kernel(h_ref, g_ref, wa_ref, wz_ref, wxbc_ref, wdt_ref, wout_ref,
                        lng_ref, lnb_ref, ws_ref, bsb_ref, convw_ref, convb_ref,
                        dtb_ref, alog_ref, dskip_ref, normg_ref, expand_ref, colsel_ref,
                        o_ref, conv_out_ref, ssm_out_ref,
                        ext_ref, st_ref, *, tile, width_a, inner):
    s = pl.program_id(1)
    n_chunks = tile // CHUNK
    halo = SUBLANES

    @pl.when(s == 0)
    def _():
        ext_ref[0:halo, :] = jnp.zeros((halo, ext_ref.shape[1]), F32)
        st_ref[...] = jnp.zeros(st_ref.shape, F32)

    x = h_ref[...]
    xn = _rms(x, g_ref[...]).astype(BF16)

    row = lax.broadcasted_iota(jnp.int32, (CHUNK, CHUNK), 0)
    col = lax.broadcasted_iota(jnp.int32, (CHUNK, CHUNK), 1)
    causal = row >= col
    lane = lax.broadcasted_iota(jnp.int32, (CHUNK, LANES), 1)
    first_half = lane < B_HEAD_DIM

    pa = _dot(xn, wa_ref[...])
    xbc_raw = _dot(xn, wxbc_ref[...])

    ga = _gelu_tanh(pa)
    u = ga[:, :width_a]
    v = ga[:, width_a:]
    mu = jnp.mean(v, axis=-1, keepdims=True)
    vc = v - mu
    var = jnp.mean(vc * vc, axis=-1, keepdims=True)
    v = vc * lax.rsqrt(var + EPS) * lng_ref[...] + lnb_ref[...]
    vb = v.astype(BF16)

    z = _dot(xn, wz_ref[...])
    dt_raw = _dot(xn, wdt_ref[...])

    ext_ref[halo:halo + tile, :] = xbc_raw
    ext = ext_ref[...]
    ext1 = pltpu.roll(ext, 1, 0)
    pair = ext * convw_ref[1:2, :] + ext1 * convw_ref[0:1, :]
    conv = (convb_ref[...] + ext * convw_ref[3:4, :] + ext1 * convw_ref[2:3, :] + pltpu.roll(pair, 2, 0))[halo:]
    tail = ext_ref[tile:tile + halo, :]
    ext_ref[0:halo, :] = tail
    conv_out_ref[...] = tail
    xbc = _silu(conv)
    gn = B_GROUPS * B_STATE
    xs = xbc[:, :inner]
    bm = xbc[:, inner:inner + gn]
    cm = xbc[:, inner + gn:]

    head_w = width_a // A_HEADS
    gate_cols = []
    for hh in range(A_HEADS):
        w = jnp.where(causal, ws_ref[hh], 0.0).astype(BF16)
        rhs = jnp.concatenate(
            [vb[c * CHUNK:(c + 1) * CHUNK, hh * head_w:(hh + 1) * head_w] for c in range(n_chunks)], axis=1)
        out = _dot(w, rhs)
        bias = bsb_ref[hh]
        gate_cols.append(jnp.concatenate(
            [out[:, c * head_w:(c + 1) * head_w] + bias for c in range(n_chunks)], axis=0))
    ya = u * jnp.concatenate(gate_cols, axis=1)
    out_a = _dot(ya.astype(BF16), wout_ref[0:width_a, :])

    dt = _softplus(dt_raw + dtb_ref[...])
    a_neg = -jnp.exp(alog_ref[...])
    da = dt * a_neg

    n_pairs = B_HEADS // 2
    heads_per_group = B_HEADS // B_GROUPS
    gw = heads_per_group * B_HEAD_DIM
    tril_ones = jnp.where(causal, 1.0, 0.0).astype(BF16)
    chunks = [slice(c * CHUNK, (c + 1) * CHUNK) for c in range(n_chunks)]
    acums = []
    for rs in chunks:
        d_hi, d_mid, d_lo = _split3(da[rs])
        acums.append(_dot(tril_ones, d_hi) + _dot(tril_ones, d_mid) + _dot(tril_ones, d_lo))
    acum = jnp.concatenate(acums, axis=0)
    decay = jnp.concatenate([jnp.exp(a[CHUNK - 1:CHUNK, :] - a) for a in acums], axis=0)
    expand = expand_ref[...]
    xd = xs * _dot(_split2_lanes(dt), expand)
    xdwb = (xs * _dot(_split2_lanes(dt * decay), expand)).astype(BF16)
    e_acum = _dot(_split2_lanes(jnp.exp(acum)), expand)
    bmb = bm.astype(BF16)
    cmb = cm.astype(BF16)
    y_rows = []
    for c, rs in enumerate(chunks):
        a_c = acums[c]
        acum_t = a_c.T
        a_cols = _dot(_split2_lanes(a_c), colsel_ref[...])
        cb = [_dot_nt(cmb[rs, g * B_STATE:(g + 1) * B_STATE], bmb[rs, g * B_STATE:(g + 1) * B_STATE])
              for g in range(B_GROUPS)]
        y_parts = []
        for p in range(n_pairs):
            g = (2 * p) // heads_per_group
            ms = []
            for hh in (2 * p, 2 * p + 1):
                seg = a_cols[:, hh * CHUNK:(hh + 1) * CHUNK] - jnp.broadcast_to(acum_t[hh:hh + 1, :], (CHUNK, CHUNK))
                lmat = jnp.where(causal, jnp.exp(seg), 0.0)
                ms.append((cb[g] * lmat).astype(BF16))
            lhs = jnp.concatenate(ms, axis=1)
            xd_p = xd[rs, p * LANES:(p + 1) * LANES]
            rhs = jnp.concatenate([jnp.where(first_half, xd_p, 0.0),
                                   jnp.where(first_half, 0.0, xd_p)], axis=0).astype(BF16)
            y_parts.append(_dot(lhs, rhs))
        y_rows.append(jnp.concatenate(y_parts, axis=1))
    for c, rs in enumerate(chunks):
        st_prev = st_ref[...]
        stb = st_prev.astype(BF16)
        y_off = jnp.concatenate(
            [_dot(cmb[rs, g * B_STATE:(g + 1) * B_STATE], stb[:, g * gw:(g + 1) * gw]) for g in range(B_GROUPS)],
            axis=1)
        st_add = jnp.concatenate(
            [_dot_tn(bmb[rs, g * B_STATE:(g + 1) * B_STATE], xdwb[rs, g * gw:(g + 1) * gw])
             for g in range(B_GROUPS)], axis=1)
        chunk_decay = e_acum[(c + 1) * CHUNK - 1:(c + 1) * CHUNK, :]
        st_ref[...] = st_prev * chunk_decay + st_add
        y_rows[c] = y_rows[c] + y_off * e_acum[rs]
    y = (jnp.concatenate(y_rows, axis=0) if n_chunks > 1 else y_rows[0]) + xs * dskip_ref[...]
    y = y * _silu(z)
    half = inner // B_GROUPS
    yn = []
    for g in range(B_GROUPS):
        yg = y[:, g * half:(g + 1) * half]
        yn.append(yg * lax.rsqrt(jnp.mean(yg * yg, axis=-1, keepdims=True) + EPS))
    yb = jnp.concatenate(yn, axis=1) * normg_ref[...]

    o_ref[...] = x + out_a + _dot(yb.astype(BF16), wout_ref[width_a:, :])

    @pl.when(s == pl.num_programs(1) - 1)
    def _():
        ssm_out_ref[...] = st_ref[...].T


def _even_prompt(h, p, tile):
    b, seq, d = h.shape
    width_a = p["wa"].shape[1] // 2
    inner = p["wz"].shape[1]
    conv_dim = p["wxbc"].shape[1]
    assert seq % tile == 0 and tile % CHUNK == 0
    small = ["lng", "lnb", "ws", "bsb", "convw", "convb", "dtb", "alog", "dskip", "normg", "expand", "colsel"]
    out, conv_tail, ssm = pl.pallas_call(
        functools.partial(_even_prompt_kernel, tile=tile, width_a=width_a, inner=inner),
        out_shape=(jax.ShapeDtypeStruct((b, seq, d), F32),
                   jax.ShapeDtypeStruct((b, SUBLANES, conv_dim), F32),
                   jax.ShapeDtypeStruct((b, inner, B_STATE), F32)),
        grid=(b, seq // tile),
        in_specs=[pl.BlockSpec((None, tile, d), lambda i, j: (i, j, 0)),
                  _resident((1, d))]
        + [_resident(p[k].shape) for k in ("wa", "wz", "wxbc", "wdt", "wout")]
        + [_resident(p[k].shape) for k in small],
        out_specs=(pl.BlockSpec((None, tile, d), lambda i, j: (i, j, 0)),
                   pl.BlockSpec((None, SUBLANES, conv_dim), lambda i, j: (i, 0, 0)),
                   pl.BlockSpec((None, inner, B_STATE), lambda i, j: (i, 0, 0))),
        scratch_shapes=[pltpu.VMEM((tile + SUBLANES, conv_dim), F32),
                        pltpu.VMEM((B_STATE, inner), F32)],
        compiler_params=pltpu.CompilerParams(
            dimension_semantics=("arbitrary", "arbitrary"), vmem_limit_bytes=VMEM_LIMIT),
        name="even_prompt",
    )(h, p["g"], p["wa"], p["wz"], p["wxbc"], p["wdt"], p["wout"], *[p[k] for k in small])
    return out, conv_tail[:, SUBLANES - (B_CONV - 1):, :], ssm


def _t5_bucket(dist):
    n = np.maximum(dist, 0)
    max_exact = REL_BUCKETS // 2
    n_safe = np.maximum(n, 1).astype(np.float32)
    scale = np.float32((REL_BUCKETS - max_exact) / math.log(REL_MAX_DIST / max_exact))
    large = max_exact + (np.log(n_safe / max_exact) * scale).astype(np.int32)
    large = np.minimum(large, REL_BUCKETS - 1)
    return np.where(n < max_exact, n, large).astype(np.int32)


def _fill_bias(bias_ref, bucket_ref, rel_ref):
    bucket = bucket_ref[...]
    has_prev = lax.broadcasted_iota(jnp.int32, bucket.shape, 1) >= CHUNK
    for hh in range(D_Q_HEADS):
        acc = jnp.full(bucket.shape, NEG, F32)
        for bkt in range(REL_BUCKETS):
            acc = jnp.where(bucket == bkt, rel_ref[bkt, hh], acc)
        bias_ref[0, hh] = acc
        bias_ref[1, hh] = jnp.where(has_prev, acc, NEG)


def _group_attention(qg, kk, vv, bias_ref, sinks_ref, grp, table):
    lq = qg.shape[0]
    lane_kv = lax.broadcasted_iota(jnp.int32, (lq, D_KV_HEADS * D_HEAD_DIM), 1) // D_HEAD_DIM
    zero = jnp.zeros_like(qg)
    lhs = jnp.concatenate([jnp.where(lane_kv == kv, qg, zero) for kv in range(D_KV_HEADS)], axis=0)
    sc = _dot_nt(lhs, kk)
    probs = []
    for kv in range(D_KV_HEADS):
        hh = kv * D_GROUP + grp
        s_h = sc[kv * lq:(kv + 1) * lq] + bias_ref[table, hh]
        sink = sinks_ref[hh]
        m = jnp.maximum(jnp.max(s_h, axis=-1, keepdims=True), sink)
        pexp = jnp.exp(s_h - m)
        denom = jnp.sum(pexp, axis=-1, keepdims=True) + jnp.exp(sink - m)
        probs.append((pexp / denom).astype(BF16))
    ov = _dot(jnp.concatenate(probs, axis=0), vv)
    out = jnp.zeros((lq, D_KV_HEADS * D_HEAD_DIM), F32)
    for kv in range(D_KV_HEADS):
        out = out + jnp.where(lane_kv == kv, ov[kv * lq:(kv + 1) * lq], 0.0)
    return out


def _odd_prompt_kernel(h_ref, g_ref, wc_ref, wq_ref, wk_ref, wv_ref, wout_ref,
                       linw_ref, cscale_ref, qn_ref, kn_ref, onesbd_ref, bucket_ref,
                       sinks_ref, rel_ref,
                       o_ref, pool_out_ref, k_out_ref, v_out_ref,
                       extc_ref, kprev_ref, vprev_ref, bias_ref, *, tile, width_c):
    b = pl.program_id(0)
    s = pl.program_id(1)
    n_blocks = tile // CHUNK

    @pl.when((b == 0) & (s == 0))
    def _():
        _fill_bias(bias_ref, bucket_ref, rel_ref)

    @pl.when(s == 0)
    def _():
        extc_ref[0:C_HALO, :] = jnp.zeros((C_HALO, width_c), F32)
        kprev_ref[...] = jnp.zeros(kprev_ref.shape, F32)
        vprev_ref[...] = jnp.zeros(vprev_ref.shape, F32)

    x = h_ref[...]
    xn = _rms(x, g_ref[...]).astype(BF16)

    c_in = _dot(xn, wc_ref[...])
    extc_ref[C_HALO:C_HALO + tile, :] = c_in
    e = extc_ref[...]
    tail = extc_ref[tile:tile + C_HALO, :]
    extc_ref[0:C_HALO, :] = tail
    pool_out_ref[...] = tail
    pos = (s * tile + lax.broadcasted_iota(jnp.int32, (tile, 1), 0) + 1).astype(F32)
    gdim = width_c // len(C_WINDOWS)
    run = e
    shift = 1
    yc = []
    for gi, win in enumerate(C_WINDOWS):
        while shift < win:
            run = run + pltpu.roll(run, shift, 0)
            shift *= 2
        cnt = jnp.minimum(pos, float(win))
        pooled = run[C_HALO:, :gdim] / cnt - c_in[:, gi * gdim:(gi + 1) * gdim]
        yc.append(_dot(pooled.astype(BF16), linw_ref[gi]))
        if gi + 1 < len(C_WINDOWS):
            run = run[:, gdim:]
    yc = jnp.concatenate(yc, axis=1) * cscale_ref[...]

    q = _dot(xn, wq_ref[...])
    k = _dot(xn, wk_ref[...])
    v = _dot(xn, wv_ref[...])
    ones_bd = onesbd_ref[...]
    inv_d = 1.0 / D_HEAD_DIM
    qn = q * lax.rsqrt(_head_sumsq(q, ones_bd) * inv_d + EPS) * qn_ref[...]
    kn = k * lax.rsqrt(_head_sumsq(k, ones_bd) * inv_d + EPS) * kn_ref[...]
    qs = (qn * (D_HEAD_DIM ** -0.5)).astype(BF16)
    kb = kn.astype(BF16)
    vb = v.astype(BF16)
    first_table = jnp.where(s == 0, 1, 0)
    gw = D_KV_HEADS * D_HEAD_DIM
    o_rows = []
    for blk in range(n_blocks):
        rs = slice(blk * CHUNK, (blk + 1) * CHUNK)
        if blk == 0:
            k_prev, v_prev = kprev_ref[...].astype(BF16), vprev_ref[...].astype(BF16)
        else:
            k_prev, v_prev = kb[(blk - 1) * CHUNK:blk * CHUNK], vb[(blk - 1) * CHUNK:blk * CHUNK]
        kk = jnp.concatenate([k_prev, kb[rs]], axis=0)
        vv = jnp.concatenate([v_prev, vb[rs]], axis=0)
        o_rows.append(jnp.concatenate(
            [_group_attention(qs[rs, grp * gw:(grp + 1) * gw], kk, vv, bias_ref, sinks_ref, grp,
                              first_table if blk == 0 else 0)
             for grp in range(D_GROUP)], axis=1))
    kprev_ref[...] = kn[tile - CHUNK:]
    vprev_ref[...] = v[tile - CHUNK:]
    yd = jnp.concatenate(o_rows, axis=0) if n_blocks > 1 else o_rows[0]

    mix = jnp.concatenate([yc, yd], axis=1).astype(BF16)
    o_ref[...] = x + _dot(mix, wout_ref[...])

    @pl.when(s == pl.num_programs(1) - 1)
    def _():
        k_out_ref[...] = kn[tile - CHUNK:]
        v_out_ref[...] = v[tile - CHUNK:]


def _odd_prompt(h, p, tile):
    b, seq, d = h.shape
    width_c = p["wc"].shape[1]
    kvw = p["wk"].shape[1]
    assert seq % tile == 0 and tile % CHUNK == 0
    r = np.arange(CHUNK) + CHUNK
    c = np.arange(2 * CHUNK)
    dist = r[:, None] - c[None, :]
    bucket = np.where((dist >= 0) & (dist < CHUNK), _t5_bucket(dist), -1).astype(np.int32)
    vm = ["linw", "cscale", "qn", "kn", "onesbd"]
    smem = pl.BlockSpec(memory_space=pltpu.SMEM)
    out, pool_tail, k_win, v_win = pl.pallas_call(
        functools.partial(_odd_prompt_kernel, tile=tile, width_c=width_c),
        out_shape=(jax.ShapeDtypeStruct((b, seq, d), F32),
                   jax.ShapeDtypeStruct((b, C_HALO, width_c), F32),
                   jax.ShapeDtypeStruct((b, CHUNK, kvw), F32),
                   jax.ShapeDtypeStruct((b, CHUNK, kvw), F32)),
        grid=(b, seq // tile),
        in_specs=[pl.BlockSpec((None, tile, d), lambda i, j: (i, j, 0)),
                  _resident((1, d))]
        + [_resident(p[k].shape) for k in ("wc", "wq", "wk", "wv", "wout")]
        + [_resident(p[k].shape) for k in vm]
        + [_resident(bucket.shape), smem, smem],
        out_specs=(pl.BlockSpec((None, tile, d), lambda i, j: (i, j, 0)),
                   pl.BlockSpec((None, C_HALO, width_c), lambda i, j: (i, 0, 0)),
                   pl.BlockSpec((None, CHUNK, kvw), lambda i, j: (i, 0, 0)),
                   pl.BlockSpec((None, CHUNK, kvw), lambda i, j: (i, 0, 0))),
        scratch_shapes=[pltpu.VMEM((tile + C_HALO, width_c), F32),
                        pltpu.VMEM((CHUNK, kvw), F32),
                        pltpu.VMEM((CHUNK, kvw), F32),
                        pltpu.VMEM((2, D_Q_HEADS, CHUNK, 2 * CHUNK), F32)],
        compiler_params=pltpu.CompilerParams(
            dimension_semantics=("arbitrary", "arbitrary"), vmem_limit_bytes=VMEM_LIMIT),
        name="odd_prompt",
    )(h, p["g"], p["wc"], p["wq"], p["wk"], p["wv"], p["wout"], *[p[k] for k in vm],
      jnp.asarray(bucket), p["sinks"], p["rel"])
    return out, pool_tail[:, C_HALO - (max(C_WINDOWS) - 1):, :], k_win, v_win


def _steps(x, n, width):
    return [x[:, t * width:(t + 1) * width] for t in range(n)]


def _stack_steps(ref, n, width):
    x = ref[...]
    return jnp.concatenate(_steps(x, n, width), axis=0)


def _even_sample_front_kernel(hs_ref, g_ref, wa_ref, wz_ref, wxbc_ref, wdt_ref,
                              lng_ref, lnb_ref, wts_ref, bts_ref, convw_ref, convb_ref,
                              dtb_ref, alog_ref, dskip_ref, cs_ref,
                              v_out, ya_out, ypart_out, eacum_out, z_out, conv_out,
                              cgt_out, xdw_out, bs_out, dec_out, *, steps, d, width_a, inner, conv_dim):
    bt = hs_ref.shape[0]
    xn = _rms(_stack_steps(hs_ref, steps, d), g_ref[...]).astype(BF16)
    blk = lambda a, t: a[t * bt:(t + 1) * bt]

    ga = _gelu_tanh(_dot(xn, wa_ref[...]))
    u = ga[:, :width_a]
    v = ga[:, width_a:]
    mu = jnp.mean(v, axis=-1, keepdims=True)
    vc = v - mu
    var = jnp.mean(vc * vc, axis=-1, keepdims=True)
    v = vc * lax.rsqrt(var + EPS) * lng_ref[...] + lnb_ref[...]
    for t in range(steps):
        v_out[:, t * width_a:(t + 1) * width_a] = blk(v, t)
        gate = bts_ref[t:t + 1, :]
        for s in range(t + 1):
            gate = gate + wts_ref[t * steps + s:t * steps + s + 1, :] * blk(v, s)
        ya_out[:, t * width_a:(t + 1) * width_a] = blk(u, t) * gate

    z = _dot(xn, wz_ref[...])
    for t in range(steps):
        z_out[:, t * inner:(t + 1) * inner] = blk(z, t)
    raw = _dot(xn, wxbc_ref[...])
    dt = _softplus(_dot(xn, wdt_ref[...]) + dtb_ref[...])
    ext = _steps(cs_ref[...], B_CONV - 1, conv_dim) + [blk(raw, t) for t in range(steps)]
    for k in range(B_CONV - 1):
        conv_out[:, k * conv_dim:(k + 1) * conv_dim] = ext[len(ext) - (B_CONV - 1) + k]
    gn = B_GROUPS * B_STATE
    n_pairs = B_HEADS // 2
    a_neg = -jnp.exp(alog_ref[...])
    xs, bm, cm, dts, acum = [], [], [], [], []
    for t in range(steps):
        conv = convb_ref[...]
        for tap in range(B_CONV):
            conv = conv + ext[t + tap] * convw_ref[tap:tap + 1, :]
        xbc = _silu(conv)
        xs.append(xbc[:, :inner])
        bm.append(xbc[:, inner:inner + gn])
        cm.append(xbc[:, inner + gn:])
        dts.append(blk(dt, t))
        da = dts[t] * a_neg
        acum.append(da if t == 0 else acum[t - 1] + da)
    lane = lax.broadcasted_iota(jnp.int32, (bt, LANES), 1)
    group0 = lane < (B_HEADS // B_GROUPS)
    dec_out[...] = jnp.exp(acum[steps - 1])
    pad_rows = SUBLANES - steps
    xdw_out[:, steps * inner:] = jnp.zeros((bt, pad_rows * inner), F32)
    bs_out[:, steps * gn:] = jnp.zeros((bt, pad_rows * gn), F32)
    xd = []
    for t in range(steps):
        xd.append(xs[t] * _expand_heads(dts[t], n_pairs))
        eacum_out[:, t * inner:(t + 1) * inner] = _expand_heads(jnp.exp(acum[t]), n_pairs)
        xdw_out[:, t * inner:(t + 1) * inner] = xs[t] * _expand_heads(
            dts[t] * jnp.exp(acum[steps - 1] - acum[t]), n_pairs)
        bs_out[:, t * gn:(t + 1) * gn] = bm[t]
        for g in range(B_GROUPS):
            r = g * steps + t
            cgt_out[:, r * B_STATE:(r + 1) * B_STATE] = cm[t][:, g * B_STATE:(g + 1) * B_STATE]
    for t in range(steps):
        y = xs[t] * dskip_ref[...]
        for s in range(t + 1):
            cb = [jnp.sum(cm[t][:, g * B_STATE:(g + 1) * B_STATE] * bm[s][:, g * B_STATE:(g + 1) * B_STATE],
                          axis=-1, keepdims=True) for g in range(B_GROUPS)]
            coef = jnp.where(group0, cb[0], cb[1]) * jnp.exp(acum[t] - acum[s])
            y = y + _expand_heads(coef, n_pairs) * xd[s]
        ypart_out[:, t * inner:(t + 1) * inner] = y


def _even_sample_state_kernel(s0_ref, cgt_ref, xdw_ref, bs_ref, dec_ref, yoff_ref, snew_ref, *, bb):
    step = pl.program_id(0)
    gw = (B_HEADS // B_GROUPS) * B_HEAD_DIM

    def body(bi, carry):
        s0 = s0_ref[bi]
        c8 = cgt_ref[bi]
        c16 = jnp.concatenate([c8, jnp.zeros_like(c8)], axis=0).astype(BF16)
        yoff_ref[bi] = _dot_nt(c16, s0.astype(BF16))[:SUBLANES]
        x8 = xdw_ref[bi]
        b8 = bs_ref[bi]
        x16 = jnp.concatenate([x8, jnp.zeros_like(x8)], axis=0).astype(BF16)
        b16 = jnp.concatenate([b8, jnp.zeros_like(b8)], axis=0).astype(BF16)
        for g in range(B_GROUPS):
            add = _dot_tn(x16[:, g * gw:(g + 1) * gw], b16[:, g * B_STATE:(g + 1) * B_STATE])
            for hl in range(B_HEADS // B_GROUPS):
                hh = g * (B_HEADS // B_GROUPS) + hl
                rs = slice(hh * B_HEAD_DIM, (hh + 1) * B_HEAD_DIM)
                snew_ref[bi, rs, :] = s0[rs] * dec_ref[step * bb + bi, hh] + \
                    add[hl * B_HEAD_DIM:(hl + 1) * B_HEAD_DIM]
        return carry

    lax.fori_loop(0, bb, body, 0)


def _even_sample_back_kernel(hs_ref, ya_ref, ypart_ref, eacum_ref, z_ref, yoff_ref, normg_ref, wout_ref,
                             o_ref, *, steps, d, inner):
    bt = hs_ref.shape[0]
    half = inner // B_GROUPS
    mixes = []
    for t in range(steps):
        sl = slice(t * inner, (t + 1) * inner)
        yoff = jnp.concatenate(
            [yoff_ref[:, (g * steps + t) * inner + g * half:(g * steps + t) * inner + (g + 1) * half]
             for g in range(B_GROUPS)], axis=1)
        y = (ypart_ref[:, sl] + yoff * eacum_ref[:, sl]) * _silu(z_ref[:, sl])
        yn = []
        for g in range(B_GROUPS):
            yg = y[:, g * half:(g + 1) * half]
            yn.append(yg * lax.rsqrt(jnp.mean(yg * yg, axis=-1, keepdims=True) + EPS))
        yb = jnp.concatenate(yn, axis=1) * normg_ref[...]
        mixes.append(jnp.concatenate([ya_ref[:, t * d:(t + 1) * d], yb], axis=1))
    out = _dot(jnp.concatenate(mixes, axis=0).astype(BF16), wout_ref[...])
    for t in range(steps):
        o_ref[:, t * d:(t + 1) * d] = hs_ref[:, t * d:(t + 1) * d] + out[t * bt:(t + 1) * bt]


def _row_tiled(width, bt):
    return pl.BlockSpec((bt, width), lambda i: (i, 0))


def _even_sample(hs2, state_conv2, state_ssm3, p, steps, bt, bb):
    nb, _ = hs2.shape
    d = p["wa"].shape[0]
    width_a = p["wa"].shape[1] // 2
    inner = p["wz"].shape[1]
    conv_dim = p["wxbc"].shape[1]
    gn = B_GROUPS * B_STATE
    assert nb % bt == 0 and nb % bb == 0 and steps <= SUBLANES
    params = pltpu.CompilerParams(dimension_semantics=("arbitrary",), vmem_limit_bytes=VMEM_LIMIT)
    small = ["lng", "lnb", "wts", "bts", "convw", "convb", "dtb", "alog", "dskip"]
    widths = dict(v=steps * width_a, ya=steps * width_a, ypart=steps * inner, eacum=steps * inner,
                  z=steps * inner, conv=(B_CONV - 1) * conv_dim, cgt=SUBLANES * B_STATE,
                  xdw=SUBLANES * inner, bs=SUBLANES * gn, dec=LANES)
    front = pl.pallas_call(
        functools.partial(_even_sample_front_kernel, steps=steps, d=d, width_a=width_a, inner=inner,
                          conv_dim=conv_dim),
        out_shape=tuple(jax.ShapeDtypeStruct((nb, w), F32) for w in widths.values()),
        grid=(nb // bt,),
        in_specs=[_row_tiled(steps * d, bt), _resident((1, d))]
        + [_resident(p[k].shape) for k in ("wa", "wz", "wxbc", "wdt")]
        + [_resident(p[k].shape) for k in small]
        + [_row_tiled((B_CONV - 1) * conv_dim, bt)],
        out_specs=tuple(_row_tiled(w, bt) for w in widths.values()),
        compiler_params=params,
        name="even_sample_front",
    )(hs2, p["g"], p["wa"], p["wz"], p["wxbc"], p["wdt"], *[p[k] for k in small], state_conv2)
    v_rows, ya, ypart, eacum, z, new_conv, cgt, xdw, bs, dec = front

    hp = state_ssm3.shape[1]
    tile3 = lambda rows, width: pl.BlockSpec((bb, rows, width), lambda i: (i, 0, 0))
    yoff, new_ssm = pl.pallas_call(
        functools.partial(_even_sample_state_kernel, bb=bb),
        out_shape=(jax.ShapeDtypeStruct((nb, SUBLANES, hp), F32),
                   jax.ShapeDtypeStruct(state_ssm3.shape, F32)),
        grid=(nb // bb,),
        in_specs=[tile3(hp, B_STATE), tile3(SUBLANES, B_STATE), tile3(SUBLANES, inner), tile3(SUBLANES, gn),
                  pl.BlockSpec(memory_space=pltpu.SMEM)],
        out_specs=(tile3(SUBLANES, hp), tile3(hp, B_STATE)),
        compiler_params=params,
        name="even_sample_state",
    )(state_ssm3, cgt.reshape(nb, SUBLANES, B_STATE), xdw.reshape(nb, SUBLANES, inner),
      bs.reshape(nb, SUBLANES, gn), dec[:, :B_HEADS])

    out = pl.pallas_call(
        functools.partial(_even_sample_back_kernel, steps=steps, d=d, inner=inner),
        out_shape=jax.ShapeDtypeStruct(hs2.shape, F32),
        grid=(nb // bt,),
        in_specs=[_row_tiled(steps * d, bt), _row_tiled(steps * width_a, bt), _row_tiled(steps * inner, bt),
                  _row_tiled(steps * inner, bt), _row_tiled(steps * inner, bt), _row_tiled(SUBLANES * hp, bt),
                  _resident((1, inner)), _resident(p["wout"].shape)],
        out_specs=_row_tiled(steps * d, bt),
        compiler_params=params,
        name="even_sample_back",
    )(hs2, ya, ypart, eacum, z, yoff.reshape(nb, SUBLANES * hp), p["normg"], p["wout"])
    return out, v_rows, new_conv, new_ssm


def _odd_sample_front_kernel(hs_ref, g_ref, wc_ref, wq_ref, wk_ref, wv_ref, linw_ref, cscale_ref,
                             qn_ref, kn_ref, onesbd_ref, ps_ref,
                             yc_out, pool_out, q_out, knew_out, vnew_out, *, steps, d, width_c, past_len):
    bt = hs_ref.shape[0]
    xn = _rms(_stack_steps(hs_ref, steps, d), g_ref[...]).astype(BF16)
    blk = lambda a, t: a[t * bt:(t + 1) * bt]
    c_in = _dot(xn, wc_ref[...])
    n_state = max(C_WINDOWS) - 1
    ext = _steps(ps_ref[...], n_state, width_c) + [blk(c_in, t) for t in range(steps)]
    for j in range(n_state):
        pool_out[:, j * width_c:(j + 1) * width_c] = ext[len(ext) - n_state + j]
    gdim = width_c // len(C_WINDOWS)
    yc_cols = []
    for gi, win in enumerate(C_WINDOWS):
        sl = slice(gi * gdim, (gi + 1) * gdim)
        pooled = []
        for t in range(steps):
            hi = n_state + t
            lo = max(hi - win + 1, 0)
            acc = ext[lo][:, sl]
            for j in range(lo + 1, hi + 1):
                acc = acc + ext[j][:, sl]
            count = float(min(past_len + t + 1, win))
            pooled.append(acc / count - ext[hi][:, sl])
        yc_cols.append(_dot(jnp.concatenate(pooled, axis=0).astype(BF16), linw_ref[gi]))
    yc = jnp.concatenate(yc_cols, axis=1) * cscale_ref[...]
    for t in range(steps):
        yc_out[:, t * width_c:(t + 1) * width_c] = blk(yc, t)

    q = _dot(xn, wq_ref[...])
    k = _dot(xn, wk_ref[...])
    v = _dot(xn, wv_ref[...])
    ones_bd = onesbd_ref[...]
    inv_d = 1.0 / D_HEAD_DIM
    qn = q * lax.rsqrt(_head_sumsq(q, ones_bd) * inv_d + EPS) * qn_ref[...] * (D_HEAD_DIM ** -0.5)
    kn = k * lax.rsqrt(_head_sumsq(k, ones_bd) * inv_d + EPS) * kn_ref[...]
    qw = q.shape[1]
    kw = k.shape[1]
    pad = SUBLANES - steps
    q_out[:, steps * qw:] = jnp.zeros((bt, pad * qw), F32)
    knew_out[:, :pad * kw] = jnp.zeros((bt, pad * kw), F32)
    vnew_out[:, :pad * kw] = jnp.zeros((bt, pad * kw), F32)
    for t in range(steps):
        q_out[:, t * qw:(t + 1) * qw] = blk(qn, t)
        knew_out[:, (pad + t) * kw:(pad + t + 1) * kw] = blk(kn, t)
        vnew_out[:, (pad + t) * kw:(pad + t + 1) * kw] = blk(v, t)


SINK_BUCKET = REL_BUCKETS


def _odd_sample_attn_kernel(q_ref, knew_ref, vnew_ref, ck_ref, cv_ref, bucket_ref, sinks_ref, rel_ref,
                            o_ref, kout_ref, vout_ref, bias_ref, kext_ref, vext_ref, *, bb, steps):
    win = ck_ref.shape[1]
    kvw = ck_ref.shape[2]
    n_keys = kext_ref.shape[0]
    tile16 = 2 * SUBLANES

    @pl.when(pl.program_id(0) == 0)
    def _():
        bucket = bucket_ref[...]
        for hh in range(D_Q_HEADS):
            acc = jnp.full(bucket.shape, NEG, F32)
            for bkt in range(REL_BUCKETS):
                acc = jnp.where(bucket == bkt, rel_ref[bkt, hh], acc)
            acc = jnp.where(bucket == SINK_BUCKET, sinks_ref[hh], acc)
            bias_ref[hh * SUBLANES:(hh + 1) * SUBLANES, :] = acc
        kext_ref[win + tile16:, :] = jnp.zeros((n_keys - win - tile16, kvw), BF16)
        vext_ref[win + tile16:, :] = jnp.zeros((n_keys - win - tile16, kvw), BF16)

    sub = lax.broadcasted_iota(jnp.int32, (SUBLANES, kvw), 0)
    new_rows = sub >= SUBLANES - steps
    lane_kv = lax.broadcasted_iota(jnp.int32, (SUBLANES, kvw), 1) // D_HEAD_DIM
    gw = D_KV_HEADS * D_HEAD_DIM

    def shift_in(cache, new8, out_ref, bi):
        rolled = pltpu.roll(cache, win - steps, 0)
        out_ref[bi, 0:win - SUBLANES, :] = rolled[:win - SUBLANES]
        out_ref[bi, win - SUBLANES:, :] = jnp.where(new_rows, new8, rolled[win - SUBLANES:])

    def body(bi, carry):
        ck = ck_ref[bi]
        cv = cv_ref[bi]
        k8 = knew_ref[bi]
        v8 = vnew_ref[bi]
        shift_in(ck, k8, kout_ref, bi)
        shift_in(cv, v8, vout_ref, bi)
        kext_ref[0:win, :] = ck.astype(BF16)
        vext_ref[0:win, :] = cv.astype(BF16)
        kext_ref[win:win + tile16, :] = jnp.concatenate([k8, jnp.zeros_like(k8)], axis=0).astype(BF16)
        vext_ref[win:win + tile16, :] = jnp.concatenate([v8, jnp.zeros_like(v8)], axis=0).astype(BF16)
        q8 = q_ref[bi]
        pieces = []
        for kv in range(D_KV_HEADS):
            for grp in range(D_GROUP):
                qg = q8[:, grp * gw:(grp + 1) * gw]
                pieces.append(jnp.where(lane_kv == kv, qg, 0.0))
        lhs = jnp.concatenate(pieces, axis=0).astype(BF16)
        sc = _dot_nt(lhs, kext_ref[...]) + bias_ref[...]
        m = jnp.max(sc, axis=-1, keepdims=True)
        pexp = jnp.exp(sc - m)
        probs = (pexp / jnp.sum(pexp, axis=-1, keepdims=True)).astype(BF16)
        ov = _dot(probs, vext_ref[...])
        outs = []
        for grp in range(D_GROUP):
            acc = jnp.zeros((SUBLANES, gw), F32)
            for kv in range(D_KV_HEADS):
                r0 = (kv * D_GROUP + grp) * SUBLANES
                acc = acc + jnp.where(lane_kv == kv, ov[r0:r0 + SUBLANES], 0.0)
            outs.append(acc)
        o_ref[bi] = jnp.concatenate(outs, axis=1)
        return carry

    lax.fori_loop(0, bb, body, 0)


def _odd_sample_back_kernel(hs_ref, yc_ref, o_ref_in, wout_ref, out_ref, *, steps, d, width_c, qw):
    bt = hs_ref.shape[0]
    mix = jnp.concatenate(
        [jnp.concatenate([yc_ref[:, t * width_c:(t + 1) * width_c], o_ref_in[:, t * qw:(t + 1) * qw]], axis=1)
         for t in range(steps)], axis=0).astype(BF16)
    out = _dot(mix, wout_ref[...])
    for t in range(steps):
        out_ref[:, t * d:(t + 1) * d] = hs_ref[:, t * d:(t + 1) * d] + out[t * bt:(t + 1) * bt]


def _odd_sample(hs2, state_pool2, cache_k3, cache_v3, p, steps, past_len, bt, bb):
    nb, _ = hs2.shape
    d = p["wc"].shape[0]
    width_c = p["wc"].shape[1]
    qw = p["wq"].shape[1]
    kw = p["wk"].shape[1]
    win = cache_k3.shape[1]
    n_state = max(C_WINDOWS) - 1
    assert nb % bt == 0 and nb % bb == 0 and steps <= SUBLANES and win == CHUNK
    params = pltpu.CompilerParams(dimension_semantics=("arbitrary",), vmem_limit_bytes=VMEM_LIMIT)
    vm = ["linw", "cscale", "qn", "kn", "onesbd"]
    widths = dict(yc=steps * width_c, pool=n_state * width_c, q=SUBLANES * qw, knew=SUBLANES * kw,
                  vnew=SUBLANES * kw)
    yc, new_pool, q8, knew8, vnew8 = pl.pallas_call(
        functools.partial(_odd_sample_front_kernel, steps=steps, d=d, width_c=width_c, past_len=past_len),
        out_shape=tuple(jax.ShapeDtypeStruct((nb, w), F32) for w in widths.values()),
        grid=(nb // bt,),
        in_specs=[_row_tiled(steps * d, bt), _resident((1, d))]
        + [_resident(p[k].shape) for k in ("wc", "wq", "wk", "wv")]
        + [_resident(p[k].shape) for k in vm]
        + [_row_tiled(n_state * width_c, bt)],
        out_specs=tuple(_row_tiled(w, bt) for w in widths.values()),
        compiler_params=params,
        name="odd_sample_front",
    )(hs2, p["g"], p["wc"], p["wq"], p["wk"], p["wv"], *[p[k] for k in vm], state_pool2)

    n_keys = 2 * CHUNK
    pad = SUBLANES - steps
    bucket = np.full((SUBLANES, n_keys), -1, np.int32)
    for t in range(steps):
        q_pos = past_len + t
        k_pos = np.full(n_keys, -10 ** 9, np.int64)
        k_pos[:win] = past_len - win + np.arange(win)
        k_pos[win + pad:win + SUBLANES] = past_len + np.arange(steps)
        dist = q_pos - k_pos
        ok = (dist >= 0) & (dist < CHUNK) & (k_pos >= 0)
        bucket[t] = np.where(ok, _t5_bucket(np.where(ok, dist, 0)), -1)
    bucket[:, n_keys - 1] = SINK_BUCKET
    smem = pl.BlockSpec(memory_space=pltpu.SMEM)
    tile3 = lambda rows, width: pl.BlockSpec((bb, rows, width), lambda i: (i, 0, 0))
    o8, new_k, new_v = pl.pallas_call(
        functools.partial(_odd_sample_attn_kernel, bb=bb, steps=steps),
        out_shape=(jax.ShapeDtypeStruct((nb, SUBLANES, qw), F32),
                   jax.ShapeDtypeStruct(cache_k3.shape, F32),
                   jax.ShapeDtypeStruct(cache_v3.shape, F32)),
        grid=(nb // bb,),
        in_specs=[tile3(SUBLANES, qw), tile3(SUBLANES, kw), tile3(SUBLANES, kw), tile3(win, kw), tile3(win, kw),
                  _resident(bucket.shape), smem, smem],
        out_specs=(tile3(SUBLANES, qw), tile3(win, kw), tile3(win, kw)),
        scratch_shapes=[pltpu.VMEM((D_Q_HEADS * SUBLANES, n_keys), F32),
                        pltpu.VMEM((n_keys, kw), BF16),
                        pltpu.VMEM((n_keys, kw), BF16)],
        compiler_params=params,
        name="odd_sample_attn",
    )(q8.reshape(nb, SUBLANES, qw), knew8.reshape(nb, SUBLANES, kw), vnew8.reshape(nb, SUBLANES, kw),
      cache_k3, cache_v3, jnp.asarray(bucket), p["sinks"], p["rel"])

    out = pl.pallas_call(
        functools.partial(_odd_sample_back_kernel, steps=steps, d=d, width_c=width_c, qw=qw),
        out_shape=jax.ShapeDtypeStruct(hs2.shape, F32),
        grid=(nb // bt,),
        in_specs=[_row_tiled(steps * d, bt), _row_tiled(steps * width_c, bt), _row_tiled(SUBLANES * qw, bt),
                  _resident(p["wout"].shape)],
        out_specs=_row_tiled(steps * d, bt),
        compiler_params=params,
        name="odd_sample_back",
    )(hs2, yc, o8.reshape(nb, SUBLANES * qw), p["wout"])
    return out, new_pool, new_k, new_v


def _row(v):
    return v.reshape(1, -1).astype(F32)


def _pad_lanes(m, width=LANES):
    return jnp.pad(m, ((0, 0), (0, width - m.shape[1])))


def _head_expand_matrix():
    e = np.zeros((LANES, B_HEADS * B_HEAD_DIM), np.float32)
    for hh in range(B_HEADS):
        e[hh, hh * B_HEAD_DIM:(hh + 1) * B_HEAD_DIM] = 1.0
    return np.concatenate([e, e], axis=0)


def _column_select_matrix():
    e = np.zeros((LANES, B_HEADS * CHUNK), np.float32)
    for hh in range(B_HEADS):
        e[hh, hh * CHUNK:(hh + 1) * CHUNK] = 1.0
    return np.concatenate([e, e], axis=0)


def _prep_even(mix_norm, w_in, w_out, ln_g, ln_b, w_s, b_s, conv_w, conv_b, dt_bias, a_log, d_skip, norm_g):
    width_a = ln_g.shape[0]
    inner = norm_g.shape[0]
    conv_dim = conv_b.shape[0]
    o1 = 2 * width_a
    o2 = o1 + inner
    o3 = o2 + conv_dim
    return dict(
        g=_row(mix_norm),
        wa=w_in[:, :o1].astype(BF16),
        wz=w_in[:, o1:o2].astype(BF16),
        wxbc=w_in[:, o2:o3].astype(BF16),
        wdt=_pad_lanes(w_in[:, o3:]).astype(BF16),
        wout=w_out.astype(BF16),
        lng=_row(ln_g), lnb=_row(ln_b), ws=w_s,
        bsb=jnp.broadcast_to(b_s[:, :, None], b_s.shape + (width_a // A_HEADS,)),
        convw=conv_w, convb=_row(conv_b),
        dtb=_pad_lanes(_row(dt_bias)), alog=_pad_lanes(_row(a_log)),
        dskip=_row(jnp.repeat(d_skip, B_HEAD_DIM)), normg=_row(norm_g),
        expand=jnp.asarray(_head_expand_matrix(), BF16), colsel=jnp.asarray(_column_select_matrix(), BF16),
    )


def _prep_odd(mix_norm, w_in, w_out, lin_w, c_scale, q_norm, k_norm, sinks, rel_table):
    d = w_in.shape[0]
    width_c = c_scale.shape[0]
    qw = D_Q_HEADS * D_HEAD_DIM
    kw = D_KV_HEADS * D_HEAD_DIM
    wq = w_in[:, width_c:width_c + qw].reshape(d, D_KV_HEADS, D_GROUP, D_HEAD_DIM)
    wq = wq.transpose(0, 2, 1, 3).reshape(d, qw)
    wo_d = w_out[width_c:].reshape(D_KV_HEADS, D_GROUP, D_HEAD_DIM, -1).transpose(1, 0, 2, 3).reshape(qw, -1)
    ones_bd = np.kron(np.eye(256 // D_HEAD_DIM), np.ones((D_HEAD_DIM, D_HEAD_DIM))).astype(np.float32)
    return dict(
        g=_row(mix_norm),
        wc=w_in[:, :width_c].astype(BF16),
        wq=wq.astype(BF16),
        wk=w_in[:, width_c + qw:width_c + qw + kw].astype(BF16),
        wv=w_in[:, width_c + qw + kw:].astype(BF16),
        wout=jnp.concatenate([w_out[:width_c], wo_d], axis=0).astype(BF16),
        linw=lin_w.astype(BF16), cscale=_row(c_scale),
        qn=_row(jnp.tile(q_norm, D_Q_HEADS)), kn=_row(jnp.tile(k_norm, D_KV_HEADS)),
        onesbd=jnp.asarray(ones_bd, BF16),
        sinks=sinks.astype(F32), rel=rel_table.astype(F32),
    )


def _prep_even_sample(w_s, b_s, steps):
    head_w = CHUNK
    w = jnp.transpose(w_s[:, :steps, :steps], (1, 2, 0)).reshape(steps * steps, A_HEADS)
    b = b_s[:, :steps].T
    return dict(wts=jnp.repeat(w, head_w, axis=1), bts=jnp.repeat(b, head_w, axis=1))


PAST_LEN = 16384
FFN_TILE = 512
MIXER_TILE = 256
SAMPLE_ROW_TILE = 32
SAMPLE_SEQ_TILE = 8


def kernel(x_prompt, x_sample, state_ssm, state_conv, state_pool, cache_k_win, cache_v_win,
           ffn1_norm, ffn1_w_gu, ffn1_w_down, mix_norm, ffn2_norm, ffn2_w_gu, ffn2_w_down,
           ev_w_in, ev_w_out, a_ln_g, a_ln_b, a_w_s, a_b_s, b_conv_w, b_conv_b, b_dt_bias, b_a_log,
           b_d_skip, b_norm_g, od_w_in, od_w_out, c_lin_w, c_scale, d_q_norm, d_k_norm, d_sinks,
           rel_bias_table):
    bp, seq, d = x_prompt.shape
    bs, steps, _ = x_sample.shape
    past_len = PAST_LEN
    hp = x_prompt
    hs = x_sample
    depth = ffn1_norm.shape[0]
    names = ("a_v_s", "ssm_p", "ssm_s", "conv_p", "conv_s", "pool_p", "pool_s", "k_p", "k_s", "v_p", "v_s")
    outs = {k: [] for k in names}

    def macaron(h_p, h_s, norm, w_gu, w_down):
        g = _row(norm)
        wgu = w_gu.astype(BF16)
        wd = w_down.astype(BF16)
        h_p = _ffn(h_p.reshape(bp * seq, d), g, wgu, wd, FFN_TILE).reshape(bp, seq, d)
        h_s = _ffn(h_s.reshape(bs * steps, d), g, wgu, wd, min(FFN_TILE, bs * steps)).reshape(bs, steps, d)
        return h_p, h_s

    for layer in range(depth):
        i = layer // 2
        hp, hs = macaron(hp, hs, ffn1_norm[layer], ffn1_w_gu[layer], ffn1_w_down[layer])
        hs2 = hs.reshape(bs, steps * d)
        if layer % 2 == 0:
            p = _prep_even(mix_norm[layer], ev_w_in[i], ev_w_out[i], a_ln_g[i], a_ln_b[i], a_w_s[i], a_b_s[i],
                           b_conv_w[i], b_conv_b[i], b_dt_bias[i], b_a_log[i], b_d_skip[i], b_norm_g[i])
            p.update(_prep_even_sample(a_w_s[i], a_b_s[i], steps))
            hp, conv_p, ssm_p = _even_prompt(hp, p, MIXER_TILE)
            hs2, v_rows, conv_s, ssm_s = _even_sample(
                hs2, state_conv[i].reshape(bs, -1), state_ssm[i].reshape(bs, B_HEADS * B_HEAD_DIM, B_STATE),
                p, steps, SAMPLE_ROW_TILE, SAMPLE_SEQ_TILE)
            outs["a_v_s"].append(v_rows.reshape(bs, steps, -1))
            outs["conv_p"].append(conv_p)
            outs["conv_s"].append(conv_s.reshape(state_conv[i].shape))
            outs["ssm_p"].append(ssm_p.reshape(bp, B_HEADS, B_HEAD_DIM, B_STATE))
            outs["ssm_s"].append(ssm_s.reshape(state_ssm[i].shape))
        else:
            p = _prep_odd(mix_norm[layer], od_w_in[i], od_w_out[i], c_lin_w[i], c_scale[i], d_q_norm[i],
                          d_k_norm[i], d_sinks[i], rel_bias_table)
            hp, pool_p, k_p, v_p = _odd_prompt(hp, p, MIXER_TILE)
            kv_shape = cache_k_win[i].shape
            hs2, pool_s, k_s, v_s = _odd_sample(
                hs2, state_pool[i].reshape(bs, -1), cache_k_win[i].reshape(bs, kv_shape[1], -1),
                cache_v_win[i].reshape(bs, kv_shape[1], -1), p, steps, past_len, SAMPLE_ROW_TILE, SAMPLE_SEQ_TILE)
            outs["pool_p"].append(pool_p)
            outs["pool_s"].append(pool_s.reshape(state_pool[i].shape))
            outs["k_p"].append(k_p.reshape(bp, CHUNK, D_KV_HEADS, D_HEAD_DIM))
            outs["v_p"].append(v_p.reshape(bp, CHUNK, D_KV_HEADS, D_HEAD_DIM))
            outs["k_s"].append(k_s.reshape(kv_shape))
            outs["v_s"].append(v_s.reshape(kv_shape))
        hs = hs2.reshape(bs, steps, d)
        hp, hs = macaron(hp, hs, ffn2_norm[layer], ffn2_w_gu[layer], ffn2_w_down[layer])
    return (hp, hs) + tuple(jnp.stack(outs[k]) for k in names)
```

```python
import functools
import math

import numpy as np
import jax
import jax.numpy as jnp
from jax import lax
from jax.experimental import pallas as pl
from jax.experimental.pallas import tpu as pltpu

F32 = jnp.float32
BF16 = jnp.bfloat16

EPS = 1e-6
NEG = -1e30

LANES = 128
SUBLANES = 8
MXU_DIM = 256
VMEM_BYTES_V7X = 64 * 1024 * 1024
VMEM_LIMIT = VMEM_BYTES_V7X - 8 * 1024 * 1024

A_HEADS = 8
B_HEADS = 16
B_HEAD_DIM = 64
B_GROUPS = 2
B_STATE = 128
B_CONV = 4
CHUNK = 128
C_WINDOWS = (2, 4, 8, 16)
C_HALO = 16
D_Q_HEADS = 16
D_KV_HEADS = 4
D_HEAD_DIM = 64
D_GROUP = D_Q_HEADS // D_KV_HEADS
REL_BUCKETS = 32
REL_MAX_DIST = 128


def _rms(x, g):
    ms = jnp.mean(x * x, axis=-1, keepdims=True)
    return x * lax.rsqrt(ms + EPS) * g


def _sigmoid(x):
    return 1.0 / (1.0 + jnp.exp(-x))


def _silu(x):
    return x * _sigmoid(x)


def _gelu_tanh(x):
    c = math.sqrt(2.0 / math.pi)
    return x * (0.5 * (1.0 + jnp.tanh(c * (x + 0.044715 * (x * x * x)))))


def _softplus(x):
    return jnp.maximum(x, 0.0) + jnp.log1p(jnp.exp(-jnp.abs(x)))


def _split3(x):
    hi = x.astype(BF16)
    r1 = x - hi.astype(F32)
    mid = r1.astype(BF16)
    lo = (r1 - mid.astype(F32)).astype(BF16)
    return hi, mid, lo


def _split2_lanes(x):
    hi = x.astype(BF16)
    lo = (x - hi.astype(F32)).astype(BF16)
    return jnp.concatenate([hi, lo], axis=1)


def _dot(a, b):
    return jnp.dot(a, b, preferred_element_type=F32)


def _dot_nt(a, b):
    return lax.dot_general(a, b, (((1,), (1,)), ((), ())), preferred_element_type=F32)


def _dot_tn(a, b):
    return lax.dot_general(a, b, (((0,), (0,)), ((), ())), preferred_element_type=F32)


def _expand_heads(m, n_pairs):
    rows = m.shape[0]
    lane = lax.broadcasted_iota(jnp.int32, (rows, LANES), 1)
    first = lane < B_HEAD_DIM
    parts = []
    for p in range(n_pairs):
        a = jnp.broadcast_to(m[:, 2 * p:2 * p + 1], (rows, LANES))
        b = jnp.broadcast_to(m[:, 2 * p + 1:2 * p + 2], (rows, LANES))
        parts.append(jnp.where(first, a, b))
    return jnp.concatenate(parts, axis=1)


def _head_sumsq(x, ones_bd):
    xx = x * x
    hi = xx.astype(BF16)
    lo = (xx - hi.astype(F32)).astype(BF16)
    outs = []
    for c in range(x.shape[1] // 256):
        sl = slice(c * 256, (c + 1) * 256)
        outs.append(_dot(hi[:, sl], ones_bd) + _dot(lo[:, sl], ones_bd))
    return jnp.concatenate(outs, axis=1) if len(outs) > 1 else outs[0]


def _ff_blocks(d_ff):
    step = FFN_BLOCK_TILES * MXU_DIM
    return [(c0, min(c0 + step, d_ff)) for c0 in range(0, d_ff, step)]


def _ffn_kernel(x_ref, g_ref, wgu_ref, wd_ref, o_ref, *, d_ff):
    tm = x_ref.shape[0]
    rows = [slice(r0, min(r0 + FFN_ROW_BLOCK, tm)) for r0 in range(0, tm, FFN_ROW_BLOCK)]
    xns = [_rms(x_ref[rs, :], g_ref[...]).astype(BF16) for rs in rows]
    for rs, xn in zip(rows, xns):
        y = None
        for c0, c1 in _ff_blocks(d_ff):
            gate = _dot(xn, wgu_ref[:, c0:c1])
            up = _dot(xn, wgu_ref[:, d_ff + c0:d_ff + c1])
            act = (_silu(gate) * up).astype(BF16)
            part = _dot(act, wd_ref[c0:c1, :])
            y = part if y is None else y + part
        o_ref[rs, :] = x_ref[rs, :] + 0.5 * y


def _resident(shape):
    nd = len(shape)
    return pl.BlockSpec(shape, lambda *_: (0,) * nd, pipeline_mode=pl.Buffered(1))


def _ffn(x2d, g, wgu, wd, tm):
    m, d = x2d.shape
    d_ff = wd.shape[0]
    assert m % tm == 0
    return pl.pallas_call(
        functools.partial(_ffn_kernel, d_ff=d_ff),
        out_shape=jax.ShapeDtypeStruct((m, d), F32),
        grid=(m // tm,),
        in_specs=[
            pl.BlockSpec((tm, d), lambda i: (i, 0)),
            _resident((1, d)),
            _resident(wgu.shape),
            _resident(wd.shape),
        ],
        out_specs=pl.BlockSpec((tm, d), lambda i: (i, 0)),
        compiler_params=pltpu.CompilerParams(
            dimension_semantics=("arbitrary",), vmem_limit_bytes=VMEM_LIMIT),
        name="ffn",
    )(x2d, g, wgu, wd)


def _even_prompt_kernel(h_ref, g_ref, wa_ref, wz_ref, wxbc_ref, wdt_ref, wout_ref,
                        lng_ref, lnb_ref, ws_ref, bsb_ref, convw_ref, convb_ref,
                        dtb_ref, alog_ref, dskip_ref, normg_ref, expand_ref, colsel_ref,
                        o_ref, conv_out_ref, ssm_out_ref,
                        ext_ref, st_ref, *, tile, width_a, inner):
    s = pl.program_id(1)
    n_chunks = tile // CHUNK
    halo = SUBLANES

    @pl.when(s == 0)
    def _():
        ext_ref[0:halo, :] = jnp.zeros((halo, ext_ref.shape[1]), F32)
        st_ref[...] = jnp.zeros(st_ref.shape, F32)

    x = h_ref[...]
    xn = _rms(x, g_ref[...]).astype(BF16)

    row = lax.broadcasted_iota(jnp.int32, (CHUNK, CHUNK), 0)
    col = lax.broadcasted_iota(jnp.int32, (CHUNK, CHUNK), 1)
    causal = row >= col
    lane = lax.broadcasted_iota(jnp.int32, (CHUNK, LANES), 1)
    first_half = lane < B_HEAD_DIM

    pa = _dot(xn, wa_ref[...])
    xbc_raw = _dot(xn, wxbc_ref[...])

    ga = _gelu_tanh(pa)
    u = ga[:, :width_a]
    v = ga[:, width_a:]
    mu = jnp.mean(v, axis=-1, keepdims=True)
    vc = v - mu
    var = jnp.mean(vc * vc, axis=-1, keepdims=True)
    v = vc * lax.rsqrt(var + EPS) * lng_ref[...] + lnb_ref[...]
    vb = v.astype(BF16)

    z = _dot(xn, wz_ref[...])
    dt_raw = _dot(xn, wdt_ref[...])

    ext_ref[halo:halo + tile, :] = xbc_raw
    ext = ext_ref[...]
    ext1 = pltpu.roll(ext, 1, 0)
    pair = ext * convw_ref[1:2, :] + ext1 * convw_ref[0:1, :]
    conv = (convb_ref[...] + ext * convw_ref[3:4, :] + ext1 * convw_ref[2:3, :] + pltpu.roll(pair, 2, 0))[halo:]
    tail = ext_ref[tile:tile + halo, :]
    ext_ref[0:halo, :] = tail
    conv_out_ref[...] = tail
    xbc = _silu(conv)
    gn = B_GROUPS * B_STATE
    xs = xbc[:, :inner]
    bm = xbc[:, inner:inner + gn]
    cm = xbc[:, inner + gn:]

    head_w = width_a // A_HEADS
    gate_cols = []
    for hh in range(A_HEADS):
        w = jnp.where(causal, ws_ref[hh], 0.0).astype(BF16)
        rhs = jnp.concatenate(
            [vb[c * CHUNK:(c + 1) * CHUNK, hh * head_w:(hh + 1) * head_w] for c in range(n_chunks)], axis=1)
        out = _dot(w, rhs)
        bias = bsb_ref[hh]
        gate_cols.append(jnp.concatenate(
            [out[:, c * head_w:(c + 1) * head_w] + bias for c in range(n_chunks)], axis=0))
    ya = u * jnp.concatenate(gate_cols, axis=1)
    out_a = _dot(ya.astype(BF16), wout_ref[0:width_a, :])

    dt = _softplus(dt_raw + dtb_ref[...])
    a_neg = -jnp.exp(alog_ref[...])
    da = dt * a_neg

    n_pairs = B_HEADS // 2
    heads_per_group = B_HEADS // B_GROUPS
    gw = heads_per_group * B_HEAD_DIM
    tril_ones = jnp.where(causal, 1.0, 0.0).astype(BF16)
    chunks = [slice(c * CHUNK, (c + 1) * CHUNK) for c in range(n_chunks)]
    acums = []
    for rs in chunks:
        d_hi, d_mid, d_lo = _split3(da[rs])
        acums.append(_dot(tril_ones, d_hi) + _dot(tril_ones, d_mid) + _dot(tril_ones, d_lo))
    acum = jnp.concatenate(acums, axis=0)
    decay = jnp.concatenate([jnp.exp(a[CHUNK - 1:CHUNK, :] - a) for a in acums], axis=0)
    expand = expand_ref[...]
    xd = xs * _dot(_split2_lanes(dt), expand)
    xdwb = (xs * _dot(_split2_lanes(dt * decay), expand)).astype(BF16)
    e_acum = _dot(_split2_lanes(jnp.exp(acum)), expand)
    bmb = bm.astype(BF16)
    cmb = cm.astype(BF16)
    y_rows = []
    for c, rs in enumerate(chunks):
        a_c = acums[c]
        acum_t = a_c.T
        a_cols = _dot(_split2_lanes(a_c), colsel_ref[...])
        cb = [_dot_nt(cmb[rs, g * B_STATE:(g + 1) * B_STATE], bmb[rs, g * B_STATE:(g + 1) * B_STATE])
              for g in range(B_GROUPS)]
        y_parts = []
        for p in range(n_pairs):
            g = (2 * p) // heads_per_group
            ms = []
            for hh in (2 * p, 2 * p + 1):
                seg = a_cols[:, hh * CHUNK:(hh + 1) * CHUNK] - jnp.broadcast_to(acum_t[hh:hh + 1, :], (CHUNK, CHUNK))
                lmat = jnp.where(causal, jnp.exp(seg), 0.0)
                ms.append((cb[g] * lmat).astype(BF16))
            lhs = jnp.concatenate(ms, axis=1)
            xd_p = xd[rs, p * LANES:(p + 1) * LANES]
            rhs = jnp.concatenate([jnp.where(first_half, xd_p, 0.0),
                                   jnp.where(first_half, 0.0, xd_p)], axis=0).astype(BF16)
            y_parts.append(_dot(lhs, rhs))
        y_rows.append(jnp.concatenate(y_parts, axis=1))
    for c, rs in enumerate(chunks):
        st_prev = st_ref[...]
        stb = st_prev.astype(BF16)
        y_off = jnp.concatenate(
            [_dot(cmb[rs, g * B_STATE:(g + 1) * B_STATE], stb[:, g * gw:(g + 1) * gw]) for g in range(B_GROUPS)],
            axis=1)
        st_add = jnp.concatenate(
            [_dot_tn(bmb[rs, g * B_STATE:(g + 1) * B_STATE], xdwb[rs, g * gw:(g + 1) * gw])
             for g in range(B_GROUPS)], axis=1)
        chunk_decay = e_acum[(c + 1) * CHUNK - 1:(c + 1) * CHUNK, :]
        st_ref[...] = st_prev * chunk_decay + st_add
        y_rows[c] = y_rows[c] + y_off * e_acum[rs]
    y = (jnp.concatenate(y_rows, axis=0) if n_chunks > 1 else y_rows[0]) + xs * dskip_ref[...]
    y = y * _silu(z)
    half = inner // B_GROUPS
    yn = []
    for g in range(B_GROUPS):
        yg = y[:, g * half:(g + 1) * half]
        yn.append(yg * lax.rsqrt(jnp.mean(yg * yg, axis=-1, keepdims=True) + EPS))
    yb = jnp.concatenate(yn, axis=1) * normg_ref[...]

    o_ref[...] = x + out_a + _dot(yb.astype(BF16), wout_ref[width_a:, :])

    @pl.when(s == pl.num_programs(1) - 1)
    def _():
        ssm_out_ref[...] = st_ref[...].T


def _even_prompt(h, p, tile):
    b, seq, d = h.shape
    width_a = p["wa"].shape[1] // 2
    inner = p["wz"].shape[1]
    conv_dim = p["wxbc"].shape[1]
    assert seq % tile == 0 and tile % CHUNK == 0
    small = ["lng", "lnb", "ws", "bsb", "convw", "convb", "dtb", "alog", "dskip", "normg", "expand", "colsel"]
    out, conv_tail, ssm = pl.pallas_call(
        functools.partial(_even_prompt_kernel, tile=tile, width_a=width_a, inner=inner),
        out_shape=(jax.ShapeDtypeStruct((b, seq, d), F32),
                   jax.ShapeDtypeStruct((b, SUBLANES, conv_dim), F32),
                   jax.ShapeDtypeStruct((b, inner, B_STATE), F32)),
        grid=(b, seq // tile),
        in_specs=[pl.BlockSpec((None, tile, d), lambda i, j: (i, j, 0)),
                  _resident((1, d))]
        + [_resident(p[k].shape) for k in ("wa", "wz", "wxbc", "wdt", "wout")]
        + [_resident(p[k].shape) for k in small],
        out_specs=(pl.BlockSpec((None, tile, d), lambda i, j: (i, j, 0)),
                   pl.BlockSpec((None, SUBLANES, conv_dim), lambda i, j: (i, 0, 0)),
                   pl.BlockSpec((None, inner, B_STATE), lambda i, j: (i, 0, 0))),
        scratch_shapes=[pltpu.VMEM((tile + SUBLANES, conv_dim), F32),
                        pltpu.VMEM((B_STATE, inner), F32)],
        compiler_params=pltpu.CompilerParams(
            dimension_semantics=("arbitrary", "arbitrary"), vmem_limit_bytes=VMEM_LIMIT),
        name="even_prompt",
    )(h, p["g"], p["wa"], p["wz"], p["wxbc"], p["wdt"], p["wout"], *[p[k] for k in small])
    return out, conv_tail[:, SUBLANES - (B_CONV - 1):, :], ssm


def _t5_bucket(dist):
    n = np.maximum(dist, 0)
    max_exact = REL_BUCKETS // 2
    n_safe = np.maximum(n, 1).astype(np.float32)
    scale = np.float32((REL_BUCKETS - max_exact) / math.log(REL_MAX_DIST / max_exact))
    large = max_exact + (np.log(n_safe / max_exact) * scale).astype(np.int32)
    large = np.minimum(large, REL_BUCKETS - 1)
    return np.where(n < max_exact, n, large).astype(np.int32)


def _fill_bias(bias_ref, bucket_ref, rel_ref):
    bucket = bucket_ref[...]
    has_prev = lax.broadcasted_iota(jnp.int32, bucket.shape, 1) >= CHUNK
    for hh in range(D_Q_HEADS):
        acc = jnp.full(bucket.shape, NEG, F32)
        for bkt in range(REL_BUCKETS):
            acc = jnp.where(bucket == bkt, rel_ref[bkt, hh], acc)
        bias_ref[0, hh] = acc
        bias_ref[1, hh] = jnp.where(has_prev, acc, NEG)


def _group_attention(qg, kk, vv, bias_ref, sinks_ref, grp, table):
    lq = qg.shape[0]
    lane_kv = lax.broadcasted_iota(jnp.int32, (lq, D_KV_HEADS * D_HEAD_DIM), 1) // D_HEAD_DIM
    zero = jnp.zeros_like(qg)
    lhs = jnp.concatenate([jnp.where(lane_kv == kv, qg, zero) for kv in range(D_KV_HEADS)], axis=0)
    sc = _dot_nt(lhs, kk)
    probs = []
    for kv in range(D_KV_HEADS):
        hh = kv * D_GROUP + grp
        s_h = sc[kv * lq:(kv + 1) * lq] + bias_ref[table, hh]
        sink = sinks_ref[hh]
        m = jnp.maximum(jnp.max(s_h, axis=-1, keepdims=True), sink)
        pexp = jnp.exp(s_h - m)
        denom = jnp.sum(pexp, axis=-1, keepdims=True) + jnp.exp(sink - m)
        probs.append((pexp / denom).astype(BF16))
    ov = _dot(jnp.concatenate(probs, axis=0), vv)
    out = ov[(D_KV_HEADS - 1) * lq:]
    for kv in range(D_KV_HEADS - 2, -1, -1):
        out = jnp.where(lane_kv == kv, ov[kv * lq:(kv + 1) * lq], out)
    return out


def _odd_prompt_kernel(h_ref, g_ref, wc_ref, wq_ref, wk_ref, wv_ref, wout_ref,
                       linw_ref, cscale_ref, qn_ref, kn_ref, onesbd_ref, bucket_ref,
                       sinks_ref, rel_ref,
                       o_ref, pool_out_ref, k_out_ref, v_out_ref,
                       extc_ref, kprev_ref, vprev_ref, bias_ref, *, tile, width_c):
    b = pl.program_id(0)
    s = pl.program_id(1)
    n_blocks = tile // CHUNK

    @pl.when((b == 0) & (s == 0))
    def _():
        _fill_bias(bias_ref, bucket_ref, rel_ref)

    @pl.when(s == 0)
    def _():
        extc_ref[0:C_HALO, :] = jnp.zeros((C_HALO, width_c), F32)
        kprev_ref[...] = jnp.zeros(kprev_ref.shape, F32)
        vprev_ref[...] = jnp.zeros(vprev_ref.shape, F32)

    x = h_ref[...]
    xn = _rms(x, g_ref[...]).astype(BF16)

    c_in = _dot(xn, wc_ref[...])
    extc_ref[C_HALO:C_HALO + tile, :] = c_in
    e = extc_ref[...]
    tail = extc_ref[tile:tile + C_HALO, :]
    extc_ref[0:C_HALO, :] = tail
    pool_out_ref[...] = tail
    pos = (s * tile + lax.broadcasted_iota(jnp.int32, (tile, 1), 0) + 1).astype(F32)
    gdim = width_c // len(C_WINDOWS)
    run = e
    shift = 1
    yc = []
    for gi, win in enumerate(C_WINDOWS):
        while shift < win:
            run = run + pltpu.roll(run, shift, 0)
            shift *= 2
        cnt = jnp.minimum(pos, float(win))
        pooled = run[C_HALO:, :gdim] / cnt - c_in[:, gi * gdim:(gi + 1) * gdim]
        yc.append(_dot(pooled.astype(BF16), linw_ref[gi]))
        if gi + 1 < len(C_WINDOWS):
            run = run[:, gdim:]
    yc = jnp.concatenate(yc, axis=1) * cscale_ref[...]

    q = _dot(xn, wq_ref[...])
    k = _dot(xn, wk_ref[...])
    v = _dot(xn, wv_ref[...])
    ones_bd = onesbd_ref[...]
    inv_d = 1.0 / D_HEAD_DIM
    qn = q * lax.rsqrt(_head_sumsq(q, ones_bd) * inv_d + EPS) * qn_ref[...]
    kn = k * lax.rsqrt(_head_sumsq(k, ones_bd) * inv_d + EPS) * kn_ref[...]
    qs = (qn * (D_HEAD_DIM ** -0.5)).astype(BF16)
    kb = kn.astype(BF16)
    vb = v.astype(BF16)
    first_table = jnp.where(s == 0, 1, 0)
    gw = D_KV_HEADS * D_HEAD_DIM
    o_rows = []
    for blk in range(n_blocks):
        rs = slice(blk * CHUNK, (blk + 1) * CHUNK)
        if blk == 0:
            k_prev, v_prev = kprev_ref[...].astype(BF16), vprev_ref[...].astype(BF16)
        else:
            k_prev, v_prev = kb[(blk - 1) * CHUNK:blk * CHUNK], vb[(blk - 1) * CHUNK:blk * CHUNK]
        kk = jnp.concatenate([k_prev, kb[rs]], axis=0)
        vv = jnp.concatenate([v_prev, vb[rs]], axis=0)
        o_rows.append(jnp.concatenate(
            [_group_attention(qs[rs, grp * gw:(grp + 1) * gw], kk, vv, bias_ref, sinks_ref, grp,
                              first_table if blk == 0 else 0)
             for grp in range(D_GROUP)], axis=1))
    kprev_ref[...] = kn[tile - CHUNK:]
    vprev_ref[...] = v[tile - CHUNK:]
    yd = jnp.concatenate(o_rows, axis=0) if n_blocks > 1 else o_rows[0]

    mix = jnp.concatenate([yc, yd], axis=1).astype(BF16)
    o_ref[...] = x + _dot(mix, wout_ref[...])

    @pl.when(s == pl.num_programs(1) - 1)
    def _():
        k_out_ref[...] = kn[tile - CHUNK:]
        v_out_ref[...] = v[tile - CHUNK:]


def _odd_prompt(h, p, tile):
    b, seq, d = h.shape
    width_c = p["wc"].shape[1]
    kvw = p["wk"].shape[1]
    assert seq % tile == 0 and tile % CHUNK == 0
    r = np.arange(CHUNK) + CHUNK
    c = np.arange(2 * CHUNK)
    dist = r[:, None] - c[None, :]
    bucket = np.where((dist >= 0) & (dist < CHUNK), _t5_bucket(dist), -1).astype(np.int32)
    vm = ["linw", "cscale", "qn", "kn", "onesbd"]
    smem = pl.BlockSpec(memory_space=pltpu.SMEM)
    out, pool_tail, k_win, v_win = pl.pallas_call(
        functools.partial(_odd_prompt_kernel, tile=tile, width_c=width_c),
        out_shape=(jax.ShapeDtypeStruct((b, seq, d), F32),
                   jax.ShapeDtypeStruct((b, C_HALO, width_c), F32),
                   jax.ShapeDtypeStruct((b, CHUNK, kvw), F32),
                   jax.ShapeDtypeStruct((b, CHUNK, kvw), F32)),
        grid=(b, seq // tile),
        in_specs=[pl.BlockSpec((None, tile, d), lambda i, j: (i, j, 0)),
                  _resident((1, d))]
        + [_resident(p[k].shape) for k in ("wc", "wq", "wk", "wv", "wout")]
        + [_resident(p[k].shape) for k in vm]
        + [_resident(bucket.shape), smem, smem],
        out_specs=(pl.BlockSpec((None, tile, d), lambda i, j: (i, j, 0)),
                   pl.BlockSpec((None, C_HALO, width_c), lambda i, j: (i, 0, 0)),
                   pl.BlockSpec((None, CHUNK, kvw), lambda i, j: (i, 0, 0)),
                   pl.BlockSpec((None, CHUNK, kvw), lambda i, j: (i, 0, 0))),
        scratch_shapes=[pltpu.VMEM((tile + C_HALO, width_c), F32),
                        pltpu.VMEM((CHUNK, kvw), F32),
                        pltpu.VMEM((CHUNK, kvw), F32),
                        pltpu.VMEM((2, D_Q_HEADS, CHUNK, 2 * CHUNK), F32)],
        compiler_params=pltpu.CompilerParams(
            dimension_semantics=("arbitrary", "arbitrary"), vmem_limit_bytes=VMEM_LIMIT),
        name="odd_prompt",
    )(h, p["g"], p["wc"], p["wq"], p["wk"], p["wv"], p["wout"], *[p[k] for k in vm],
      jnp.asarray(bucket), p["sinks"], p["rel"])
    return out, pool_tail[:, C_HALO - (max(C_WINDOWS) - 1):, :], k_win, v_win


def _steps(x, n, width):
    return [x[:, t * width:(t + 1) * width] for t in range(n)]


def _stack_steps(ref, n, width):
    x = ref[...]
    return jnp.concatenate(_steps(x, n, width), axis=0)


def _even_sample_front_kernel(hs_ref, g_ref, wa_ref, wz_ref, wxbc_ref, wdt_ref,
                              lng_ref, lnb_ref, wts_ref, bts_ref, convw_ref, convb_ref,
                              dtb_ref, alog_ref, dskip_ref, cs_ref,
                              v_out, ya_out, ypart_out, eacum_out, z_out, conv_out,
                              cgt_out, xdw_out, bs_out, dec_out, *, steps, d, width_a, inner, conv_dim):
    bt = hs_ref.shape[0]
    xn = _rms(_stack_steps(hs_ref, steps, d), g_ref[...]).astype(BF16)
    blk = lambda a, t: a[t * bt:(t + 1) * bt]

    ga = _gelu_tanh(_dot(xn, wa_ref[...]))
    u = ga[:, :width_a]
    v = ga[:, width_a:]
    mu = jnp.mean(v, axis=-1, keepdims=True)
    vc = v - mu
    var = jnp.mean(vc * vc, axis=-1, keepdims=True)
    v = vc * lax.rsqrt(var + EPS) * lng_ref[...] + lnb_ref[...]
    for t in range(steps):
        v_out[:, t * width_a:(t + 1) * width_a] = blk(v, t)
        gate = bts_ref[t:t + 1, :]
        for s in range(t + 1):
            gate = gate + wts_ref[t * steps + s:t * steps + s + 1, :] * blk(v, s)
        ya_out[:, t * width_a:(t + 1) * width_a] = blk(u, t) * gate

    z = _dot(xn, wz_ref[...])
    for t in range(steps):
        z_out[:, t * inner:(t + 1) * inner] = blk(z, t)
    raw = _dot(xn, wxbc_ref[...])
    dt = _softplus(_dot(xn, wdt_ref[...]) + dtb_ref[...])
    ext = _steps(cs_ref[...], B_CONV - 1, conv_dim) + [blk(raw, t) for t in range(steps)]
    for k in range(B_CONV - 1):
        conv_out[:, k * conv_dim:(k + 1) * conv_dim] = ext[len(ext) - (B_CONV - 1) + k]
    gn = B_GROUPS * B_STATE
    n_pairs = B_HEADS // 2
    a_neg = -jnp.exp(alog_ref[...])
    xs, bm, cm, dts, acum = [], [], [], [], []
    for t in range(steps):
        conv = convb_ref[...]
        for tap in range(B_CONV):
            conv = conv + ext[t + tap] * convw_ref[tap:tap + 1, :]
        xbc = _silu(conv)
        xs.append(xbc[:, :inner])
        bm.append(xbc[:, inner:inner + gn])
        cm.append(xbc[:, inner + gn:])
        dts.append(blk(dt, t))
        da = dts[t] * a_neg
        acum.append(da if t == 0 else acum[t - 1] + da)
    lane = lax.broadcasted_iota(jnp.int32, (bt, LANES), 1)
    group0 = lane < (B_HEADS // B_GROUPS)
    dec_out[...] = jnp.exp(acum[steps - 1])
    pad_rows = SUBLANES - steps
    xdw_out[:, steps * inner:] = jnp.zeros((bt, pad_rows * inner), F32)
    bs_out[:, steps * gn:] = jnp.zeros((bt, pad_rows * gn), F32)
    xd = []
    for t in range(steps):
        xd.append(xs[t] * _expand_heads(dts[t], n_pairs))
        eacum_out[:, t * inner:(t + 1) * inner] = _expand_heads(jnp.exp(acum[t]), n_pairs)
        xdw_out[:, t * inner:(t + 1) * inner] = xs[t] * _expand_heads(
            dts[t] * jnp.exp(acum[steps - 1] - acum[t]), n_pairs)
        bs_out[:, t * gn:(t + 1) * gn] = bm[t]
        for g in range(B_GROUPS):
            r = g * steps + t
            cgt_out[:, r * B_STATE:(r + 1) * B_STATE] = cm[t][:, g * B_STATE:(g + 1) * B_STATE]
    for t in range(steps):
        y = xs[t] * dskip_ref[...]
        for s in range(t + 1):
            cb = [jnp.sum(cm[t][:, g * B_STATE:(g + 1) * B_STATE] * bm[s][:, g * B_STATE:(g + 1) * B_STATE],
                          axis=-1, keepdims=True) for g in range(B_GROUPS)]
            coef = jnp.where(group0, cb[0], cb[1]) * jnp.exp(acum[t] - acum[s])
            y = y + _expand_heads(coef, n_pairs) * xd[s]
        ypart_out[:, t * inner:(t + 1) * inner] = y


def _even_sample_state_kernel(s0_ref, cgt_ref, xdw_ref, bs_ref, dec_ref, yoff_ref, snew_ref, *, bb):
    step = pl.program_id(0)
    gw = (B_HEADS // B_GROUPS) * B_HEAD_DIM

    def body(bi, carry):
        s0 = s0_ref[bi]
        c8 = cgt_ref[bi]
        c16 = jnp.concatenate([c8, jnp.zeros_like(c8)], axis=0).astype(BF16)
        yoff_ref[bi] = _dot_nt(c16, s0.astype(BF16))[:SUBLANES]
        x8 = xdw_ref[bi]
        b8 = bs_ref[bi]
        x16 = jnp.concatenate([x8, jnp.zeros_like(x8)], axis=0).astype(BF16)
        b16 = jnp.concatenate([b8, jnp.zeros_like(b8)], axis=0).astype(BF16)
        for g in range(B_GROUPS):
            add = _dot_tn(x16[:, g * gw:(g + 1) * gw], b16[:, g * B_STATE:(g + 1) * B_STATE])
            for hl in range(B_HEADS // B_GROUPS):
                hh = g * (B_HEADS // B_GROUPS) + hl
                rs = slice(hh * B_HEAD_DIM, (hh + 1) * B_HEAD_DIM)
                snew_ref[bi, rs, :] = s0[rs] * dec_ref[step * bb + bi, hh] + \
                    add[hl * B_HEAD_DIM:(hl + 1) * B_HEAD_DIM]
        return carry

    lax.fori_loop(0, bb, body, 0, unroll=2)


def _even_sample_back_kernel(hs_ref, ya_ref, ypart_ref, eacum_ref, z_ref, yoff_ref, normg_ref, wout_ref,
                             o_ref, *, steps, d, inner):
    bt = hs_ref.shape[0]
    half = inner // B_GROUPS
    mixes = []
    for t in range(steps):
        sl = slice(t * inner, (t + 1) * inner)
        yoff = jnp.concatenate(
            [yoff_ref[:, (g * steps + t) * inner + g * half:(g * steps + t) * inner + (g + 1) * half]
             for g in range(B_GROUPS)], axis=1)
        y = (ypart_ref[:, sl] + yoff * eacum_ref[:, sl]) * _silu(z_ref[:, sl])
        yn = []
        for g in range(B_GROUPS):
            yg = y[:, g * half:(g + 1) * half]
            yn.append(yg * lax.rsqrt(jnp.mean(yg * yg, axis=-1, keepdims=True) + EPS))
        yb = jnp.concatenate(yn, axis=1) * normg_ref[...]
        mixes.append(jnp.concatenate([ya_ref[:, t * d:(t + 1) * d], yb], axis=1))
    out = _dot(jnp.concatenate(mixes, axis=0).astype(BF16), wout_ref[...])
    for t in range(steps):
        o_ref[:, t * d:(t + 1) * d] = hs_ref[:, t * d:(t + 1) * d] + out[t * bt:(t + 1) * bt]


def _row_tiled(width, bt):
    return pl.BlockSpec((bt, width), lambda i: (i, 0))


def _even_sample(hs2, state_conv2, state_ssm3, p, steps, bt, bb):
    nb, _ = hs2.shape
    d = p["wa"].shape[0]
    width_a = p["wa"].shape[1] // 2
    inner = p["wz"].shape[1]
    conv_dim = p["wxbc"].shape[1]
    gn = B_GROUPS * B_STATE
    assert nb % bt == 0 and nb % bb == 0 and steps <= SUBLANES
    params = pltpu.CompilerParams(dimension_semantics=("arbitrary",), vmem_limit_bytes=VMEM_LIMIT)
    small = ["lng", "lnb", "wts", "bts", "convw", "convb", "dtb", "alog", "dskip"]
    widths = dict(v=steps * width_a, ya=steps * width_a, ypart=steps * inner, eacum=steps * inner,
                  z=steps * inner, conv=(B_CONV - 1) * conv_dim, cgt=SUBLANES * B_STATE,
                  xdw=SUBLANES * inner, bs=SUBLANES * gn, dec=LANES)
    front = pl.pallas_call(
        functools.partial(_even_sample_front_kernel, steps=steps, d=d, width_a=width_a, inner=inner,
                          conv_dim=conv_dim),
        out_shape=tuple(jax.ShapeDtypeStruct((nb, w), F32) for w in widths.values()),
        grid=(nb // bt,),
        in_specs=[_row_tiled(steps * d, bt), _resident((1, d))]
        + [_resident(p[k].shape) for k in ("wa", "wz", "wxbc", "wdt")]
        + [_resident(p[k].shape) for k in small]
        + [_row_tiled((B_CONV - 1) * conv_dim, bt)],
        out_specs=tuple(_row_tiled(w, bt) for w in widths.values()),
        compiler_params=params,
        name="even_sample_front",
    )(hs2, p["g"], p["wa"], p["wz"], p["wxbc"], p["wdt"], *[p[k] for k in small], state_conv2)
    v_rows, ya, ypart, eacum, z, new_conv, cgt, xdw, bs, dec = front

    hp = state_ssm3.shape[1]
    tile3 = lambda rows, width: pl.BlockSpec((bb, rows, width), lambda i: (i, 0, 0))
    yoff, new_ssm = pl.pallas_call(
        functools.partial(_even_sample_state_kernel, bb=bb),
        out_shape=(jax.ShapeDtypeStruct((nb, SUBLANES, hp), F32),
                   jax.ShapeDtypeStruct(state_ssm3.shape, F32)),
        grid=(nb // bb,),
        in_specs=[tile3(hp, B_STATE), tile3(SUBLANES, B_STATE), tile3(SUBLANES, inner), tile3(SUBLANES, gn),
                  pl.BlockSpec(memory_space=pltpu.SMEM)],
        out_specs=(tile3(SUBLANES, hp), tile3(hp, B_STATE)),
        compiler_params=params,
        name="even_sample_state",
    )(state_ssm3, cgt.reshape(nb, SUBLANES, B_STATE), xdw.reshape(nb, SUBLANES, inner),
      bs.reshape(nb, SUBLANES, gn), dec[:, :B_HEADS])

    out = pl.pallas_call(
        functools.partial(_even_sample_back_kernel, steps=steps, d=d, inner=inner),
        out_shape=jax.ShapeDtypeStruct(hs2.shape, F32),
        grid=(nb // bt,),
        in_specs=[_row_tiled(steps * d, bt), _row_tiled(steps * width_a, bt), _row_tiled(steps * inner, bt),
                  _row_tiled(steps * inner, bt), _row_tiled(steps * inner, bt), _row_tiled(SUBLANES * hp, bt),
                  _resident((1, inner)), _resident(p["wout"].shape)],
        out_specs=_row_tiled(steps * d, bt),
        compiler_params=params,
        name="even_sample_back",
    )(hs2, ya, ypart, eacum, z, yoff.reshape(nb, SUBLANES * hp), p["normg"], p["wout"])
    return out, v_rows, new_conv, new_ssm


def _odd_sample_front_kernel(hs_ref, g_ref, wc_ref, wq_ref, wk_ref, wv_ref, linw_ref, cscale_ref,
                             qn_ref, kn_ref, onesbd_ref, ps_ref,
                             yc_out, pool_out, q_out, knew_out, vnew_out, *, steps, d, width_c, past_len):
    bt = hs_ref.shape[0]
    xn = _rms(_stack_steps(hs_ref, steps, d), g_ref[...]).astype(BF16)
    blk = lambda a, t: a[t * bt:(t + 1) * bt]
    c_in = _dot(xn, wc_ref[...])
    n_state = max(C_WINDOWS) - 1
    ext = _steps(ps_ref[...], n_state, width_c) + [blk(c_in, t) for t in range(steps)]
    for j in range(n_state):
        pool_out[:, j * width_c:(j + 1) * width_c] = ext[len(ext) - n_state + j]
    gdim = width_c // len(C_WINDOWS)
    yc_cols = []
    for gi, win in enumerate(C_WINDOWS):
        sl = slice(gi * gdim, (gi + 1) * gdim)
        pooled = []
        for t in range(steps):
            hi = n_state + t
            lo = max(hi - win + 1, 0)
            acc = ext[lo][:, sl]
            for j in range(lo + 1, hi + 1):
                acc = acc + ext[j][:, sl]
            count = float(min(past_len + t + 1, win))
            pooled.append(acc / count - ext[hi][:, sl])
        yc_cols.append(_dot(jnp.concatenate(pooled, axis=0).astype(BF16), linw_ref[gi]))
    yc = jnp.concatenate(yc_cols, axis=1) * cscale_ref[...]
    for t in range(steps):
        yc_out[:, t * width_c:(t + 1) * width_c] = blk(yc, t)

    q = _dot(xn, wq_ref[...])
    k = _dot(xn, wk_ref[...])
    v = _dot(xn, wv_ref[...])
    ones_bd = onesbd_ref[...]
    inv_d = 1.0 / D_HEAD_DIM
    qn = q * lax.rsqrt(_head_sumsq(q, ones_bd) * inv_d + EPS) * qn_ref[...] * (D_HEAD_DIM ** -0.5)
    kn = k * lax.rsqrt(_head_sumsq(k, ones_bd) * inv_d + EPS) * kn_ref[...]
    qw = q.shape[1]
    kw = k.shape[1]
    pad = SUBLANES - steps
    q_out[:, steps * qw:] = jnp.zeros((bt, pad * qw), F32)
    knew_out[:, :pad * kw] = jnp.zeros((bt, pad * kw), F32)
    vnew_out[:, :pad * kw] = jnp.zeros((bt, pad * kw), F32)
    for t in range(steps):
        q_out[:, t * qw:(t + 1) * qw] = blk(qn, t)
        knew_out[:, (pad + t) * kw:(pad + t + 1) * kw] = blk(kn, t)
        vnew_out[:, (pad + t) * kw:(pad + t + 1) * kw] = blk(v, t)


SINK_BUCKET = REL_BUCKETS


def _odd_sample_attn_kernel(q_ref, knew_ref, vnew_ref, ck_ref, cv_ref, bucket_ref, sinks_ref, rel_ref,
                            o_ref, kout_ref, vout_ref, bias_ref, *, bb, steps, n_keys):
    win = ck_ref.shape[1]
    kvw = ck_ref.shape[2]
    tile16 = 2 * SUBLANES

    @pl.when(pl.program_id(0) == 0)
    def _():
        bucket = bucket_ref[...]
        for hh in range(D_Q_HEADS):
            acc = jnp.full(bucket.shape, NEG, F32)
            for bkt in range(REL_BUCKETS):
                acc = jnp.where(bucket == bkt, rel_ref[bkt, hh], acc)
            acc = jnp.where(bucket == SINK_BUCKET, sinks_ref[hh], acc)
            bias_ref[hh * SUBLANES:(hh + 1) * SUBLANES, :] = acc

    sub = lax.broadcasted_iota(jnp.int32, (SUBLANES, kvw), 0)
    new_rows = sub >= SUBLANES - steps
    lane_kv = lax.broadcasted_iota(jnp.int32, (SUBLANES, kvw), 1) // D_HEAD_DIM
    gw = D_KV_HEADS * D_HEAD_DIM
    zero_keys = jnp.zeros((n_keys - win - tile16, kvw), BF16)

    def extend(cache, new8):
        new16 = jnp.concatenate([new8, jnp.zeros_like(new8)], axis=0).astype(BF16)
        return jnp.concatenate([cache.astype(BF16), new16, zero_keys], axis=0)

    def shift_in(cache, new8, out_ref, bi):
        rolled = pltpu.roll(cache, win - steps, 0)
        out_ref[bi, 0:win - SUBLANES, :] = rolled[:win - SUBLANES]
        out_ref[bi, win - SUBLANES:, :] = jnp.where(new_rows, new8, rolled[win - SUBLANES:])

    def body(bi, carry):
        ck = ck_ref[bi]
        cv = cv_ref[bi]
        k8 = knew_ref[bi]
        v8 = vnew_ref[bi]
        shift_in(ck, k8, kout_ref, bi)
        shift_in(cv, v8, vout_ref, bi)
        q8 = q_ref[bi]
        pieces = []
        for kv in range(D_KV_HEADS):
            for grp in range(D_GROUP):
                qg = q8[:, grp * gw:(grp + 1) * gw]
                pieces.append(jnp.where(lane_kv == kv, qg, 0.0))
        lhs = jnp.concatenate(pieces, axis=0).astype(BF16)
        sc = _dot_nt(lhs, extend(ck, k8)) + bias_ref[...]
        m = jnp.max(sc, axis=-1, keepdims=True)
        pexp = jnp.exp(sc - m)
        probs = (pexp / jnp.sum(pexp, axis=-1, keepdims=True)).astype(BF16)
        ov = _dot(probs, extend(cv, v8))
        outs = []
        for grp in range(D_GROUP):
            r_last = ((D_KV_HEADS - 1) * D_GROUP + grp) * SUBLANES
            acc = ov[r_last:r_last + SUBLANES]
            for kv in range(D_KV_HEADS - 2, -1, -1):
                r0 = (kv * D_GROUP + grp) * SUBLANES
                acc = jnp.where(lane_kv == kv, ov[r0:r0 + SUBLANES], acc)
            outs.append(acc)
        o_ref[bi] = jnp.concatenate(outs, axis=1)
        return carry

    lax.fori_loop(0, bb, body, 0, unroll=SAMPLE_ATTN_UNROLL)


def _odd_sample_back_kernel(hs_ref, yc_ref, o_ref_in, wout_ref, out_ref, *, steps, d, width_c, qw):
    bt = hs_ref.shape[0]
    mix = jnp.concatenate(
        [jnp.concatenate([yc_ref[:, t * width_c:(t + 1) * width_c], o_ref_in[:, t * qw:(t + 1) * qw]], axis=1)
         for t in range(steps)], axis=0).astype(BF16)
    out = _dot(mix, wout_ref[...])
    for t in range(steps):
        out_ref[:, t * d:(t + 1) * d] = hs_ref[:, t * d:(t + 1) * d] + out[t * bt:(t + 1) * bt]


def _odd_sample(hs2, state_pool2, cache_k3, cache_v3, p, steps, past_len, bt, bb):
    nb, _ = hs2.shape
    d = p["wc"].shape[0]
    width_c = p["wc"].shape[1]
    qw = p["wq"].shape[1]
    kw = p["wk"].shape[1]
    win = cache_k3.shape[1]
    n_state = max(C_WINDOWS) - 1
    assert nb % bt == 0 and nb % bb == 0 and steps <= SUBLANES and win == CHUNK
    params = pltpu.CompilerParams(dimension_semantics=("arbitrary",), vmem_limit_bytes=VMEM_LIMIT)
    vm = ["linw", "cscale", "qn", "kn", "onesbd"]
    widths = dict(yc=steps * width_c, pool=n_state * width_c, q=SUBLANES * qw, knew=SUBLANES * kw,
                  vnew=SUBLANES * kw)
    yc, new_pool, q8, knew8, vnew8 = pl.pallas_call(
        functools.partial(_odd_sample_front_kernel, steps=steps, d=d, width_c=width_c, past_len=past_len),
        out_shape=tuple(jax.ShapeDtypeStruct((nb, w), F32) for w in widths.values()),
        grid=(nb // bt,),
        in_specs=[_row_tiled(steps * d, bt), _resident((1, d))]
        + [_resident(p[k].shape) for k in ("wc", "wq", "wk", "wv")]
        + [_resident(p[k].shape) for k in vm]
        + [_row_tiled(n_state * width_c, bt)],
        out_specs=tuple(_row_tiled(w, bt) for w in widths.values()),
        compiler_params=params,
        name="odd_sample_front",
    )(hs2, p["g"], p["wc"], p["wq"], p["wk"], p["wv"], *[p[k] for k in vm], state_pool2)

    n_keys = 2 * CHUNK
    pad = SUBLANES - steps
    bucket = np.full((SUBLANES, n_keys), -1, np.int32)
    for t in range(steps):
        q_pos = past_len + t
        k_pos = np.full(n_keys, -10 ** 9, np.int64)
        k_pos[:win] = past_len - win + np.arange(win)
        k_pos[win + pad:win + SUBLANES] = past_len + np.arange(steps)
        dist = q_pos - k_pos
        ok = (dist >= 0) & (dist < CHUNK) & (k_pos >= 0)
        bucket[t] = np.where(ok, _t5_bucket(np.where(ok, dist, 0)), -1)
    bucket[:, n_keys - 1] = SINK_BUCKET
    smem = pl.BlockSpec(memory_space=pltpu.SMEM)
    tile3 = lambda rows, width: pl.BlockSpec((bb, rows, width), lambda i: (i, 0, 0))
    o8, new_k, new_v = pl.pallas_call(
        functools.partial(_odd_sample_attn_kernel, bb=bb, steps=steps, n_keys=n_keys),
        out_shape=(jax.ShapeDtypeStruct((nb, SUBLANES, qw), F32),
                   jax.ShapeDtypeStruct(cache_k3.shape, F32),
                   jax.ShapeDtypeStruct(cache_v3.shape, F32)),
        grid=(nb // bb,),
        in_specs=[tile3(SUBLANES, qw), tile3(SUBLANES, kw), tile3(SUBLANES, kw), tile3(win, kw), tile3(win, kw),
                  _resident(bucket.shape), smem, smem],
        out_specs=(tile3(SUBLANES, qw), tile3(win, kw), tile3(win, kw)),
        scratch_shapes=[pltpu.VMEM((D_Q_HEADS * SUBLANES, n_keys), F32)],
        compiler_params=params,
        name="odd_sample_attn",
    )(q8.reshape(nb, SUBLANES, qw), knew8.reshape(nb, SUBLANES, kw), vnew8.reshape(nb, SUBLANES, kw),
      cache_k3, cache_v3, jnp.asarray(bucket), p["sinks"], p["rel"])

    out = pl.pallas_call(
        functools.partial(_odd_sample_back_kernel, steps=steps, d=d, width_c=width_c, qw=qw),
        out_shape=jax.ShapeDtypeStruct(hs2.shape, F32),
        grid=(nb // bt,),
        in_specs=[_row_tiled(steps * d, bt), _row_tiled(steps * width_c, bt), _row_tiled(SUBLANES * qw, bt),
                  _resident(p["wout"].shape)],
        out_specs=_row_tiled(steps * d, bt),
        compiler_params=params,
        name="odd_sample_back",
    )(hs2, yc, o8.reshape(nb, SUBLANES * qw), p["wout"])
    return out, new_pool, new_k, new_v


def _row(v):
    return v.reshape(1, -1).astype(F32)


def _pad_lanes(m, width=LANES):
    return jnp.pad(m, ((0, 0), (0, width - m.shape[1])))


def _head_expand_matrix():
    e = np.zeros((LANES, B_HEADS * B_HEAD_DIM), np.float32)
    for hh in range(B_HEADS):
        e[hh, hh * B_HEAD_DIM:(hh + 1) * B_HEAD_DIM] = 1.0
    return np.concatenate([e, e], axis=0)


def _column_select_matrix():
    e = np.zeros((LANES, B_HEADS * CHUNK), np.float32)
    for hh in range(B_HEADS):
        e[hh, hh * CHUNK:(hh + 1) * CHUNK] = 1.0
    return np.concatenate([e, e], axis=0)


def _prep_even(mix_norm, w_in, w_out, ln_g, ln_b, w_s, b_s, conv_w, conv_b, dt_bias, a_log, d_skip, norm_g):
    width_a = ln_g.shape[0]
    inner = norm_g.shape[0]
    conv_dim = conv_b.shape[0]
    o1 = 2 * width_a
    o2 = o1 + inner
    o3 = o2 + conv_dim
    return dict(
        g=_row(mix_norm),
        wa=w_in[:, :o1].astype(BF16),
        wz=w_in[:, o1:o2].astype(BF16),
        wxbc=w_in[:, o2:o3].astype(BF16),
        wdt=_pad_lanes(w_in[:, o3:]).astype(BF16),
        wout=w_out.astype(BF16),
        lng=_row(ln_g), lnb=_row(ln_b), ws=w_s,
        bsb=jnp.broadcast_to(b_s[:, :, None], b_s.shape + (width_a // A_HEADS,)),
        convw=conv_w, convb=_row(conv_b),
        dtb=_pad_lanes(_row(dt_bias)), alog=_pad_lanes(_row(a_log)),
        dskip=_row(jnp.repeat(d_skip, B_HEAD_DIM)), normg=_row(norm_g),
        expand=jnp.asarray(_head_expand_matrix(), BF16), colsel=jnp.asarray(_column_select_matrix(), BF16),
    )


def _prep_odd(mix_norm, w_in, w_out, lin_w, c_scale, q_norm, k_norm, sinks, rel_table):
    d = w_in.shape[0]
    width_c = c_scale.shape[0]
    qw = D_Q_HEADS * D_HEAD_DIM
    kw = D_KV_HEADS * D_HEAD_DIM
    wq = w_in[:, width_c:width_c + qw].reshape(d, D_KV_HEADS, D_GROUP, D_HEAD_DIM)
    wq = wq.transpose(0, 2, 1, 3).reshape(d, qw)
    wo_d = w_out[width_c:].reshape(D_KV_HEADS, D_GROUP, D_HEAD_DIM, -1).transpose(1, 0, 2, 3).reshape(qw, -1)
    ones_bd = np.kron(np.eye(256 // D_HEAD_DIM), np.ones((D_HEAD_DIM, D_HEAD_DIM))).astype(np.float32)
    return dict(
        g=_row(mix_norm),
        wc=w_in[:, :width_c].astype(BF16),
        wq=wq.astype(BF16),
        wk=w_in[:, width_c + qw:width_c + qw + kw].astype(BF16),
        wv=w_in[:, width_c + qw + kw:].astype(BF16),
        wout=jnp.concatenate([w_out[:width_c], wo_d], axis=0).astype(BF16),
        linw=lin_w.astype(BF16), cscale=_row(c_scale),
        qn=_row(jnp.tile(q_norm, D_Q_HEADS)), kn=_row(jnp.tile(k_norm, D_KV_HEADS)),
        onesbd=jnp.asarray(ones_bd, BF16),
        sinks=sinks.astype(F32), rel=rel_table.astype(F32),
    )


def _prep_even_sample(w_s, b_s, steps):
    head_w = CHUNK
    w = jnp.transpose(w_s[:, :steps, :steps], (1, 2, 0)).reshape(steps * steps, A_HEADS)
    b = b_s[:, :steps].T
    return dict(wts=jnp.repeat(w, head_w, axis=1), bts=jnp.repeat(b, head_w, axis=1))


PAST_LEN = 16384
FFN_TILE = 1024
FFN_ROW_BLOCK = 256
FFN_BLOCK_TILES = 3
MIXER_TILE = 256
SAMPLE_ROW_TILE = 32
SAMPLE_SEQ_TILE = 8
SAMPLE_ATTN_UNROLL = 4


def kernel(x_prompt, x_sample, state_ssm, state_conv, state_pool, cache_k_win, cache_v_win,
           ffn1_norm, ffn1_w_gu, ffn1_w_down, mix_norm, ffn2_norm, ffn2_w_gu, ffn2_w_down,
           ev_w_in, ev_w_out, a_ln_g, a_ln_b, a_w_s, a_b_s, b_conv_w, b_conv_b, b_dt_bias, b_a_log,
           b_d_skip, b_norm_g, od_w_in, od_w_out, c_lin_w, c_scale, d_q_norm, d_k_norm, d_sinks,
           rel_bias_table):
    bp, seq, d = x_prompt.shape
    bs, steps, _ = x_sample.shape
    past_len = PAST_LEN
    hp = x_prompt
    hs = x_sample
    depth = ffn1_norm.shape[0]
    names = ("a_v_s", "ssm_p", "ssm_s", "conv_p", "conv_s", "pool_p", "pool_s", "k_p", "k_s", "v_p", "v_s")
    outs = {k: [] for k in names}

    def macaron(h_p, h_s, norm, w_gu, w_down):
        g = _row(norm)
        wgu = w_gu.astype(BF16)
        wd = w_down.astype(BF16)
        h_p = _ffn(h_p.reshape(bp * seq, d), g, wgu, wd, FFN_TILE).reshape(bp, seq, d)
        h_s = _ffn(h_s.reshape(bs * steps, d), g, wgu, wd, min(FFN_TILE, bs * steps)).reshape(bs, steps, d)
        return h_p, h_s

    for layer in range(depth):
        i = layer // 2
        hp, hs = macaron(hp, hs, ffn1_norm[layer], ffn1_w_gu[layer], ffn1_w_down[layer])
        hs2 = hs.reshape(bs, steps * d)
        if layer % 2 == 0:
            p = _prep_even(mix_norm[layer], ev_w_in[i], ev_w_out[i], a_ln_g[i], a_ln_b[i], a_w_s[i], a_b_s[i],
                           b_conv_w[i], b_conv_b[i], b_dt_bias[i], b_a_log[i], b_d_skip[i], b_norm_g[i])
            p.update(_prep_even_sample(a_w_s[i], a_b_s[i], steps))
            hp, conv_p, ssm_p = _even_prompt(hp, p, MIXER_TILE)
            hs2, v_rows, conv_s, ssm_s = _even_sample(
                hs2, state_conv[i].reshape(bs, -1), state_ssm[i].reshape(bs, B_HEADS * B_HEAD_DIM, B_STATE),
                p, steps, SAMPLE_ROW_TILE, SAMPLE_SEQ_TILE)
            outs["a_v_s"].append(v_rows.reshape(bs, steps, -1))
            outs["conv_p"].append(conv_p)
            outs["conv_s"].append(conv_s.reshape(state_conv[i].shape))
            outs["ssm_p"].append(ssm_p.reshape(bp, B_HEADS, B_HEAD_DIM, B_STATE))
            outs["ssm_s"].append(ssm_s.reshape(state_ssm[i].shape))
        else:
            p = _prep_odd(mix_norm[layer], od_w_in[i], od_w_out[i], c_lin_w[i], c_scale[i], d_q_norm[i],
                          d_k_norm[i], d_sinks[i], rel_bias_table)
            hp, pool_p, k_p, v_p = _odd_prompt(hp, p, MIXER_TILE)
            kv_shape = cache_k_win[i].shape
            hs2, pool_s, k_s, v_s = _odd_sample(
                hs2, state_pool[i].reshape(bs, -1), cache_k_win[i].reshape(bs, kv_shape[1], -1),
                cache_v_win[i].reshape(bs, kv_shape[1], -1), p, steps, past_len, SAMPLE_ROW_TILE, SAMPLE_SEQ_TILE)
            outs["pool_p"].append(pool_p)
            outs["pool_s"].append(pool_s.reshape(state_pool[i].shape))
            outs["k_p"].append(k_p.reshape(bp, CHUNK, D_KV_HEADS, D_HEAD_DIM))
            outs["v_p"].append(v_p.reshape(bp, CHUNK, D_KV_HEADS, D_HEAD_DIM))
            outs["k_s"].append(k_s.reshape(kv_shape))
            outs["v_s"].append(v_s.reshape(kv_shape))
        hs = hs2.reshape(bs, steps, d)
        hp, hs = macaron(hp, hs, ffn2_norm[layer], ffn2_w_gu[layer], ffn2_w_down[layer])
    return (hp, hs) + tuple(jnp.stack(outs[k]) for k in names)
```

```python
import functools
import math

import numpy as np
import jax
import jax.numpy as jnp
from jax import lax
from jax.experimental import pallas as pl
from jax.experimental.pallas import tpu as pltpu

F32 = jnp.float32
BF16 = jnp.bfloat16

EPS = 1e-6
NEG = -1e30

LANES = 128
SUBLANES = 8
MXU_DIM = 256
VMEM_BYTES_V7X = 64 * 1024 * 1024
VMEM_LIMIT = VMEM_BYTES_V7X - 8 * 1024 * 1024

A_HEADS = 8
B_HEADS = 16
B_HEAD_DIM = 64
B_GROUPS = 2
B_STATE = 128
B_CONV = 4
CHUNK = 128
C_WINDOWS = (2, 4, 8, 16)
C_HALO = 16
D_Q_HEADS = 16
D_KV_HEADS = 4
D_HEAD_DIM = 64
D_GROUP = D_Q_HEADS // D_KV_HEADS
REL_BUCKETS = 32
REL_MAX_DIST = 128


def _rms(x, g):
    ms = jnp.mean(x * x, axis=-1, keepdims=True)
    return x * lax.rsqrt(ms + EPS) * g


def _sigmoid(x):
    return 1.0 / (1.0 + jnp.exp(-x))


def _silu(x):
    return x * _sigmoid(x)


def _gelu_tanh(x):
    c = math.sqrt(2.0 / math.pi)
    return x * (0.5 * (1.0 + jnp.tanh(c * (x + 0.044715 * (x * x * x)))))


def _softplus(x):
    return jnp.maximum(x, 0.0) + jnp.log1p(jnp.exp(-jnp.abs(x)))


def _split3(x):
    hi = x.astype(BF16)
    r1 = x - hi.astype(F32)
    mid = r1.astype(BF16)
    lo = (r1 - mid.astype(F32)).astype(BF16)
    return hi, mid, lo


def _split2_lanes(x):
    hi = x.astype(BF16)
    lo = (x - hi.astype(F32)).astype(BF16)
    return jnp.concatenate([hi, lo], axis=1)


def _dot(a, b):
    return jnp.dot(a, b, preferred_element_type=F32)


def _dot_nt(a, b):
    return lax.dot_general(a, b, (((1,), (1,)), ((), ())), preferred_element_type=F32)


def _dot_tn(a, b):
    return lax.dot_general(a, b, (((0,), (0,)), ((), ())), preferred_element_type=F32)


def _expand_heads(m, n_pairs):
    rows = m.shape[0]
    lane = lax.broadcasted_iota(jnp.int32, (rows, LANES), 1)
    first = lane < B_HEAD_DIM
    parts = []
    for p in range(n_pairs):
        a = jnp.broadcast_to(m[:, 2 * p:2 * p + 1], (rows, LANES))
        b = jnp.broadcast_to(m[:, 2 * p + 1:2 * p + 2], (rows, LANES))
        parts.append(jnp.where(first, a, b))
    return jnp.concatenate(parts, axis=1)


def _head_sumsq(x, ones_bd):
    xx = x * x
    hi = xx.astype(BF16)
    lo = (xx - hi.astype(F32)).astype(BF16)
    outs = []
    for c in range(x.shape[1] // 256):
        sl = slice(c * 256, (c + 1) * 256)
        outs.append(_dot(hi[:, sl], ones_bd) + _dot(lo[:, sl], ones_bd))
    return jnp.concatenate(outs, axis=1) if len(outs) > 1 else outs[0]


def _ff_blocks(d_ff):
    step = FFN_BLOCK_TILES * MXU_DIM
    return [(c0, min(c0 + step, d_ff)) for c0 in range(0, d_ff, step)]


def _load_cast_rows(src_hbm, dst_ref, stage_ref, sem_ref, rows):
    n = src_hbm.shape[0] // rows

    def copy(c):
        return pltpu.make_async_copy(src_hbm.at[pl.ds(c * rows, rows), :], stage_ref.at[c % 2], sem_ref.at[c % 2])

    copy(0).start()
    for c in range(n):
        if c + 1 < n:
            copy(c + 1).start()
        copy(c).wait()
        dst_ref[c * rows:(c + 1) * rows, :] = stage_ref[c % 2].astype(BF16)


def _ffn_rows(x_ref, g_ref, wgu_ref, wd_ref, o_ref, d_ff):
    tm = x_ref.shape[0]
    rows = [slice(r0, min(r0 + FFN_ROW_BLOCK, tm)) for r0 in range(0, tm, FFN_ROW_BLOCK)]
    xns = [_rms(x_ref[rs, :], g_ref[...]).astype(BF16) for rs in rows]
    for rs, xn in zip(rows, xns):
        y = None
        for c0, c1 in _ff_blocks(d_ff):
            gate = _dot(xn, wgu_ref[:, c0:c1])
            up = _dot(xn, wgu_ref[:, d_ff + c0:d_ff + c1])
            act = (_silu(gate) * up).astype(BF16)
            part = _dot(act, wd_ref[c0:c1, :])
            y = part if y is None else y + part
        o_ref[rs, :] = x_ref[rs, :] + 0.5 * y


def _ffn_kernel(xp_ref, xs_ref, g_ref, wgu_hbm, wd_hbm, op_ref, os_ref,
                wgu_ref, wd_ref, stage_gu_ref, stage_d_ref, sem_ref, *, d_ff, prompt_steps):
    i = pl.program_id(0)

    @pl.when(i == 0)
    def _():
        _load_cast_rows(wgu_hbm, wgu_ref, stage_gu_ref, sem_ref, stage_gu_ref.shape[1])
        _load_cast_rows(wd_hbm, wd_ref, stage_d_ref, sem_ref, stage_d_ref.shape[1])

    @pl.when(i < prompt_steps)
    def _():
        _ffn_rows(xp_ref, g_ref, wgu_ref, wd_ref, op_ref, d_ff)

    @pl.when(i == prompt_steps)
    def _():
        _ffn_rows(xs_ref, g_ref, wgu_ref, wd_ref, os_ref, d_ff)


def _resident(shape):
    nd = len(shape)
    return pl.BlockSpec(shape, lambda *_: (0,) * nd, pipeline_mode=pl.Buffered(1))


def _ffn(xp2d, xs2d, g, wgu, wd, tm):
    m, d = xp2d.shape
    ms = xs2d.shape[0]
    d_ff = wd.shape[0]
    assert m % tm == 0 and d % FFN_STAGE_ROWS_GU == 0 and d_ff % FFN_STAGE_ROWS_D == 0
    steps = m // tm
    last = steps - 1
    whole = lambda shape: pl.BlockSpec(shape, lambda i: (0, 0))
    prompt_tile = pl.BlockSpec((tm, d), lambda i: (jnp.minimum(i, last), 0))
    hbm = pl.BlockSpec(memory_space=pl.ANY)
    return pl.pallas_call(
        functools.partial(_ffn_kernel, d_ff=d_ff, prompt_steps=steps),
        out_shape=(jax.ShapeDtypeStruct((m, d), F32), jax.ShapeDtypeStruct((ms, d), F32)),
        grid=(steps + 1,),
        in_specs=[prompt_tile, whole((ms, d)), _resident((1, d)), hbm, hbm],
        out_specs=(prompt_tile, whole((ms, d))),
        scratch_shapes=[pltpu.VMEM(wgu.shape, BF16), pltpu.VMEM(wd.shape, BF16),
                        pltpu.VMEM((2, FFN_STAGE_ROWS_GU, wgu.shape[1]), F32),
                        pltpu.VMEM((2, FFN_STAGE_ROWS_D, wd.shape[1]), F32),
                        pltpu.SemaphoreType.DMA((2,))],
        compiler_params=pltpu.CompilerParams(
            dimension_semantics=("arbitrary",), vmem_limit_bytes=VMEM_LIMIT),
        name="ffn",
    )(xp2d, xs2d, g, wgu, wd)


def _even_prompt_kernel(h_ref, g_ref, wa_ref, wz_ref, wxbc_ref, wdt_ref, wout_ref,
                        lng_ref, lnb_ref, ws_ref, bsb_ref, convw_ref, convb_ref,
                        dtb_ref, alog_ref, dskip_ref, normg_ref, expand_ref, colsel_ref,
                        o_ref, conv_out_ref, ssm_out_ref,
                        ext_ref, st_ref, *, tile, width_a, inner):
    s = pl.program_id(1)
    n_chunks = tile // CHUNK
    halo = SUBLANES

    @pl.when(s == 0)
    def _():
        ext_ref[0:halo, :] = jnp.zeros((halo, ext_ref.shape[1]), F32)
        st_ref[...] = jnp.zeros(st_ref.shape, F32)

    x = h_ref[...]
    xn = _rms(x, g_ref[...]).astype(BF16)

    row = lax.broadcasted_iota(jnp.int32, (CHUNK, CHUNK), 0)
    col = lax.broadcasted_iota(jnp.int32, (CHUNK, CHUNK), 1)
    causal = row >= col
    lane = lax.broadcasted_iota(jnp.int32, (CHUNK, LANES), 1)
    first_half = lane < B_HEAD_DIM

    pa = _dot(xn, wa_ref[...])
    xbc_raw = _dot(xn, wxbc_ref[...])

    ga = _gelu_tanh(pa)
    u = ga[:, :width_a]
    v = ga[:, width_a:]
    mu = jnp.mean(v, axis=-1, keepdims=True)
    vc = v - mu
    var = jnp.mean(vc * vc, axis=-1, keepdims=True)
    v = vc * lax.rsqrt(var + EPS) * lng_ref[...] + lnb_ref[...]
    vb = v.astype(BF16)

    z = _dot(xn, wz_ref[...])
    dt_raw = _dot(xn, wdt_ref[...])

    ext_ref[halo:halo + tile, :] = xbc_raw
    ext = ext_ref[...]
    ext1 = pltpu.roll(ext, 1, 0)
    pair = ext * convw_ref[1:2, :] + ext1 * convw_ref[0:1, :]
    conv = (convb_ref[...] + ext * convw_ref[3:4, :] + ext1 * convw_ref[2:3, :] + pltpu.roll(pair, 2, 0))[halo:]
    tail = ext_ref[tile:tile + halo, :]
    ext_ref[0:halo, :] = tail
    conv_out_ref[...] = tail
    xbc = _silu(conv)
    gn = B_GROUPS * B_STATE
    xs = xbc[:, :inner]
    bm = xbc[:, inner:inner + gn]
    cm = xbc[:, inner + gn:]

    head_w = width_a // A_HEADS
    gate_cols = []
    for hh in range(A_HEADS):
        w = jnp.where(causal, ws_ref[hh], 0.0).astype(BF16)
        rhs = jnp.concatenate(
            [vb[c * CHUNK:(c + 1) * CHUNK, hh * head_w:(hh + 1) * head_w] for c in range(n_chunks)], axis=1)
        out = _dot(w, rhs)
        bias = bsb_ref[hh]
        gate_cols.append(jnp.concatenate(
            [out[:, c * head_w:(c + 1) * head_w] + bias for c in range(n_chunks)], axis=0))
    ya = u * jnp.concatenate(gate_cols, axis=1)
    out_a = _dot(ya.astype(BF16), wout_ref[0:width_a, :])

    dt = _softplus(dt_raw + dtb_ref[...])
    a_neg = -jnp.exp(alog_ref[...])
    da = dt * a_neg

    n_pairs = B_HEADS // 2
    heads_per_group = B_HEADS // B_GROUPS
    gw = heads_per_group * B_HEAD_DIM
    tril_ones = jnp.where(causal, 1.0, 0.0).astype(BF16)
    chunks = [slice(c * CHUNK, (c + 1) * CHUNK) for c in range(n_chunks)]
    acums = []
    for rs in chunks:
        d_hi, d_mid, d_lo = _split3(da[rs])
        acums.append(_dot(tril_ones, d_hi) + _dot(tril_ones, d_mid) + _dot(tril_ones, d_lo))
    acum = jnp.concatenate(acums, axis=0)
    decay = jnp.concatenate([jnp.exp(a[CHUNK - 1:CHUNK, :] - a) for a in acums], axis=0)
    expand = expand_ref[...]
    xd = xs * _dot(_split2_lanes(dt), expand)
    xdwb = (xs * _dot(_split2_lanes(dt * decay), expand)).astype(BF16)
    e_acum = _dot(_split2_lanes(jnp.exp(acum)), expand)
    bmb = bm.astype(BF16)
    cmb = cm.astype(BF16)
    y_rows = []
    for c, rs in enumerate(chunks):
        a_c = acums[c]
        acum_t = a_c.T
        a_cols = _dot(_split2_lanes(a_c), colsel_ref[...])
        cb = [_dot_nt(cmb[rs, g * B_STATE:(g + 1) * B_STATE], bmb[rs, g * B_STATE:(g + 1) * B_STATE])
              for g in range(B_GROUPS)]
        y_parts = []
        for p in range(n_pairs):
            g = (2 * p) // heads_per_group
            ms = []
            for hh in (2 * p, 2 * p + 1):
                seg = a_cols[:, hh * CHUNK:(hh + 1) * CHUNK] - jnp.broadcast_to(acum_t[hh:hh + 1, :], (CHUNK, CHUNK))
                lmat = jnp.where(causal, jnp.exp(seg), 0.0)
                ms.append((cb[g] * lmat).astype(BF16))
            lhs = jnp.concatenate(ms, axis=1)
            xd_p = xd[rs, p * LANES:(p + 1) * LANES]
            rhs = jnp.concatenate([jnp.where(first_half, xd_p, 0.0),
                                   jnp.where(first_half, 0.0, xd_p)], axis=0).astype(BF16)
            y_parts.append(_dot(lhs, rhs))
        y_rows.append(jnp.concatenate(y_parts, axis=1))
    for c, rs in enumerate(chunks):
        st_prev = st_ref[...]
        stb = st_prev.astype(BF16)
        y_off = jnp.concatenate(
            [_dot(cmb[rs, g * B_STATE:(g + 1) * B_STATE], stb[:, g * gw:(g + 1) * gw]) for g in range(B_GROUPS)],
            axis=1)
        st_add = jnp.concatenate(
            [_dot_tn(bmb[rs, g * B_STATE:(g + 1) * B_STATE], xdwb[rs, g * gw:(g + 1) * gw])
             for g in range(B_GROUPS)], axis=1)
        chunk_decay = e_acum[(c + 1) * CHUNK - 1:(c + 1) * CHUNK, :]
        st_ref[...] = st_prev * chunk_decay + st_add
        y_rows[c] = y_rows[c] + y_off * e_acum[rs]
    y = (jnp.concatenate(y_rows, axis=0) if n_chunks > 1 else y_rows[0]) + xs * dskip_ref[...]
    y = y * _silu(z)
    half = inner // B_GROUPS
    yn = []
    for g in range(B_GROUPS):
        yg = y[:, g * half:(g + 1) * half]
        yn.append(yg * lax.rsqrt(jnp.mean(yg * yg, axis=-1, keepdims=True) + EPS))
    yb = jnp.concatenate(yn, axis=1) * normg_ref[...]

    o_ref[...] = x + out_a + _dot(yb.astype(BF16), wout_ref[width_a:, :])

    @pl.when(s == pl.num_programs(1) - 1)
    def _():
        ssm_out_ref[...] = st_ref[...].T


def _even_prompt(h, p, tile):
    b, seq, d = h.shape
    width_a = p["wa"].shape[1] // 2
    inner = p["wz"].shape[1]
    conv_dim = p["wxbc"].shape[1]
    assert seq % tile == 0 and tile % CHUNK == 0
    small = ["lng", "lnb", "ws", "bsb", "convw", "convb", "dtb", "alog", "dskip", "normg", "expand", "colsel"]
    out, conv_tail, ssm = pl.pallas_call(
        functools.partial(_even_prompt_kernel, tile=tile, width_a=width_a, inner=inner),
        out_shape=(jax.ShapeDtypeStruct((b, seq, d), F32),
                   jax.ShapeDtypeStruct((b, SUBLANES, conv_dim), F32),
                   jax.ShapeDtypeStruct((b, inner, B_STATE), F32)),
        grid=(b, seq // tile),
        in_specs=[pl.BlockSpec((None, tile, d), lambda i, j: (i, j, 0)),
                  _resident((1, d))]
        + [_resident(p[k].shape) for k in ("wa", "wz", "wxbc", "wdt", "wout")]
        + [_resident(p[k].shape) for k in small],
        out_specs=(pl.BlockSpec((None, tile, d), lambda i, j: (i, j, 0)),
                   pl.BlockSpec((None, SUBLANES, conv_dim), lambda i, j: (i, 0, 0)),
                   pl.BlockSpec((None, inner, B_STATE), lambda i, j: (i, 0, 0))),
        scratch_shapes=[pltpu.VMEM((tile + SUBLANES, conv_dim), F32),
                        pltpu.VMEM((B_STATE, inner), F32)],
        compiler_params=pltpu.CompilerParams(
            dimension_semantics=("arbitrary", "arbitrary"), vmem_limit_bytes=VMEM_LIMIT),
        name="even_prompt",
    )(h, p["g"], p["wa"], p["wz"], p["wxbc"], p["wdt"], p["wout"], *[p[k] for k in small])
    return out, conv_tail[:, SUBLANES - (B_CONV - 1):, :], ssm


def _t5_bucket(dist):
    n = np.maximum(dist, 0)
    max_exact = REL_BUCKETS // 2
    n_safe = np.maximum(n, 1).astype(np.float32)
    scale = np.float32((REL_BUCKETS - max_exact) / math.log(REL_MAX_DIST / max_exact))
    large = max_exact + (np.log(n_safe / max_exact) * scale).astype(np.int32)
    large = np.minimum(large, REL_BUCKETS - 1)
    return np.where(n < max_exact, n, large).astype(np.int32)


def _fill_bias(bias_ref, bucket_ref, rel_ref):
    bucket = bucket_ref[...]
    has_prev = lax.broadcasted_iota(jnp.int32, bucket.shape, 1) >= CHUNK
    for hh in range(D_Q_HEADS):
        acc = jnp.full(bucket.shape, NEG, F32)
        for bkt in range(REL_BUCKETS):
            acc = jnp.where(bucket == bkt, rel_ref[bkt, hh], acc)
        bias_ref[0, hh] = acc
        bias_ref[1, hh] = jnp.where(has_prev, acc, NEG)


def _group_attention(qg, kk, vv, bias_ref, sinks_ref, grp, table):
    lq = qg.shape[0]
    lane_kv = lax.broadcasted_iota(jnp.int32, (lq, D_KV_HEADS * D_HEAD_DIM), 1) // D_HEAD_DIM
    zero = jnp.zeros_like(qg)
    lhs = jnp.concatenate([jnp.where(lane_kv == kv, qg, zero) for kv in range(D_KV_HEADS)], axis=0)
    sc = _dot_nt(lhs, kk)
    probs = []
    for kv in range(D_KV_HEADS):
        hh = kv * D_GROUP + grp
        s_h = sc[kv * lq:(kv + 1) * lq] + bias_ref[table, hh]
        sink = sinks_ref[hh]
        m = jnp.maximum(jnp.max(s_h, axis=-1, keepdims=True), sink)
        pexp = jnp.exp(s_h - m)
        denom = jnp.sum(pexp, axis=-1, keepdims=True) + jnp.exp(sink - m)
        probs.append((pexp / denom).astype(BF16))
    ov = _dot(jnp.concatenate(probs, axis=0), vv)
    out = ov[(D_KV_HEADS - 1) * lq:]
    for kv in range(D_KV_HEADS - 2, -1, -1):
        out = jnp.where(lane_kv == kv, ov[kv * lq:(kv + 1) * lq], out)
    return out


def _odd_prompt_kernel(h_ref, g_ref, wc_ref, wq_ref, wk_ref, wv_ref, wout_ref,
                       linw_ref, cscale_ref, qn_ref, kn_ref, onesbd_ref, bucket_ref,
                       sinks_ref, rel_ref,
                       o_ref, pool_out_ref, k_out_ref, v_out_ref,
                       extc_ref, kprev_ref, vprev_ref, bias_ref, *, tile, width_c):
    b = pl.program_id(0)
    s = pl.program_id(1)
    n_blocks = tile // CHUNK

    @pl.when((b == 0) & (s == 0))
    def _():
        _fill_bias(bias_ref, bucket_ref, rel_ref)

    @pl.when(s == 0)
    def _():
        extc_ref[0:C_HALO, :] = jnp.zeros((C_HALO, width_c), F32)
        kprev_ref[...] = jnp.zeros(kprev_ref.shape, F32)
        vprev_ref[...] = jnp.zeros(vprev_ref.shape, F32)

    x = h_ref[...]
    xn = _rms(x, g_ref[...]).astype(BF16)

    c_in = _dot(xn, wc_ref[...])
    extc_ref[C_HALO:C_HALO + tile, :] = c_in
    e = extc_ref[...]
    tail = extc_ref[tile:tile + C_HALO, :]
    extc_ref[0:C_HALO, :] = tail
    pool_out_ref[...] = tail
    pos = (s * tile + lax.broadcasted_iota(jnp.int32, (tile, 1), 0) + 1).astype(F32)
    gdim = width_c // len(C_WINDOWS)
    run = e
    shift = 1
    yc = []
    for gi, win in enumerate(C_WINDOWS):
        while shift < win:
            run = run + pltpu.roll(run, shift, 0)
            shift *= 2
        cnt = jnp.minimum(pos, float(win))
        pooled = run[C_HALO:, :gdim] / cnt - c_in[:, gi * gdim:(gi + 1) * gdim]
        yc.append(_dot(pooled.astype(BF16), linw_ref[gi]))
        if gi + 1 < len(C_WINDOWS):
            run = run[:, gdim:]
    yc = jnp.concatenate(yc, axis=1) * cscale_ref[...]

    q = _dot(xn, wq_ref[...])
    k = _dot(xn, wk_ref[...])
    v = _dot(xn, wv_ref[...])
    ones_bd = onesbd_ref[...]
    inv_d = 1.0 / D_HEAD_DIM
    qn = q * lax.rsqrt(_head_sumsq(q, ones_bd) * inv_d + EPS) * qn_ref[...]
    kn = k * lax.rsqrt(_head_sumsq(k, ones_bd) * inv_d + EPS) * kn_ref[...]
    qs = (qn * (D_HEAD_DIM ** -0.5)).astype(BF16)
    kb = kn.astype(BF16)
    vb = v.astype(BF16)
    first_table = jnp.where(s == 0, 1, 0)
    gw = D_KV_HEADS * D_HEAD_DIM
    o_rows = []
    for blk in range(n_blocks):
        rs = slice(blk * CHUNK, (blk + 1) * CHUNK)
        if blk == 0:
            k_prev, v_prev = kprev_ref[...].astype(BF16), vprev_ref[...].astype(BF16)
        else:
            k_prev, v_prev = kb[(blk - 1) * CHUNK:blk * CHUNK], vb[(blk - 1) * CHUNK:blk * CHUNK]
        kk = jnp.concatenate([k_prev, kb[rs]], axis=0)
        vv = jnp.concatenate([v_prev, vb[rs]], axis=0)
        o_rows.append(jnp.concatenate(
            [_group_attention(qs[rs, grp * gw:(grp + 1) * gw], kk, vv, bias_ref, sinks_ref, grp,
                              first_table if blk == 0 else 0)
             for grp in range(D_GROUP)], axis=1))
    kprev_ref[...] = kn[tile - CHUNK:]
    vprev_ref[...] = v[tile - CHUNK:]
    yd = jnp.concatenate(o_rows, axis=0) if n_blocks > 1 else o_rows[0]

    mix = jnp.concatenate([yc, yd], axis=1).astype(BF16)
    o_ref[...] = x + _dot(mix, wout_ref[...])

    @pl.when(s == pl.num_programs(1) - 1)
    def _():
        k_out_ref[...] = kn[tile - CHUNK:]
        v_out_ref[...] = v[tile - CHUNK:]


def _odd_prompt(h, p, tile):
    b, seq, d = h.shape
    width_c = p["wc"].shape[1]
    kvw = p["wk"].shape[1]
    assert seq % tile == 0 and tile % CHUNK == 0
    r = np.arange(CHUNK) + CHUNK
    c = np.arange(2 * CHUNK)
    dist = r[:, None] - c[None, :]
    bucket = np.where((dist >= 0) & (dist < CHUNK), _t5_bucket(dist), -1).astype(np.int32)
    vm = ["linw", "cscale", "qn", "kn", "onesbd"]
    smem = pl.BlockSpec(memory_space=pltpu.SMEM)
    out, pool_tail, k_win, v_win = pl.pallas_call(
        functools.partial(_odd_prompt_kernel, tile=tile, width_c=width_c),
        out_shape=(jax.ShapeDtypeStruct((b, seq, d), F32),
                   jax.ShapeDtypeStruct((b, C_HALO, width_c), F32),
                   jax.ShapeDtypeStruct((b, CHUNK, kvw), F32),
                   jax.ShapeDtypeStruct((b, CHUNK, kvw), F32)),
        grid=(b, seq // tile),
        in_specs=[pl.BlockSpec((None, tile, d), lambda i, j: (i, j, 0)),
                  _resident((1, d))]
        + [_resident(p[k].shape) for k in ("wc", "wq", "wk", "wv", "wout")]
        + [_resident(p[k].shape) for k in vm]
        + [_resident(bucket.shape), smem, smem],
        out_specs=(pl.BlockSpec((None, tile, d), lambda i, j: (i, j, 0)),
                   pl.BlockSpec((None, C_HALO, width_c), lambda i, j: (i, 0, 0)),
                   pl.BlockSpec((None, CHUNK, kvw), lambda i, j: (i, 0, 0)),
                   pl.BlockSpec((None, CHUNK, kvw), lambda i, j: (i, 0, 0))),
        scratch_shapes=[pltpu.VMEM((tile + C_HALO, width_c), F32),
                        pltpu.VMEM((CHUNK, kvw), F32),
                        pltpu.VMEM((CHUNK, kvw), F32),
                        pltpu.VMEM((2, D_Q_HEADS, CHUNK, 2 * CHUNK), F32)],
        compiler_params=pltpu.CompilerParams(
            dimension_semantics=("arbitrary", "arbitrary"), vmem_limit_bytes=VMEM_LIMIT),
        name="odd_prompt",
    )(h, p["g"], p["wc"], p["wq"], p["wk"], p["wv"], p["wout"], *[p[k] for k in vm],
      jnp.asarray(bucket), p["sinks"], p["rel"])
    return out, pool_tail[:, C_HALO - (max(C_WINDOWS) - 1):, :], k_win, v_win


def _steps(x, n, width):
    return [x[:, t * width:(t + 1) * width] for t in range(n)]


def _stack_steps(ref, n, width):
    x = ref[...]
    return jnp.concatenate(_steps(x, n, width), axis=0)


def _even_sample_front_kernel(hs_ref, g_ref, wa_ref, wz_ref, wxbc_ref, wdt_ref,
                              lng_ref, lnb_ref, wts_ref, bts_ref, convw_ref, convb_ref,
                              dtb_ref, alog_ref, dskip_ref, cs_ref,
                              v_out, ya_out, ypart_out, eacum_out, z_out, conv_out,
                              cgt_out, xdw_out, bs_out, dec_out, *, steps, d, width_a, inner, conv_dim):
    bt = hs_ref.shape[0]
    xn = _rms(_stack_steps(hs_ref, steps, d), g_ref[...]).astype(BF16)
    blk = lambda a, t: a[t * bt:(t + 1) * bt]

    ga = _gelu_tanh(_dot(xn, wa_ref[...]))
    u = ga[:, :width_a]
    v = ga[:, width_a:]
    mu = jnp.mean(v, axis=-1, keepdims=True)
    vc = v - mu
    var = jnp.mean(vc * vc, axis=-1, keepdims=True)
    v = vc * lax.rsqrt(var + EPS) * lng_ref[...] + lnb_ref[...]
    for t in range(steps):
        v_out[:, t * width_a:(t + 1) * width_a] = blk(v, t)
        gate = bts_ref[t:t + 1, :]
        for s in range(t + 1):
            gate = gate + wts_ref[t * steps + s:t * steps + s + 1, :] * blk(v, s)
        ya_out[:, t * width_a:(t + 1) * width_a] = blk(u, t) * gate

    z = _dot(xn, wz_ref[...])
    for t in range(steps):
        z_out[:, t * inner:(t + 1) * inner] = blk(z, t)
    raw = _dot(xn, wxbc_ref[...])
    dt = _softplus(_dot(xn, wdt_ref[...]) + dtb_ref[...])
    ext = _steps(cs_ref[...], B_CONV - 1, conv_dim) + [blk(raw, t) for t in range(steps)]
    for k in range(B_CONV - 1):
        conv_out[:, k * conv_dim:(k + 1) * conv_dim] = ext[len(ext) - (B_CONV - 1) + k]
    gn = B_GROUPS * B_STATE
    n_pairs = B_HEADS // 2
    a_neg = -jnp.exp(alog_ref[...])
    xs, bm, cm, dts, acum = [], [], [], [], []
    for t in range(steps):
        conv = convb_ref[...]
        for tap in range(B_CONV):
            conv = conv + ext[t + tap] * convw_ref[tap:tap + 1, :]
        xbc = _silu(conv)
        xs.append(xbc[:, :inner])
        bm.append(xbc[:, inner:inner + gn])
        cm.append(xbc[:, inner + gn:])
        dts.append(blk(dt, t))
        da = dts[t] * a_neg
        acum.append(da if t == 0 else acum[t - 1] + da)
    lane = lax.broadcasted_iota(jnp.int32, (bt, LANES), 1)
    group0 = lane < (B_HEADS // B_GROUPS)
    dec_out[...] = jnp.exp(acum[steps - 1])
    pad_rows = SUBLANES - steps
    xdw_out[:, steps * inner:] = jnp.zeros((bt, pad_rows * inner), F32)
    bs_out[:, steps * gn:] = jnp.zeros((bt, pad_rows * gn), F32)
    xd = []
    for t in range(steps):
        xd.append(xs[t] * _expand_heads(dts[t], n_pairs))
        eacum_out[:, t * inner:(t + 1) * inner] = _expand_heads(jnp.exp(acum[t]), n_pairs)
        xdw_out[:, t * inner:(t + 1) * inner] = xs[t] * _expand_heads(
            dts[t] * jnp.exp(acum[steps - 1] - acum[t]), n_pairs)
        bs_out[:, t * gn:(t + 1) * gn] = bm[t]
        for g in range(B_GROUPS):
            r = g * steps + t
            cgt_out[:, r * B_STATE:(r + 1) * B_STATE] = cm[t][:, g * B_STATE:(g + 1) * B_STATE]
    for t in range(steps):
        y = xs[t] * dskip_ref[...]
        for s in range(t + 1):
            cb = [jnp.sum(cm[t][:, g * B_STATE:(g + 1) * B_STATE] * bm[s][:, g * B_STATE:(g + 1) * B_STATE],
                          axis=-1, keepdims=True) for g in range(B_GROUPS)]
            coef = jnp.where(group0, cb[0], cb[1]) * jnp.exp(acum[t] - acum[s])
            y = y + _expand_heads(coef, n_pairs) * xd[s]
        ypart_out[:, t * inner:(t + 1) * inner] = y


def _even_sample_state_kernel(s0_ref, cgt_ref, xdw_ref, bs_ref, dec_ref, yoff_ref, snew_ref, *, bb):
    step = pl.program_id(0)
    gw = (B_HEADS // B_GROUPS) * B_HEAD_DIM

    def body(bi, carry):
        s0 = s0_ref[bi]
        c8 = cgt_ref[bi]
        c16 = jnp.concatenate([c8, jnp.zeros_like(c8)], axis=0).astype(BF16)
        yoff_ref[bi] = _dot_nt(c16, s0.astype(BF16))[:SUBLANES]
        x8 = xdw_ref[bi]
        b8 = bs_ref[bi]
        x16 = jnp.concatenate([x8, jnp.zeros_like(x8)], axis=0).astype(BF16)
        b16 = jnp.concatenate([b8, jnp.zeros_like(b8)], axis=0).astype(BF16)
        for g in range(B_GROUPS):
            add = _dot_tn(x16[:, g * gw:(g + 1) * gw], b16[:, g * B_STATE:(g + 1) * B_STATE])
            for hl in range(B_HEADS // B_GROUPS):
                hh = g * (B_HEADS // B_GROUPS) + hl
                rs = slice(hh * B_HEAD_DIM, (hh + 1) * B_HEAD_DIM)
                snew_ref[bi, rs, :] = s0[rs] * dec_ref[step * bb + bi, hh] + \
                    add[hl * B_HEAD_DIM:(hl + 1) * B_HEAD_DIM]
        return carry

    lax.fori_loop(0, bb, body, 0, unroll=2)


def _even_sample_back_kernel(hs_ref, ya_ref, ypart_ref, eacum_ref, z_ref, yoff_ref, normg_ref, wout_ref,
                             o_ref, *, steps, d, inner):
    bt = hs_ref.shape[0]
    half = inner // B_GROUPS
    mixes = []
    for t in range(steps):
        sl = slice(t * inner, (t + 1) * inner)
        yoff = jnp.concatenate(
            [yoff_ref[:, (g * steps + t) * inner + g * half:(g * steps + t) * inner + (g + 1) * half]
             for g in range(B_GROUPS)], axis=1)
        y = (ypart_ref[:, sl] + yoff * eacum_ref[:, sl]) * _silu(z_ref[:, sl])
        yn = []
        for g in range(B_GROUPS):
            yg = y[:, g * half:(g + 1) * half]
            yn.append(yg * lax.rsqrt(jnp.mean(yg * yg, axis=-1, keepdims=True) + EPS))
        yb = jnp.concatenate(yn, axis=1) * normg_ref[...]
        mixes.append(jnp.concatenate([ya_ref[:, t * d:(t + 1) * d], yb], axis=1))
    out = _dot(jnp.concatenate(mixes, axis=0).astype(BF16), wout_ref[...])
    for t in range(steps):
        o_ref[:, t * d:(t + 1) * d] = hs_ref[:, t * d:(t + 1) * d] + out[t * bt:(t + 1) * bt]


def _row_tiled(width, bt):
    return pl.BlockSpec((bt, width), lambda i: (i, 0))


def _even_sample(hs2, state_conv2, state_ssm3, p, steps, bt, bb):
    nb, _ = hs2.shape
    d = p["wa"].shape[0]
    width_a = p["wa"].shape[1] // 2
    inner = p["wz"].shape[1]
    conv_dim = p["wxbc"].shape[1]
    gn = B_GROUPS * B_STATE
    assert nb % bt == 0 and nb % bb == 0 and steps <= SUBLANES
    params = pltpu.CompilerParams(dimension_semantics=("arbitrary",), vmem_limit_bytes=VMEM_LIMIT)
    small = ["lng", "lnb", "wts", "bts", "convw", "convb", "dtb", "alog", "dskip"]
    widths = dict(v=steps * width_a, ya=steps * width_a, ypart=steps * inner, eacum=steps * inner,
                  z=steps * inner, conv=(B_CONV - 1) * conv_dim, cgt=SUBLANES * B_STATE,
                  xdw=SUBLANES * inner, bs=SUBLANES * gn, dec=LANES)
    front = pl.pallas_call(
        functools.partial(_even_sample_front_kernel, steps=steps, d=d, width_a=width_a, inner=inner,
                          conv_dim=conv_dim),
        out_shape=tuple(jax.ShapeDtypeStruct((nb, w), F32) for w in widths.values()),
        grid=(nb // bt,),
        in_specs=[_row_tiled(steps * d, bt), _resident((1, d))]
        + [_resident(p[k].shape) for k in ("wa", "wz", "wxbc", "wdt")]
        + [_resident(p[k].shape) for k in small]
        + [_row_tiled((B_CONV - 1) * conv_dim, bt)],
        out_specs=tuple(_row_tiled(w, bt) for w in widths.values()),
        compiler_params=params,
        name="even_sample_front",
    )(hs2, p["g"], p["wa"], p["wz"], p["wxbc"], p["wdt"], *[p[k] for k in small], state_conv2)
    v_rows, ya, ypart, eacum, z, new_conv, cgt, xdw, bs, dec = front

    hp = state_ssm3.shape[1]
    tile3 = lambda rows, width: pl.BlockSpec((bb, rows, width), lambda i: (i, 0, 0))
    yoff, new_ssm = pl.pallas_call(
        functools.partial(_even_sample_state_kernel, bb=bb),
        out_shape=(jax.ShapeDtypeStruct((nb, SUBLANES, hp), F32),
                   jax.ShapeDtypeStruct(state_ssm3.shape, F32)),
        grid=(nb // bb,),
        in_specs=[tile3(hp, B_STATE), tile3(SUBLANES, B_STATE), tile3(SUBLANES, inner), tile3(SUBLANES, gn),
                  pl.BlockSpec(memory_space=pltpu.SMEM)],
        out_specs=(tile3(SUBLANES, hp), tile3(hp, B_STATE)),
        compiler_params=params,
        name="even_sample_state",
    )(state_ssm3, cgt.reshape(nb, SUBLANES, B_STATE), xdw.reshape(nb, SUBLANES, inner),
      bs.reshape(nb, SUBLANES, gn), dec[:, :B_HEADS])

    out = pl.pallas_call(
        functools.partial(_even_sample_back_kernel, steps=steps, d=d, inner=inner),
        out_shape=jax.ShapeDtypeStruct(hs2.shape, F32),
        grid=(nb // bt,),
        in_specs=[_row_tiled(steps * d, bt), _row_tiled(steps * width_a, bt), _row_tiled(steps * inner, bt),
                  _row_tiled(steps * inner, bt), _row_tiled(steps * inner, bt), _row_tiled(SUBLANES * hp, bt),
                  _resident((1, inner)), _resident(p["wout"].shape)],
        out_specs=_row_tiled(steps * d, bt),
        compiler_params=params,
        name="even_sample_back",
    )(hs2, ya, ypart, eacum, z, yoff.reshape(nb, SUBLANES * hp), p["normg"], p["wout"])
    return out, v_rows, new_conv, new_ssm


def _odd_sample_front_kernel(hs_ref, g_ref, wc_ref, wq_ref, wk_ref, wv_ref, linw_ref, cscale_ref,
                             qn_ref, kn_ref, onesbd_ref, ps_ref,
                             yc_out, pool_out, q_out, knew_out, vnew_out, *, steps, d, width_c, past_len):
    bt = hs_ref.shape[0]
    xn = _rms(_stack_steps(hs_ref, steps, d), g_ref[...]).astype(BF16)
    blk = lambda a, t: a[t * bt:(t + 1) * bt]
    c_in = _dot(xn, wc_ref[...])
    n_state = max(C_WINDOWS) - 1
    ext = _steps(ps_ref[...], n_state, width_c) + [blk(c_in, t) for t in range(steps)]
    for j in range(n_state):
        pool_out[:, j * width_c:(j + 1) * width_c] = ext[len(ext) - n_state + j]
    gdim = width_c // len(C_WINDOWS)
    yc_cols = []
    for gi, win in enumerate(C_WINDOWS):
        sl = slice(gi * gdim, (gi + 1) * gdim)
        pooled = []
        for t in range(steps):
            hi = n_state + t
            lo = max(hi - win + 1, 0)
            acc = ext[lo][:, sl]
            for j in range(lo + 1, hi + 1):
                acc = acc + ext[j][:, sl]
            count = float(min(past_len + t + 1, win))
            pooled.append(acc / count - ext[hi][:, sl])
        yc_cols.append(_dot(jnp.concatenate(pooled, axis=0).astype(BF16), linw_ref[gi]))
    yc = jnp.concatenate(yc_cols, axis=1) * cscale_ref[...]
    for t in range(steps):
        yc_out[:, t * width_c:(t + 1) * width_c] = blk(yc, t)

    q = _dot(xn, wq_ref[...])
    k = _dot(xn, wk_ref[...])
    v = _dot(xn, wv_ref[...])
    ones_bd = onesbd_ref[...]
    inv_d = 1.0 / D_HEAD_DIM
    qn = q * lax.rsqrt(_head_sumsq(q, ones_bd) * inv_d + EPS) * qn_ref[...] * (D_HEAD_DIM ** -0.5)
    kn = k * lax.rsqrt(_head_sumsq(k, ones_bd) * inv_d + EPS) * kn_ref[...]
    qw = q.shape[1]
    kw = k.shape[1]
    pad = SUBLANES - steps
    q_out[:, steps * qw:] = jnp.zeros((bt, pad * qw), F32)
    knew_out[:, :pad * kw] = jnp.zeros((bt, pad * kw), F32)
    vnew_out[:, :pad * kw] = jnp.zeros((bt, pad * kw), F32)
    for t in range(steps):
        q_out[:, t * qw:(t + 1) * qw] = blk(qn, t)
        knew_out[:, (pad + t) * kw:(pad + t + 1) * kw] = blk(kn, t)
        vnew_out[:, (pad + t) * kw:(pad + t + 1) * kw] = blk(v, t)


SINK_BUCKET = REL_BUCKETS


def _odd_sample_attn_kernel(q_ref, knew_ref, vnew_ref, ck_ref, cv_ref, bucket_ref, sinks_ref, rel_ref,
                            o_ref, kout_ref, vout_ref, bias_ref, *, bb, steps, n_keys):
    win = ck_ref.shape[1]
    kvw = ck_ref.shape[2]
    tile16 = 2 * SUBLANES

    @pl.when(pl.program_id(0) == 0)
    def _():
        bucket = bucket_ref[...]
        for hh in range(D_Q_HEADS):
            acc = jnp.full(bucket.shape, NEG, F32)
            for bkt in range(REL_BUCKETS):
                acc = jnp.where(bucket == bkt, rel_ref[bkt, hh], acc)
            acc = jnp.where(bucket == SINK_BUCKET, sinks_ref[hh], acc)
            bias_ref[hh * SUBLANES:(hh + 1) * SUBLANES, :] = acc

    sub = lax.broadcasted_iota(jnp.int32, (SUBLANES, kvw), 0)
    new_rows = sub >= SUBLANES - steps
    lane_kv = lax.broadcasted_iota(jnp.int32, (SUBLANES, kvw), 1) // D_HEAD_DIM
    gw = D_KV_HEADS * D_HEAD_DIM
    zero_keys = jnp.zeros((n_keys - win - tile16, kvw), BF16)

    def extend(cache, new8):
        new16 = jnp.concatenate([new8, jnp.zeros_like(new8)], axis=0).astype(BF16)
        return jnp.concatenate([cache.astype(BF16), new16, zero_keys], axis=0)

    def shift_in(cache, new8, out_ref, bi):
        rolled = pltpu.roll(cache, win - steps, 0)
        out_ref[bi, 0:win - SUBLANES, :] = rolled[:win - SUBLANES]
        out_ref[bi, win - SUBLANES:, :] = jnp.where(new_rows, new8, rolled[win - SUBLANES:])

    def body(bi, carry):
        ck = ck_ref[bi]
        cv = cv_ref[bi]
        k8 = knew_ref[bi]
        v8 = vnew_ref[bi]
        shift_in(ck, k8, kout_ref, bi)
        shift_in(cv, v8, vout_ref, bi)
        q8 = q_ref[bi]
        pieces = []
        for kv in range(D_KV_HEADS):
            for grp in range(D_GROUP):
                qg = q8[:, grp * gw:(grp + 1) * gw]
                pieces.append(jnp.where(lane_kv == kv, qg, 0.0))
        lhs = jnp.concatenate(pieces, axis=0).astype(BF16)
        sc = _dot_nt(lhs, extend(ck, k8)) + bias_ref[...]
        m = jnp.max(sc, axis=-1, keepdims=True)
        pexp = jnp.exp(sc - m)
        probs = (pexp / jnp.sum(pexp, axis=-1, keepdims=True)).astype(BF16)
        ov = _dot(probs, extend(cv, v8))
        outs = []
        for grp in range(D_GROUP):
            r_last = ((D_KV_HEADS - 1) * D_GROUP + grp) * SUBLANES
            acc = ov[r_last:r_last + SUBLANES]
            for kv in range(D_KV_HEADS - 2, -1, -1):
                r0 = (kv * D_GROUP + grp) * SUBLANES
                acc = jnp.where(lane_kv == kv, ov[r0:r0 + SUBLANES], acc)
            outs.append(acc)
        o_ref[bi] = jnp.concatenate(outs, axis=1)
        return carry

    lax.fori_loop(0, bb, body, 0, unroll=SAMPLE_ATTN_UNROLL)


def _odd_sample_back_kernel(hs_ref, yc_ref, o_ref_in, wout_ref, out_ref, *, steps, d, width_c, qw):
    bt = hs_ref.shape[0]
    mix = jnp.concatenate(
        [jnp.concatenate([yc_ref[:, t * width_c:(t + 1) * width_c], o_ref_in[:, t * qw:(t + 1) * qw]], axis=1)
         for t in range(steps)], axis=0).astype(BF16)
    out = _dot(mix, wout_ref[...])
    for t in range(steps):
        out_ref[:, t * d:(t + 1) * d] = hs_ref[:, t * d:(t + 1) * d] + out[t * bt:(t + 1) * bt]


def _odd_sample(hs2, state_pool2, cache_k3, cache_v3, p, steps, past_len, bt, bb):
    nb, _ = hs2.shape
    d = p["wc"].shape[0]
    width_c = p["wc"].shape[1]
    qw = p["wq"].shape[1]
    kw = p["wk"].shape[1]
    win = cache_k3.shape[1]
    n_state = max(C_WINDOWS) - 1
    assert nb % bt == 0 and nb % bb == 0 and steps <= SUBLANES and win == CHUNK
    params = pltpu.CompilerParams(dimension_semantics=("arbitrary",), vmem_limit_bytes=VMEM_LIMIT)
    vm = ["linw", "cscale", "qn", "kn", "onesbd"]
    widths = dict(yc=steps * width_c, pool=n_state * width_c, q=SUBLANES * qw, knew=SUBLANES * kw,
                  vnew=SUBLANES * kw)
    yc, new_pool, q8, knew8, vnew8 = pl.pallas_call(
        functools.partial(_odd_sample_front_kernel, steps=steps, d=d, width_c=width_c, past_len=past_len),
        out_shape=tuple(jax.ShapeDtypeStruct((nb, w), F32) for w in widths.values()),
        grid=(nb // bt,),
        in_specs=[_row_tiled(steps * d, bt), _resident((1, d))]
        + [_resident(p[k].shape) for k in ("wc", "wq", "wk", "wv")]
        + [_resident(p[k].shape) for k in vm]
        + [_row_tiled(n_state * width_c, bt)],
        out_specs=tuple(_row_tiled(w, bt) for w in widths.values()),
        compiler_params=params,
        name="odd_sample_front",
    )(hs2, p["g"], p["wc"], p["wq"], p["wk"], p["wv"], *[p[k] for k in vm], state_pool2)

    n_keys = 2 * CHUNK
    pad = SUBLANES - steps
    bucket = np.full((SUBLANES, n_keys), -1, np.int32)
    for t in range(steps):
        q_pos = past_len + t
        k_pos = np.full(n_keys, -10 ** 9, np.int64)
        k_pos[:win] = past_len - win + np.arange(win)
        k_pos[win + pad:win + SUBLANES] = past_len + np.arange(steps)
        dist = q_pos - k_pos
        ok = (dist >= 0) & (dist < CHUNK) & (k_pos >= 0)
        bucket[t] = np.where(ok, _t5_bucket(np.where(ok, dist, 0)), -1)
    bucket[:, n_keys - 1] = SINK_BUCKET
    smem = pl.BlockSpec(memory_space=pltpu.SMEM)
    tile3 = lambda rows, width: pl.BlockSpec((bb, rows, width), lambda i: (i, 0, 0))
    o8, new_k, new_v = pl.pallas_call(
        functools.partial(_odd_sample_attn_kernel, bb=bb, steps=steps, n_keys=n_keys),
        out_shape=(jax.ShapeDtypeStruct((nb, SUBLANES, qw), F32),
                   jax.ShapeDtypeStruct(cache_k3.shape, F32),
                   jax.ShapeDtypeStruct(cache_v3.shape, F32)),
        grid=(nb // bb,),
        in_specs=[tile3(SUBLANES, qw), tile3(SUBLANES, kw), tile3(SUBLANES, kw), tile3(win, kw), tile3(win, kw),
                  _resident(bucket.shape), smem, smem],
        out_specs=(tile3(SUBLANES, qw), tile3(win, kw), tile3(win, kw)),
        scratch_shapes=[pltpu.VMEM((D_Q_HEADS * SUBLANES, n_keys), F32)],
        compiler_params=params,
        name="odd_sample_attn",
    )(q8.reshape(nb, SUBLANES, qw), knew8.reshape(nb, SUBLANES, kw), vnew8.reshape(nb, SUBLANES, kw),
      cache_k3, cache_v3, jnp.asarray(bucket), p["sinks"], p["rel"])

    out = pl.pallas_call(
        functools.partial(_odd_sample_back_kernel, steps=steps, d=d, width_c=width_c, qw=qw),
        out_shape=jax.ShapeDtypeStruct(hs2.shape, F32),
        grid=(nb // bt,),
        in_specs=[_row_tiled(steps * d, bt), _row_tiled(steps * width_c, bt), _row_tiled(SUBLANES * qw, bt),
                  _resident(p["wout"].shape)],
        out_specs=_row_tiled(steps * d, bt),
        compiler_params=params,
        name="odd_sample_back",
    )(hs2, yc, o8.reshape(nb, SUBLANES * qw), p["wout"])
    return out, new_pool, new_k, new_v


def _row(v):
    return v.reshape(1, -1).astype(F32)


def _pad_lanes(m, width=LANES):
    return jnp.pad(m, ((0, 0), (0, width - m.shape[1])))


def _head_expand_matrix():
    e = np.zeros((LANES, B_HEADS * B_HEAD_DIM), np.float32)
    for hh in range(B_HEADS):
        e[hh, hh * B_HEAD_DIM:(hh + 1) * B_HEAD_DIM] = 1.0
    return np.concatenate([e, e], axis=0)


def _column_select_matrix():
    e = np.zeros((LANES, B_HEADS * CHUNK), np.float32)
    for hh in range(B_HEADS):
        e[hh, hh * CHUNK:(hh + 1) * CHUNK] = 1.0
    return np.concatenate([e, e], axis=0)


def _prep_even(mix_norm, w_in, w_out, ln_g, ln_b, w_s, b_s, conv_w, conv_b, dt_bias, a_log, d_skip, norm_g):
    width_a = ln_g.shape[0]
    inner = norm_g.shape[0]
    conv_dim = conv_b.shape[0]
    o1 = 2 * width_a
    o2 = o1 + inner
    o3 = o2 + conv_dim
    return dict(
        g=_row(mix_norm),
        wa=w_in[:, :o1].astype(BF16),
        wz=w_in[:, o1:o2].astype(BF16),
        wxbc=w_in[:, o2:o3].astype(BF16),
        wdt=_pad_lanes(w_in[:, o3:]).astype(BF16),
        wout=w_out.astype(BF16),
        lng=_row(ln_g), lnb=_row(ln_b), ws=w_s,
        bsb=jnp.broadcast_to(b_s[:, :, None], b_s.shape + (width_a // A_HEADS,)),
        convw=conv_w, convb=_row(conv_b),
        dtb=_pad_lanes(_row(dt_bias)), alog=_pad_lanes(_row(a_log)),
        dskip=_row(jnp.repeat(d_skip, B_HEAD_DIM)), normg=_row(norm_g),
        expand=jnp.asarray(_head_expand_matrix(), BF16), colsel=jnp.asarray(_column_select_matrix(), BF16),
    )


def _prep_odd(mix_norm, w_in, w_out, lin_w, c_scale, q_norm, k_norm, sinks, rel_table):
    d = w_in.shape[0]
    width_c = c_scale.shape[0]
    qw = D_Q_HEADS * D_HEAD_DIM
    kw = D_KV_HEADS * D_HEAD_DIM
    wq = w_in[:, width_c:width_c + qw].reshape(d, D_KV_HEADS, D_GROUP, D_HEAD_DIM)
    wq = wq.transpose(0, 2, 1, 3).reshape(d, qw)
    wo_d = w_out[width_c:].reshape(D_KV_HEADS, D_GROUP, D_HEAD_DIM, -1).transpose(1, 0, 2, 3).reshape(qw, -1)
    ones_bd = np.kron(np.eye(256 // D_HEAD_DIM), np.ones((D_HEAD_DIM, D_HEAD_DIM))).astype(np.float32)
    return dict(
        g=_row(mix_norm),
        wc=w_in[:, :width_c].astype(BF16),
        wq=wq.astype(BF16),
        wk=w_in[:, width_c + qw:width_c + qw + kw].astype(BF16),
        wv=w_in[:, width_c + qw + kw:].astype(BF16),
        wout=jnp.concatenate([w_out[:width_c], wo_d], axis=0).astype(BF16),
        linw=lin_w.astype(BF16), cscale=_row(c_scale),
        qn=_row(jnp.tile(q_norm, D_Q_HEADS)), kn=_row(jnp.tile(k_norm, D_KV_HEADS)),
        onesbd=jnp.asarray(ones_bd, BF16),
        sinks=sinks.astype(F32), rel=rel_table.astype(F32),
    )


def _prep_even_sample(w_s, b_s, steps):
    head_w = CHUNK
    w = jnp.transpose(w_s[:, :steps, :steps], (1, 2, 0)).reshape(steps * steps, A_HEADS)
    b = b_s[:, :steps].T
    return dict(wts=jnp.repeat(w, head_w, axis=1), bts=jnp.repeat(b, head_w, axis=1))


PAST_LEN = 16384
FFN_TILE = 1024
FFN_STAGE_ROWS_GU = 64
FFN_STAGE_ROWS_D = 256
FFN_ROW_BLOCK = 256
FFN_BLOCK_TILES = 3
MIXER_TILE = 256
SAMPLE_ROW_TILE = 32
SAMPLE_SEQ_TILE = 8
SAMPLE_ATTN_UNROLL = 4


def kernel(x_prompt, x_sample, state_ssm, state_conv, state_pool, cache_k_win, cache_v_win,
           ffn1_norm, ffn1_w_gu, ffn1_w_down, mix_norm, ffn2_norm, ffn2_w_gu, ffn2_w_down,
           ev_w_in, ev_w_out, a_ln_g, a_ln_b, a_w_s, a_b_s, b_conv_w, b_conv_b, b_dt_bias, b_a_log,
           b_d_skip, b_norm_g, od_w_in, od_w_out, c_lin_w, c_scale, d_q_norm, d_k_norm, d_sinks,
           rel_bias_table):
    bp, seq, d = x_prompt.shape
    bs, steps, _ = x_sample.shape
    past_len = PAST_LEN
    hp = x_prompt
    hs = x_sample
    depth = ffn1_norm.shape[0]
    names = ("a_v_s", "ssm_p", "ssm_s", "conv_p", "conv_s", "pool_p", "pool_s", "k_p", "k_s", "v_p", "v_s")
    outs = {k: [] for k in names}

    def macaron(h_p, h_s, norm, w_gu, w_down):
        o_p, o_s = _ffn(h_p.reshape(bp * seq, d), h_s.reshape(bs * steps, d), _row(norm), w_gu, w_down, FFN_TILE)
        return o_p.reshape(bp, seq, d), o_s.reshape(bs, steps, d)

    for layer in range(depth):
        i = layer // 2
        hp, hs = macaron(hp, hs, ffn1_norm[layer], ffn1_w_gu[layer], ffn1_w_down[layer])
        hs2 = hs.reshape(bs, steps * d)
        if layer % 2 == 0:
            p = _prep_even(mix_norm[layer], ev_w_in[i], ev_w_out[i], a_ln_g[i], a_ln_b[i], a_w_s[i], a_b_s[i],
                           b_conv_w[i], b_conv_b[i], b_dt_bias[i], b_a_log[i], b_d_skip[i], b_norm_g[i])
            p.update(_prep_even_sample(a_w_s[i], a_b_s[i], steps))
            hp, conv_p, ssm_p = _even_prompt(hp, p, MIXER_TILE)
            hs2, v_rows, conv_s, ssm_s = _even_sample(
                hs2, state_conv[i].reshape(bs, -1), state_ssm[i].reshape(bs, B_HEADS * B_HEAD_DIM, B_STATE),
                p, steps, SAMPLE_ROW_TILE, SAMPLE_SEQ_TILE)
            outs["a_v_s"].append(v_rows.reshape(bs, steps, -1))
            outs["conv_p"].append(conv_p)
            outs["conv_s"].append(conv_s.reshape(state_conv[i].shape))
            outs["ssm_p"].append(ssm_p.reshape(bp, B_HEADS, B_HEAD_DIM, B_STATE))
            outs["ssm_s"].append(ssm_s.reshape(state_ssm[i].shape))
        else:
            p = _prep_odd(mix_norm[layer], od_w_in[i], od_w_out[i], c_lin_w[i], c_scale[i], d_q_norm[i],
                          d_k_norm[i], d_sinks[i], rel_bias_table)
            hp, pool_p, k_p, v_p = _odd_prompt(hp, p, MIXER_TILE)
            kv_shape = cache_k_win[i].shape
            hs2, pool_s, k_s, v_s = _odd_sample(
                hs2, state_pool[i].reshape(bs, -1), cache_k_win[i].reshape(bs, kv_shape[1], -1),
                cache_v_win[i].reshape(bs, kv_shape[1], -1), p, steps, past_len, SAMPLE_ROW_TILE, SAMPLE_SEQ_TILE)
            outs["pool_p"].append(pool_p)
            outs["pool_s"].append(pool_s.reshape(state_pool[i].shape))
            outs["k_p"].append(k_p.reshape(bp, CHUNK, D_KV_HEADS, D_HEAD_DIM))
            outs["v_p"].append(v_p.reshape(bp, CHUNK, D_KV_HEADS, D_HEAD_DIM))
            outs["k_s"].append(k_s.reshape(kv_shape))
            outs["v_s"].append(v_s.reshape(kv_shape))
        hs = hs2.reshape(bs, steps, d)
        hp, hs = macaron(hp, hs, ffn2_norm[layer], ffn2_w_gu[layer], ffn2_w_down[layer])
    return (hp, hs) + tuple(jnp.stack(outs[k]) for k in names)
```

```python
import functools
import math

import numpy as np
import jax
import jax.numpy as jnp
from jax import lax
from jax.experimental import pallas as pl
from jax.experimental.pallas import tpu as pltpu

F32 = jnp.float32
BF16 = jnp.bfloat16

EPS = 1e-6
NEG = -1e30

LANES = 128
SUBLANES = 8
MXU_DIM = 256
VMEM_BYTES_V7X = 64 * 1024 * 1024
VMEM_LIMIT = VMEM_BYTES_V7X - 8 * 1024 * 1024

A_HEADS = 8
B_HEADS = 16
B_HEAD_DIM = 64
B_GROUPS = 2
B_STATE = 128
B_CONV = 4
CHUNK = 128
C_WINDOWS = (2, 4, 8, 16)
C_HALO = 16
D_Q_HEADS = 16
D_KV_HEADS = 4
D_HEAD_DIM = 64
D_GROUP = D_Q_HEADS // D_KV_HEADS
REL_BUCKETS = 32
REL_MAX_DIST = 128


def _rms(x, g):
    ms = jnp.mean(x * x, axis=-1, keepdims=True)
    return x * lax.rsqrt(ms + EPS) * g


def _sigmoid(x):
    return 1.0 / (1.0 + jnp.exp(-x))


def _silu(x):
    return x * _sigmoid(x)


def _gelu_tanh(x):
    c = math.sqrt(2.0 / math.pi)
    return x * (0.5 * (1.0 + jnp.tanh(c * (x + 0.044715 * (x * x * x)))))


def _softplus(x):
    return jnp.maximum(x, 0.0) + jnp.log1p(jnp.exp(-jnp.abs(x)))


def _split3(x):
    hi = x.astype(BF16)
    r1 = x - hi.astype(F32)
    mid = r1.astype(BF16)
    lo = (r1 - mid.astype(F32)).astype(BF16)
    return hi, mid, lo


def _split2_lanes(x):
    hi = x.astype(BF16)
    lo = (x - hi.astype(F32)).astype(BF16)
    return jnp.concatenate([hi, lo], axis=1)


def _dot(a, b):
    return jnp.dot(a, b, preferred_element_type=F32)


def _dot_nt(a, b):
    return lax.dot_general(a, b, (((1,), (1,)), ((), ())), preferred_element_type=F32)


def _dot_tn(a, b):
    return lax.dot_general(a, b, (((0,), (0,)), ((), ())), preferred_element_type=F32)


def _expand_heads(m, n_pairs):
    rows = m.shape[0]
    lane = lax.broadcasted_iota(jnp.int32, (rows, LANES), 1)
    first = lane < B_HEAD_DIM
    parts = []
    for p in range(n_pairs):
        a = jnp.broadcast_to(m[:, 2 * p:2 * p + 1], (rows, LANES))
        b = jnp.broadcast_to(m[:, 2 * p + 1:2 * p + 2], (rows, LANES))
        parts.append(jnp.where(first, a, b))
    return jnp.concatenate(parts, axis=1)


def _head_sumsq(x, ones_bd):
    xx = x * x
    hi = xx.astype(BF16)
    lo = (xx - hi.astype(F32)).astype(BF16)
    outs = []
    for c in range(x.shape[1] // 256):
        sl = slice(c * 256, (c + 1) * 256)
        outs.append(_dot(hi[:, sl], ones_bd) + _dot(lo[:, sl], ones_bd))
    return jnp.concatenate(outs, axis=1) if len(outs) > 1 else outs[0]


def _ff_blocks(d_ff):
    step = FFN_BLOCK_TILES * MXU_DIM
    return [(c0, min(c0 + step, d_ff)) for c0 in range(0, d_ff, step)]


def _load_cast_rows(src_hbm, dst_ref, stage_ref, sem_ref):
    slots, rows, _ = stage_ref.shape
    n = src_hbm.shape[0] // rows

    def copy(c):
        slot = c % slots
        return pltpu.make_async_copy(src_hbm.at[pl.ds(c * rows, rows), :], stage_ref.at[slot], sem_ref.at[slot])

    for c in range(min(slots, n)):
        copy(c).start()
    for c in range(n):
        copy(c).wait()
        dst_ref[c * rows:(c + 1) * rows, :] = stage_ref[c % slots].astype(BF16)
        if c + slots < n:
            copy(c + slots).start()


def _ffn_rows(x_ref, g_ref, wgu_ref, wd_ref, o_ref, d_ff):
    tm = x_ref.shape[0]
    rows = [slice(r0, min(r0 + FFN_ROW_BLOCK, tm)) for r0 in range(0, tm, FFN_ROW_BLOCK)]
    xns = [_rms(x_ref[rs, :], g_ref[...]).astype(BF16) for rs in rows]
    for rs, xn in zip(rows, xns):
        y = None
        for c0, c1 in _ff_blocks(d_ff):
            gate = _dot(xn, wgu_ref[:, c0:c1])
            up = _dot(xn, wgu_ref[:, d_ff + c0:d_ff + c1])
            act = (_silu(gate) * up).astype(BF16)
            part = _dot(act, wd_ref[c0:c1, :])
            y = part if y is None else y + part
        o_ref[rs, :] = x_ref[rs, :] + 0.5 * y


def _ffn_kernel(xp_ref, xs_ref, g_ref, wgu_hbm, wd_hbm, op_ref, os_ref,
                wgu_ref, wd_ref, stage_gu_ref, stage_d_ref, sem_ref, *, d_ff, prompt_steps, layer):
    i = pl.program_id(0)

    @pl.when(i == 0)
    def _():
        _load_cast_rows(wgu_hbm.at[layer], wgu_ref, stage_gu_ref, sem_ref)
        _load_cast_rows(wd_hbm.at[layer], wd_ref, stage_d_ref, sem_ref)

    @pl.when(i < prompt_steps)
    def _():
        _ffn_rows(xp_ref, g_ref, wgu_ref, wd_ref, op_ref, d_ff)

    @pl.when(i == prompt_steps)
    def _():
        _ffn_rows(xs_ref, g_ref, wgu_ref, wd_ref, os_ref, d_ff)


def _resident(shape):
    nd = len(shape)
    return pl.BlockSpec(shape, lambda *_: (0,) * nd, pipeline_mode=pl.Buffered(1))


def _ffn(xp2d, xs2d, g, wgu_all, wd_all, layer, tm):
    m, d = xp2d.shape
    ms = xs2d.shape[0]
    wgu_shape, wd_shape = wgu_all.shape[1:], wd_all.shape[1:]
    d_ff = wd_shape[0]
    assert m % tm == 0 and d % FFN_STAGE_ROWS_GU == 0 and d_ff % FFN_STAGE_ROWS_D == 0
    steps = m // tm
    last = steps - 1
    whole = lambda shape: pl.BlockSpec(shape, lambda i: (0, 0))
    prompt_tile = pl.BlockSpec((tm, d), lambda i: (jnp.minimum(i, last), 0))
    hbm = pl.BlockSpec(memory_space=pl.ANY)
    return pl.pallas_call(
        functools.partial(_ffn_kernel, d_ff=d_ff, prompt_steps=steps, layer=layer),
        out_shape=(jax.ShapeDtypeStruct((m, d), F32), jax.ShapeDtypeStruct((ms, d), F32)),
        grid=(steps + 1,),
        in_specs=[prompt_tile, whole((ms, d)), _resident((1, d)), hbm, hbm],
        out_specs=(prompt_tile, whole((ms, d))),
        scratch_shapes=[pltpu.VMEM(wgu_shape, BF16), pltpu.VMEM(wd_shape, BF16),
                        pltpu.VMEM((FFN_STAGE_SLOTS, FFN_STAGE_ROWS_GU, wgu_shape[1]), F32),
                        pltpu.VMEM((FFN_STAGE_SLOTS, FFN_STAGE_ROWS_D, wd_shape[1]), F32),
                        pltpu.SemaphoreType.DMA((FFN_STAGE_SLOTS,))],
        compiler_params=pltpu.CompilerParams(
            dimension_semantics=("arbitrary",), vmem_limit_bytes=VMEM_LIMIT),
        name="ffn",
    )(xp2d, xs2d, g, wgu_all, wd_all)


def _even_prompt_kernel(h_ref, g_ref, wa_ref, wz_ref, wxbc_ref, wdt_ref, wout_ref,
                        lng_ref, lnb_ref, ws_ref, bsb_ref, convw_ref, convb_ref,
                        dtb_ref, alog_ref, dskip_ref, normg_ref, expand_ref, colsel_ref,
                        o_ref, conv_out_ref, ssm_out_ref,
                        ext_ref, st_ref, *, tile, width_a, inner):
    s = pl.program_id(1)
    n_chunks = tile // CHUNK
    halo = SUBLANES

    @pl.when(s == 0)
    def _():
        ext_ref[0:halo, :] = jnp.zeros((halo, ext_ref.shape[1]), F32)
        st_ref[...] = jnp.zeros(st_ref.shape, F32)

    x = h_ref[...]
    xn = _rms(x, g_ref[...]).astype(BF16)

    row = lax.broadcasted_iota(jnp.int32, (CHUNK, CHUNK), 0)
    col = lax.broadcasted_iota(jnp.int32, (CHUNK, CHUNK), 1)
    causal = row >= col
    lane = lax.broadcasted_iota(jnp.int32, (CHUNK, LANES), 1)
    first_half = lane < B_HEAD_DIM

    pa = _dot(xn, wa_ref[...])
    xbc_raw = _dot(xn, wxbc_ref[...])

    ga = _gelu_tanh(pa)
    u = ga[:, :width_a]
    v = ga[:, width_a:]
    mu = jnp.mean(v, axis=-1, keepdims=True)
    vc = v - mu
    var = jnp.mean(vc * vc, axis=-1, keepdims=True)
    v = vc * lax.rsqrt(var + EPS) * lng_ref[...] + lnb_ref[...]
    vb = v.astype(BF16)

    z = _dot(xn, wz_ref[...])
    dt_raw = _dot(xn, wdt_ref[...])

    ext_ref[halo:halo + tile, :] = xbc_raw
    ext = ext_ref[...]
    ext1 = pltpu.roll(ext, 1, 0)
    pair = ext * convw_ref[1:2, :] + ext1 * convw_ref[0:1, :]
    conv = (convb_ref[...] + ext * convw_ref[3:4, :] + ext1 * convw_ref[2:3, :] + pltpu.roll(pair, 2, 0))[halo:]
    tail = ext_ref[tile:tile + halo, :]
    ext_ref[0:halo, :] = tail
    conv_out_ref[...] = tail
    xbc = _silu(conv)
    gn = B_GROUPS * B_STATE
    xs = xbc[:, :inner]
    bm = xbc[:, inner:inner + gn]
    cm = xbc[:, inner + gn:]

    head_w = width_a // A_HEADS
    gate_cols = []
    for hh in range(A_HEADS):
        w = jnp.where(causal, ws_ref[hh], 0.0).astype(BF16)
        rhs = jnp.concatenate(
            [vb[c * CHUNK:(c + 1) * CHUNK, hh * head_w:(hh + 1) * head_w] for c in range(n_chunks)], axis=1)
        out = _dot(w, rhs)
        bias = bsb_ref[hh]
        gate_cols.append(jnp.concatenate(
            [out[:, c * head_w:(c + 1) * head_w] + bias for c in range(n_chunks)], axis=0))
    ya = u * jnp.concatenate(gate_cols, axis=1)
    out_a = _dot(ya.astype(BF16), wout_ref[0:width_a, :])

    dt = _softplus(dt_raw + dtb_ref[...])
    a_neg = -jnp.exp(alog_ref[...])
    da = dt * a_neg

    n_pairs = B_HEADS // 2
    heads_per_group = B_HEADS // B_GROUPS
    gw = heads_per_group * B_HEAD_DIM
    tril_ones = jnp.where(causal, 1.0, 0.0).astype(BF16)
    chunks = [slice(c * CHUNK, (c + 1) * CHUNK) for c in range(n_chunks)]
    acums = []
    for rs in chunks:
        d_hi, d_mid, d_lo = _split3(da[rs])
        acums.append(_dot(tril_ones, d_hi) + _dot(tril_ones, d_mid) + _dot(tril_ones, d_lo))
    acum = jnp.concatenate(acums, axis=0)
    decay = jnp.concatenate([jnp.exp(a[CHUNK - 1:CHUNK, :] - a) for a in acums], axis=0)
    expand = expand_ref[...]
    xd = xs * _dot(_split2_lanes(dt), expand)
    xdwb = (xs * _dot(_split2_lanes(dt * decay), expand)).astype(BF16)
    e_acum = _dot(_split2_lanes(jnp.exp(acum)), expand)
    bmb = bm.astype(BF16)
    cmb = cm.astype(BF16)
    y_rows = []
    for c, rs in enumerate(chunks):
        a_c = acums[c]
        acum_t = a_c.T
        a_cols = _dot(_split2_lanes(a_c), colsel_ref[...])
        cb = [_dot_nt(cmb[rs, g * B_STATE:(g + 1) * B_STATE], bmb[rs, g * B_STATE:(g + 1) * B_STATE])
              for g in range(B_GROUPS)]
        y_parts = []
        for p in range(n_pairs):
            g = (2 * p) // heads_per_group
            ms = []
            for hh in (2 * p, 2 * p + 1):
                seg = a_cols[:, hh * CHUNK:(hh + 1) * CHUNK] - jnp.broadcast_to(acum_t[hh:hh + 1, :], (CHUNK, CHUNK))
                lmat = jnp.where(causal, jnp.exp(seg), 0.0)
                ms.append((cb[g] * lmat).astype(BF16))
            lhs = jnp.concatenate(ms, axis=1)
            xd_p = xd[rs, p * LANES:(p + 1) * LANES]
            rhs = jnp.concatenate([jnp.where(first_half, xd_p, 0.0),
                                   jnp.where(first_half, 0.0, xd_p)], axis=0).astype(BF16)
            y_parts.append(_dot(lhs, rhs))
        y_rows.append(jnp.concatenate(y_parts, axis=1))
    for c, rs in enumerate(chunks):
        st_prev = st_ref[...]
        stb = st_prev.astype(BF16)
        y_off = jnp.concatenate(
            [_dot(cmb[rs, g * B_STATE:(g + 1) * B_STATE], stb[:, g * gw:(g + 1) * gw]) for g in range(B_GROUPS)],
            axis=1)
        st_add = jnp.concatenate(
            [_dot_tn(bmb[rs, g * B_STATE:(g + 1) * B_STATE], xdwb[rs, g * gw:(g + 1) * gw])
             for g in range(B_GROUPS)], axis=1)
        chunk_decay = e_acum[(c + 1) * CHUNK - 1:(c + 1) * CHUNK, :]
        st_ref[...] = st_prev * chunk_decay + st_add
        y_rows[c] = y_rows[c] + y_off * e_acum[rs]
    y = (jnp.concatenate(y_rows, axis=0) if n_chunks > 1 else y_rows[0]) + xs * dskip_ref[...]
    y = y * _silu(z)
    half = inner // B_GROUPS
    yn = []
    for g in range(B_GROUPS):
        yg = y[:, g * half:(g + 1) * half]
        yn.append(yg * lax.rsqrt(jnp.mean(yg * yg, axis=-1, keepdims=True) + EPS))
    yb = jnp.concatenate(yn, axis=1) * normg_ref[...]

    o_ref[...] = x + out_a + _dot(yb.astype(BF16), wout_ref[width_a:, :])

    @pl.when(s == pl.num_programs(1) - 1)
    def _():
        ssm_out_ref[...] = st_ref[...].T


def _even_prompt(h, p, tile):
    b, seq, d = h.shape
    width_a = p["wa"].shape[1] // 2
    inner = p["wz"].shape[1]
    conv_dim = p["wxbc"].shape[1]
    assert seq % tile == 0 and tile % CHUNK == 0
    small = ["lng", "lnb", "ws", "bsb", "convw", "convb", "dtb", "alog", "dskip", "normg", "expand", "colsel"]
    out, conv_tail, ssm = pl.pallas_call(
        functools.partial(_even_prompt_kernel, tile=tile, width_a=width_a, inner=inner),
        out_shape=(jax.ShapeDtypeStruct((b, seq, d), F32),
                   jax.ShapeDtypeStruct((b, SUBLANES, conv_dim), F32),
                   jax.ShapeDtypeStruct((b, inner, B_STATE), F32)),
        grid=(b, seq // tile),
        in_specs=[pl.BlockSpec((None, tile, d), lambda i, j: (i, j, 0)),
                  _resident((1, d))]
        + [_resident(p[k].shape) for k in ("wa", "wz", "wxbc", "wdt", "wout")]
        + [_resident(p[k].shape) for k in small],
        out_specs=(pl.BlockSpec((None, tile, d), lambda i, j: (i, j, 0)),
                   pl.BlockSpec((None, SUBLANES, conv_dim), lambda i, j: (i, 0, 0)),
                   pl.BlockSpec((None, inner, B_STATE), lambda i, j: (i, 0, 0))),
        scratch_shapes=[pltpu.VMEM((tile + SUBLANES, conv_dim), F32),
                        pltpu.VMEM((B_STATE, inner), F32)],
        compiler_params=pltpu.CompilerParams(
            dimension_semantics=("arbitrary", "arbitrary"), vmem_limit_bytes=VMEM_LIMIT),
        name="even_prompt",
    )(h, p["g"], p["wa"], p["wz"], p["wxbc"], p["wdt"], p["wout"], *[p[k] for k in small])
    return out, conv_tail[:, SUBLANES - (B_CONV - 1):, :], ssm


def _t5_bucket(dist):
    n = np.maximum(dist, 0)
    max_exact = REL_BUCKETS // 2
    n_safe = np.maximum(n, 1).astype(np.float32)
    scale = np.float32((REL_BUCKETS - max_exact) / math.log(REL_MAX_DIST / max_exact))
    large = max_exact + (np.log(n_safe / max_exact) * scale).astype(np.int32)
    large = np.minimum(large, REL_BUCKETS - 1)
    return np.where(n < max_exact, n, large).astype(np.int32)


def _fill_bias(bias_ref, bucket_ref, rel_ref):
    bucket = bucket_ref[...]
    has_prev = lax.broadcasted_iota(jnp.int32, bucket.shape, 1) >= CHUNK
    for hh in range(D_Q_HEADS):
        acc = jnp.full(bucket.shape, NEG, F32)
        for bkt in range(REL_BUCKETS):
            acc = jnp.where(bucket == bkt, rel_ref[bkt, hh], acc)
        bias_ref[0, hh] = acc
        bias_ref[1, hh] = jnp.where(has_prev, acc, NEG)


def _group_attention(qg, kk, vv, bias_ref, sinks_ref, grp, table):
    lq = qg.shape[0]
    lane_kv = lax.broadcasted_iota(jnp.int32, (lq, D_KV_HEADS * D_HEAD_DIM), 1) // D_HEAD_DIM
    zero = jnp.zeros_like(qg)
    lhs = jnp.concatenate([jnp.where(lane_kv == kv, qg, zero) for kv in range(D_KV_HEADS)], axis=0)
    sc = _dot_nt(lhs, kk)
    probs = []
    for kv in range(D_KV_HEADS):
        hh = kv * D_GROUP + grp
        s_h = sc[kv * lq:(kv + 1) * lq] + bias_ref[table, hh]
        sink = sinks_ref[hh]
        m = jnp.maximum(jnp.max(s_h, axis=-1, keepdims=True), sink)
        pexp = jnp.exp(s_h - m)
        denom = jnp.sum(pexp, axis=-1, keepdims=True) + jnp.exp(sink - m)
        probs.append((pexp / denom).astype(BF16))
    ov = _dot(jnp.concatenate(probs, axis=0), vv)
    out = ov[(D_KV_HEADS - 1) * lq:]
    for kv in range(D_KV_HEADS - 2, -1, -1):
        out = jnp.where(lane_kv == kv, ov[kv * lq:(kv + 1) * lq], out)
    return out


def _odd_prompt_kernel(h_ref, g_ref, wc_ref, wq_ref, wk_ref, wv_ref, wout_ref,
                       linw_ref, cscale_ref, qn_ref, kn_ref, onesbd_ref, bucket_ref,
                       sinks_ref, rel_ref,
                       o_ref, pool_out_ref, k_out_ref, v_out_ref,
                       extc_ref, kprev_ref, vprev_ref, bias_ref, *, tile, width_c):
    b = pl.program_id(0)
    s = pl.program_id(1)
    n_blocks = tile // CHUNK

    @pl.when((b == 0) & (s == 0))
    def _():
        _fill_bias(bias_ref, bucket_ref, rel_ref)

    @pl.when(s == 0)
    def _():
        extc_ref[0:C_HALO, :] = jnp.zeros((C_HALO, width_c), F32)
        kprev_ref[...] = jnp.zeros(kprev_ref.shape, F32)
        vprev_ref[...] = jnp.zeros(vprev_ref.shape, F32)

    x = h_ref[...]
    xn = _rms(x, g_ref[...]).astype(BF16)

    c_in = _dot(xn, wc_ref[...])
    extc_ref[C_HALO:C_HALO + tile, :] = c_in
    e = extc_ref[...]
    tail = extc_ref[tile:tile + C_HALO, :]
    extc_ref[0:C_HALO, :] = tail
    pool_out_ref[...] = tail
    pos = (s * tile + lax.broadcasted_iota(jnp.int32, (tile, 1), 0) + 1).astype(F32)
    gdim = width_c // len(C_WINDOWS)
    run = e
    shift = 1
    yc = []
    for gi, win in enumerate(C_WINDOWS):
        while shift < win:
            run = run + pltpu.roll(run, shift, 0)
            shift *= 2
        cnt = jnp.minimum(pos, float(win))
        pooled = run[C_HALO:, :gdim] / cnt - c_in[:, gi * gdim:(gi + 1) * gdim]
        yc.append(_dot(pooled.astype(BF16), linw_ref[gi]))
        if gi + 1 < len(C_WINDOWS):
            run = run[:, gdim:]
    yc = jnp.concatenate(yc, axis=1) * cscale_ref[...]

    q = _dot(xn, wq_ref[...])
    k = _dot(xn, wk_ref[...])
    v = _dot(xn, wv_ref[...])
    ones_bd = onesbd_ref[...]
    inv_d = 1.0 / D_HEAD_DIM
    qn = q * lax.rsqrt(_head_sumsq(q, ones_bd) * inv_d + EPS) * qn_ref[...]
    kn = k * lax.rsqrt(_head_sumsq(k, ones_bd) * inv_d + EPS) * kn_ref[...]
    qs = (qn * (D_HEAD_DIM ** -0.5)).astype(BF16)
    kb = kn.astype(BF16)
    vb = v.astype(BF16)
    first_table = jnp.where(s == 0, 1, 0)
    gw = D_KV_HEADS * D_HEAD_DIM
    o_rows = []
    for blk in range(n_blocks):
        rs = slice(blk * CHUNK, (blk + 1) * CHUNK)
        if blk == 0:
            k_prev, v_prev = kprev_ref[...].astype(BF16), vprev_ref[...].astype(BF16)
        else:
            k_prev, v_prev = kb[(blk - 1) * CHUNK:blk * CHUNK], vb[(blk - 1) * CHUNK:blk * CHUNK]
        kk = jnp.concatenate([k_prev, kb[rs]], axis=0)
        vv = jnp.concatenate([v_prev, vb[rs]], axis=0)
        o_rows.append(jnp.concatenate(
            [_group_attention(qs[rs, grp * gw:(grp + 1) * gw], kk, vv, bias_ref, sinks_ref, grp,
                              first_table if blk == 0 else 0)
             for grp in range(D_GROUP)], axis=1))
    kprev_ref[...] = kn[tile - CHUNK:]
    vprev_ref[...] = v[tile - CHUNK:]
    yd = jnp.concatenate(o_rows, axis=0) if n_blocks > 1 else o_rows[0]

    mix = jnp.concatenate([yc, yd], axis=1).astype(BF16)
    o_ref[...] = x + _dot(mix, wout_ref[...])

    @pl.when(s == pl.num_programs(1) - 1)
    def _():
        k_out_ref[...] = kn[tile - CHUNK:]
        v_out_ref[...] = v[tile - CHUNK:]


def _odd_prompt(h, p, tile):
    b, seq, d = h.shape
    width_c = p["wc"].shape[1]
    kvw = p["wk"].shape[1]
    assert seq % tile == 0 and tile % CHUNK == 0
    r = np.arange(CHUNK) + CHUNK
    c = np.arange(2 * CHUNK)
    dist = r[:, None] - c[None, :]
    bucket = np.where((dist >= 0) & (dist < CHUNK), _t5_bucket(dist), -1).astype(np.int32)
    vm = ["linw", "cscale", "qn", "kn", "onesbd"]
    smem = pl.BlockSpec(memory_space=pltpu.SMEM)
    out, pool_tail, k_win, v_win = pl.pallas_call(
        functools.partial(_odd_prompt_kernel, tile=tile, width_c=width_c),
        out_shape=(jax.ShapeDtypeStruct((b, seq, d), F32),
                   jax.ShapeDtypeStruct((b, C_HALO, width_c), F32),
                   jax.ShapeDtypeStruct((b, CHUNK, kvw), F32),
                   jax.ShapeDtypeStruct((b, CHUNK, kvw), F32)),
        grid=(b, seq // tile),
        in_specs=[pl.BlockSpec((None, tile, d), lambda i, j: (i, j, 0)),
                  _resident((1, d))]
        + [_resident(p[k].shape) for k in ("wc", "wq", "wk", "wv", "wout")]
        + [_resident(p[k].shape) for k in vm]
        + [_resident(bucket.shape), smem, smem],
        out_specs=(pl.BlockSpec((None, tile, d), lambda i, j: (i, j, 0)),
                   pl.BlockSpec((None, C_HALO, width_c), lambda i, j: (i, 0, 0)),
                   pl.BlockSpec((None, CHUNK, kvw), lambda i, j: (i, 0, 0)),
                   pl.BlockSpec((None, CHUNK, kvw), lambda i, j: (i, 0, 0))),
        scratch_shapes=[pltpu.VMEM((tile + C_HALO, width_c), F32),
                        pltpu.VMEM((CHUNK, kvw), F32),
                        pltpu.VMEM((CHUNK, kvw), F32),
                        pltpu.VMEM((2, D_Q_HEADS, CHUNK, 2 * CHUNK), F32)],
        compiler_params=pltpu.CompilerParams(
            dimension_semantics=("arbitrary", "arbitrary"), vmem_limit_bytes=VMEM_LIMIT),
        name="odd_prompt",
    )(h, p["g"], p["wc"], p["wq"], p["wk"], p["wv"], p["wout"], *[p[k] for k in vm],
      jnp.asarray(bucket), p["sinks"], p["rel"])
    return out, pool_tail[:, C_HALO - (max(C_WINDOWS) - 1):, :], k_win, v_win


def _steps(x, n, width):
    return [x[:, t * width:(t + 1) * width] for t in range(n)]


def _stack_steps(ref, n, width):
    x = ref[...]
    return jnp.concatenate(_steps(x, n, width), axis=0)


def _even_sample_front_kernel(hs_ref, g_ref, wa_ref, wz_ref, wxbc_ref, wdt_ref,
                              lng_ref, lnb_ref, wts_ref, bts_ref, convw_ref, convb_ref,
                              dtb_ref, alog_ref, dskip_ref, cs_ref,
                              v_out, ya_out, ypart_out, eacum_out, z_out, conv_out,
                              cgt_out, xdw_out, bs_out, dec_out, *, steps, d, width_a, inner, conv_dim):
    bt = hs_ref.shape[0]
    xn = _rms(_stack_steps(hs_ref, steps, d), g_ref[...]).astype(BF16)
    blk = lambda a, t: a[t * bt:(t + 1) * bt]

    ga = _gelu_tanh(_dot(xn, wa_ref[...]))
    u = ga[:, :width_a]
    v = ga[:, width_a:]
    mu = jnp.mean(v, axis=-1, keepdims=True)
    vc = v - mu
    var = jnp.mean(vc * vc, axis=-1, keepdims=True)
    v = vc * lax.rsqrt(var + EPS) * lng_ref[...] + lnb_ref[...]
    for t in range(steps):
        v_out[:, t * width_a:(t + 1) * width_a] = blk(v, t)
        gate = bts_ref[t:t + 1, :]
        for s in range(t + 1):
            gate = gate + wts_ref[t * steps + s:t * steps + s + 1, :] * blk(v, s)
        ya_out[:, t * width_a:(t + 1) * width_a] = blk(u, t) * gate

    z = _dot(xn, wz_ref[...])
    for t in range(steps):
        z_out[:, t * inner:(t + 1) * inner] = blk(z, t)
    raw = _dot(xn, wxbc_ref[...])
    dt = _softplus(_dot(xn, wdt_ref[...]) + dtb_ref[...])
    ext = _steps(cs_ref[...], B_CONV - 1, conv_dim) + [blk(raw, t) for t in range(steps)]
    for k in range(B_CONV - 1):
        conv_out[:, k * conv_dim:(k + 1) * conv_dim] = ext[len(ext) - (B_CONV - 1) + k]
    gn = B_GROUPS * B_STATE
    n_pairs = B_HEADS // 2
    a_neg = -jnp.exp(alog_ref[...])
    xs, bm, cm, dts, acum = [], [], [], [], []
    for t in range(steps):
        conv = convb_ref[...]
        for tap in range(B_CONV):
            conv = conv + ext[t + tap] * convw_ref[tap:tap + 1, :]
        xbc = _silu(conv)
        xs.append(xbc[:, :inner])
        bm.append(xbc[:, inner:inner + gn])
        cm.append(xbc[:, inner + gn:])
        dts.append(blk(dt, t))
        da = dts[t] * a_neg
        acum.append(da if t == 0 else acum[t - 1] + da)
    lane = lax.broadcasted_iota(jnp.int32, (bt, LANES), 1)
    group0 = lane < (B_HEADS // B_GROUPS)
    dec_out[...] = jnp.exp(acum[steps - 1])
    pad_rows = SUBLANES - steps
    xdw_out[:, steps * inner:] = jnp.zeros((bt, pad_rows * inner), F32)
    bs_out[:, steps * gn:] = jnp.zeros((bt, pad_rows * gn), F32)
    xd = []
    for t in range(steps):
        xd.append(xs[t] * _expand_heads(dts[t], n_pairs))
        eacum_out[:, t * inner:(t + 1) * inner] = _expand_heads(jnp.exp(acum[t]), n_pairs)
        xdw_out[:, t * inner:(t + 1) * inner] = xs[t] * _expand_heads(
            dts[t] * jnp.exp(acum[steps - 1] - acum[t]), n_pairs)
        bs_out[:, t * gn:(t + 1) * gn] = bm[t]
        for g in range(B_GROUPS):
            r = g * steps + t
            cgt_out[:, r * B_STATE:(r + 1) * B_STATE] = cm[t][:, g * B_STATE:(g + 1) * B_STATE]
    for t in range(steps):
        y = xs[t] * dskip_ref[...]
        for s in range(t + 1):
            cb = [jnp.sum(cm[t][:, g * B_STATE:(g + 1) * B_STATE] * bm[s][:, g * B_STATE:(g + 1) * B_STATE],
                          axis=-1, keepdims=True) for g in range(B_GROUPS)]
            coef = jnp.where(group0, cb[0], cb[1]) * jnp.exp(acum[t] - acum[s])
            y = y + _expand_heads(coef, n_pairs) * xd[s]
        ypart_out[:, t * inner:(t + 1) * inner] = y


def _even_sample_state_kernel(s0_ref, cgt_ref, xdw_ref, bs_ref, dec_ref, yoff_ref, snew_ref, *, bb):
    step = pl.program_id(0)
    gw = (B_HEADS // B_GROUPS) * B_HEAD_DIM

    def body(bi, carry):
        s0 = s0_ref[bi]
        c8 = cgt_ref[bi]
        c16 = jnp.concatenate([c8, jnp.zeros_like(c8)], axis=0).astype(BF16)
        yoff_ref[bi] = _dot_nt(c16, s0.astype(BF16))[:SUBLANES]
        x8 = xdw_ref[bi]
        b8 = bs_ref[bi]
        x16 = jnp.concatenate([x8, jnp.zeros_like(x8)], axis=0).astype(BF16)
        b16 = jnp.concatenate([b8, jnp.zeros_like(b8)], axis=0).astype(BF16)
        for g in range(B_GROUPS):
            add = _dot_tn(x16[:, g * gw:(g + 1) * gw], b16[:, g * B_STATE:(g + 1) * B_STATE])
            for hl in range(B_HEADS // B_GROUPS):
                hh = g * (B_HEADS // B_GROUPS) + hl
                rs = slice(hh * B_HEAD_DIM, (hh + 1) * B_HEAD_DIM)
                snew_ref[bi, rs, :] = s0[rs] * dec_ref[step * bb + bi, hh] + \
                    add[hl * B_HEAD_DIM:(hl + 1) * B_HEAD_DIM]
        return carry

    lax.fori_loop(0, bb, body, 0, unroll=2)


def _even_sample_back_kernel(hs_ref, ya_ref, ypart_ref, eacum_ref, z_ref, yoff_ref, normg_ref, wout_ref,
                             o_ref, *, steps, d, inner):
    bt = hs_ref.shape[0]
    half = inner // B_GROUPS
    mixes = []
    for t in range(steps):
        sl = slice(t * inner, (t + 1) * inner)
        yoff = jnp.concatenate(
            [yoff_ref[:, (g * steps + t) * inner + g * half:(g * steps + t) * inner + (g + 1) * half]
             for g in range(B_GROUPS)], axis=1)
        y = (ypart_ref[:, sl] + yoff * eacum_ref[:, sl]) * _silu(z_ref[:, sl])
        yn = []
        for g in range(B_GROUPS):
            yg = y[:, g * half:(g + 1) * half]
            yn.append(yg * lax.rsqrt(jnp.mean(yg * yg, axis=-1, keepdims=True) + EPS))
        yb = jnp.concatenate(yn, axis=1) * normg_ref[...]
        mixes.append(jnp.concatenate([ya_ref[:, t * d:(t + 1) * d], yb], axis=1))
    out = _dot(jnp.concatenate(mixes, axis=0).astype(BF16), wout_ref[...])
    for t in range(steps):
        o_ref[:, t * d:(t + 1) * d] = hs_ref[:, t * d:(t + 1) * d] + out[t * bt:(t + 1) * bt]


def _row_tiled(width, bt):
    return pl.BlockSpec((bt, width), lambda i: (i, 0))


def _even_sample(hs2, state_conv2, state_ssm3, p, steps, bt, bb):
    nb, _ = hs2.shape
    d = p["wa"].shape[0]
    width_a = p["wa"].shape[1] // 2
    inner = p["wz"].shape[1]
    conv_dim = p["wxbc"].shape[1]
    gn = B_GROUPS * B_STATE
    assert nb % bt == 0 and nb % bb == 0 and steps <= SUBLANES
    params = pltpu.CompilerParams(dimension_semantics=("arbitrary",), vmem_limit_bytes=VMEM_LIMIT)
    small = ["lng", "lnb", "wts", "bts", "convw", "convb", "dtb", "alog", "dskip"]
    widths = dict(v=steps * width_a, ya=steps * width_a, ypart=steps * inner, eacum=steps * inner,
                  z=steps * inner, conv=(B_CONV - 1) * conv_dim, cgt=SUBLANES * B_STATE,
                  xdw=SUBLANES * inner, bs=SUBLANES * gn, dec=LANES)
    front = pl.pallas_call(
        functools.partial(_even_sample_front_kernel, steps=steps, d=d, width_a=width_a, inner=inner,
                          conv_dim=conv_dim),
        out_shape=tuple(jax.ShapeDtypeStruct((nb, w), F32) for w in widths.values()),
        grid=(nb // bt,),
        in_specs=[_row_tiled(steps * d, bt), _resident((1, d))]
        + [_resident(p[k].shape) for k in ("wa", "wz", "wxbc", "wdt")]
        + [_resident(p[k].shape) for k in small]
        + [_row_tiled((B_CONV - 1) * conv_dim, bt)],
        out_specs=tuple(_row_tiled(w, bt) for w in widths.values()),
        compiler_params=params,
        name="even_sample_front",
    )(hs2, p["g"], p["wa"], p["wz"], p["wxbc"], p["wdt"], *[p[k] for k in small], state_conv2)
    v_rows, ya, ypart, eacum, z, new_conv, cgt, xdw, bs, dec = front

    hp = state_ssm3.shape[1]
    tile3 = lambda rows, width: pl.BlockSpec((bb, rows, width), lambda i: (i, 0, 0))
    yoff, new_ssm = pl.pallas_call(
        functools.partial(_even_sample_state_kernel, bb=bb),
        out_shape=(jax.ShapeDtypeStruct((nb, SUBLANES, hp), F32),
                   jax.ShapeDtypeStruct(state_ssm3.shape, F32)),
        grid=(nb // bb,),
        in_specs=[tile3(hp, B_STATE), tile3(SUBLANES, B_STATE), tile3(SUBLANES, inner), tile3(SUBLANES, gn),
                  pl.BlockSpec(memory_space=pltpu.SMEM)],
        out_specs=(tile3(SUBLANES, hp), tile3(hp, B_STATE)),
        compiler_params=params,
        name="even_sample_state",
    )(state_ssm3, cgt.reshape(nb, SUBLANES, B_STATE), xdw.reshape(nb, SUBLANES, inner),
      bs.reshape(nb, SUBLANES, gn), dec[:, :B_HEADS])

    out = pl.pallas_call(
        functools.partial(_even_sample_back_kernel, steps=steps, d=d, inner=inner),
        out_shape=jax.ShapeDtypeStruct(hs2.shape, F32),
        grid=(nb // bt,),
        in_specs=[_row_tiled(steps * d, bt), _row_tiled(steps * width_a, bt), _row_tiled(steps * inner, bt),
                  _row_tiled(steps * inner, bt), _row_tiled(steps * inner, bt), _row_tiled(SUBLANES * hp, bt),
                  _resident((1, inner)), _resident(p["wout"].shape)],
        out_specs=_row_tiled(steps * d, bt),
        compiler_params=params,
        name="even_sample_back",
    )(hs2, ya, ypart, eacum, z, yoff.reshape(nb, SUBLANES * hp), p["normg"], p["wout"])
    return out, v_rows, new_conv, new_ssm


def _odd_sample_front_kernel(hs_ref, g_ref, wc_ref, wq_ref, wk_ref, wv_ref, linw_ref, cscale_ref,
                             qn_ref, kn_ref, onesbd_ref, ps_ref,
                             yc_out, pool_out, q_out, knew_out, vnew_out, *, steps, d, width_c, past_len):
    bt = hs_ref.shape[0]
    xn = _rms(_stack_steps(hs_ref, steps, d), g_ref[...]).astype(BF16)
    blk = lambda a, t: a[t * bt:(t + 1) * bt]
    c_in = _dot(xn, wc_ref[...])
    n_state = max(C_WINDOWS) - 1
    ext = _steps(ps_ref[...], n_state, width_c) + [blk(c_in, t) for t in range(steps)]
    for j in range(n_state):
        pool_out[:, j * width_c:(j + 1) * width_c] = ext[len(ext) - n_state + j]
    gdim = width_c // len(C_WINDOWS)
    yc_cols = []
    for gi, win in enumerate(C_WINDOWS):
        sl = slice(gi * gdim, (gi + 1) * gdim)
        pooled = []
        for t in range(steps):
            hi = n_state + t
            lo = max(hi - win + 1, 0)
            acc = ext[lo][:, sl]
            for j in range(lo + 1, hi + 1):
                acc = acc + ext[j][:, sl]
            count = float(min(past_len + t + 1, win))
            pooled.append(acc / count - ext[hi][:, sl])
        yc_cols.append(_dot(jnp.concatenate(pooled, axis=0).astype(BF16), linw_ref[gi]))
    yc = jnp.concatenate(yc_cols, axis=1) * cscale_ref[...]
    for t in range(steps):
        yc_out[:, t * width_c:(t + 1) * width_c] = blk(yc, t)

    q = _dot(xn, wq_ref[...])
    k = _dot(xn, wk_ref[...])
    v = _dot(xn, wv_ref[...])
    ones_bd = onesbd_ref[...]
    inv_d = 1.0 / D_HEAD_DIM
    qn = q * lax.rsqrt(_head_sumsq(q, ones_bd) * inv_d + EPS) * qn_ref[...] * (D_HEAD_DIM ** -0.5)
    kn = k * lax.rsqrt(_head_sumsq(k, ones_bd) * inv_d + EPS) * kn_ref[...]
    qw = q.shape[1]
    kw = k.shape[1]
    pad = SUBLANES - steps
    q_out[:, steps * qw:] = jnp.zeros((bt, pad * qw), F32)
    knew_out[:, :pad * kw] = jnp.zeros((bt, pad * kw), F32)
    vnew_out[:, :pad * kw] = jnp.zeros((bt, pad * kw), F32)
    for t in range(steps):
        q_out[:, t * qw:(t + 1) * qw] = blk(qn, t)
        knew_out[:, (pad + t) * kw:(pad + t + 1) * kw] = blk(kn, t)
        vnew_out[:, (pad + t) * kw:(pad + t + 1) * kw] = blk(v, t)


SINK_BUCKET = REL_BUCKETS


def _odd_sample_attn_kernel(q_ref, knew_ref, vnew_ref, ck_ref, cv_ref, bucket_ref, sinks_ref, rel_ref,
                            o_ref, kout_ref, vout_ref, bias_ref, *, bb, steps, n_keys):
    win = ck_ref.shape[1]
    kvw = ck_ref.shape[2]
    tile16 = 2 * SUBLANES

    @pl.when(pl.program_id(0) == 0)
    def _():
        bucket = bucket_ref[...]
        for hh in range(D_Q_HEADS):
            acc = jnp.full(bucket.shape, NEG, F32)
            for bkt in range(REL_BUCKETS):
                acc = jnp.where(bucket == bkt, rel_ref[bkt, hh], acc)
            acc = jnp.where(bucket == SINK_BUCKET, sinks_ref[hh], acc)
            bias_ref[hh * SUBLANES:(hh + 1) * SUBLANES, :] = acc

    sub = lax.broadcasted_iota(jnp.int32, (SUBLANES, kvw), 0)
    new_rows = sub >= SUBLANES - steps
    lane_kv = lax.broadcasted_iota(jnp.int32, (SUBLANES, kvw), 1) // D_HEAD_DIM
    gw = D_KV_HEADS * D_HEAD_DIM
    zero_keys = jnp.zeros((n_keys - win - tile16, kvw), BF16)

    def extend(cache, new8):
        new16 = jnp.concatenate([new8, jnp.zeros_like(new8)], axis=0).astype(BF16)
        return jnp.concatenate([cache.astype(BF16), new16, zero_keys], axis=0)

    def shift_in(cache, new8, out_ref, bi):
        rolled = pltpu.roll(cache, win - steps, 0)
        out_ref[bi, 0:win - SUBLANES, :] = rolled[:win - SUBLANES]
        out_ref[bi, win - SUBLANES:, :] = jnp.where(new_rows, new8, rolled[win - SUBLANES:])

    def body(bi, carry):
        ck = ck_ref[bi]
        cv = cv_ref[bi]
        k8 = knew_ref[bi]
        v8 = vnew_ref[bi]
        shift_in(ck, k8, kout_ref, bi)
        shift_in(cv, v8, vout_ref, bi)
        q8 = q_ref[bi]
        pieces = []
        for kv in range(D_KV_HEADS):
            for grp in range(D_GROUP):
                qg = q8[:, grp * gw:(grp + 1) * gw]
                pieces.append(jnp.where(lane_kv == kv, qg, 0.0))
        lhs = jnp.concatenate(pieces, axis=0).astype(BF16)
        sc = _dot_nt(lhs, extend(ck, k8)) + bias_ref[...]
        m = jnp.max(sc, axis=-1, keepdims=True)
        pexp = jnp.exp(sc - m)
        probs = (pexp / jnp.sum(pexp, axis=-1, keepdims=True)).astype(BF16)
        ov = _dot(probs, extend(cv, v8))
        outs = []
        for grp in range(D_GROUP):
            r_last = ((D_KV_HEADS - 1) * D_GROUP + grp) * SUBLANES
            acc = ov[r_last:r_last + SUBLANES]
            for kv in range(D_KV_HEADS - 2, -1, -1):
                r0 = (kv * D_GROUP + grp) * SUBLANES
                acc = jnp.where(lane_kv == kv, ov[r0:r0 + SUBLANES], acc)
            outs.append(acc)
        o_ref[bi] = jnp.concatenate(outs, axis=1)
        return carry

    lax.fori_loop(0, bb, body, 0, unroll=SAMPLE_ATTN_UNROLL)


def _odd_sample_back_kernel(hs_ref, yc_ref, o_ref_in, wout_ref, out_ref, *, steps, d, width_c, qw):
    bt = hs_ref.shape[0]
    mix = jnp.concatenate(
        [jnp.concatenate([yc_ref[:, t * width_c:(t + 1) * width_c], o_ref_in[:, t * qw:(t + 1) * qw]], axis=1)
         for t in range(steps)], axis=0).astype(BF16)
    out = _dot(mix, wout_ref[...])
    for t in range(steps):
        out_ref[:, t * d:(t + 1) * d] = hs_ref[:, t * d:(t + 1) * d] + out[t * bt:(t + 1) * bt]


def _odd_sample(hs2, state_pool2, cache_k3, cache_v3, p, steps, past_len, bt, bb):
    nb, _ = hs2.shape
    d = p["wc"].shape[0]
    width_c = p["wc"].shape[1]
    qw = p["wq"].shape[1]
    kw = p["wk"].shape[1]
    win = cache_k3.shape[1]
    n_state = max(C_WINDOWS) - 1
    assert nb % bt == 0 and nb % bb == 0 and steps <= SUBLANES and win == CHUNK
    params = pltpu.CompilerParams(dimension_semantics=("arbitrary",), vmem_limit_bytes=VMEM_LIMIT)
    vm = ["linw", "cscale", "qn", "kn", "onesbd"]
    widths = dict(yc=steps * width_c, pool=n_state * width_c, q=SUBLANES * qw, knew=SUBLANES * kw,
                  vnew=SUBLANES * kw)
    yc, new_pool, q8, knew8, vnew8 = pl.pallas_call(
        functools.partial(_odd_sample_front_kernel, steps=steps, d=d, width_c=width_c, past_len=past_len),
        out_shape=tuple(jax.ShapeDtypeStruct((nb, w), F32) for w in widths.values()),
        grid=(nb // bt,),
        in_specs=[_row_tiled(steps * d, bt), _resident((1, d))]
        + [_resident(p[k].shape) for k in ("wc", "wq", "wk", "wv")]
        + [_resident(p[k].shape) for k in vm]
        + [_row_tiled(n_state * width_c, bt)],
        out_specs=tuple(_row_tiled(w, bt) for w in widths.values()),
        compiler_params=params,
        name="odd_sample_front",
    )(hs2, p["g"], p["wc"], p["wq"], p["wk"], p["wv"], *[p[k] for k in vm], state_pool2)

    n_keys = 2 * CHUNK
    pad = SUBLANES - steps
    bucket = np.full((SUBLANES, n_keys), -1, np.int32)
    for t in range(steps):
        q_pos = past_len + t
        k_pos = np.full(n_keys, -10 ** 9, np.int64)
        k_pos[:win] = past_len - win + np.arange(win)
        k_pos[win + pad:win + SUBLANES] = past_len + np.arange(steps)
        dist = q_pos - k_pos
        ok = (dist >= 0) & (dist < CHUNK) & (k_pos >= 0)
        bucket[t] = np.where(ok, _t5_bucket(np.where(ok, dist, 0)), -1)
    bucket[:, n_keys - 1] = SINK_BUCKET
    smem = pl.BlockSpec(memory_space=pltpu.SMEM)
    tile3 = lambda rows, width: pl.BlockSpec((bb, rows, width), lambda i: (i, 0, 0))
    o8, new_k, new_v = pl.pallas_call(
        functools.partial(_odd_sample_attn_kernel, bb=bb, steps=steps, n_keys=n_keys),
        out_shape=(jax.ShapeDtypeStruct((nb, SUBLANES, qw), F32),
                   jax.ShapeDtypeStruct(cache_k3.shape, F32),
                   jax.ShapeDtypeStruct(cache_v3.shape, F32)),
        grid=(nb // bb,),
        in_specs=[tile3(SUBLANES, qw), tile3(SUBLANES, kw), tile3(SUBLANES, kw), tile3(win, kw), tile3(win, kw),
                  _resident(bucket.shape), smem, smem],
        out_specs=(tile3(SUBLANES, qw), tile3(win, kw), tile3(win, kw)),
        scratch_shapes=[pltpu.VMEM((D_Q_HEADS * SUBLANES, n_keys), F32)],
        compiler_params=params,
        name="odd_sample_attn",
    )(q8.reshape(nb, SUBLANES, qw), knew8.reshape(nb, SUBLANES, kw), vnew8.reshape(nb, SUBLANES, kw),
      cache_k3, cache_v3, jnp.asarray(bucket), p["sinks"], p["rel"])

    out = pl.pallas_call(
        functools.partial(_odd_sample_back_kernel, steps=steps, d=d, width_c=width_c, qw=qw),
        out_shape=jax.ShapeDtypeStruct(hs2.shape, F32),
        grid=(nb // bt,),
        in_specs=[_row_tiled(steps * d, bt), _row_tiled(steps * width_c, bt), _row_tiled(SUBLANES * qw, bt),
                  _resident(p["wout"].shape)],
        out_specs=_row_tiled(steps * d, bt),
        compiler_params=params,
        name="odd_sample_back",
    )(hs2, yc, o8.reshape(nb, SUBLANES * qw), p["wout"])
    return out, new_pool, new_k, new_v


def _row(v):
    return v.reshape(1, -1).astype(F32)


def _pad_lanes(m, width=LANES):
    return jnp.pad(m, ((0, 0), (0, width - m.shape[1])))


def _head_expand_matrix():
    e = np.zeros((LANES, B_HEADS * B_HEAD_DIM), np.float32)
    for hh in range(B_HEADS):
        e[hh, hh * B_HEAD_DIM:(hh + 1) * B_HEAD_DIM] = 1.0
    return np.concatenate([e, e], axis=0)


def _column_select_matrix():
    e = np.zeros((LANES, B_HEADS * CHUNK), np.float32)
    for hh in range(B_HEADS):
        e[hh, hh * CHUNK:(hh + 1) * CHUNK] = 1.0
    return np.concatenate([e, e], axis=0)


def _prep_even(mix_norm, w_in, w_out, ln_g, ln_b, w_s, b_s, conv_w, conv_b, dt_bias, a_log, d_skip, norm_g):
    width_a = ln_g.shape[0]
    inner = norm_g.shape[0]
    conv_dim = conv_b.shape[0]
    o1 = 2 * width_a
    o2 = o1 + inner
    o3 = o2 + conv_dim
    return dict(
        g=_row(mix_norm),
        wa=w_in[:, :o1].astype(BF16),
        wz=w_in[:, o1:o2].astype(BF16),
        wxbc=w_in[:, o2:o3].astype(BF16),
        wdt=_pad_lanes(w_in[:, o3:]).astype(BF16),
        wout=w_out.astype(BF16),
        lng=_row(ln_g), lnb=_row(ln_b), ws=w_s,
        bsb=jnp.broadcast_to(b_s[:, :, None], b_s.shape + (width_a // A_HEADS,)),
        convw=conv_w, convb=_row(conv_b),
        dtb=_pad_lanes(_row(dt_bias)), alog=_pad_lanes(_row(a_log)),
        dskip=_row(jnp.repeat(d_skip, B_HEAD_DIM)), normg=_row(norm_g),
        expand=jnp.asarray(_head_expand_matrix(), BF16), colsel=jnp.asarray(_column_select_matrix(), BF16),
    )


def _prep_odd(mix_norm, w_in, w_out, lin_w, c_scale, q_norm, k_norm, sinks, rel_table):
    d = w_in.shape[0]
    width_c = c_scale.shape[0]
    qw = D_Q_HEADS * D_HEAD_DIM
    kw = D_KV_HEADS * D_HEAD_DIM
    wq = w_in[:, width_c:width_c + qw].reshape(d, D_KV_HEADS, D_GROUP, D_HEAD_DIM)
    wq = wq.transpose(0, 2, 1, 3).reshape(d, qw)
    wo_d = w_out[width_c:].reshape(D_KV_HEADS, D_GROUP, D_HEAD_DIM, -1).transpose(1, 0, 2, 3).reshape(qw, -1)
    ones_bd = np.kron(np.eye(256 // D_HEAD_DIM), np.ones((D_HEAD_DIM, D_HEAD_DIM))).astype(np.float32)
    return dict(
        g=_row(mix_norm),
        wc=w_in[:, :width_c].astype(BF16),
        wq=wq.astype(BF16),
        wk=w_in[:, width_c + qw:width_c + qw + kw].astype(BF16),
        wv=w_in[:, width_c + qw + kw:].astype(BF16),
        wout=jnp.concatenate([w_out[:width_c], wo_d], axis=0).astype(BF16),
        linw=lin_w.astype(BF16), cscale=_row(c_scale),
        qn=_row(jnp.tile(q_norm, D_Q_HEADS)), kn=_row(jnp.tile(k_norm, D_KV_HEADS)),
        onesbd=jnp.asarray(ones_bd, BF16),
        sinks=sinks.astype(F32), rel=rel_table.astype(F32),
    )


def _prep_even_sample(w_s, b_s, steps):
    head_w = CHUNK
    w = jnp.transpose(w_s[:, :steps, :steps], (1, 2, 0)).reshape(steps * steps, A_HEADS)
    b = b_s[:, :steps].T
    return dict(wts=jnp.repeat(w, head_w, axis=1), bts=jnp.repeat(b, head_w, axis=1))


PAST_LEN = 16384
FFN_TILE = 1024
FFN_STAGE_SLOTS = 4
FFN_STAGE_ROWS_GU = 32
FFN_STAGE_ROWS_D = 128
FFN_ROW_BLOCK = 256
FFN_BLOCK_TILES = 3
MIXER_TILE = 256
SAMPLE_ROW_TILE = 32
SAMPLE_SEQ_TILE = 8
SAMPLE_ATTN_UNROLL = 4


def kernel(x_prompt, x_sample, state_ssm, state_conv, state_pool, cache_k_win, cache_v_win,
           ffn1_norm, ffn1_w_gu, ffn1_w_down, mix_norm, ffn2_norm, ffn2_w_gu, ffn2_w_down,
           ev_w_in, ev_w_out, a_ln_g, a_ln_b, a_w_s, a_b_s, b_conv_w, b_conv_b, b_dt_bias, b_a_log,
           b_d_skip, b_norm_g, od_w_in, od_w_out, c_lin_w, c_scale, d_q_norm, d_k_norm, d_sinks,
           rel_bias_table):
    bp, seq, d = x_prompt.shape
    bs, steps, _ = x_sample.shape
    past_len = PAST_LEN
    hp = x_prompt
    hs = x_sample
    depth = ffn1_norm.shape[0]
    names = ("a_v_s", "ssm_p", "ssm_s", "conv_p", "conv_s", "pool_p", "pool_s", "k_p", "k_s", "v_p", "v_s")
    outs = {k: [] for k in names}

    def macaron(h_p, h_s, norm, w_gu_all, w_down_all, layer):
        o_p, o_s = _ffn(h_p.reshape(bp * seq, d), h_s.reshape(bs * steps, d), _row(norm), w_gu_all, w_down_all,
                        layer, FFN_TILE)
        return o_p.reshape(bp, seq, d), o_s.reshape(bs, steps, d)

    for layer in range(depth):
        i = layer // 2
        hp, hs = macaron(hp, hs, ffn1_norm[layer], ffn1_w_gu, ffn1_w_down, layer)
        hs2 = hs.reshape(bs, steps * d)
        if layer % 2 == 0:
            p = _prep_even(mix_norm[layer], ev_w_in[i], ev_w_out[i], a_ln_g[i], a_ln_b[i], a_w_s[i], a_b_s[i],
                           b_conv_w[i], b_conv_b[i], b_dt_bias[i], b_a_log[i], b_d_skip[i], b_norm_g[i])
            p.update(_prep_even_sample(a_w_s[i], a_b_s[i], steps))
            hp, conv_p, ssm_p = _even_prompt(hp, p, MIXER_TILE)
            hs2, v_rows, conv_s, ssm_s = _even_sample(
                hs2, state_conv[i].reshape(bs, -1), state_ssm[i].reshape(bs, B_HEADS * B_HEAD_DIM, B_STATE),
                p, steps, SAMPLE_ROW_TILE, SAMPLE_SEQ_TILE)
            outs["a_v_s"].append(v_rows.reshape(bs, steps, -1))
            outs["conv_p"].append(conv_p)
            outs["conv_s"].append(conv_s.reshape(state_conv[i].shape))
            outs["ssm_p"].append(ssm_p.reshape(bp, B_HEADS, B_HEAD_DIM, B_STATE))
            outs["ssm_s"].append(ssm_s.reshape(state_ssm[i].shape))
        else:
            p = _prep_odd(mix_norm[layer], od_w_in[i], od_w_out[i], c_lin_w[i], c_scale[i], d_q_norm[i],
                          d_k_norm[i], d_sinks[i], rel_bias_table)
            hp, pool_p, k_p, v_p = _odd_prompt(hp, p, MIXER_TILE)
            kv_shape = cache_k_win[i].shape
            hs2, pool_s, k_s, v_s = _odd_sample(
                hs2, state_pool[i].reshape(bs, -1), cache_k_win[i].reshape(bs, kv_shape[1], -1),
                cache_v_win[i].reshape(bs, kv_shape[1], -1), p, steps, past_len, SAMPLE_ROW_TILE, SAMPLE_SEQ_TILE)
            outs["pool_p"].append(pool_p)
            outs["pool_s"].append(pool_s.reshape(state_pool[i].shape))
            outs["k_p"].append(k_p.reshape(bp, CHUNK, D_KV_HEADS, D_HEAD_DIM))
            outs["v_p"].append(v_p.reshape(bp, CHUNK, D_KV_HEADS, D_HEAD_DIM))
            outs["k_s"].append(k_s.reshape(kv_shape))
            outs["v_s"].append(v_s.reshape(kv_shape))
        hs = hs2.reshape(bs, steps, d)
        hp, hs = macaron(hp, hs, ffn2_norm[layer], ffn2_w_gu, ffn2_w_down, layer)
    return (hp, hs) + tuple(jnp.stack(outs[k]) for k in names)
```

```python
import functools
import math

import numpy as np
import jax
import jax.numpy as jnp
from jax import lax
from jax.experimental import pallas as pl
from jax.experimental.pallas import tpu as pltpu

F32 = jnp.float32
BF16 = jnp.bfloat16

EPS = 1e-6
NEG = -1e30

LANES = 128
SUBLANES = 8
MXU_DIM = 256
VMEM_BYTES_V7X = 64 * 1024 * 1024
VMEM_LIMIT = VMEM_BYTES_V7X - 8 * 1024 * 1024

A_HEADS = 8
B_HEADS = 16
B_HEAD_DIM = 64
B_GROUPS = 2
B_STATE = 128
B_CONV = 4
CHUNK = 128
C_WINDOWS = (2, 4, 8, 16)
C_HALO = 16
D_Q_HEADS = 16
D_KV_HEADS = 4
D_HEAD_DIM = 64
D_GROUP = D_Q_HEADS // D_KV_HEADS
REL_BUCKETS = 32
REL_MAX_DIST = 128


def _rms(x, g):
    ms = jnp.mean(x * x, axis=-1, keepdims=True)
    return x * lax.rsqrt(ms + EPS) * g


def _sigmoid(x):
    return 1.0 / (1.0 + jnp.exp(-x))


def _silu(x):
    return x * _sigmoid(x)


def _gelu_tanh(x):
    c = math.sqrt(2.0 / math.pi)
    return x * (0.5 * (1.0 + jnp.tanh(c * (x + 0.044715 * (x * x * x)))))


def _softplus(x):
    return jnp.maximum(x, 0.0) + jnp.log1p(jnp.exp(-jnp.abs(x)))


def _split3(x):
    hi = x.astype(BF16)
    r1 = x - hi.astype(F32)
    mid = r1.astype(BF16)
    lo = (r1 - mid.astype(F32)).astype(BF16)
    return hi, mid, lo


def _split2_lanes(x):
    hi = x.astype(BF16)
    lo = (x - hi.astype(F32)).astype(BF16)
    return jnp.concatenate([hi, lo], axis=1)


def _dot(a, b):
    return jnp.dot(a, b, preferred_element_type=F32)


def _dot_nt(a, b):
    return lax.dot_general(a, b, (((1,), (1,)), ((), ())), preferred_element_type=F32)


def _dot_tn(a, b):
    return lax.dot_general(a, b, (((0,), (0,)), ((), ())), preferred_element_type=F32)


def _expand_heads(m, n_pairs):
    rows = m.shape[0]
    lane = lax.broadcasted_iota(jnp.int32, (rows, LANES), 1)
    first = lane < B_HEAD_DIM
    parts = []
    for p in range(n_pairs):
        a = jnp.broadcast_to(m[:, 2 * p:2 * p + 1], (rows, LANES))
        b = jnp.broadcast_to(m[:, 2 * p + 1:2 * p + 2], (rows, LANES))
        parts.append(jnp.where(first, a, b))
    return jnp.concatenate(parts, axis=1)


def _head_sumsq(x, ones_bd):
    xx = (x * x).astype(BF16)
    outs = [_dot(xx[:, c * 256:(c + 1) * 256], ones_bd) for c in range(x.shape[1] // 256)]
    return jnp.concatenate(outs, axis=1) if len(outs) > 1 else outs[0]


def _ff_blocks(d_ff):
    step = FFN_BLOCK_TILES * MXU_DIM
    return [(c0, min(c0 + step, d_ff)) for c0 in range(0, d_ff, step)]


def _load_cast_rows(src_hbm, dst_ref, stage_ref, sem_ref):
    slots, rows, _ = stage_ref.shape
    n = src_hbm.shape[0] // rows

    def copy(c):
        slot = c % slots
        return pltpu.make_async_copy(src_hbm.at[pl.ds(c * rows, rows), :], stage_ref.at[slot], sem_ref.at[slot])

    for c in range(min(slots, n)):
        copy(c).start()
    for c in range(n):
        copy(c).wait()
        dst_ref[c * rows:(c + 1) * rows, :] = stage_ref[c % slots].astype(BF16)
        if c + slots < n:
            copy(c + slots).start()


def _ffn_rows(x_ref, g_ref, wgu_ref, wd_ref, o_ref, d_ff):
    tm = x_ref.shape[0]
    rows = [slice(r0, min(r0 + FFN_ROW_BLOCK, tm)) for r0 in range(0, tm, FFN_ROW_BLOCK)]
    xns = [_rms(x_ref[rs, :], g_ref[...]).astype(BF16) for rs in rows]
    for rs, xn in zip(rows, xns):
        y = None
        for c0, c1 in _ff_blocks(d_ff):
            gate = _dot(xn, wgu_ref[:, c0:c1])
            up = _dot(xn, wgu_ref[:, d_ff + c0:d_ff + c1])
            act = (_silu(gate) * up).astype(BF16)
            part = _dot(act, wd_ref[c0:c1, :])
            y = part if y is None else y + part
        o_ref[rs, :] = x_ref[rs, :] + 0.5 * y


def _ffn_kernel(xp_ref, xs_ref, g_ref, wgu_hbm, wd_hbm, op_ref, os_ref,
                wgu_ref, wd_ref, stage_gu_ref, stage_d_ref, sem_ref, *, d_ff, prompt_steps, layer):
    i = pl.program_id(0)

    @pl.when(i == 0)
    def _():
        _load_cast_rows(wgu_hbm.at[layer], wgu_ref, stage_gu_ref, sem_ref)
        _load_cast_rows(wd_hbm.at[layer], wd_ref, stage_d_ref, sem_ref)

    @pl.when(i < prompt_steps)
    def _():
        _ffn_rows(xp_ref, g_ref, wgu_ref, wd_ref, op_ref, d_ff)

    @pl.when(i == prompt_steps)
    def _():
        _ffn_rows(xs_ref, g_ref, wgu_ref, wd_ref, os_ref, d_ff)


def _resident(shape):
    nd = len(shape)
    return pl.BlockSpec(shape, lambda *_: (0,) * nd, pipeline_mode=pl.Buffered(1))


def _ffn(xp2d, xs2d, g, wgu_all, wd_all, layer, tm):
    m, d = xp2d.shape
    ms = xs2d.shape[0]
    wgu_shape, wd_shape = wgu_all.shape[1:], wd_all.shape[1:]
    d_ff = wd_shape[0]
    assert m % tm == 0 and d % FFN_STAGE_ROWS_GU == 0 and d_ff % FFN_STAGE_ROWS_D == 0
    steps = m // tm
    last = steps - 1
    whole = lambda shape: pl.BlockSpec(shape, lambda i: (0, 0))
    prompt_tile = pl.BlockSpec((tm, d), lambda i: (jnp.minimum(i, last), 0))
    hbm = pl.BlockSpec(memory_space=pl.ANY)
    return pl.pallas_call(
        functools.partial(_ffn_kernel, d_ff=d_ff, prompt_steps=steps, layer=layer),
        out_shape=(jax.ShapeDtypeStruct((m, d), F32), jax.ShapeDtypeStruct((ms, d), F32)),
        grid=(steps + 1,),
        in_specs=[prompt_tile, whole((ms, d)), _resident((1, d)), hbm, hbm],
        out_specs=(prompt_tile, whole((ms, d))),
        scratch_shapes=[pltpu.VMEM(wgu_shape, BF16), pltpu.VMEM(wd_shape, BF16),
                        pltpu.VMEM((FFN_STAGE_SLOTS, FFN_STAGE_ROWS_GU, wgu_shape[1]), F32),
                        pltpu.VMEM((FFN_STAGE_SLOTS, FFN_STAGE_ROWS_D, wd_shape[1]), F32),
                        pltpu.SemaphoreType.DMA((FFN_STAGE_SLOTS,))],
        compiler_params=pltpu.CompilerParams(
            dimension_semantics=("arbitrary",), vmem_limit_bytes=VMEM_LIMIT),
        name="ffn",
    )(xp2d, xs2d, g, wgu_all, wd_all)


def _even_prompt_kernel(h_ref, g_ref, wa_ref, wz_ref, wxbc_ref, wdt_ref, wout_ref,
                        lng_ref, lnb_ref, ws_ref, bsb_ref, convw_ref, convb_ref,
                        dtb_ref, alog_ref, dskip_ref, normg_ref, expand_ref, colsel_ref,
                        o_ref, conv_out_ref, ssm_out_ref,
                        ext_ref, st_ref, *, tile, width_a, inner):
    s = pl.program_id(1)
    n_chunks = tile // CHUNK
    halo = SUBLANES

    @pl.when(s == 0)
    def _():
        ext_ref[0:halo, :] = jnp.zeros((halo, ext_ref.shape[1]), F32)
        st_ref[...] = jnp.zeros(st_ref.shape, F32)

    x = h_ref[...]
    xn = _rms(x, g_ref[...]).astype(BF16)

    row = lax.broadcasted_iota(jnp.int32, (CHUNK, CHUNK), 0)
    col = lax.broadcasted_iota(jnp.int32, (CHUNK, CHUNK), 1)
    causal = row >= col
    lane = lax.broadcasted_iota(jnp.int32, (CHUNK, LANES), 1)
    first_half = lane < B_HEAD_DIM

    pa = _dot(xn, wa_ref[...])
    xbc_raw = _dot(xn, wxbc_ref[...])

    ga = _gelu_tanh(pa)
    u = ga[:, :width_a]
    v = ga[:, width_a:]
    mu = jnp.mean(v, axis=-1, keepdims=True)
    vc = v - mu
    var = jnp.mean(vc * vc, axis=-1, keepdims=True)
    v = vc * lax.rsqrt(var + EPS) * lng_ref[...] + lnb_ref[...]
    vb = v.astype(BF16)

    z = _dot(xn, wz_ref[...])
    dt_raw = _dot(xn, wdt_ref[...])

    ext_ref[halo:halo + tile, :] = xbc_raw
    ext = ext_ref[...]
    ext1 = pltpu.roll(ext, 1, 0)
    pair = ext * convw_ref[1:2, :] + ext1 * convw_ref[0:1, :]
    conv = (convb_ref[...] + ext * convw_ref[3:4, :] + ext1 * convw_ref[2:3, :] + pltpu.roll(pair, 2, 0))[halo:]
    tail = ext_ref[tile:tile + halo, :]
    ext_ref[0:halo, :] = tail
    conv_out_ref[...] = tail
    xbc = _silu(conv)
    gn = B_GROUPS * B_STATE
    xs = xbc[:, :inner]
    bm = xbc[:, inner:inner + gn]
    cm = xbc[:, inner + gn:]

    head_w = width_a // A_HEADS
    gate_cols = []
    for hh in range(A_HEADS):
        w = jnp.where(causal, ws_ref[hh], 0.0).astype(BF16)
        rhs = jnp.concatenate(
            [vb[c * CHUNK:(c + 1) * CHUNK, hh * head_w:(hh + 1) * head_w] for c in range(n_chunks)], axis=1)
        out = _dot(w, rhs)
        bias = bsb_ref[hh]
        gate_cols.append(jnp.concatenate(
            [out[:, c * head_w:(c + 1) * head_w] + bias for c in range(n_chunks)], axis=0))
    ya = u * jnp.concatenate(gate_cols, axis=1)
    out_a = _dot(ya.astype(BF16), wout_ref[0:width_a, :])

    dt = _softplus(dt_raw + dtb_ref[...])
    a_neg = -jnp.exp(alog_ref[...])
    da = dt * a_neg

    n_pairs = B_HEADS // 2
    heads_per_group = B_HEADS // B_GROUPS
    gw = heads_per_group * B_HEAD_DIM
    tril_ones = jnp.where(causal, 1.0, 0.0).astype(BF16)
    chunks = [slice(c * CHUNK, (c + 1) * CHUNK) for c in range(n_chunks)]
    acums = []
    for rs in chunks:
        d_hi, d_mid, d_lo = _split3(da[rs])
        acums.append(_dot(tril_ones, d_hi) + _dot(tril_ones, d_mid) + _dot(tril_ones, d_lo))
    acum = jnp.concatenate(acums, axis=0)
    decay = jnp.concatenate([jnp.exp(a[CHUNK - 1:CHUNK, :] - a) for a in acums], axis=0)
    expand = expand_ref[...]
    xd = xs * _dot(_split2_lanes(dt), expand)
    xdwb = (xs * _dot(_split2_lanes(dt * decay), expand)).astype(BF16)
    e_acum = _dot(_split2_lanes(jnp.exp(acum)), expand)
    bmb = bm.astype(BF16)
    cmb = cm.astype(BF16)
    y_rows = []
    for c, rs in enumerate(chunks):
        a_c = acums[c]
        acum_t = a_c.T
        a_cols = jnp.concatenate([jnp.broadcast_to(a_c[:, hh:hh + 1], (CHUNK, CHUNK)) for hh in range(B_HEADS)],
                                 axis=1)
        cb = [_dot_nt(cmb[rs, g * B_STATE:(g + 1) * B_STATE], bmb[rs, g * B_STATE:(g + 1) * B_STATE])
              for g in range(B_GROUPS)]
        y_parts = []
        for p in range(n_pairs):
            g = (2 * p) // heads_per_group
            ms = []
            for hh in (2 * p, 2 * p + 1):
                seg = a_cols[:, hh * CHUNK:(hh + 1) * CHUNK] - jnp.broadcast_to(acum_t[hh:hh + 1, :], (CHUNK, CHUNK))
                lmat = jnp.where(causal, jnp.exp(seg), 0.0)
                ms.append((cb[g] * lmat).astype(BF16))
            lhs = jnp.concatenate(ms, axis=1)
            xd_p = xd[rs, p * LANES:(p + 1) * LANES]
            rhs = jnp.concatenate([jnp.where(first_half, xd_p, 0.0),
                                   jnp.where(first_half, 0.0, xd_p)], axis=0).astype(BF16)
            y_parts.append(_dot(lhs, rhs))
        y_rows.append(jnp.concatenate(y_parts, axis=1))
    for c, rs in enumerate(chunks):
        st_prev = st_ref[...]
        stb = st_prev.astype(BF16)
        y_off = jnp.concatenate(
            [_dot(cmb[rs, g * B_STATE:(g + 1) * B_STATE], stb[:, g * gw:(g + 1) * gw]) for g in range(B_GROUPS)],
            axis=1)
        st_add = jnp.concatenate(
            [_dot_tn(bmb[rs, g * B_STATE:(g + 1) * B_STATE], xdwb[rs, g * gw:(g + 1) * gw])
             for g in range(B_GROUPS)], axis=1)
        chunk_decay = e_acum[(c + 1) * CHUNK - 1:(c + 1) * CHUNK, :]
        st_ref[...] = st_prev * chunk_decay + st_add
        y_rows[c] = y_rows[c] + y_off * e_acum[rs]
    y = (jnp.concatenate(y_rows, axis=0) if n_chunks > 1 else y_rows[0]) + xs * dskip_ref[...]
    y = y * _silu(z)
    half = inner // B_GROUPS
    yn = []
    for g in range(B_GROUPS):
        yg = y[:, g * half:(g + 1) * half]
        yn.append(yg * lax.rsqrt(jnp.mean(yg * yg, axis=-1, keepdims=True) + EPS))
    yb = jnp.concatenate(yn, axis=1) * normg_ref[...]

    o_ref[...] = x + out_a + _dot(yb.astype(BF16), wout_ref[width_a:, :])

    @pl.when(s == pl.num_programs(1) - 1)
    def _():
        ssm_out_ref[...] = st_ref[...].T


def _even_prompt(h, p, tile):
    b, seq, d = h.shape
    width_a = p["wa"].shape[1] // 2
    inner = p["wz"].shape[1]
    conv_dim = p["wxbc"].shape[1]
    assert seq % tile == 0 and tile % CHUNK == 0
    small = ["lng", "lnb", "ws", "bsb", "convw", "convb", "dtb", "alog", "dskip", "normg", "expand", "colsel"]
    out, conv_tail, ssm = pl.pallas_call(
        functools.partial(_even_prompt_kernel, tile=tile, width_a=width_a, inner=inner),
        out_shape=(jax.ShapeDtypeStruct((b, seq, d), F32),
                   jax.ShapeDtypeStruct((b, SUBLANES, conv_dim), F32),
                   jax.ShapeDtypeStruct((b, inner, B_STATE), F32)),
        grid=(b, seq // tile),
        in_specs=[pl.BlockSpec((None, tile, d), lambda i, j: (i, j, 0)),
                  _resident((1, d))]
        + [_resident(p[k].shape) for k in ("wa", "wz", "wxbc", "wdt", "wout")]
        + [_resident(p[k].shape) for k in small],
        out_specs=(pl.BlockSpec((None, tile, d), lambda i, j: (i, j, 0)),
                   pl.BlockSpec((None, SUBLANES, conv_dim), lambda i, j: (i, 0, 0)),
                   pl.BlockSpec((None, inner, B_STATE), lambda i, j: (i, 0, 0))),
        scratch_shapes=[pltpu.VMEM((tile + SUBLANES, conv_dim), F32),
                        pltpu.VMEM((B_STATE, inner), F32)],
        compiler_params=pltpu.CompilerParams(
            dimension_semantics=("arbitrary", "arbitrary"), vmem_limit_bytes=VMEM_LIMIT),
        name="even_prompt",
    )(h, p["g"], p["wa"], p["wz"], p["wxbc"], p["wdt"], p["wout"], *[p[k] for k in small])
    return out, conv_tail[:, SUBLANES - (B_CONV - 1):, :], ssm


def _t5_bucket(dist):
    n = np.maximum(dist, 0)
    max_exact = REL_BUCKETS // 2
    n_safe = np.maximum(n, 1).astype(np.float32)
    scale = np.float32((REL_BUCKETS - max_exact) / math.log(REL_MAX_DIST / max_exact))
    large = max_exact + (np.log(n_safe / max_exact) * scale).astype(np.int32)
    large = np.minimum(large, REL_BUCKETS - 1)
    return np.where(n < max_exact, n, large).astype(np.int32)


def _fill_bias(bias_ref, bucket_ref, rel_ref):
    bucket = bucket_ref[...]
    lq = bucket.shape[0]
    has_prev = lax.broadcasted_iota(jnp.int32, bucket.shape, 1) >= CHUNK
    for hh in range(D_Q_HEADS):
        kv, grp = divmod(hh, D_GROUP)
        acc = jnp.full(bucket.shape, NEG, F32)
        for bkt in range(REL_BUCKETS):
            acc = jnp.where(bucket == bkt, rel_ref[bkt, hh], acc)
        bias_ref[0, grp, kv * lq:(kv + 1) * lq, :] = acc
        bias_ref[1, grp, kv * lq:(kv + 1) * lq, :] = jnp.where(has_prev, acc, NEG)


def _group_attention(qg, kk, vv, bias_ref, sinks_ref, grp, table):
    lq = qg.shape[0]
    lane_kv = lax.broadcasted_iota(jnp.int32, (lq, D_KV_HEADS * D_HEAD_DIM), 1) // D_HEAD_DIM
    zero = jnp.zeros_like(qg)
    lhs = jnp.concatenate([jnp.where(lane_kv == kv, qg, zero) for kv in range(D_KV_HEADS)], axis=0)
    sc = _dot_nt(lhs, kk)
    probs = []
    for kv in range(D_KV_HEADS):
        rs = slice(kv * lq, (kv + 1) * lq)
        s_h = sc[rs] + bias_ref[table, grp, rs, :]
        sink = sinks_ref[kv * D_GROUP + grp]
        m = jnp.maximum(jnp.max(s_h, axis=-1, keepdims=True), sink)
        pexp = jnp.exp(s_h - m)
        denom = jnp.sum(pexp, axis=-1, keepdims=True) + jnp.exp(sink - m)
        probs.append((pexp / denom).astype(BF16))
    ov = _dot(jnp.concatenate(probs, axis=0), vv)
    out = ov[(D_KV_HEADS - 1) * lq:]
    for kv in range(D_KV_HEADS - 2, -1, -1):
        out = jnp.where(lane_kv == kv, ov[kv * lq:(kv + 1) * lq], out)
    return out


def _odd_prompt_kernel(h_ref, g_ref, wc_ref, wq_ref, wk_ref, wv_ref, wout_ref,
                       linw_ref, cscale_ref, qn_ref, kn_ref, onesbd_ref, bucket_ref,
                       sinks_ref, rel_ref,
                       o_ref, pool_out_ref, k_out_ref, v_out_ref,
                       extc_ref, kprev_ref, vprev_ref, bias_ref, *, tile, width_c):
    b = pl.program_id(0)
    s = pl.program_id(1)
    n_blocks = tile // CHUNK

    @pl.when((b == 0) & (s == 0))
    def _():
        _fill_bias(bias_ref, bucket_ref, rel_ref)

    @pl.when(s == 0)
    def _():
        extc_ref[0:C_HALO, :] = jnp.zeros((C_HALO, width_c), F32)
        kprev_ref[...] = jnp.zeros(kprev_ref.shape, F32)
        vprev_ref[...] = jnp.zeros(vprev_ref.shape, F32)

    x = h_ref[...]
    xn = _rms(x, g_ref[...]).astype(BF16)

    c_in = _dot(xn, wc_ref[...])
    extc_ref[C_HALO:C_HALO + tile, :] = c_in
    e = extc_ref[...]
    tail = extc_ref[tile:tile + C_HALO, :]
    extc_ref[0:C_HALO, :] = tail
    pool_out_ref[...] = tail
    pos = (s * tile + lax.broadcasted_iota(jnp.int32, (tile, 1), 0) + 1).astype(F32)
    gdim = width_c // len(C_WINDOWS)
    run = e
    shift = 1
    yc = []
    for gi, win in enumerate(C_WINDOWS):
        while shift < win:
            run = run + pltpu.roll(run, shift, 0)
            shift *= 2
        cnt = jnp.minimum(pos, float(win))
        pooled = run[C_HALO:, :gdim] / cnt - c_in[:, gi * gdim:(gi + 1) * gdim]
        yc.append(_dot(pooled.astype(BF16), linw_ref[gi]))
        if gi + 1 < len(C_WINDOWS):
            run = run[:, gdim:]
    yc = jnp.concatenate(yc, axis=1) * cscale_ref[...]

    q = _dot(xn, wq_ref[...])
    k = _dot(xn, wk_ref[...])
    v = _dot(xn, wv_ref[...])
    ones_bd = onesbd_ref[...]
    inv_d = 1.0 / D_HEAD_DIM
    qn = q * lax.rsqrt(_head_sumsq(q, ones_bd) * inv_d + EPS) * qn_ref[...]
    kn = k * lax.rsqrt(_head_sumsq(k, ones_bd) * inv_d + EPS) * kn_ref[...]
    qs = (qn * (D_HEAD_DIM ** -0.5)).astype(BF16)
    kb = kn.astype(BF16)
    vb = v.astype(BF16)
    first_table = jnp.where(s == 0, 1, 0)
    gw = D_KV_HEADS * D_HEAD_DIM
    o_rows = []
    for blk in range(n_blocks):
        rs = slice(blk * CHUNK, (blk + 1) * CHUNK)
        if blk == 0:
            k_prev, v_prev = kprev_ref[...].astype(BF16), vprev_ref[...].astype(BF16)
        else:
            k_prev, v_prev = kb[(blk - 1) * CHUNK:blk * CHUNK], vb[(blk - 1) * CHUNK:blk * CHUNK]
        kk = jnp.concatenate([k_prev, kb[rs]], axis=0)
        vv = jnp.concatenate([v_prev, vb[rs]], axis=0)
        o_rows.append(jnp.concatenate(
            [_group_attention(qs[rs, grp * gw:(grp + 1) * gw], kk, vv, bias_ref, sinks_ref, grp,
                              first_table if blk == 0 else 0)
             for grp in range(D_GROUP)], axis=1))
    kprev_ref[...] = kn[tile - CHUNK:]
    vprev_ref[...] = v[tile - CHUNK:]
    yd = jnp.concatenate(o_rows, axis=0) if n_blocks > 1 else o_rows[0]

    mix = jnp.concatenate([yc, yd], axis=1).astype(BF16)
    o_ref[...] = x + _dot(mix, wout_ref[...])

    @pl.when(s == pl.num_programs(1) - 1)
    def _():
        k_out_ref[...] = kn[tile - CHUNK:]
        v_out_ref[...] = v[tile - CHUNK:]


def _odd_prompt(h, p, tile):
    b, seq, d = h.shape
    width_c = p["wc"].shape[1]
    kvw = p["wk"].shape[1]
    assert seq % tile == 0 and tile % CHUNK == 0
    r = np.arange(CHUNK) + CHUNK
    c = np.arange(2 * CHUNK)
    dist = r[:, None] - c[None, :]
    bucket = np.where((dist >= 0) & (dist < CHUNK), _t5_bucket(dist), -1).astype(np.int32)
    vm = ["linw", "cscale", "qn", "kn", "onesbd"]
    smem = pl.BlockSpec(memory_space=pltpu.SMEM)
    out, pool_tail, k_win, v_win = pl.pallas_call(
        functools.partial(_odd_prompt_kernel, tile=tile, width_c=width_c),
        out_shape=(jax.ShapeDtypeStruct((b, seq, d), F32),
                   jax.ShapeDtypeStruct((b, C_HALO, width_c), F32),
                   jax.ShapeDtypeStruct((b, CHUNK, kvw), F32),
                   jax.ShapeDtypeStruct((b, CHUNK, kvw), F32)),
        grid=(b, seq // tile),
        in_specs=[pl.BlockSpec((None, tile, d), lambda i, j: (i, j, 0)),
                  _resident((1, d))]
        + [_resident(p[k].shape) for k in ("wc", "wq", "wk", "wv", "wout")]
        + [_resident(p[k].shape) for k in vm]
        + [_resident(bucket.shape), smem, smem],
        out_specs=(pl.BlockSpec((None, tile, d), lambda i, j: (i, j, 0)),
                   pl.BlockSpec((None, C_HALO, width_c), lambda i, j: (i, 0, 0)),
                   pl.BlockSpec((None, CHUNK, kvw), lambda i, j: (i, 0, 0)),
                   pl.BlockSpec((None, CHUNK, kvw), lambda i, j: (i, 0, 0))),
        scratch_shapes=[pltpu.VMEM((tile + C_HALO, width_c), F32),
                        pltpu.VMEM((CHUNK, kvw), F32),
                        pltpu.VMEM((CHUNK, kvw), F32),
                        pltpu.VMEM((2, D_GROUP, D_KV_HEADS * CHUNK, 2 * CHUNK), F32)],
        compiler_params=pltpu.CompilerParams(
            dimension_semantics=("arbitrary", "arbitrary"), vmem_limit_bytes=VMEM_LIMIT),
        name="odd_prompt",
    )(h, p["g"], p["wc"], p["wq"], p["wk"], p["wv"], p["wout"], *[p[k] for k in vm],
      jnp.asarray(bucket), p["sinks"], p["rel"])
    return out, pool_tail[:, C_HALO - (max(C_WINDOWS) - 1):, :], k_win, v_win


def _steps(x, n, width):
    return [x[:, t * width:(t + 1) * width] for t in range(n)]


def _stack_steps(ref, n, width):
    x = ref[...]
    return jnp.concatenate(_steps(x, n, width), axis=0)


def _even_sample_front_kernel(hs_ref, g_ref, wa_ref, wz_ref, wxbc_ref, wdt_ref,
                              lng_ref, lnb_ref, wts_ref, bts_ref, convw_ref, convb_ref,
                              dtb_ref, alog_ref, dskip_ref, cs_ref,
                              v_out, ya_out, ypart_out, eacum_out, z_out, conv_out,
                              cgt_out, xdw_out, bs_out, dec_out, *, steps, d, width_a, inner, conv_dim):
    bt = hs_ref.shape[0]
    xn = _rms(_stack_steps(hs_ref, steps, d), g_ref[...]).astype(BF16)
    blk = lambda a, t: a[t * bt:(t + 1) * bt]

    ga = _gelu_tanh(_dot(xn, wa_ref[...]))
    u = ga[:, :width_a]
    v = ga[:, width_a:]
    mu = jnp.mean(v, axis=-1, keepdims=True)
    vc = v - mu
    var = jnp.mean(vc * vc, axis=-1, keepdims=True)
    v = vc * lax.rsqrt(var + EPS) * lng_ref[...] + lnb_ref[...]
    for t in range(steps):
        v_out[:, t * width_a:(t + 1) * width_a] = blk(v, t)
        gate = bts_ref[t:t + 1, :]
        for s in range(t + 1):
            gate = gate + wts_ref[t * steps + s:t * steps + s + 1, :] * blk(v, s)
        ya_out[:, t * width_a:(t + 1) * width_a] = blk(u, t) * gate

    z = _dot(xn, wz_ref[...])
    for t in range(steps):
        z_out[:, t * inner:(t + 1) * inner] = blk(z, t)
    raw = _dot(xn, wxbc_ref[...])
    dt = _softplus(_dot(xn, wdt_ref[...]) + dtb_ref[...])
    ext = _steps(cs_ref[...], B_CONV - 1, conv_dim) + [blk(raw, t) for t in range(steps)]
    for k in range(B_CONV - 1):
        conv_out[:, k * conv_dim:(k + 1) * conv_dim] = ext[len(ext) - (B_CONV - 1) + k]
    gn = B_GROUPS * B_STATE
    n_pairs = B_HEADS // 2
    a_neg = -jnp.exp(alog_ref[...])
    xs, bm, cm, dts, acum = [], [], [], [], []
    for t in range(steps):
        conv = convb_ref[...]
        for tap in range(B_CONV):
            conv = conv + ext[t + tap] * convw_ref[tap:tap + 1, :]
        xbc = _silu(conv)
        xs.append(xbc[:, :inner])
        bm.append(xbc[:, inner:inner + gn])
        cm.append(xbc[:, inner + gn:])
        dts.append(blk(dt, t))
        da = dts[t] * a_neg
        acum.append(da if t == 0 else acum[t - 1] + da)
    lane = lax.broadcasted_iota(jnp.int32, (bt, LANES), 1)
    group0 = lane < (B_HEADS // B_GROUPS)
    dec_out[...] = jnp.exp(acum[steps - 1])
    pad_rows = SUBLANES - steps
    xdw_out[:, steps * inner:] = jnp.zeros((bt, pad_rows * inner), F32)
    bs_out[:, steps * gn:] = jnp.zeros((bt, pad_rows * gn), F32)
    xd = []
    for t in range(steps):
        xd.append(xs[t] * _expand_heads(dts[t], n_pairs))
        eacum_out[:, t * inner:(t + 1) * inner] = _expand_heads(jnp.exp(acum[t]), n_pairs)
        xdw_out[:, t * inner:(t + 1) * inner] = xs[t] * _expand_heads(
            dts[t] * jnp.exp(acum[steps - 1] - acum[t]), n_pairs)
        bs_out[:, t * gn:(t + 1) * gn] = bm[t]
        for g in range(B_GROUPS):
            r = g * steps + t
            cgt_out[:, r * B_STATE:(r + 1) * B_STATE] = cm[t][:, g * B_STATE:(g + 1) * B_STATE]
    for t in range(steps):
        y = xs[t] * dskip_ref[...]
        for s in range(t + 1):
            cb = [jnp.sum(cm[t][:, g * B_STATE:(g + 1) * B_STATE] * bm[s][:, g * B_STATE:(g + 1) * B_STATE],
                          axis=-1, keepdims=True) for g in range(B_GROUPS)]
            coef = jnp.where(group0, cb[0], cb[1]) * jnp.exp(acum[t] - acum[s])
            y = y + _expand_heads(coef, n_pairs) * xd[s]
        ypart_out[:, t * inner:(t + 1) * inner] = y


def _even_sample_state_kernel(s0_ref, cgt_ref, xdw_ref, bs_ref, dec_ref, yoff_ref, snew_ref, *, bb):
    step = pl.program_id(0)
    gw = (B_HEADS // B_GROUPS) * B_HEAD_DIM

    def body(bi, carry):
        s0 = s0_ref[bi]
        c8 = cgt_ref[bi]
        c16 = jnp.concatenate([c8, jnp.zeros_like(c8)], axis=0).astype(BF16)
        yoff_ref[bi] = _dot_nt(c16, s0.astype(BF16))[:SUBLANES]
        x8 = xdw_ref[bi]
        b8 = bs_ref[bi]
        x16 = jnp.concatenate([x8, jnp.zeros_like(x8)], axis=0).astype(BF16)
        b16 = jnp.concatenate([b8, jnp.zeros_like(b8)], axis=0).astype(BF16)
        for g in range(B_GROUPS):
            add = _dot_tn(x16[:, g * gw:(g + 1) * gw], b16[:, g * B_STATE:(g + 1) * B_STATE])
            for hl in range(B_HEADS // B_GROUPS):
                hh = g * (B_HEADS // B_GROUPS) + hl
                rs = slice(hh * B_HEAD_DIM, (hh + 1) * B_HEAD_DIM)
                snew_ref[bi, rs, :] = s0[rs] * dec_ref[step * bb + bi, hh] + \
                    add[hl * B_HEAD_DIM:(hl + 1) * B_HEAD_DIM]
        return carry

    lax.fori_loop(0, bb, body, 0, unroll=2)


def _even_sample_back_kernel(hs_ref, ya_ref, ypart_ref, eacum_ref, z_ref, yoff_ref, normg_ref, wout_ref,
                             o_ref, *, steps, d, inner):
    bt = hs_ref.shape[0]
    half = inner // B_GROUPS
    mixes = []
    for t in range(steps):
        sl = slice(t * inner, (t + 1) * inner)
        yoff = jnp.concatenate(
            [yoff_ref[:, (g * steps + t) * inner + g * half:(g * steps + t) * inner + (g + 1) * half]
             for g in range(B_GROUPS)], axis=1)
        y = (ypart_ref[:, sl] + yoff * eacum_ref[:, sl]) * _silu(z_ref[:, sl])
        yn = []
        for g in range(B_GROUPS):
            yg = y[:, g * half:(g + 1) * half]
            yn.append(yg * lax.rsqrt(jnp.mean(yg * yg, axis=-1, keepdims=True) + EPS))
        yb = jnp.concatenate(yn, axis=1) * normg_ref[...]
        mixes.append(jnp.concatenate([ya_ref[:, t * d:(t + 1) * d], yb], axis=1))
    out = _dot(jnp.concatenate(mixes, axis=0).astype(BF16), wout_ref[...])
    for t in range(steps):
        o_ref[:, t * d:(t + 1) * d] = hs_ref[:, t * d:(t + 1) * d] + out[t * bt:(t + 1) * bt]


def _row_tiled(width, bt):
    return pl.BlockSpec((bt, width), lambda i: (i, 0))


def _even_sample(hs2, state_conv2, state_ssm3, p, steps, bt, bb):
    nb, _ = hs2.shape
    d = p["wa"].shape[0]
    width_a = p["wa"].shape[1] // 2
    inner = p["wz"].shape[1]
    conv_dim = p["wxbc"].shape[1]
    gn = B_GROUPS * B_STATE
    assert nb % bt == 0 and nb % bb == 0 and steps <= SUBLANES
    params = pltpu.CompilerParams(dimension_semantics=("arbitrary",), vmem_limit_bytes=VMEM_LIMIT)
    small = ["lng", "lnb", "wts", "bts", "convw", "convb", "dtb", "alog", "dskip"]
    widths = dict(v=steps * width_a, ya=steps * width_a, ypart=steps * inner, eacum=steps * inner,
                  z=steps * inner, conv=(B_CONV - 1) * conv_dim, cgt=SUBLANES * B_STATE,
                  xdw=SUBLANES * inner, bs=SUBLANES * gn, dec=LANES)
    front = pl.pallas_call(
        functools.partial(_even_sample_front_kernel, steps=steps, d=d, width_a=width_a, inner=inner,
                          conv_dim=conv_dim),
        out_shape=tuple(jax.ShapeDtypeStruct((nb, w), F32) for w in widths.values()),
        grid=(nb // bt,),
        in_specs=[_row_tiled(steps * d, bt), _resident((1, d))]
        + [_resident(p[k].shape) for k in ("wa", "wz", "wxbc", "wdt")]
        + [_resident(p[k].shape) for k in small]
        + [_row_tiled((B_CONV - 1) * conv_dim, bt)],
        out_specs=tuple(_row_tiled(w, bt) for w in widths.values()),
        compiler_params=params,
        name="even_sample_front",
    )(hs2, p["g"], p["wa"], p["wz"], p["wxbc"], p["wdt"], *[p[k] for k in small], state_conv2)
    v_rows, ya, ypart, eacum, z, new_conv, cgt, xdw, bs, dec = front

    hp = state_ssm3.shape[1]
    tile3 = lambda rows, width: pl.BlockSpec((bb, rows, width), lambda i: (i, 0, 0))
    yoff, new_ssm = pl.pallas_call(
        functools.partial(_even_sample_state_kernel, bb=bb),
        out_shape=(jax.ShapeDtypeStruct((nb, SUBLANES, hp), F32),
                   jax.ShapeDtypeStruct(state_ssm3.shape, F32)),
        grid=(nb // bb,),
        in_specs=[tile3(hp, B_STATE), tile3(SUBLANES, B_STATE), tile3(SUBLANES, inner), tile3(SUBLANES, gn),
                  pl.BlockSpec(memory_space=pltpu.SMEM)],
        out_specs=(tile3(SUBLANES, hp), tile3(hp, B_STATE)),
        compiler_params=params,
        name="even_sample_state",
    )(state_ssm3, cgt.reshape(nb, SUBLANES, B_STATE), xdw.reshape(nb, SUBLANES, inner),
      bs.reshape(nb, SUBLANES, gn), dec[:, :B_HEADS])

    out = pl.pallas_call(
        functools.partial(_even_sample_back_kernel, steps=steps, d=d, inner=inner),
        out_shape=jax.ShapeDtypeStruct(hs2.shape, F32),
        grid=(nb // bt,),
        in_specs=[_row_tiled(steps * d, bt), _row_tiled(steps * width_a, bt), _row_tiled(steps * inner, bt),
                  _row_tiled(steps * inner, bt), _row_tiled(steps * inner, bt), _row_tiled(SUBLANES * hp, bt),
                  _resident((1, inner)), _resident(p["wout"].shape)],
        out_specs=_row_tiled(steps * d, bt),
        compiler_params=params,
        name="even_sample_back",
    )(hs2, ya, ypart, eacum, z, yoff.reshape(nb, SUBLANES * hp), p["normg"], p["wout"])
    return out, v_rows, new_conv, new_ssm


def _odd_sample_front_kernel(hs_ref, g_ref, wc_ref, wq_ref, wk_ref, wv_ref, linw_ref, cscale_ref,
                             qn_ref, kn_ref, onesbd_ref, ps_ref,
                             yc_out, pool_out, q_out, knew_out, vnew_out, *, steps, d, width_c, past_len):
    bt = hs_ref.shape[0]
    xn = _rms(_stack_steps(hs_ref, steps, d), g_ref[...]).astype(BF16)
    blk = lambda a, t: a[t * bt:(t + 1) * bt]
    c_in = _dot(xn, wc_ref[...])
    n_state = max(C_WINDOWS) - 1
    ext = _steps(ps_ref[...], n_state, width_c) + [blk(c_in, t) for t in range(steps)]
    for j in range(n_state):
        pool_out[:, j * width_c:(j + 1) * width_c] = ext[len(ext) - n_state + j]
    gdim = width_c // len(C_WINDOWS)
    yc_cols = []
    for gi, win in enumerate(C_WINDOWS):
        sl = slice(gi * gdim, (gi + 1) * gdim)
        pooled = []
        for t in range(steps):
            hi = n_state + t
            lo = max(hi - win + 1, 0)
            acc = ext[lo][:, sl]
            for j in range(lo + 1, hi + 1):
                acc = acc + ext[j][:, sl]
            count = float(min(past_len + t + 1, win))
            pooled.append(acc / count - ext[hi][:, sl])
        yc_cols.append(_dot(jnp.concatenate(pooled, axis=0).astype(BF16), linw_ref[gi]))
    yc = jnp.concatenate(yc_cols, axis=1) * cscale_ref[...]
    for t in range(steps):
        yc_out[:, t * width_c:(t + 1) * width_c] = blk(yc, t)

    q = _dot(xn, wq_ref[...])
    k = _dot(xn, wk_ref[...])
    v = _dot(xn, wv_ref[...])
    ones_bd = onesbd_ref[...]
    inv_d = 1.0 / D_HEAD_DIM
    qn = q * lax.rsqrt(_head_sumsq(q, ones_bd) * inv_d + EPS) * qn_ref[...] * (D_HEAD_DIM ** -0.5)
    kn = k * lax.rsqrt(_head_sumsq(k, ones_bd) * inv_d + EPS) * kn_ref[...]
    qw = q.shape[1]
    kw = k.shape[1]
    pad = SUBLANES - steps
    q_out[:, steps * qw:] = jnp.zeros((bt, pad * qw), F32)
    knew_out[:, :pad * kw] = jnp.zeros((bt, pad * kw), F32)
    vnew_out[:, :pad * kw] = jnp.zeros((bt, pad * kw), F32)
    for t in range(steps):
        q_out[:, t * qw:(t + 1) * qw] = blk(qn, t)
        knew_out[:, (pad + t) * kw:(pad + t + 1) * kw] = blk(kn, t)
        vnew_out[:, (pad + t) * kw:(pad + t + 1) * kw] = blk(v, t)


SINK_BUCKET = REL_BUCKETS


def _odd_sample_attn_kernel(q_ref, knew_ref, vnew_ref, ck_ref, cv_ref, bucket_ref, sinks_ref, rel_ref,
                            o_ref, kout_ref, vout_ref, bias_ref, *, bb, steps, n_keys):
    win = ck_ref.shape[1]
    kvw = ck_ref.shape[2]
    tile16 = 2 * SUBLANES

    @pl.when(pl.program_id(0) == 0)
    def _():
        bucket = bucket_ref[...]
        for hh in range(D_Q_HEADS):
            acc = jnp.full(bucket.shape, NEG, F32)
            for bkt in range(REL_BUCKETS):
                acc = jnp.where(bucket == bkt, rel_ref[bkt, hh], acc)
            acc = jnp.where(bucket == SINK_BUCKET, sinks_ref[hh], acc)
            bias_ref[hh * SUBLANES:(hh + 1) * SUBLANES, :] = acc

    sub = lax.broadcasted_iota(jnp.int32, (SUBLANES, kvw), 0)
    new_rows = sub >= SUBLANES - steps
    lane_kv = lax.broadcasted_iota(jnp.int32, (SUBLANES, kvw), 1) // D_HEAD_DIM
    gw = D_KV_HEADS * D_HEAD_DIM
    zero_keys = jnp.zeros((n_keys - win - tile16, kvw), BF16)

    def extend(cache, new8):
        new16 = jnp.concatenate([new8, jnp.zeros_like(new8)], axis=0).astype(BF16)
        return jnp.concatenate([cache.astype(BF16), new16, zero_keys], axis=0)

    def shift_in(cache, new8, out_ref, bi):
        rolled = pltpu.roll(cache, win - steps, 0)
        out_ref[bi, 0:win - SUBLANES, :] = rolled[:win - SUBLANES]
        out_ref[bi, win - SUBLANES:, :] = jnp.where(new_rows, new8, rolled[win - SUBLANES:])

    def body(bi, carry):
        ck = ck_ref[bi]
        cv = cv_ref[bi]
        k8 = knew_ref[bi]
        v8 = vnew_ref[bi]
        shift_in(ck, k8, kout_ref, bi)
        shift_in(cv, v8, vout_ref, bi)
        q8 = q_ref[bi]
        pieces = []
        for kv in range(D_KV_HEADS):
            for grp in range(D_GROUP):
                qg = q8[:, grp * gw:(grp + 1) * gw]
                pieces.append(jnp.where(lane_kv == kv, qg, 0.0))
        lhs = jnp.concatenate(pieces, axis=0).astype(BF16)
        sc = _dot_nt(lhs, extend(ck, k8)) + bias_ref[...]
        m = jnp.max(sc, axis=-1, keepdims=True)
        pexp = jnp.exp(sc - m)
        probs = (pexp / jnp.sum(pexp, axis=-1, keepdims=True)).astype(BF16)
        ov = _dot(probs, extend(cv, v8))
        outs = []
        for grp in range(D_GROUP):
            r_last = ((D_KV_HEADS - 1) * D_GROUP + grp) * SUBLANES
            acc = ov[r_last:r_last + SUBLANES]
            for kv in range(D_KV_HEADS - 2, -1, -1):
                r0 = (kv * D_GROUP + grp) * SUBLANES
                acc = jnp.where(lane_kv == kv, ov[r0:r0 + SUBLANES], acc)
            outs.append(acc)
        o_ref[bi] = jnp.concatenate(outs, axis=1)
        return carry

    lax.fori_loop(0, bb, body, 0, unroll=SAMPLE_ATTN_UNROLL)


def _odd_sample_back_kernel(hs_ref, yc_ref, o_ref_in, wout_ref, out_ref, *, steps, d, width_c, qw):
    bt = hs_ref.shape[0]
    mix = jnp.concatenate(
        [jnp.concatenate([yc_ref[:, t * width_c:(t + 1) * width_c], o_ref_in[:, t * qw:(t + 1) * qw]], axis=1)
         for t in range(steps)], axis=0).astype(BF16)
    out = _dot(mix, wout_ref[...])
    for t in range(steps):
        out_ref[:, t * d:(t + 1) * d] = hs_ref[:, t * d:(t + 1) * d] + out[t * bt:(t + 1) * bt]


def _odd_sample(hs2, state_pool2, cache_k3, cache_v3, p, steps, past_len, bt, bb):
    nb, _ = hs2.shape
    d = p["wc"].shape[0]
    width_c = p["wc"].shape[1]
    qw = p["wq"].shape[1]
    kw = p["wk"].shape[1]
    win = cache_k3.shape[1]
    n_state = max(C_WINDOWS) - 1
    assert nb % bt == 0 and nb % bb == 0 and steps <= SUBLANES and win == CHUNK
    params = pltpu.CompilerParams(dimension_semantics=("arbitrary",), vmem_limit_bytes=VMEM_LIMIT)
    vm = ["linw", "cscale", "qn", "kn", "onesbd"]
    widths = dict(yc=steps * width_c, pool=n_state * width_c, q=SUBLANES * qw, knew=SUBLANES * kw,
                  vnew=SUBLANES * kw)
    yc, new_pool, q8, knew8, vnew8 = pl.pallas_call(
        functools.partial(_odd_sample_front_kernel, steps=steps, d=d, width_c=width_c, past_len=past_len),
        out_shape=tuple(jax.ShapeDtypeStruct((nb, w), F32) for w in widths.values()),
        grid=(nb // bt,),
        in_specs=[_row_tiled(steps * d, bt), _resident((1, d))]
        + [_resident(p[k].shape) for k in ("wc", "wq", "wk", "wv")]
        + [_resident(p[k].shape) for k in vm]
        + [_row_tiled(n_state * width_c, bt)],
        out_specs=tuple(_row_tiled(w, bt) for w in widths.values()),
        compiler_params=params,
        name="odd_sample_front",
    )(hs2, p["g"], p["wc"], p["wq"], p["wk"], p["wv"], *[p[k] for k in vm], state_pool2)

    n_keys = 2 * CHUNK
    pad = SUBLANES - steps
    bucket = np.full((SUBLANES, n_keys), -1, np.int32)
    for t in range(steps):
        q_pos = past_len + t
        k_pos = np.full(n_keys, -10 ** 9, np.int64)
        k_pos[:win] = past_len - win + np.arange(win)
        k_pos[win + pad:win + SUBLANES] = past_len + np.arange(steps)
        dist = q_pos - k_pos
        ok = (dist >= 0) & (dist < CHUNK) & (k_pos >= 0)
        bucket[t] = np.where(ok, _t5_bucket(np.where(ok, dist, 0)), -1)
    bucket[:, n_keys - 1] = SINK_BUCKET
    smem = pl.BlockSpec(memory_space=pltpu.SMEM)
    tile3 = lambda rows, width: pl.BlockSpec((bb, rows, width), lambda i: (i, 0, 0))
    o8, new_k, new_v = pl.pallas_call(
        functools.partial(_odd_sample_attn_kernel, bb=bb, steps=steps, n_keys=n_keys),
        out_shape=(jax.ShapeDtypeStruct((nb, SUBLANES, qw), F32),
                   jax.ShapeDtypeStruct(cache_k3.shape, F32),
                   jax.ShapeDtypeStruct(cache_v3.shape, F32)),
        grid=(nb // bb,),
        in_specs=[tile3(SUBLANES, qw), tile3(SUBLANES, kw), tile3(SUBLANES, kw), tile3(win, kw), tile3(win, kw),
                  _resident(bucket.shape), smem, smem],
        out_specs=(tile3(SUBLANES, qw), tile3(win, kw), tile3(win, kw)),
        scratch_shapes=[pltpu.VMEM((D_Q_HEADS * SUBLANES, n_keys), F32)],
        compiler_params=params,
        name="odd_sample_attn",
    )(q8.reshape(nb, SUBLANES, qw), knew8.reshape(nb, SUBLANES, kw), vnew8.reshape(nb, SUBLANES, kw),
      cache_k3, cache_v3, jnp.asarray(bucket), p["sinks"], p["rel"])

    out = pl.pallas_call(
        functools.partial(_odd_sample_back_kernel, steps=steps, d=d, width_c=width_c, qw=qw),
        out_shape=jax.ShapeDtypeStruct(hs2.shape, F32),
        grid=(nb // bt,),
        in_specs=[_row_tiled(steps * d, bt), _row_tiled(steps * width_c, bt), _row_tiled(SUBLANES * qw, bt),
                  _resident(p["wout"].shape)],
        out_specs=_row_tiled(steps * d, bt),
        compiler_params=params,
        name="odd_sample_back",
    )(hs2, yc, o8.reshape(nb, SUBLANES * qw), p["wout"])
    return out, new_pool, new_k, new_v


def _row(v):
    return v.reshape(1, -1).astype(F32)


def _pad_lanes(m, width=LANES):
    return jnp.pad(m, ((0, 0), (0, width - m.shape[1])))


def _head_expand_matrix():
    e = np.zeros((LANES, B_HEADS * B_HEAD_DIM), np.float32)
    for hh in range(B_HEADS):
        e[hh, hh * B_HEAD_DIM:(hh + 1) * B_HEAD_DIM] = 1.0
    return np.concatenate([e, e], axis=0)


def _column_select_matrix():
    e = np.zeros((LANES, B_HEADS * CHUNK), np.float32)
    for hh in range(B_HEADS):
        e[hh, hh * CHUNK:(hh + 1) * CHUNK] = 1.0
    return np.concatenate([e, e], axis=0)


def _prep_even(mix_norm, w_in, w_out, ln_g, ln_b, w_s, b_s, conv_w, conv_b, dt_bias, a_log, d_skip, norm_g):
    width_a = ln_g.shape[0]
    inner = norm_g.shape[0]
    conv_dim = conv_b.shape[0]
    o1 = 2 * width_a
    o2 = o1 + inner
    o3 = o2 + conv_dim
    return dict(
        g=_row(mix_norm),
        wa=w_in[:, :o1].astype(BF16),
        wz=w_in[:, o1:o2].astype(BF16),
        wxbc=w_in[:, o2:o3].astype(BF16),
        wdt=_pad_lanes(w_in[:, o3:]).astype(BF16),
        wout=w_out.astype(BF16),
        lng=_row(ln_g), lnb=_row(ln_b), ws=w_s,
        bsb=jnp.broadcast_to(b_s[:, :, None], b_s.shape + (width_a // A_HEADS,)),
        convw=conv_w, convb=_row(conv_b),
        dtb=_pad_lanes(_row(dt_bias)), alog=_pad_lanes(_row(a_log)),
        dskip=_row(jnp.repeat(d_skip, B_HEAD_DIM)), normg=_row(norm_g),
        expand=jnp.asarray(_head_expand_matrix(), BF16), colsel=jnp.asarray(_column_select_matrix(), BF16),
    )


def _prep_odd(mix_norm, w_in, w_out, lin_w, c_scale, q_norm, k_norm, sinks, rel_table):
    d = w_in.shape[0]
    width_c = c_scale.shape[0]
    qw = D_Q_HEADS * D_HEAD_DIM
    kw = D_KV_HEADS * D_HEAD_DIM
    wq = w_in[:, width_c:width_c + qw].reshape(d, D_KV_HEADS, D_GROUP, D_HEAD_DIM)
    wq = wq.transpose(0, 2, 1, 3).reshape(d, qw)
    wo_d = w_out[width_c:].reshape(D_KV_HEADS, D_GROUP, D_HEAD_DIM, -1).transpose(1, 0, 2, 3).reshape(qw, -1)
    ones_bd = np.kron(np.eye(256 // D_HEAD_DIM), np.ones((D_HEAD_DIM, D_HEAD_DIM))).astype(np.float32)
    return dict(
        g=_row(mix_norm),
        wc=w_in[:, :width_c].astype(BF16),
        wq=wq.astype(BF16),
        wk=w_in[:, width_c + qw:width_c + qw + kw].astype(BF16),
        wv=w_in[:, width_c + qw + kw:].astype(BF16),
        wout=jnp.concatenate([w_out[:width_c], wo_d], axis=0).astype(BF16),
        linw=lin_w.astype(BF16), cscale=_row(c_scale),
        qn=_row(jnp.tile(q_norm, D_Q_HEADS)), kn=_row(jnp.tile(k_norm, D_KV_HEADS)),
        onesbd=jnp.asarray(ones_bd, BF16),
        sinks=sinks.astype(F32), rel=rel_table.astype(F32),
    )


def _prep_even_sample(w_s, b_s, steps):
    head_w = CHUNK
    w = jnp.transpose(w_s[:, :steps, :steps], (1, 2, 0)).reshape(steps * steps, A_HEADS)
    b = b_s[:, :steps].T
    return dict(wts=jnp.repeat(w, head_w, axis=1), bts=jnp.repeat(b, head_w, axis=1))


PAST_LEN = 16384
FFN_TILE = 1024
FFN_STAGE_SLOTS = 2
FFN_STAGE_ROWS_GU = 128
FFN_STAGE_ROWS_D = 352
FFN_ROW_BLOCK = 256
FFN_BLOCK_TILES = 3
MIXER_TILE = 512
SAMPLE_ROW_TILE = 32
SAMPLE_SEQ_TILE = 8
SAMPLE_ATTN_UNROLL = 4


def kernel(x_prompt, x_sample, state_ssm, state_conv, state_pool, cache_k_win, cache_v_win,
           ffn1_norm, ffn1_w_gu, ffn1_w_down, mix_norm, ffn2_norm, ffn2_w_gu, ffn2_w_down,
           ev_w_in, ev_w_out, a_ln_g, a_ln_b, a_w_s, a_b_s, b_conv_w, b_conv_b, b_dt_bias, b_a_log,
           b_d_skip, b_norm_g, od_w_in, od_w_out, c_lin_w, c_scale, d_q_norm, d_k_norm, d_sinks,
           rel_bias_table):
    bp, seq, d = x_prompt.shape
    bs, steps, _ = x_sample.shape
    past_len = PAST_LEN
    hp = x_prompt
    hs = x_sample
    depth = ffn1_norm.shape[0]
    names = ("a_v_s", "ssm_p", "ssm_s", "conv_p", "conv_s", "pool_p", "pool_s", "k_p", "k_s", "v_p", "v_s")
    outs = {k: [] for k in names}

    def macaron(h_p, h_s, norm, w_gu_all, w_down_all, layer):
        o_p, o_s = _ffn(h_p.reshape(bp * seq, d), h_s.reshape(bs * steps, d), _row(norm), w_gu_all, w_down_all,
                        layer, FFN_TILE)
        return o_p.reshape(bp, seq, d), o_s.reshape(bs, steps, d)

    for layer in range(depth):
        i = layer // 2
        hp, hs = macaron(hp, hs, ffn1_norm[layer], ffn1_w_gu, ffn1_w_down, layer)
        hs2 = hs.reshape(bs, steps * d)
        if layer % 2 == 0:
            p = _prep_even(mix_norm[layer], ev_w_in[i], ev_w_out[i], a_ln_g[i], a_ln_b[i], a_w_s[i], a_b_s[i],
                           b_conv_w[i], b_conv_b[i], b_dt_bias[i], b_a_log[i], b_d_skip[i], b_norm_g[i])
            p.update(_prep_even_sample(a_w_s[i], a_b_s[i], steps))
            hp, conv_p, ssm_p = _even_prompt(hp, p, MIXER_TILE)
            hs2, v_rows, conv_s, ssm_s = _even_sample(
                hs2, state_conv[i].reshape(bs, -1), state_ssm[i].reshape(bs, B_HEADS * B_HEAD_DIM, B_STATE),
                p, steps, SAMPLE_ROW_TILE, SAMPLE_SEQ_TILE)
            outs["a_v_s"].append(v_rows.reshape(bs, steps, -1))
            outs["conv_p"].append(conv_p)
            outs["conv_s"].append(conv_s.reshape(state_conv[i].shape))
            outs["ssm_p"].append(ssm_p.reshape(bp, B_HEADS, B_HEAD_DIM, B_STATE))
            outs["ssm_s"].append(ssm_s.reshape(state_ssm[i].shape))
        else:
            p = _prep_odd(mix_norm[layer], od_w_in[i], od_w_out[i], c_lin_w[i], c_scale[i], d_q_norm[i],
                          d_k_norm[i], d_sinks[i], rel_bias_table)
            hp, pool_p, k_p, v_p = _odd_prompt(hp, p, MIXER_TILE)
            kv_shape = cache_k_win[i].shape
            hs2, pool_s, k_s, v_s = _odd_sample(
                hs2, state_pool[i].reshape(bs, -1), cache_k_win[i].reshape(bs, kv_shape[1], -1),
                cache_v_win[i].reshape(bs, kv_shape[1], -1), p, steps, past_len, SAMPLE_ROW_TILE, SAMPLE_SEQ_TILE)
            outs["pool_p"].append(pool_p)
            outs["pool_s"].append(pool_s.reshape(state_pool[i].shape))
            outs["k_p"].append(k_p.reshape(bp, CHUNK, D_KV_HEADS, D_HEAD_DIM))
            outs["v_p"].append(v_p.reshape(bp, CHUNK, D_KV_HEADS, D_HEAD_DIM))
            outs["k_s"].append(k_s.reshape(kv_shape))
            outs["v_s"].append(v_s.reshape(kv_shape))
        hs = hs2.reshape(bs, steps, d)
        hp, hs = macaron(hp, hs, ffn2_norm[layer], ffn2_w_gu, ffn2_w_down, layer)
    return (hp, hs) + tuple(jnp.stack(outs[k]) for k in names)
```

```python
import functools
import math

import numpy as np
import jax
import jax.numpy as jnp
from jax import lax
from jax.experimental import pallas as pl
from jax.experimental.pallas import tpu as pltpu

F32 = jnp.float32
BF16 = jnp.bfloat16

EPS = 1e-6
NEG = -1e30

LANES = 128
SUBLANES = 8
MXU_DIM = 256
VMEM_BYTES_V7X = 64 * 1024 * 1024
VMEM_LIMIT = VMEM_BYTES_V7X - 8 * 1024 * 1024

A_HEADS = 8
B_HEADS = 16
B_HEAD_DIM = 64
B_GROUPS = 2
B_STATE = 128
B_CONV = 4
CHUNK = 128
C_WINDOWS = (2, 4, 8, 16)
C_HALO = 16
D_Q_HEADS = 16
D_KV_HEADS = 4
D_HEAD_DIM = 64
D_GROUP = D_Q_HEADS // D_KV_HEADS
REL_BUCKETS = 32
REL_MAX_DIST = 128


def _rms(x, g):
    ms = jnp.mean(x * x, axis=-1, keepdims=True)
    return x * lax.rsqrt(ms + EPS) * g


def _sigmoid(x):
    return 1.0 / (1.0 + jnp.exp(-x))


def _silu(x):
    return x * _sigmoid(x)


def _gelu_tanh(x):
    c = math.sqrt(2.0 / math.pi)
    return x * (0.5 * (1.0 + jnp.tanh(c * (x + 0.044715 * (x * x * x)))))


def _softplus(x):
    return jnp.maximum(x, 0.0) + jnp.log1p(jnp.exp(-jnp.abs(x)))


def _split3(x):
    hi = x.astype(BF16)
    r1 = x - hi.astype(F32)
    mid = r1.astype(BF16)
    lo = (r1 - mid.astype(F32)).astype(BF16)
    return hi, mid, lo


def _split2_lanes(x):
    hi = x.astype(BF16)
    lo = (x - hi.astype(F32)).astype(BF16)
    return jnp.concatenate([hi, lo], axis=1)


def _dot(a, b):
    return jnp.dot(a, b, preferred_element_type=F32)


def _dot_nt(a, b):
    return lax.dot_general(a, b, (((1,), (1,)), ((), ())), preferred_element_type=F32)


def _dot_tn(a, b):
    return lax.dot_general(a, b, (((0,), (0,)), ((), ())), preferred_element_type=F32)


def _expand_heads(m, n_pairs):
    rows = m.shape[0]
    lane = lax.broadcasted_iota(jnp.int32, (rows, LANES), 1)
    first = lane < B_HEAD_DIM
    parts = []
    for p in range(n_pairs):
        a = jnp.broadcast_to(m[:, 2 * p:2 * p + 1], (rows, LANES))
        b = jnp.broadcast_to(m[:, 2 * p + 1:2 * p + 2], (rows, LANES))
        parts.append(jnp.where(first, a, b))
    return jnp.concatenate(parts, axis=1)


def _head_sumsq(x, ones_bd):
    xx = (x * x).astype(BF16)
    outs = [_dot(xx[:, c * 256:(c + 1) * 256], ones_bd) for c in range(x.shape[1] // 256)]
    return jnp.concatenate(outs, axis=1) if len(outs) > 1 else outs[0]


def _ff_blocks(d_ff):
    step = FFN_BLOCK_TILES * MXU_DIM
    return [(c0, min(c0 + step, d_ff)) for c0 in range(0, d_ff, step)]


def _load_cast_rows(src_hbm, dst_ref, stage_ref, sem_ref):
    slots, rows, _ = stage_ref.shape
    n = src_hbm.shape[0] // rows

    def copy(c):
        slot = c % slots
        return pltpu.make_async_copy(src_hbm.at[pl.ds(c * rows, rows), :], stage_ref.at[slot], sem_ref.at[slot])

    for c in range(min(slots, n)):
        copy(c).start()
    for c in range(n):
        copy(c).wait()
        dst_ref[c * rows:(c + 1) * rows, :] = stage_ref[c % slots].astype(BF16)
        if c + slots < n:
            copy(c + slots).start()


def _ffn_rows(x_ref, g_ref, wgu_ref, wd_ref, o_ref, d_ff):
    tm = x_ref.shape[0]
    rows = [slice(r0, min(r0 + FFN_ROW_BLOCK, tm)) for r0 in range(0, tm, FFN_ROW_BLOCK)]
    xns = [_rms(x_ref[rs, :], g_ref[...]).astype(BF16) for rs in rows]
    for rs, xn in zip(rows, xns):
        y = None
        for c0, c1 in _ff_blocks(d_ff):
            gate = _dot(xn, wgu_ref[:, c0:c1])
            up = _dot(xn, wgu_ref[:, d_ff + c0:d_ff + c1])
            act = (_silu(gate) * up).astype(BF16)
            part = _dot(act, wd_ref[c0:c1, :])
            y = part if y is None else y + part
        o_ref[rs, :] = x_ref[rs, :] + 0.5 * y


def _ffn_kernel(xp_ref, xs_ref, g_ref, wgu_hbm, wd_hbm, op_ref, os_ref,
                wgu_ref, wd_ref, stage_gu_ref, stage_d_ref, sem_ref, *, d_ff, prompt_steps, layer):
    i = pl.program_id(0)

    @pl.when(i == 0)
    def _():
        _load_cast_rows(wgu_hbm.at[layer], wgu_ref, stage_gu_ref, sem_ref)
        _load_cast_rows(wd_hbm.at[layer], wd_ref, stage_d_ref, sem_ref)

    @pl.when(i < prompt_steps)
    def _():
        _ffn_rows(xp_ref, g_ref, wgu_ref, wd_ref, op_ref, d_ff)

    @pl.when(i == prompt_steps)
    def _():
        _ffn_rows(xs_ref, g_ref, wgu_ref, wd_ref, os_ref, d_ff)


def _resident(shape):
    nd = len(shape)
    return pl.BlockSpec(shape, lambda *_: (0,) * nd, pipeline_mode=pl.Buffered(1))


def _ffn(xp2d, xs2d, g, wgu_all, wd_all, layer, tm):
    m, d = xp2d.shape
    ms = xs2d.shape[0]
    wgu_shape, wd_shape = wgu_all.shape[1:], wd_all.shape[1:]
    d_ff = wd_shape[0]
    assert m % tm == 0 and d % FFN_STAGE_ROWS_GU == 0 and d_ff % FFN_STAGE_ROWS_D == 0
    steps = m // tm
    last = steps - 1
    whole = lambda shape: pl.BlockSpec(shape, lambda i: (0, 0))
    prompt_tile = pl.BlockSpec((tm, d), lambda i: (jnp.minimum(i, last), 0))
    hbm = pl.BlockSpec(memory_space=pl.ANY)
    return pl.pallas_call(
        functools.partial(_ffn_kernel, d_ff=d_ff, prompt_steps=steps, layer=layer),
        out_shape=(jax.ShapeDtypeStruct((m, d), F32), jax.ShapeDtypeStruct((ms, d), F32)),
        grid=(steps + 1,),
        in_specs=[prompt_tile, whole((ms, d)), _resident((1, d)), hbm, hbm],
        out_specs=(prompt_tile, whole((ms, d))),
        scratch_shapes=[pltpu.VMEM(wgu_shape, BF16), pltpu.VMEM(wd_shape, BF16),
                        pltpu.VMEM((FFN_STAGE_SLOTS, FFN_STAGE_ROWS_GU, wgu_shape[1]), F32),
                        pltpu.VMEM((FFN_STAGE_SLOTS, FFN_STAGE_ROWS_D, wd_shape[1]), F32),
                        pltpu.SemaphoreType.DMA((FFN_STAGE_SLOTS,))],
        compiler_params=pltpu.CompilerParams(
            dimension_semantics=("arbitrary",), vmem_limit_bytes=VMEM_LIMIT),
        name="ffn",
    )(xp2d, xs2d, g, wgu_all, wd_all)


def _even_prompt_kernel(h_ref, g_ref, wa_ref, wz_ref, wxbc_ref, wdt_ref, wout_ref,
                        lng_ref, lnb_ref, ws_ref, bsb_ref, convw_ref, convb_ref,
                        dtb_ref, alog_ref, dskip_ref, normg_ref, expand_ref, colsel_ref,
                        o_ref, conv_out_ref, ssm_out_ref,
                        ext_ref, st_ref, *, tile, width_a, inner):
    s = pl.program_id(1)
    n_chunks = tile // CHUNK
    halo = SUBLANES

    @pl.when(s == 0)
    def _():
        ext_ref[0:halo, :] = jnp.zeros((halo, ext_ref.shape[1]), F32)
        st_ref[...] = jnp.zeros(st_ref.shape, F32)

    x = h_ref[...]
    xn = _rms(x, g_ref[...]).astype(BF16)

    row = lax.broadcasted_iota(jnp.int32, (CHUNK, CHUNK), 0)
    col = lax.broadcasted_iota(jnp.int32, (CHUNK, CHUNK), 1)
    causal = row >= col
    lane = lax.broadcasted_iota(jnp.int32, (CHUNK, LANES), 1)
    first_half = lane < B_HEAD_DIM

    pa = _dot(xn, wa_ref[...])
    xbc_raw = _dot(xn, wxbc_ref[...])

    ga = _gelu_tanh(pa)
    u = ga[:, :width_a]
    v = ga[:, width_a:]
    mu = jnp.mean(v, axis=-1, keepdims=True)
    vc = v - mu
    var = jnp.mean(vc * vc, axis=-1, keepdims=True)
    v = vc * lax.rsqrt(var + EPS) * lng_ref[...] + lnb_ref[...]
    vb = v.astype(BF16)

    z = _dot(xn, wz_ref[...])
    dt_raw = _dot(xn, wdt_ref[...])

    ext_ref[halo:halo + tile, :] = xbc_raw
    ext = ext_ref[...]
    ext1 = pltpu.roll(ext, 1, 0)
    pair = ext * convw_ref[1:2, :] + ext1 * convw_ref[0:1, :]
    conv = (convb_ref[...] + ext * convw_ref[3:4, :] + ext1 * convw_ref[2:3, :] + pltpu.roll(pair, 2, 0))[halo:]
    tail = ext_ref[tile:tile + halo, :]
    ext_ref[0:halo, :] = tail
    conv_out_ref[...] = tail
    xbc = _silu(conv)
    gn = B_GROUPS * B_STATE
    xs = xbc[:, :inner]
    bm = xbc[:, inner:inner + gn]
    cm = xbc[:, inner + gn:]

    head_w = width_a // A_HEADS
    gate_cols = []
    for hh in range(A_HEADS):
        w = jnp.where(causal, ws_ref[hh], 0.0).astype(BF16)
        rhs = jnp.concatenate(
            [vb[c * CHUNK:(c + 1) * CHUNK, hh * head_w:(hh + 1) * head_w] for c in range(n_chunks)], axis=1)
        out = _dot(w, rhs)
        bias = bsb_ref[hh]
        gate_cols.append(jnp.concatenate(
            [out[:, c * head_w:(c + 1) * head_w] + bias for c in range(n_chunks)], axis=0))
    ya = u * jnp.concatenate(gate_cols, axis=1)
    out_a = _dot(ya.astype(BF16), wout_ref[0:width_a, :])

    dt = _softplus(dt_raw + dtb_ref[...])
    a_neg = -jnp.exp(alog_ref[...])
    da = dt * a_neg

    n_pairs = B_HEADS // 2
    heads_per_group = B_HEADS // B_GROUPS
    gw = heads_per_group * B_HEAD_DIM
    tril_ones = jnp.where(causal, 1.0, 0.0).astype(BF16)
    chunks = [slice(c * CHUNK, (c + 1) * CHUNK) for c in range(n_chunks)]
    acums = []
    for rs in chunks:
        d_hi, d_mid, d_lo = _split3(da[rs])
        acums.append(_dot(tril_ones, d_hi) + _dot(tril_ones, d_mid) + _dot(tril_ones, d_lo))
    acum = jnp.concatenate(acums, axis=0)
    decay = jnp.concatenate([jnp.exp(a[CHUNK - 1:CHUNK, :] - a) for a in acums], axis=0)
    expand = expand_ref[...]
    xd = xs * _dot(_split2_lanes(dt), expand)
    xdwb = (xs * _dot(_split2_lanes(dt * decay), expand)).astype(BF16)
    e_acum = _dot(_split2_lanes(jnp.exp(acum)), expand)
    bmb = bm.astype(BF16)
    cmb = cm.astype(BF16)
    y_rows = []
    for c, rs in enumerate(chunks):
        a_c = acums[c]
        acum_t = a_c.T
        a_cols = jnp.concatenate([jnp.broadcast_to(a_c[:, hh:hh + 1], (CHUNK, CHUNK)) for hh in range(B_HEADS)],
                                 axis=1)
        cb = [_dot_nt(cmb[rs, g * B_STATE:(g + 1) * B_STATE], bmb[rs, g * B_STATE:(g + 1) * B_STATE])
              for g in range(B_GROUPS)]
        y_parts = []
        for p in range(n_pairs):
            g = (2 * p) // heads_per_group
            ms = []
            for hh in (2 * p, 2 * p + 1):
                seg = a_cols[:, hh * CHUNK:(hh + 1) * CHUNK] - jnp.broadcast_to(acum_t[hh:hh + 1, :], (CHUNK, CHUNK))
                lmat = jnp.where(causal, jnp.exp(seg), 0.0)
                ms.append((cb[g] * lmat).astype(BF16))
            lhs = jnp.concatenate(ms, axis=1)
            xd_p = xd[rs, p * LANES:(p + 1) * LANES]
            rhs = jnp.concatenate([jnp.where(first_half, xd_p, 0.0),
                                   jnp.where(first_half, 0.0, xd_p)], axis=0).astype(BF16)
            y_parts.append(_dot(lhs, rhs))
        y_rows.append(jnp.concatenate(y_parts, axis=1))
    for c, rs in enumerate(chunks):
        st_prev = st_ref[...]
        stb = st_prev.astype(BF16)
        y_off = jnp.concatenate(
            [_dot(cmb[rs, g * B_STATE:(g + 1) * B_STATE], stb[:, g * gw:(g + 1) * gw]) for g in range(B_GROUPS)],
            axis=1)
        st_add = jnp.concatenate(
            [_dot_tn(bmb[rs, g * B_STATE:(g + 1) * B_STATE], xdwb[rs, g * gw:(g + 1) * gw])
             for g in range(B_GROUPS)], axis=1)
        chunk_decay = e_acum[(c + 1) * CHUNK - 1:(c + 1) * CHUNK, :]
        st_ref[...] = st_prev * chunk_decay + st_add
        y_rows[c] = y_rows[c] + y_off * e_acum[rs]
    y = (jnp.concatenate(y_rows, axis=0) if n_chunks > 1 else y_rows[0]) + xs * dskip_ref[...]
    y = y * _silu(z)
    half = inner // B_GROUPS
    yn = []
    for g in range(B_GROUPS):
        yg = y[:, g * half:(g + 1) * half]
        yn.append(yg * lax.rsqrt(jnp.mean(yg * yg, axis=-1, keepdims=True) + EPS))
    yb = jnp.concatenate(yn, axis=1) * normg_ref[...]

    o_ref[...] = x + out_a + _dot(yb.astype(BF16), wout_ref[width_a:, :])

    @pl.when(s == pl.num_programs(1) - 1)
    def _():
        ssm_out_ref[...] = st_ref[...].T


def _even_prompt(h, p, tile):
    b, seq, d = h.shape
    width_a = p["wa"].shape[1] // 2
    inner = p["wz"].shape[1]
    conv_dim = p["wxbc"].shape[1]
    assert seq % tile == 0 and tile % CHUNK == 0
    small = ["lng", "lnb", "ws", "bsb", "convw", "convb", "dtb", "alog", "dskip", "normg", "expand", "colsel"]
    out, conv_tail, ssm = pl.pallas_call(
        functools.partial(_even_prompt_kernel, tile=tile, width_a=width_a, inner=inner),
        out_shape=(jax.ShapeDtypeStruct((b, seq, d), F32),
                   jax.ShapeDtypeStruct((b, SUBLANES, conv_dim), F32),
                   jax.ShapeDtypeStruct((b, inner, B_STATE), F32)),
        grid=(b, seq // tile),
        in_specs=[pl.BlockSpec((None, tile, d), lambda i, j: (i, j, 0)),
                  _resident((1, d))]
        + [_resident(p[k].shape) for k in ("wa", "wz", "wxbc", "wdt", "wout")]
        + [_resident(p[k].shape) for k in small],
        out_specs=(pl.BlockSpec((None, tile, d), lambda i, j: (i, j, 0)),
                   pl.BlockSpec((None, SUBLANES, conv_dim), lambda i, j: (i, 0, 0)),
                   pl.BlockSpec((None, inner, B_STATE), lambda i, j: (i, 0, 0))),
        scratch_shapes=[pltpu.VMEM((tile + SUBLANES, conv_dim), F32),
                        pltpu.VMEM((B_STATE, inner), F32)],
        compiler_params=pltpu.CompilerParams(
            dimension_semantics=("arbitrary", "arbitrary"), vmem_limit_bytes=VMEM_LIMIT),
        name="even_prompt",
    )(h, p["g"], p["wa"], p["wz"], p["wxbc"], p["wdt"], p["wout"], *[p[k] for k in small])
    return out, conv_tail[:, SUBLANES - (B_CONV - 1):, :], ssm


def _t5_bucket(dist):
    n = np.maximum(dist, 0)
    max_exact = REL_BUCKETS // 2
    n_safe = np.maximum(n, 1).astype(np.float32)
    scale = np.float32((REL_BUCKETS - max_exact) / math.log(REL_MAX_DIST / max_exact))
    large = max_exact + (np.log(n_safe / max_exact) * scale).astype(np.int32)
    large = np.minimum(large, REL_BUCKETS - 1)
    return np.where(n < max_exact, n, large).astype(np.int32)


def _fill_bias(bias_ref, bucket_ref, rel_ref):
    bucket = bucket_ref[...]
    lq = bucket.shape[0]
    has_prev = lax.broadcasted_iota(jnp.int32, bucket.shape, 1) >= CHUNK
    for hh in range(D_Q_HEADS):
        kv, grp = divmod(hh, D_GROUP)
        acc = jnp.full(bucket.shape, NEG, F32)
        for bkt in range(REL_BUCKETS):
            acc = jnp.where(bucket == bkt, rel_ref[bkt, hh], acc)
        bias_ref[0, grp, kv * lq:(kv + 1) * lq, :] = acc
        bias_ref[1, grp, kv * lq:(kv + 1) * lq, :] = jnp.where(has_prev, acc, NEG)


def _group_attention(qg, kk, vv, bias_ref, sinks_ref, grp, table):
    lq = qg.shape[0]
    lane_kv = lax.broadcasted_iota(jnp.int32, (lq, D_KV_HEADS * D_HEAD_DIM), 1) // D_HEAD_DIM
    zero = jnp.zeros_like(qg)
    lhs = jnp.concatenate([jnp.where(lane_kv == kv, qg, zero) for kv in range(D_KV_HEADS)], axis=0)
    sc = _dot_nt(lhs, kk)
    probs = []
    for kv in range(D_KV_HEADS):
        rs = slice(kv * lq, (kv + 1) * lq)
        s_h = sc[rs] + bias_ref[table, grp, rs, :]
        sink = sinks_ref[kv * D_GROUP + grp]
        m = jnp.maximum(jnp.max(s_h, axis=-1, keepdims=True), sink)
        pexp = jnp.exp(s_h - m)
        denom = jnp.sum(pexp, axis=-1, keepdims=True) + jnp.exp(sink - m)
        probs.append((pexp / denom).astype(BF16))
    ov = _dot(jnp.concatenate(probs, axis=0), vv)
    out = ov[(D_KV_HEADS - 1) * lq:]
    for kv in range(D_KV_HEADS - 2, -1, -1):
        out = jnp.where(lane_kv == kv, ov[kv * lq:(kv + 1) * lq], out)
    return out


def _odd_prompt_kernel(h_ref, g_ref, wc_ref, wq_ref, wk_ref, wv_ref, wout_ref,
                       linw_ref, cscale_ref, qn_ref, kn_ref, onesbd_ref, bucket_ref,
                       sinks_ref, rel_ref,
                       o_ref, pool_out_ref, k_out_ref, v_out_ref,
                       extc_ref, kprev_ref, vprev_ref, bias_ref, *, tile, width_c):
    b = pl.program_id(0)
    s = pl.program_id(1)
    n_blocks = tile // CHUNK

    @pl.when((b == 0) & (s == 0))
    def _():
        _fill_bias(bias_ref, bucket_ref, rel_ref)

    @pl.when(s == 0)
    def _():
        extc_ref[0:C_HALO, :] = jnp.zeros((C_HALO, width_c), F32)
        kprev_ref[...] = jnp.zeros(kprev_ref.shape, F32)
        vprev_ref[...] = jnp.zeros(vprev_ref.shape, F32)

    x = h_ref[...]
    xn = _rms(x, g_ref[...]).astype(BF16)

    c_in = _dot(xn, wc_ref[...])
    extc_ref[C_HALO:C_HALO + tile, :] = c_in
    e = extc_ref[...]
    tail = extc_ref[tile:tile + C_HALO, :]
    extc_ref[0:C_HALO, :] = tail
    pool_out_ref[...] = tail
    pos = (s * tile + lax.broadcasted_iota(jnp.int32, (tile, 1), 0) + 1).astype(F32)
    gdim = width_c // len(C_WINDOWS)
    run = e
    shift = 1
    yc = []
    for gi, win in enumerate(C_WINDOWS):
        while shift < win:
            run = run + pltpu.roll(run, shift, 0)
            shift *= 2
        cnt = jnp.minimum(pos, float(win))
        pooled = run[C_HALO:, :gdim] / cnt - c_in[:, gi * gdim:(gi + 1) * gdim]
        yc.append(_dot(pooled.astype(BF16), linw_ref[gi]))
        if gi + 1 < len(C_WINDOWS):
            run = run[:, gdim:]
    yc = jnp.concatenate(yc, axis=1) * cscale_ref[...]

    q = _dot(xn, wq_ref[...])
    k = _dot(xn, wk_ref[...])
    v = _dot(xn, wv_ref[...])
    ones_bd = onesbd_ref[...]
    inv_d = 1.0 / D_HEAD_DIM
    qn = q * lax.rsqrt(_head_sumsq(q, ones_bd) * inv_d + EPS) * qn_ref[...]
    kn = k * lax.rsqrt(_head_sumsq(k, ones_bd) * inv_d + EPS) * kn_ref[...]
    qs = (qn * (D_HEAD_DIM ** -0.5)).astype(BF16)
    kb = kn.astype(BF16)
    vb = v.astype(BF16)
    first_table = jnp.where(s == 0, 1, 0)
    gw = D_KV_HEADS * D_HEAD_DIM
    o_rows = []
    for blk in range(n_blocks):
        rs = slice(blk * CHUNK, (blk + 1) * CHUNK)
        if blk == 0:
            k_prev, v_prev = kprev_ref[...].astype(BF16), vprev_ref[...].astype(BF16)
        else:
            k_prev, v_prev = kb[(blk - 1) * CHUNK:blk * CHUNK], vb[(blk - 1) * CHUNK:blk * CHUNK]
        kk = jnp.concatenate([k_prev, kb[rs]], axis=0)
        vv = jnp.concatenate([v_prev, vb[rs]], axis=0)
        o_rows.append(jnp.concatenate(
            [_group_attention(qs[rs, grp * gw:(grp + 1) * gw], kk, vv, bias_ref, sinks_ref, grp,
                              first_table if blk == 0 else 0)
             for grp in range(D_GROUP)], axis=1))
    kprev_ref[...] = kn[tile - CHUNK:]
    vprev_ref[...] = v[tile - CHUNK:]
    yd = jnp.concatenate(o_rows, axis=0) if n_blocks > 1 else o_rows[0]

    mix = jnp.concatenate([yc, yd], axis=1).astype(BF16)
    o_ref[...] = x + _dot(mix, wout_ref[...])

    @pl.when(s == pl.num_programs(1) - 1)
    def _():
        k_out_ref[...] = kn[tile - CHUNK:]
        v_out_ref[...] = v[tile - CHUNK:]


def _odd_prompt(h, p, tile):
    b, seq, d = h.shape
    width_c = p["wc"].shape[1]
    kvw = p["wk"].shape[1]
    assert seq % tile == 0 and tile % CHUNK == 0
    r = np.arange(CHUNK) + CHUNK
    c = np.arange(2 * CHUNK)
    dist = r[:, None] - c[None, :]
    bucket = np.where((dist >= 0) & (dist < CHUNK), _t5_bucket(dist), -1).astype(np.int32)
    vm = ["linw", "cscale", "qn", "kn", "onesbd"]
    smem = pl.BlockSpec(memory_space=pltpu.SMEM)
    out, pool_tail, k_win, v_win = pl.pallas_call(
        functools.partial(_odd_prompt_kernel, tile=tile, width_c=width_c),
        out_shape=(jax.ShapeDtypeStruct((b, seq, d), F32),
                   jax.ShapeDtypeStruct((b, C_HALO, width_c), F32),
                   jax.ShapeDtypeStruct((b, CHUNK, kvw), F32),
                   jax.ShapeDtypeStruct((b, CHUNK, kvw), F32)),
        grid=(b, seq // tile),
        in_specs=[pl.BlockSpec((None, tile, d), lambda i, j: (i, j, 0)),
                  _resident((1, d))]
        + [_resident(p[k].shape) for k in ("wc", "wq", "wk", "wv", "wout")]
        + [_resident(p[k].shape) for k in vm]
        + [_resident(bucket.shape), smem, smem],
        out_specs=(pl.BlockSpec((None, tile, d), lambda i, j: (i, j, 0)),
                   pl.BlockSpec((None, C_HALO, width_c), lambda i, j: (i, 0, 0)),
                   pl.BlockSpec((None, CHUNK, kvw), lambda i, j: (i, 0, 0)),
                   pl.BlockSpec((None, CHUNK, kvw), lambda i, j: (i, 0, 0))),
        scratch_shapes=[pltpu.VMEM((tile + C_HALO, width_c), F32),
                        pltpu.VMEM((CHUNK, kvw), F32),
                        pltpu.VMEM((CHUNK, kvw), F32),
                        pltpu.VMEM((2, D_GROUP, D_KV_HEADS * CHUNK, 2 * CHUNK), F32)],
        compiler_params=pltpu.CompilerParams(
            dimension_semantics=("arbitrary", "arbitrary"), vmem_limit_bytes=VMEM_LIMIT),
        name="odd_prompt",
    )(h, p["g"], p["wc"], p["wq"], p["wk"], p["wv"], p["wout"], *[p[k] for k in vm],
      jnp.asarray(bucket), p["sinks"], p["rel"])
    return out, pool_tail[:, C_HALO - (max(C_WINDOWS) - 1):, :], k_win, v_win


def _steps(x, n, width):
    return [x[:, t * width:(t + 1) * width] for t in range(n)]


def _stack_steps(ref, n, width):
    x = ref[...]
    return jnp.concatenate(_steps(x, n, width), axis=0)


def _even_sample_front_kernel(hs_ref, g_ref, wa_ref, wz_ref, wxbc_ref, wdt_ref,
                              lng_ref, lnb_ref, wts_ref, bts_ref, convw_ref, convb_ref,
                              dtb_ref, alog_ref, dskip_ref, cs_ref,
                              v_out, ya_out, ypart_out, eacum_out, z_out, conv_out,
                              cgt_out, xdw_out, bs_out, dec_out, *, steps, d, width_a, inner, conv_dim):
    bt = hs_ref.shape[0]
    xn = _rms(_stack_steps(hs_ref, steps, d), g_ref[...]).astype(BF16)
    blk = lambda a, t: a[t * bt:(t + 1) * bt]

    ga = _gelu_tanh(_dot(xn, wa_ref[...]))
    u = ga[:, :width_a]
    v = ga[:, width_a:]
    mu = jnp.mean(v, axis=-1, keepdims=True)
    vc = v - mu
    var = jnp.mean(vc * vc, axis=-1, keepdims=True)
    v = vc * lax.rsqrt(var + EPS) * lng_ref[...] + lnb_ref[...]
    for t in range(steps):
        v_out[:, t * width_a:(t + 1) * width_a] = blk(v, t)
        gate = bts_ref[t:t + 1, :]
        for s in range(t + 1):
            gate = gate + wts_ref[t * steps + s:t * steps + s + 1, :] * blk(v, s)
        ya_out[:, t * width_a:(t + 1) * width_a] = blk(u, t) * gate

    z = _dot(xn, wz_ref[...])
    for t in range(steps):
        z_out[:, t * inner:(t + 1) * inner] = blk(z, t)
    raw = _dot(xn, wxbc_ref[...])
    dt = _softplus(_dot(xn, wdt_ref[...]) + dtb_ref[...])
    ext = _steps(cs_ref[...], B_CONV - 1, conv_dim) + [blk(raw, t) for t in range(steps)]
    for k in range(B_CONV - 1):
        conv_out[:, k * conv_dim:(k + 1) * conv_dim] = ext[len(ext) - (B_CONV - 1) + k]
    gn = B_GROUPS * B_STATE
    n_pairs = B_HEADS // 2
    a_neg = -jnp.exp(alog_ref[...])
    xs, bm, cm, dts, acum = [], [], [], [], []
    for t in range(steps):
        conv = convb_ref[...]
        for tap in range(B_CONV):
            conv = conv + ext[t + tap] * convw_ref[tap:tap + 1, :]
        xbc = _silu(conv)
        xs.append(xbc[:, :inner])
        bm.append(xbc[:, inner:inner + gn])
        cm.append(xbc[:, inner + gn:])
        dts.append(blk(dt, t))
        da = dts[t] * a_neg
        acum.append(da if t == 0 else acum[t - 1] + da)
    lane = lax.broadcasted_iota(jnp.int32, (bt, LANES), 1)
    group0 = lane < (B_HEADS // B_GROUPS)
    dec_out[...] = jnp.exp(acum[steps - 1])
    pad_rows = SUBLANES - steps
    xdw_out[:, steps * inner:] = jnp.zeros((bt, pad_rows * inner), F32)
    bs_out[:, steps * gn:] = jnp.zeros((bt, pad_rows * gn), F32)
    xd = []
    for t in range(steps):
        xd.append(xs[t] * _expand_heads(dts[t], n_pairs))
        eacum_out[:, t * inner:(t + 1) * inner] = _expand_heads(jnp.exp(acum[t]), n_pairs)
        xdw_out[:, t * inner:(t + 1) * inner] = xs[t] * _expand_heads(
            dts[t] * jnp.exp(acum[steps - 1] - acum[t]), n_pairs)
        bs_out[:, t * gn:(t + 1) * gn] = bm[t]
        for g in range(B_GROUPS):
            r = g * steps + t
            cgt_out[:, r * B_STATE:(r + 1) * B_STATE] = cm[t][:, g * B_STATE:(g + 1) * B_STATE]
    for t in range(steps):
        y = xs[t] * dskip_ref[...]
        for s in range(t + 1):
            cb = [jnp.sum(cm[t][:, g * B_STATE:(g + 1) * B_STATE] * bm[s][:, g * B_STATE:(g + 1) * B_STATE],
                          axis=-1, keepdims=True) for g in range(B_GROUPS)]
            coef = jnp.where(group0, cb[0], cb[1]) * jnp.exp(acum[t] - acum[s])
            y = y + _expand_heads(coef, n_pairs) * xd[s]
        ypart_out[:, t * inner:(t + 1) * inner] = y


def _even_sample_state_kernel(s0_ref, cgt_ref, xdw_ref, bs_ref, dec_ref, yoff_ref, snew_ref, *, bb):
    step = pl.program_id(0)
    gw = (B_HEADS // B_GROUPS) * B_HEAD_DIM

    def body(bi, carry):
        s0 = s0_ref[bi]
        c8 = cgt_ref[bi]
        c16 = jnp.concatenate([c8, jnp.zeros_like(c8)], axis=0).astype(BF16)
        yoff_ref[bi] = _dot_nt(c16, s0.astype(BF16))[:SUBLANES]
        x8 = xdw_ref[bi]
        b8 = bs_ref[bi]
        x16 = jnp.concatenate([x8, jnp.zeros_like(x8)], axis=0).astype(BF16)
        b16 = jnp.concatenate([b8, jnp.zeros_like(b8)], axis=0).astype(BF16)
        for g in range(B_GROUPS):
            add = _dot_tn(x16[:, g * gw:(g + 1) * gw], b16[:, g * B_STATE:(g + 1) * B_STATE])
            for hl in range(B_HEADS // B_GROUPS):
                hh = g * (B_HEADS // B_GROUPS) + hl
                rs = slice(hh * B_HEAD_DIM, (hh + 1) * B_HEAD_DIM)
                snew_ref[bi, rs, :] = s0[rs] * dec_ref[step * bb + bi, hh] + \
                    add[hl * B_HEAD_DIM:(hl + 1) * B_HEAD_DIM]
        return carry

    lax.fori_loop(0, bb, body, 0, unroll=2)


def _even_sample_back_kernel(hs_ref, ya_ref, ypart_ref, eacum_ref, z_ref, yoff_ref, normg_ref, wout_ref,
                             o_ref, *, steps, d, inner):
    bt = hs_ref.shape[0]
    half = inner // B_GROUPS
    mixes = []
    for t in range(steps):
        sl = slice(t * inner, (t + 1) * inner)
        yoff = jnp.concatenate(
            [yoff_ref[:, (g * steps + t) * inner + g * half:(g * steps + t) * inner + (g + 1) * half]
             for g in range(B_GROUPS)], axis=1)
        y = (ypart_ref[:, sl] + yoff * eacum_ref[:, sl]) * _silu(z_ref[:, sl])
        yn = []
        for g in range(B_GROUPS):
            yg = y[:, g * half:(g + 1) * half]
            yn.append(yg * lax.rsqrt(jnp.mean(yg * yg, axis=-1, keepdims=True) + EPS))
        yb = jnp.concatenate(yn, axis=1) * normg_ref[...]
        mixes.append(jnp.concatenate([ya_ref[:, t * d:(t + 1) * d], yb], axis=1))
    out = _dot(jnp.concatenate(mixes, axis=0).astype(BF16), wout_ref[...])
    for t in range(steps):
        o_ref[:, t * d:(t + 1) * d] = hs_ref[:, t * d:(t + 1) * d] + out[t * bt:(t + 1) * bt]


def _row_tiled(width, bt):
    return pl.BlockSpec((bt, width), lambda i: (i, 0))


def _even_sample(hs2, state_conv2, state_ssm3, p, steps, bt, bb):
    nb, _ = hs2.shape
    d = p["wa"].shape[0]
    width_a = p["wa"].shape[1] // 2
    inner = p["wz"].shape[1]
    conv_dim = p["wxbc"].shape[1]
    gn = B_GROUPS * B_STATE
    assert nb % bt == 0 and nb % bb == 0 and steps <= SUBLANES
    params = pltpu.CompilerParams(dimension_semantics=("arbitrary",), vmem_limit_bytes=VMEM_LIMIT)
    small = ["lng", "lnb", "wts", "bts", "convw", "convb", "dtb", "alog", "dskip"]
    widths = dict(v=steps * width_a, ya=steps * width_a, ypart=steps * inner, eacum=steps * inner,
                  z=steps * inner, conv=(B_CONV - 1) * conv_dim, cgt=SUBLANES * B_STATE,
                  xdw=SUBLANES * inner, bs=SUBLANES * gn, dec=LANES)
    front = pl.pallas_call(
        functools.partial(_even_sample_front_kernel, steps=steps, d=d, width_a=width_a, inner=inner,
                          conv_dim=conv_dim),
        out_shape=tuple(jax.ShapeDtypeStruct((nb, w), F32) for w in widths.values()),
        grid=(nb // bt,),
        in_specs=[_row_tiled(steps * d, bt), _resident((1, d))]
        + [_resident(p[k].shape) for k in ("wa", "wz", "wxbc", "wdt")]
        + [_resident(p[k].shape) for k in small]
        + [_row_tiled((B_CONV - 1) * conv_dim, bt)],
        out_specs=tuple(_row_tiled(w, bt) for w in widths.values()),
        compiler_params=params,
        name="even_sample_front",
    )(hs2, p["g"], p["wa"], p["wz"], p["wxbc"], p["wdt"], *[p[k] for k in small], state_conv2)
    v_rows, ya, ypart, eacum, z, new_conv, cgt, xdw, bs, dec = front

    hp = state_ssm3.shape[1]
    tile3 = lambda rows, width: pl.BlockSpec((bb, rows, width), lambda i: (i, 0, 0))
    yoff, new_ssm = pl.pallas_call(
        functools.partial(_even_sample_state_kernel, bb=bb),
        out_shape=(jax.ShapeDtypeStruct((nb, SUBLANES, hp), F32),
                   jax.ShapeDtypeStruct(state_ssm3.shape, F32)),
        grid=(nb // bb,),
        in_specs=[tile3(hp, B_STATE), tile3(SUBLANES, B_STATE), tile3(SUBLANES, inner), tile3(SUBLANES, gn),
                  pl.BlockSpec(memory_space=pltpu.SMEM)],
        out_specs=(tile3(SUBLANES, hp), tile3(hp, B_STATE)),
        compiler_params=params,
        name="even_sample_state",
    )(state_ssm3, cgt.reshape(nb, SUBLANES, B_STATE), xdw.reshape(nb, SUBLANES, inner),
      bs.reshape(nb, SUBLANES, gn), dec[:, :B_HEADS])

    out = pl.pallas_call(
        functools.partial(_even_sample_back_kernel, steps=steps, d=d, inner=inner),
        out_shape=jax.ShapeDtypeStruct(hs2.shape, F32),
        grid=(nb // bt,),
        in_specs=[_row_tiled(steps * d, bt), _row_tiled(steps * width_a, bt), _row_tiled(steps * inner, bt),
                  _row_tiled(steps * inner, bt), _row_tiled(steps * inner, bt), _row_tiled(SUBLANES * hp, bt),
                  _resident((1, inner)), _resident(p["wout"].shape)],
        out_specs=_row_tiled(steps * d, bt),
        compiler_params=params,
        name="even_sample_back",
    )(hs2, ya, ypart, eacum, z, yoff.reshape(nb, SUBLANES * hp), p["normg"], p["wout"])
    return out, v_rows, new_conv, new_ssm


def _odd_sample_front_kernel(hs_ref, g_ref, wc_ref, wq_ref, wk_ref, wv_ref, linw_ref, cscale_ref,
                             qn_ref, kn_ref, onesbd_ref, ps_ref,
                             yc_out, pool_out, q_out, knew_out, vnew_out, *, steps, d, width_c, past_len):
    bt = hs_ref.shape[0]
    xn = _rms(_stack_steps(hs_ref, steps, d), g_ref[...]).astype(BF16)
    blk = lambda a, t: a[t * bt:(t + 1) * bt]
    c_in = _dot(xn, wc_ref[...])
    n_state = max(C_WINDOWS) - 1
    ext = _steps(ps_ref[...], n_state, width_c) + [blk(c_in, t) for t in range(steps)]
    for j in range(n_state):
        pool_out[:, j * width_c:(j + 1) * width_c] = ext[len(ext) - n_state + j]
    gdim = width_c // len(C_WINDOWS)
    yc_cols = []
    for gi, win in enumerate(C_WINDOWS):
        sl = slice(gi * gdim, (gi + 1) * gdim)
        pooled = []
        for t in range(steps):
            hi = n_state + t
            lo = max(hi - win + 1, 0)
            acc = ext[lo][:, sl]
            for j in range(lo + 1, hi + 1):
                acc = acc + ext[j][:, sl]
            count = float(min(past_len + t + 1, win))
            pooled.append(acc / count - ext[hi][:, sl])
        yc_cols.append(_dot(jnp.concatenate(pooled, axis=0).astype(BF16), linw_ref[gi]))
    yc = jnp.concatenate(yc_cols, axis=1) * cscale_ref[...]
    for t in range(steps):
        yc_out[:, t * width_c:(t + 1) * width_c] = blk(yc, t)

    q = _dot(xn, wq_ref[...])
    k = _dot(xn, wk_ref[...])
    v = _dot(xn, wv_ref[...])
    ones_bd = onesbd_ref[...]
    inv_d = 1.0 / D_HEAD_DIM
    qn = q * lax.rsqrt(_head_sumsq(q, ones_bd) * inv_d + EPS) * qn_ref[...] * (D_HEAD_DIM ** -0.5)
    kn = k * lax.rsqrt(_head_sumsq(k, ones_bd) * inv_d + EPS) * kn_ref[...]
    qw = q.shape[1]
    kw = k.shape[1]
    pad = SUBLANES - steps
    q_out[:, steps * qw:] = jnp.zeros((bt, pad * qw), F32)
    knew_out[:, :pad * kw] = jnp.zeros((bt, pad * kw), F32)
    vnew_out[:, :pad * kw] = jnp.zeros((bt, pad * kw), F32)
    for t in range(steps):
        q_out[:, t * qw:(t + 1) * qw] = blk(qn, t)
        knew_out[:, (pad + t) * kw:(pad + t + 1) * kw] = blk(kn, t)
        vnew_out[:, (pad + t) * kw:(pad + t + 1) * kw] = blk(v, t)


SINK_BUCKET = REL_BUCKETS


def _odd_sample_attn_kernel(q_ref, knew_ref, vnew_ref, ck_ref, cv_ref, bucket_ref, sinks_ref, rel_ref,
                            o_ref, kout_ref, vout_ref, bias_ref, *, bb, steps, n_keys):
    win = ck_ref.shape[1]
    kvw = ck_ref.shape[2]
    tile16 = 2 * SUBLANES

    @pl.when(pl.program_id(0) == 0)
    def _():
        bucket = bucket_ref[...]
        for hh in range(D_Q_HEADS):
            acc = jnp.full(bucket.shape, NEG, F32)
            for bkt in range(REL_BUCKETS):
                acc = jnp.where(bucket == bkt, rel_ref[bkt, hh], acc)
            acc = jnp.where(bucket == SINK_BUCKET, sinks_ref[hh], acc)
            bias_ref[hh * SUBLANES:(hh + 1) * SUBLANES, :] = acc

    sub = lax.broadcasted_iota(jnp.int32, (SUBLANES, kvw), 0)
    new_rows = sub >= SUBLANES - steps
    lane_kv = lax.broadcasted_iota(jnp.int32, (SUBLANES, kvw), 1) // D_HEAD_DIM
    gw = D_KV_HEADS * D_HEAD_DIM
    zero_keys = jnp.zeros((n_keys - win - tile16, kvw), BF16)

    def extend(cache, new8):
        new16 = jnp.concatenate([new8, jnp.zeros_like(new8)], axis=0).astype(BF16)
        return jnp.concatenate([cache.astype(BF16), new16, zero_keys], axis=0)

    def shift_in(cache, new8, out_ref, bi):
        rolled = pltpu.roll(cache, win - steps, 0)
        out_ref[bi, 0:win - SUBLANES, :] = rolled[:win - SUBLANES]
        out_ref[bi, win - SUBLANES:, :] = jnp.where(new_rows, new8, rolled[win - SUBLANES:])

    def body(bi, carry):
        ck = ck_ref[bi]
        cv = cv_ref[bi]
        k8 = knew_ref[bi]
        v8 = vnew_ref[bi]
        shift_in(ck, k8, kout_ref, bi)
        shift_in(cv, v8, vout_ref, bi)
        q8 = q_ref[bi]
        pieces = []
        for kv in range(D_KV_HEADS):
            for grp in range(D_GROUP):
                qg = q8[:, grp * gw:(grp + 1) * gw]
                pieces.append(jnp.where(lane_kv == kv, qg, 0.0))
        lhs = jnp.concatenate(pieces, axis=0).astype(BF16)
        sc = _dot_nt(lhs, extend(ck, k8)) + bias_ref[...]
        m = jnp.max(sc, axis=-1, keepdims=True)
        pexp = jnp.exp(sc - m)
        probs = (pexp / jnp.sum(pexp, axis=-1, keepdims=True)).astype(BF16)
        ov = _dot(probs, extend(cv, v8))
        outs = []
        for grp in range(D_GROUP):
            r_last = ((D_KV_HEADS - 1) * D_GROUP + grp) * SUBLANES
            acc = ov[r_last:r_last + SUBLANES]
            for kv in range(D_KV_HEADS - 2, -1, -1):
                r0 = (kv * D_GROUP + grp) * SUBLANES
                acc = jnp.where(lane_kv == kv, ov[r0:r0 + SUBLANES], acc)
            outs.append(acc)
        o_ref[bi] = jnp.concatenate(outs, axis=1)
        return carry

    lax.fori_loop(0, bb, body, 0, unroll=SAMPLE_ATTN_UNROLL)


def _odd_sample_back_kernel(hs_ref, yc_ref, o_ref_in, wout_ref, out_ref, *, steps, d, width_c, qw):
    bt = hs_ref.shape[0]
    mix = jnp.concatenate(
        [jnp.concatenate([yc_ref[:, t * width_c:(t + 1) * width_c], o_ref_in[:, t * qw:(t + 1) * qw]], axis=1)
         for t in range(steps)], axis=0).astype(BF16)
    out = _dot(mix, wout_ref[...])
    for t in range(steps):
        out_ref[:, t * d:(t + 1) * d] = hs_ref[:, t * d:(t + 1) * d] + out[t * bt:(t + 1) * bt]


def _odd_sample(hs2, state_pool2, cache_k3, cache_v3, p, steps, past_len, bt, bb):
    nb, _ = hs2.shape
    d = p["wc"].shape[0]
    width_c = p["wc"].shape[1]
    qw = p["wq"].shape[1]
    kw = p["wk"].shape[1]
    win = cache_k3.shape[1]
    n_state = max(C_WINDOWS) - 1
    assert nb % bt == 0 and nb % bb == 0 and steps <= SUBLANES and win == CHUNK
    params = pltpu.CompilerParams(dimension_semantics=("arbitrary",), vmem_limit_bytes=VMEM_LIMIT)
    vm = ["linw", "cscale", "qn", "kn", "onesbd"]
    widths = dict(yc=steps * width_c, pool=n_state * width_c, q=SUBLANES * qw, knew=SUBLANES * kw,
                  vnew=SUBLANES * kw)
    yc, new_pool, q8, knew8, vnew8 = pl.pallas_call(
        functools.partial(_odd_sample_front_kernel, steps=steps, d=d, width_c=width_c, past_len=past_len),
        out_shape=tuple(jax.ShapeDtypeStruct((nb, w), F32) for w in widths.values()),
        grid=(nb // bt,),
        in_specs=[_row_tiled(steps * d, bt), _resident((1, d))]
        + [_resident(p[k].shape) for k in ("wc", "wq", "wk", "wv")]
        + [_resident(p[k].shape) for k in vm]
        + [_row_tiled(n_state * width_c, bt)],
        out_specs=tuple(_row_tiled(w, bt) for w in widths.values()),
        compiler_params=params,
        name="odd_sample_front",
    )(hs2, p["g"], p["wc"], p["wq"], p["wk"], p["wv"], *[p[k] for k in vm], state_pool2)

    n_keys = 2 * CHUNK
    pad = SUBLANES - steps
    bucket = np.full((SUBLANES, n_keys), -1, np.int32)
    for t in range(steps):
        q_pos = past_len + t
        k_pos = np.full(n_keys, -10 ** 9, np.int64)
        k_pos[:win] = past_len - win + np.arange(win)
        k_pos[win + pad:win + SUBLANES] = past_len + np.arange(steps)
        dist = q_pos - k_pos
        ok = (dist >= 0) & (dist < CHUNK) & (k_pos >= 0)
        bucket[t] = np.where(ok, _t5_bucket(np.where(ok, dist, 0)), -1)
    bucket[:, n_keys - 1] = SINK_BUCKET
    smem = pl.BlockSpec(memory_space=pltpu.SMEM)
    tile3 = lambda rows, width: pl.BlockSpec((bb, rows, width), lambda i: (i, 0, 0))
    o8, new_k, new_v = pl.pallas_call(
        functools.partial(_odd_sample_attn_kernel, bb=bb, steps=steps, n_keys=n_keys),
        out_shape=(jax.ShapeDtypeStruct((nb, SUBLANES, qw), F32),
                   jax.ShapeDtypeStruct(cache_k3.shape, F32),
                   jax.ShapeDtypeStruct(cache_v3.shape, F32)),
        grid=(nb // bb,),
        in_specs=[tile3(SUBLANES, qw), tile3(SUBLANES, kw), tile3(SUBLANES, kw), tile3(win, kw), tile3(win, kw),
                  _resident(bucket.shape), smem, smem],
        out_specs=(tile3(SUBLANES, qw), tile3(win, kw), tile3(win, kw)),
        scratch_shapes=[pltpu.VMEM((D_Q_HEADS * SUBLANES, n_keys), F32)],
        compiler_params=params,
        name="odd_sample_attn",
    )(q8.reshape(nb, SUBLANES, qw), knew8.reshape(nb, SUBLANES, kw), vnew8.reshape(nb, SUBLANES, kw),
      cache_k3, cache_v3, jnp.asarray(bucket), p["sinks"], p["rel"])

    out = pl.pallas_call(
        functools.partial(_odd_sample_back_kernel, steps=steps, d=d, width_c=width_c, qw=qw),
        out_shape=jax.ShapeDtypeStruct(hs2.shape, F32),
        grid=(nb // bt,),
        in_specs=[_row_tiled(steps * d, bt), _row_tiled(steps * width_c, bt), _row_tiled(SUBLANES * qw, bt),
                  _resident(p["wout"].shape)],
        out_specs=_row_tiled(steps * d, bt),
        compiler_params=params,
        name="odd_sample_back",
    )(hs2, yc, o8.reshape(nb, SUBLANES * qw), p["wout"])
    return out, new_pool, new_k, new_v


def _row(v):
    return v.reshape(1, -1).astype(F32)


def _pad_lanes(m, width=LANES):
    return jnp.pad(m, ((0, 0), (0, width - m.shape[1])))


def _head_expand_matrix():
    e = np.zeros((LANES, B_HEADS * B_HEAD_DIM), np.float32)
    for hh in range(B_HEADS):
        e[hh, hh * B_HEAD_DIM:(hh + 1) * B_HEAD_DIM] = 1.0
    return np.concatenate([e, e], axis=0)


def _column_select_matrix():
    e = np.zeros((LANES, B_HEADS * CHUNK), np.float32)
    for hh in range(B_HEADS):
        e[hh, hh * CHUNK:(hh + 1) * CHUNK] = 1.0
    return np.concatenate([e, e], axis=0)


def _prep_even(mix_norm, w_in, w_out, ln_g, ln_b, w_s, b_s, conv_w, conv_b, dt_bias, a_log, d_skip, norm_g):
    width_a = ln_g.shape[0]
    inner = norm_g.shape[0]
    conv_dim = conv_b.shape[0]
    o1 = 2 * width_a
    o2 = o1 + inner
    o3 = o2 + conv_dim
    return dict(
        g=_row(mix_norm),
        wa=w_in[:, :o1].astype(BF16),
        wz=w_in[:, o1:o2].astype(BF16),
        wxbc=w_in[:, o2:o3].astype(BF16),
        wdt=_pad_lanes(w_in[:, o3:]).astype(BF16),
        wout=w_out.astype(BF16),
        lng=_row(ln_g), lnb=_row(ln_b), ws=w_s,
        bsb=jnp.broadcast_to(b_s[:, :, None], b_s.shape + (width_a // A_HEADS,)),
        convw=conv_w, convb=_row(conv_b),
        dtb=_pad_lanes(_row(dt_bias)), alog=_pad_lanes(_row(a_log)),
        dskip=_row(jnp.repeat(d_skip, B_HEAD_DIM)), normg=_row(norm_g),
        expand=jnp.asarray(_head_expand_matrix(), BF16), colsel=jnp.asarray(_column_select_matrix(), BF16),
    )


def _prep_odd(mix_norm, w_in, w_out, lin_w, c_scale, q_norm, k_norm, sinks, rel_table):
    d = w_in.shape[0]
    width_c = c_scale.shape[0]
    qw = D_Q_HEADS * D_HEAD_DIM
    kw = D_KV_HEADS * D_HEAD_DIM
    wq = w_in[:, width_c:width_c + qw].reshape(d, D_KV_HEADS, D_GROUP, D_HEAD_DIM)
    wq = wq.transpose(0, 2, 1, 3).reshape(d, qw)
    wo_d = w_out[width_c:].reshape(D_KV_HEADS, D_GROUP, D_HEAD_DIM, -1).transpose(1, 0, 2, 3).reshape(qw, -1)
    ones_bd = np.kron(np.eye(256 // D_HEAD_DIM), np.ones((D_HEAD_DIM, D_HEAD_DIM))).astype(np.float32)
    return dict(
        g=_row(mix_norm),
        wc=w_in[:, :width_c].astype(BF16),
        wq=wq.astype(BF16),
        wk=w_in[:, width_c + qw:width_c + qw + kw].astype(BF16),
        wv=w_in[:, width_c + qw + kw:].astype(BF16),
        wout=jnp.concatenate([w_out[:width_c], wo_d], axis=0).astype(BF16),
        linw=lin_w.astype(BF16), cscale=_row(c_scale),
        qn=_row(jnp.tile(q_norm, D_Q_HEADS)), kn=_row(jnp.tile(k_norm, D_KV_HEADS)),
        onesbd=jnp.asarray(ones_bd, BF16),
        sinks=sinks.astype(F32), rel=rel_table.astype(F32),
    )


def _prep_even_sample(w_s, b_s, steps):
    head_w = CHUNK
    w = jnp.transpose(w_s[:, :steps, :steps], (1, 2, 0)).reshape(steps * steps, A_HEADS)
    b = b_s[:, :steps].T
    return dict(wts=jnp.repeat(w, head_w, axis=1), bts=jnp.repeat(b, head_w, axis=1))


PAST_LEN = 16384
FFN_TILE = 1024
FFN_STAGE_SLOTS = 8
FFN_STAGE_ROWS_GU = 32
FFN_STAGE_ROWS_D = 128
FFN_ROW_BLOCK = 256
FFN_BLOCK_TILES = 3
MIXER_TILE = 512
SAMPLE_ROW_TILE = 32
SAMPLE_SEQ_TILE = 8
SAMPLE_ATTN_UNROLL = 4


def kernel(x_prompt, x_sample, state_ssm, state_conv, state_pool, cache_k_win, cache_v_win,
           ffn1_norm, ffn1_w_gu, ffn1_w_down, mix_norm, ffn2_norm, ffn2_w_gu, ffn2_w_down,
           ev_w_in, ev_w_out, a_ln_g, a_ln_b, a_w_s, a_b_s, b_conv_w, b_conv_b, b_dt_bias, b_a_log,
           b_d_skip, b_norm_g, od_w_in, od_w_out, c_lin_w, c_scale, d_q_norm, d_k_norm, d_sinks,
           rel_bias_table):
    bp, seq, d = x_prompt.shape
    bs, steps, _ = x_sample.shape
    past_len = PAST_LEN
    hp = x_prompt
    hs = x_sample
    depth = ffn1_norm.shape[0]
    names = ("a_v_s", "ssm_p", "ssm_s", "conv_p", "conv_s", "pool_p", "pool_s", "k_p", "k_s", "v_p", "v_s")
    outs = {k: [] for k in names}

    def macaron(h_p, h_s, norm, w_gu_all, w_down_all, layer):
        o_p, o_s = _ffn(h_p.reshape(bp * seq, d), h_s.reshape(bs * steps, d), _row(norm), w_gu_all, w_down_all,
                        layer, FFN_TILE)
        return o_p.reshape(bp, seq, d), o_s.reshape(bs, steps, d)

    for layer in range(depth):
        i = layer // 2
        hp, hs = macaron(hp, hs, ffn1_norm[layer], ffn1_w_gu, ffn1_w_down, layer)
        hs2 = hs.reshape(bs, steps * d)
        if layer % 2 == 0:
            p = _prep_even(mix_norm[layer], ev_w_in[i], ev_w_out[i], a_ln_g[i], a_ln_b[i], a_w_s[i], a_b_s[i],
                           b_conv_w[i], b_conv_b[i], b_dt_bias[i], b_a_log[i], b_d_skip[i], b_norm_g[i])
            p.update(_prep_even_sample(a_w_s[i], a_b_s[i], steps))
            hp, conv_p, ssm_p = _even_prompt(hp, p, MIXER_TILE)
            hs2, v_rows, conv_s, ssm_s = _even_sample(
                hs2, state_conv[i].reshape(bs, -1), state_ssm[i].reshape(bs, B_HEADS * B_HEAD_DIM, B_STATE),
                p, steps, SAMPLE_ROW_TILE, SAMPLE_SEQ_TILE)
            outs["a_v_s"].append(v_rows.reshape(bs, steps, -1))
            outs["conv_p"].append(conv_p)
            outs["conv_s"].append(conv_s.reshape(state_conv[i].shape))
            outs["ssm_p"].append(ssm_p.reshape(bp, B_HEADS, B_HEAD_DIM, B_STATE))
            outs["ssm_s"].append(ssm_s.reshape(state_ssm[i].shape))
        else:
            p = _prep_odd(mix_norm[layer], od_w_in[i], od_w_out[i], c_lin_w[i], c_scale[i], d_q_norm[i],
                          d_k_norm[i], d_sinks[i], rel_bias_table)
            hp, pool_p, k_p, v_p = _odd_prompt(hp, p, MIXER_TILE)
            kv_shape = cache_k_win[i].shape
            hs2, pool_s, k_s, v_s = _odd_sample(
                hs2, state_pool[i].reshape(bs, -1), cache_k_win[i].reshape(bs, kv_shape[1], -1),
                cache_v_win[i].reshape(bs, kv_shape[1], -1), p, steps, past_len, SAMPLE_ROW_TILE, SAMPLE_SEQ_TILE)
            outs["pool_p"].append(pool_p)
            outs["pool_s"].append(pool_s.reshape(state_pool[i].shape))
            outs["k_p"].append(k_p.reshape(bp, CHUNK, D_KV_HEADS, D_HEAD_DIM))
            outs["v_p"].append(v_p.reshape(bp, CHUNK, D_KV_HEADS, D_HEAD_DIM))
            outs["k_s"].append(k_s.reshape(kv_shape))
            outs["v_s"].append(v_s.reshape(kv_shape))
        hs = hs2.reshape(bs, steps, d)
        hp, hs = macaron(hp, hs, ffn2_norm[layer], ffn2_w_gu, ffn2_w_down, layer)
    return (hp, hs) + tuple(jnp.stack(outs[k]) for k in names)
```

```python
import functools
import math

import numpy as np
import jax
import jax.numpy as jnp
from jax import lax
from jax.experimental import pallas as pl
from jax.experimental.pallas import tpu as pltpu

F32 = jnp.float32
BF16 = jnp.bfloat16

EPS = 1e-6
NEG = -1e30

LANES = 128
SUBLANES = 8
MXU_DIM = 256
VMEM_BYTES_V7X = 64 * 1024 * 1024
VMEM_LIMIT = VMEM_BYTES_V7X - 8 * 1024 * 1024

A_HEADS = 8
B_HEADS = 16
B_HEAD_DIM = 64
B_GROUPS = 2
B_STATE = 128
B_CONV = 4
CHUNK = 128
C_WINDOWS = (2, 4, 8, 16)
C_HALO = 16
D_Q_HEADS = 16
D_KV_HEADS = 4
D_HEAD_DIM = 64
D_GROUP = D_Q_HEADS // D_KV_HEADS
REL_BUCKETS = 32
REL_MAX_DIST = 128


def _rms(x, g):
    ms = jnp.mean(x * x, axis=-1, keepdims=True)
    return x * lax.rsqrt(ms + EPS) * g


def _sigmoid(x):
    return 1.0 / (1.0 + jnp.exp(-x))


def _silu(x):
    return x * _sigmoid(x)


def _gelu_tanh(x):
    c = math.sqrt(2.0 / math.pi)
    return x * (0.5 * (1.0 + jnp.tanh(c * (x + 0.044715 * (x * x * x)))))


def _softplus(x):
    return jnp.maximum(x, 0.0) + jnp.log1p(jnp.exp(-jnp.abs(x)))


def _split3(x):
    hi = x.astype(BF16)
    r1 = x - hi.astype(F32)
    mid = r1.astype(BF16)
    lo = (r1 - mid.astype(F32)).astype(BF16)
    return hi, mid, lo


def _split2_lanes(x):
    hi = x.astype(BF16)
    lo = (x - hi.astype(F32)).astype(BF16)
    return jnp.concatenate([hi, lo], axis=1)


def _dot(a, b):
    return jnp.dot(a, b, preferred_element_type=F32)


def _dot_nt(a, b):
    return lax.dot_general(a, b, (((1,), (1,)), ((), ())), preferred_element_type=F32)


def _dot_tn(a, b):
    return lax.dot_general(a, b, (((0,), (0,)), ((), ())), preferred_element_type=F32)


def _expand_heads(m, n_pairs):
    rows = m.shape[0]
    lane = lax.broadcasted_iota(jnp.int32, (rows, LANES), 1)
    first = lane < B_HEAD_DIM
    parts = []
    for p in range(n_pairs):
        a = jnp.broadcast_to(m[:, 2 * p:2 * p + 1], (rows, LANES))
        b = jnp.broadcast_to(m[:, 2 * p + 1:2 * p + 2], (rows, LANES))
        parts.append(jnp.where(first, a, b))
    return jnp.concatenate(parts, axis=1)


def _head_sumsq(x, ones_bd):
    xx = (x * x).astype(BF16)
    outs = [_dot(xx[:, c * 256:(c + 1) * 256], ones_bd) for c in range(x.shape[1] // 256)]
    return jnp.concatenate(outs, axis=1) if len(outs) > 1 else outs[0]


def _ff_blocks(d_ff):
    step = FFN_BLOCK_TILES * MXU_DIM
    return [(c0, min(c0 + step, d_ff)) for c0 in range(0, d_ff, step)]


def _load_cast_rows(src_hbm, dst_ref, stage_ref, sem_ref):
    slots, rows, _ = stage_ref.shape
    n = src_hbm.shape[0] // rows

    def copy(c):
        slot = c % slots
        return pltpu.make_async_copy(src_hbm.at[pl.ds(c * rows, rows), :], stage_ref.at[slot], sem_ref.at[slot])

    for c in range(min(slots, n)):
        copy(c).start(priority=c % 2)
    for c in range(n):
        copy(c).wait()
        dst_ref[c * rows:(c + 1) * rows, :] = stage_ref[c % slots].astype(BF16)
        if c + slots < n:
            copy(c + slots).start(priority=(c + slots) % 2)


def _ffn_rows(x_ref, g_ref, wgu_ref, wd_ref, o_ref, d_ff):
    tm = x_ref.shape[0]
    rows = [slice(r0, min(r0 + FFN_ROW_BLOCK, tm)) for r0 in range(0, tm, FFN_ROW_BLOCK)]
    xns = [_rms(x_ref[rs, :], g_ref[...]).astype(BF16) for rs in rows]
    for rs, xn in zip(rows, xns):
        y = None
        for c0, c1 in _ff_blocks(d_ff):
            gate = _dot(xn, wgu_ref[:, c0:c1])
            up = _dot(xn, wgu_ref[:, d_ff + c0:d_ff + c1])
            act = (_silu(gate) * up).astype(BF16)
            part = _dot(act, wd_ref[c0:c1, :])
            y = part if y is None else y + part
        o_ref[rs, :] = x_ref[rs, :] + 0.5 * y


def _ffn_kernel(xp_ref, xs_ref, g_ref, wgu_hbm, wd_hbm, op_ref, os_ref,
                wgu_ref, wd_ref, stage_gu_ref, stage_d_ref, sem_ref, *, d_ff, prompt_steps, layer):
    i = pl.program_id(0)

    @pl.when(i == 0)
    def _():
        _load_cast_rows(wgu_hbm.at[layer], wgu_ref, stage_gu_ref, sem_ref)
        _load_cast_rows(wd_hbm.at[layer], wd_ref, stage_d_ref, sem_ref)

    @pl.when(i < prompt_steps)
    def _():
        _ffn_rows(xp_ref, g_ref, wgu_ref, wd_ref, op_ref, d_ff)

    @pl.when(i == prompt_steps)
    def _():
        _ffn_rows(xs_ref, g_ref, wgu_ref, wd_ref, os_ref, d_ff)


def _resident(shape):
    nd = len(shape)
    return pl.BlockSpec(shape, lambda *_: (0,) * nd, pipeline_mode=pl.Buffered(1))


def _ffn(xp2d, xs2d, g, wgu_all, wd_all, layer, tm):
    m, d = xp2d.shape
    ms = xs2d.shape[0]
    wgu_shape, wd_shape = wgu_all.shape[1:], wd_all.shape[1:]
    d_ff = wd_shape[0]
    assert m % tm == 0 and d % FFN_STAGE_ROWS_GU == 0 and d_ff % FFN_STAGE_ROWS_D == 0
    steps = m // tm
    last = steps - 1
    whole = lambda shape: pl.BlockSpec(shape, lambda i: (0, 0))
    prompt_tile = pl.BlockSpec((tm, d), lambda i: (jnp.minimum(i, last), 0))
    hbm = pl.BlockSpec(memory_space=pl.ANY)
    return pl.pallas_call(
        functools.partial(_ffn_kernel, d_ff=d_ff, prompt_steps=steps, layer=layer),
        out_shape=(jax.ShapeDtypeStruct((m, d), F32), jax.ShapeDtypeStruct((ms, d), F32)),
        grid=(steps + 1,),
        in_specs=[prompt_tile, whole((ms, d)), _resident((1, d)), hbm, hbm],
        out_specs=(prompt_tile, whole((ms, d))),
        scratch_shapes=[pltpu.VMEM(wgu_shape, BF16), pltpu.VMEM(wd_shape, BF16),
                        pltpu.VMEM((FFN_STAGE_SLOTS, FFN_STAGE_ROWS_GU, wgu_shape[1]), F32),
                        pltpu.VMEM((FFN_STAGE_SLOTS, FFN_STAGE_ROWS_D, wd_shape[1]), F32),
                        pltpu.SemaphoreType.DMA((FFN_STAGE_SLOTS,))],
        compiler_params=pltpu.CompilerParams(
            dimension_semantics=("arbitrary",), vmem_limit_bytes=VMEM_LIMIT),
        name="ffn",
    )(xp2d, xs2d, g, wgu_all, wd_all)


def _even_prompt_kernel(h_ref, g_ref, wa_ref, wz_ref, wxbc_ref, wdt_ref, wout_ref,
                        lng_ref, lnb_ref, ws_ref, bsb_ref, convw_ref, convb_ref,
                        dtb_ref, alog_ref, dskip_ref, normg_ref, expand_ref, colsel_ref,
                        o_ref, conv_out_ref, ssm_out_ref,
                        ext_ref, st_ref, *, tile, width_a, inner):
    s = pl.program_id(1)
    n_chunks = tile // CHUNK
    halo = SUBLANES

    @pl.when(s == 0)
    def _():
        ext_ref[0:halo, :] = jnp.zeros((halo, ext_ref.shape[1]), F32)
        st_ref[...] = jnp.zeros(st_ref.shape, F32)

    x = h_ref[...]
    xn = _rms(x, g_ref[...]).astype(BF16)

    row = lax.broadcasted_iota(jnp.int32, (CHUNK, CHUNK), 0)
    col = lax.broadcasted_iota(jnp.int32, (CHUNK, CHUNK), 1)
    causal = row >= col
    lane = lax.broadcasted_iota(jnp.int32, (CHUNK, LANES), 1)
    first_half = lane < B_HEAD_DIM

    pa = _dot(xn, wa_ref[...])
    xbc_raw = _dot(xn, wxbc_ref[...])

    ga = _gelu_tanh(pa)
    u = ga[:, :width_a]
    v = ga[:, width_a:]
    mu = jnp.mean(v, axis=-1, keepdims=True)
    vc = v - mu
    var = jnp.mean(vc * vc, axis=-1, keepdims=True)
    v = vc * lax.rsqrt(var + EPS) * lng_ref[...] + lnb_ref[...]
    vb = v.astype(BF16)

    z = _dot(xn, wz_ref[...])
    dt_raw = _dot(xn, wdt_ref[...])

    ext_ref[halo:halo + tile, :] = xbc_raw
    ext = ext_ref[...]
    ext1 = pltpu.roll(ext, 1, 0)
    pair = ext * convw_ref[1:2, :] + ext1 * convw_ref[0:1, :]
    conv = (convb_ref[...] + ext * convw_ref[3:4, :] + ext1 * convw_ref[2:3, :] + pltpu.roll(pair, 2, 0))[halo:]
    tail = ext_ref[tile:tile + halo, :]
    ext_ref[0:halo, :] = tail
    conv_out_ref[...] = tail
    xbc = _silu(conv)
    gn = B_GROUPS * B_STATE
    xs = xbc[:, :inner]
    bm = xbc[:, inner:inner + gn]
    cm = xbc[:, inner + gn:]

    head_w = width_a // A_HEADS
    gate_cols = []
    for hh in range(A_HEADS):
        w = jnp.where(causal, ws_ref[hh], 0.0).astype(BF16)
        rhs = jnp.concatenate(
            [vb[c * CHUNK:(c + 1) * CHUNK, hh * head_w:(hh + 1) * head_w] for c in range(n_chunks)], axis=1)
        out = _dot(w, rhs)
        bias = bsb_ref[hh]
        gate_cols.append(jnp.concatenate(
            [out[:, c * head_w:(c + 1) * head_w] + bias for c in range(n_chunks)], axis=0))
    ya = u * jnp.concatenate(gate_cols, axis=1)
    out_a = _dot(ya.astype(BF16), wout_ref[0:width_a, :])

    dt = _softplus(dt_raw + dtb_ref[...])
    a_neg = -jnp.exp(alog_ref[...])
    da = dt * a_neg

    n_pairs = B_HEADS // 2
    heads_per_group = B_HEADS // B_GROUPS
    gw = heads_per_group * B_HEAD_DIM
    tril_ones = jnp.where(causal, 1.0, 0.0).astype(BF16)
    chunks = [slice(c * CHUNK, (c + 1) * CHUNK) for c in range(n_chunks)]
    acums = []
    for rs in chunks:
        d_hi, d_mid, d_lo = _split3(da[rs])
        acums.append(_dot(tril_ones, d_hi) + _dot(tril_ones, d_mid) + _dot(tril_ones, d_lo))
    acum = jnp.concatenate(acums, axis=0)
    decay = jnp.concatenate([jnp.exp(a[CHUNK - 1:CHUNK, :] - a) for a in acums], axis=0)
    expand = expand_ref[...]
    xd = xs * _dot(_split2_lanes(dt), expand)
    xdwb = (xs * _dot(_split2_lanes(dt * decay), expand)).astype(BF16)
    e_acum = _dot(_split2_lanes(jnp.exp(acum)), expand)
    bmb = bm.astype(BF16)
    cmb = cm.astype(BF16)
    y_rows = []
    for c, rs in enumerate(chunks):
        a_c = acums[c]
        acum_t = a_c.T
        a_cols = jnp.concatenate([jnp.broadcast_to(a_c[:, hh:hh + 1], (CHUNK, CHUNK)) for hh in range(B_HEADS)],
                                 axis=1)
        cb = [_dot_nt(cmb[rs, g * B_STATE:(g + 1) * B_STATE], bmb[rs, g * B_STATE:(g + 1) * B_STATE])
              for g in range(B_GROUPS)]
        y_parts = []
        for p in range(n_pairs):
            g = (2 * p) // heads_per_group
            ms = []
            for hh in (2 * p, 2 * p + 1):
                seg = a_cols[:, hh * CHUNK:(hh + 1) * CHUNK] - jnp.broadcast_to(acum_t[hh:hh + 1, :], (CHUNK, CHUNK))
                lmat = jnp.where(causal, jnp.exp(seg), 0.0)
                ms.append((cb[g] * lmat).astype(BF16))
            lhs = jnp.concatenate(ms, axis=1)
            xd_p = xd[rs, p * LANES:(p + 1) * LANES]
            rhs = jnp.concatenate([jnp.where(first_half, xd_p, 0.0),
                                   jnp.where(first_half, 0.0, xd_p)], axis=0).astype(BF16)
            y_parts.append(_dot(lhs, rhs))
        y_rows.append(jnp.concatenate(y_parts, axis=1))
    for c, rs in enumerate(chunks):
        st_prev = st_ref[...]
        stb = st_prev.astype(BF16)
        y_off = jnp.concatenate(
            [_dot(cmb[rs, g * B_STATE:(g + 1) * B_STATE], stb[:, g * gw:(g + 1) * gw]) for g in range(B_GROUPS)],
            axis=1)
        st_add = jnp.concatenate(
            [_dot_tn(bmb[rs, g * B_STATE:(g + 1) * B_STATE], xdwb[rs, g * gw:(g + 1) * gw])
             for g in range(B_GROUPS)], axis=1)
        chunk_decay = e_acum[(c + 1) * CHUNK - 1:(c + 1) * CHUNK, :]
        st_ref[...] = st_prev * chunk_decay + st_add
        y_rows[c] = y_rows[c] + y_off * e_acum[rs]
    y = (jnp.concatenate(y_rows, axis=0) if n_chunks > 1 else y_rows[0]) + xs * dskip_ref[...]
    y = y * _silu(z)
    half = inner // B_GROUPS
    yn = []
    for g in range(B_GROUPS):
        yg = y[:, g * half:(g + 1) * half]
        yn.append(yg * lax.rsqrt(jnp.mean(yg * yg, axis=-1, keepdims=True) + EPS))
    yb = jnp.concatenate(yn, axis=1) * normg_ref[...]

    o_ref[...] = x + out_a + _dot(yb.astype(BF16), wout_ref[width_a:, :])

    @pl.when(s == pl.num_programs(1) - 1)
    def _():
        ssm_out_ref[...] = st_ref[...].T


def _even_prompt(h, p, tile):
    b, seq, d = h.shape
    width_a = p["wa"].shape[1] // 2
    inner = p["wz"].shape[1]
    conv_dim = p["wxbc"].shape[1]
    assert seq % tile == 0 and tile % CHUNK == 0
    small = ["lng", "lnb", "ws", "bsb", "convw", "convb", "dtb", "alog", "dskip", "normg", "expand", "colsel"]
    out, conv_tail, ssm = pl.pallas_call(
        functools.partial(_even_prompt_kernel, tile=tile, width_a=width_a, inner=inner),
        out_shape=(jax.ShapeDtypeStruct((b, seq, d), F32),
                   jax.ShapeDtypeStruct((b, SUBLANES, conv_dim), F32),
                   jax.ShapeDtypeStruct((b, inner, B_STATE), F32)),
        grid=(b, seq // tile),
        in_specs=[pl.BlockSpec((None, tile, d), lambda i, j: (i, j, 0)),
                  _resident((1, d))]
        + [_resident(p[k].shape) for k in ("wa", "wz", "wxbc", "wdt", "wout")]
        + [_resident(p[k].shape) for k in small],
        out_specs=(pl.BlockSpec((None, tile, d), lambda i, j: (i, j, 0)),
                   pl.BlockSpec((None, SUBLANES, conv_dim), lambda i, j: (i, 0, 0)),
                   pl.BlockSpec((None, inner, B_STATE), lambda i, j: (i, 0, 0))),
        scratch_shapes=[pltpu.VMEM((tile + SUBLANES, conv_dim), F32),
                        pltpu.VMEM((B_STATE, inner), F32)],
        compiler_params=pltpu.CompilerParams(
            dimension_semantics=("arbitrary", "arbitrary"), vmem_limit_bytes=VMEM_LIMIT),
        name="even_prompt",
    )(h, p["g"], p["wa"], p["wz"], p["wxbc"], p["wdt"], p["wout"], *[p[k] for k in small])
    return out, conv_tail[:, SUBLANES - (B_CONV - 1):, :], ssm


def _t5_bucket(dist):
    n = np.maximum(dist, 0)
    max_exact = REL_BUCKETS // 2
    n_safe = np.maximum(n, 1).astype(np.float32)
    scale = np.float32((REL_BUCKETS - max_exact) / math.log(REL_MAX_DIST / max_exact))
    large = max_exact + (np.log(n_safe / max_exact) * scale).astype(np.int32)
    large = np.minimum(large, REL_BUCKETS - 1)
    return np.where(n < max_exact, n, large).astype(np.int32)


def _fill_bias(bias_ref, bucket_ref, rel_ref):
    bucket = bucket_ref[...]
    lq = bucket.shape[0]
    has_prev = lax.broadcasted_iota(jnp.int32, bucket.shape, 1) >= CHUNK
    for hh in range(D_Q_HEADS):
        kv, grp = divmod(hh, D_GROUP)
        acc = jnp.full(bucket.shape, NEG, F32)
        for bkt in range(REL_BUCKETS):
            acc = jnp.where(bucket == bkt, rel_ref[bkt, hh], acc)
        bias_ref[0, grp, kv * lq:(kv + 1) * lq, :] = acc
        bias_ref[1, grp, kv * lq:(kv + 1) * lq, :] = jnp.where(has_prev, acc, NEG)


def _group_attention(qg, kk, vv, bias_ref, sinks_ref, grp, table):
    lq = qg.shape[0]
    lane_kv = lax.broadcasted_iota(jnp.int32, (lq, D_KV_HEADS * D_HEAD_DIM), 1) // D_HEAD_DIM
    zero = jnp.zeros_like(qg)
    lhs = jnp.concatenate([jnp.where(lane_kv == kv, qg, zero) for kv in range(D_KV_HEADS)], axis=0)
    sc = _dot_nt(lhs, kk)
    probs = []
    for kv in range(D_KV_HEADS):
        rs = slice(kv * lq, (kv + 1) * lq)
        s_h = sc[rs] + bias_ref[table, grp, rs, :]
        sink = sinks_ref[kv * D_GROUP + grp]
        m = jnp.maximum(jnp.max(s_h, axis=-1, keepdims=True), sink)
        pexp = jnp.exp(s_h - m)
        denom = jnp.sum(pexp, axis=-1, keepdims=True) + jnp.exp(sink - m)
        probs.append((pexp / denom).astype(BF16))
    ov = _dot(jnp.concatenate(probs, axis=0), vv)
    out = ov[(D_KV_HEADS - 1) * lq:]
    for kv in range(D_KV_HEADS - 2, -1, -1):
        out = jnp.where(lane_kv == kv, ov[kv * lq:(kv + 1) * lq], out)
    return out


def _odd_prompt_kernel(h_ref, g_ref, wc_ref, wq_ref, wk_ref, wv_ref, wout_ref,
                       linw_ref, cscale_ref, qn_ref, kn_ref, onesbd_ref, bucket_ref,
                       sinks_ref, rel_ref,
                       o_ref, pool_out_ref, k_out_ref, v_out_ref,
                       extc_ref, kprev_ref, vprev_ref, bias_ref, *, tile, width_c):
    b = pl.program_id(0)
    s = pl.program_id(1)
    n_blocks = tile // CHUNK

    @pl.when((b == 0) & (s == 0))
    def _():
        _fill_bias(bias_ref, bucket_ref, rel_ref)

    @pl.when(s == 0)
    def _():
        extc_ref[0:C_HALO, :] = jnp.zeros((C_HALO, width_c), F32)
        kprev_ref[...] = jnp.zeros(kprev_ref.shape, F32)
        vprev_ref[...] = jnp.zeros(vprev_ref.shape, F32)

    x = h_ref[...]
    xn = _rms(x, g_ref[...]).astype(BF16)

    c_in = _dot(xn, wc_ref[...])
    extc_ref[C_HALO:C_HALO + tile, :] = c_in
    e = extc_ref[...]
    tail = extc_ref[tile:tile + C_HALO, :]
    extc_ref[0:C_HALO, :] = tail
    pool_out_ref[...] = tail
    pos = (s * tile + lax.broadcasted_iota(jnp.int32, (tile, 1), 0) + 1).astype(F32)
    gdim = width_c // len(C_WINDOWS)
    run = e
    shift = 1
    yc = []
    for gi, win in enumerate(C_WINDOWS):
        while shift < win:
            run = run + pltpu.roll(run, shift, 0)
            shift *= 2
        cnt = jnp.minimum(pos, float(win))
        pooled = run[C_HALO:, :gdim] / cnt - c_in[:, gi * gdim:(gi + 1) * gdim]
        yc.append(_dot(pooled.astype(BF16), linw_ref[gi]))
        if gi + 1 < len(C_WINDOWS):
            run = run[:, gdim:]
    yc = jnp.concatenate(yc, axis=1) * cscale_ref[...]

    q = _dot(xn, wq_ref[...])
    k = _dot(xn, wk_ref[...])
    v = _dot(xn, wv_ref[...])
    ones_bd = onesbd_ref[...]
    inv_d = 1.0 / D_HEAD_DIM
    qn = q * lax.rsqrt(_head_sumsq(q, ones_bd) * inv_d + EPS) * qn_ref[...]
    kn = k * lax.rsqrt(_head_sumsq(k, ones_bd) * inv_d + EPS) * kn_ref[...]
    qs = (qn * (D_HEAD_DIM ** -0.5)).astype(BF16)
    kb = kn.astype(BF16)
    vb = v.astype(BF16)
    first_table = jnp.where(s == 0, 1, 0)
    gw = D_KV_HEADS * D_HEAD_DIM
    o_rows = []
    for blk in range(n_blocks):
        rs = slice(blk * CHUNK, (blk + 1) * CHUNK)
        if blk == 0:
            k_prev, v_prev = kprev_ref[...].astype(BF16), vprev_ref[...].astype(BF16)
        else:
            k_prev, v_prev = kb[(blk - 1) * CHUNK:blk * CHUNK], vb[(blk - 1) * CHUNK:blk * CHUNK]
        kk = jnp.concatenate([k_prev, kb[rs]], axis=0)
        vv = jnp.concatenate([v_prev, vb[rs]], axis=0)
        o_rows.append(jnp.concatenate(
            [_group_attention(qs[rs, grp * gw:(grp + 1) * gw], kk, vv, bias_ref, sinks_ref, grp,
                              first_table if blk == 0 else 0)
             for grp in range(D_GROUP)], axis=1))
    kprev_ref[...] = kn[tile - CHUNK:]
    vprev_ref[...] = v[tile - CHUNK:]
    yd = jnp.concatenate(o_rows, axis=0) if n_blocks > 1 else o_rows[0]

    mix = jnp.concatenate([yc, yd], axis=1).astype(BF16)
    o_ref[...] = x + _dot(mix, wout_ref[...])

    @pl.when(s == pl.num_programs(1) - 1)
    def _():
        k_out_ref[...] = kn[tile - CHUNK:]
        v_out_ref[...] = v[tile - CHUNK:]


def _odd_prompt(h, p, tile):
    b, seq, d = h.shape
    width_c = p["wc"].shape[1]
    kvw = p["wk"].shape[1]
    assert seq % tile == 0 and tile % CHUNK == 0
    r = np.arange(CHUNK) + CHUNK
    c = np.arange(2 * CHUNK)
    dist = r[:, None] - c[None, :]
    bucket = np.where((dist >= 0) & (dist < CHUNK), _t5_bucket(dist), -1).astype(np.int32)
    vm = ["linw", "cscale", "qn", "kn", "onesbd"]
    smem = pl.BlockSpec(memory_space=pltpu.SMEM)
    out, pool_tail, k_win, v_win = pl.pallas_call(
        functools.partial(_odd_prompt_kernel, tile=tile, width_c=width_c),
        out_shape=(jax.ShapeDtypeStruct((b, seq, d), F32),
                   jax.ShapeDtypeStruct((b, C_HALO, width_c), F32),
                   jax.ShapeDtypeStruct((b, CHUNK, kvw), F32),
                   jax.ShapeDtypeStruct((b, CHUNK, kvw), F32)),
        grid=(b, seq // tile),
        in_specs=[pl.BlockSpec((None, tile, d), lambda i, j: (i, j, 0)),
                  _resident((1, d))]
        + [_resident(p[k].shape) for k in ("wc", "wq", "wk", "wv", "wout")]
        + [_resident(p[k].shape) for k in vm]
        + [_resident(bucket.shape), smem, smem],
        out_specs=(pl.BlockSpec((None, tile, d), lambda i, j: (i, j, 0)),
                   pl.BlockSpec((None, C_HALO, width_c), lambda i, j: (i, 0, 0)),
                   pl.BlockSpec((None, CHUNK, kvw), lambda i, j: (i, 0, 0)),
                   pl.BlockSpec((None, CHUNK, kvw), lambda i, j: (i, 0, 0))),
        scratch_shapes=[pltpu.VMEM((tile + C_HALO, width_c), F32),
                        pltpu.VMEM((CHUNK, kvw), F32),
                        pltpu.VMEM((CHUNK, kvw), F32),
                        pltpu.VMEM((2, D_GROUP, D_KV_HEADS * CHUNK, 2 * CHUNK), F32)],
        compiler_params=pltpu.CompilerParams(
            dimension_semantics=("arbitrary", "arbitrary"), vmem_limit_bytes=VMEM_LIMIT),
        name="odd_prompt",
    )(h, p["g"], p["wc"], p["wq"], p["wk"], p["wv"], p["wout"], *[p[k] for k in vm],
      jnp.asarray(bucket), p["sinks"], p["rel"])
    return out, pool_tail[:, C_HALO - (max(C_WINDOWS) - 1):, :], k_win, v_win


def _steps(x, n, width):
    return [x[:, t * width:(t + 1) * width] for t in range(n)]


def _stack_steps(ref, n, width):
    x = ref[...]
    return jnp.concatenate(_steps(x, n, width), axis=0)


def _even_sample_front_kernel(hs_ref, g_ref, wa_ref, wz_ref, wxbc_ref, wdt_ref,
                              lng_ref, lnb_ref, wts_ref, bts_ref, convw_ref, convb_ref,
                              dtb_ref, alog_ref, dskip_ref, cs_ref,
                              v_out, ya_out, ypart_out, eacum_out, z_out, conv_out,
                              cgt_out, xdw_out, bs_out, dec_out, *, steps, d, width_a, inner, conv_dim):
    bt = hs_ref.shape[0]
    xn = _rms(_stack_steps(hs_ref, steps, d), g_ref[...]).astype(BF16)
    blk = lambda a, t: a[t * bt:(t + 1) * bt]

    ga = _gelu_tanh(_dot(xn, wa_ref[...]))
    u = ga[:, :width_a]
    v = ga[:, width_a:]
    mu = jnp.mean(v, axis=-1, keepdims=True)
    vc = v - mu
    var = jnp.mean(vc * vc, axis=-1, keepdims=True)
    v = vc * lax.rsqrt(var + EPS) * lng_ref[...] + lnb_ref[...]
    for t in range(steps):
        v_out[:, t * width_a:(t + 1) * width_a] = blk(v, t)
        gate = bts_ref[t:t + 1, :]
        for s in range(t + 1):
            gate = gate + wts_ref[t * steps + s:t * steps + s + 1, :] * blk(v, s)
        ya_out[:, t * width_a:(t + 1) * width_a] = blk(u, t) * gate

    z = _dot(xn, wz_ref[...])
    for t in range(steps):
        z_out[:, t * inner:(t + 1) * inner] = blk(z, t)
    raw = _dot(xn, wxbc_ref[...])
    dt = _softplus(_dot(xn, wdt_ref[...]) + dtb_ref[...])
    ext = _steps(cs_ref[...], B_CONV - 1, conv_dim) + [blk(raw, t) for t in range(steps)]
    for k in range(B_CONV - 1):
        conv_out[:, k * conv_dim:(k + 1) * conv_dim] = ext[len(ext) - (B_CONV - 1) + k]
    gn = B_GROUPS * B_STATE
    n_pairs = B_HEADS // 2
    a_neg = -jnp.exp(alog_ref[...])
    xs, bm, cm, dts, acum = [], [], [], [], []
    for t in range(steps):
        conv = convb_ref[...]
        for tap in range(B_CONV):
            conv = conv + ext[t + tap] * convw_ref[tap:tap + 1, :]
        xbc = _silu(conv)
        xs.append(xbc[:, :inner])
        bm.append(xbc[:, inner:inner + gn])
        cm.append(xbc[:, inner + gn:])
        dts.append(blk(dt, t))
        da = dts[t] * a_neg
        acum.append(da if t == 0 else acum[t - 1] + da)
    lane = lax.broadcasted_iota(jnp.int32, (bt, LANES), 1)
    group0 = lane < (B_HEADS // B_GROUPS)
    dec_out[...] = jnp.exp(acum[steps - 1])
    pad_rows = SUBLANES - steps
    xdw_out[:, steps * inner:] = jnp.zeros((bt, pad_rows * inner), F32)
    bs_out[:, steps * gn:] = jnp.zeros((bt, pad_rows * gn), F32)
    xd = []
    for t in range(steps):
        xd.append(xs[t] * _expand_heads(dts[t], n_pairs))
        eacum_out[:, t * inner:(t + 1) * inner] = _expand_heads(jnp.exp(acum[t]), n_pairs)
        xdw_out[:, t * inner:(t + 1) * inner] = xs[t] * _expand_heads(
            dts[t] * jnp.exp(acum[steps - 1] - acum[t]), n_pairs)
        bs_out[:, t * gn:(t + 1) * gn] = bm[t]
        for g in range(B_GROUPS):
            r = g * steps + t
            cgt_out[:, r * B_STATE:(r + 1) * B_STATE] = cm[t][:, g * B_STATE:(g + 1) * B_STATE]
    for t in range(steps):
        y = xs[t] * dskip_ref[...]
        for s in range(t + 1):
            cb = [jnp.sum(cm[t][:, g * B_STATE:(g + 1) * B_STATE] * bm[s][:, g * B_STATE:(g + 1) * B_STATE],
                          axis=-1, keepdims=True) for g in range(B_GROUPS)]
            coef = jnp.where(group0, cb[0], cb[1]) * jnp.exp(acum[t] - acum[s])
            y = y + _expand_heads(coef, n_pairs) * xd[s]
        ypart_out[:, t * inner:(t + 1) * inner] = y


def _even_sample_state_kernel(s0_ref, cgt_ref, xdw_ref, bs_ref, dec_ref, yoff_ref, snew_ref, *, bb):
    step = pl.program_id(0)
    gw = (B_HEADS // B_GROUPS) * B_HEAD_DIM

    def body(bi, carry):
        s0 = s0_ref[bi]
        c8 = cgt_ref[bi]
        c16 = jnp.concatenate([c8, jnp.zeros_like(c8)], axis=0).astype(BF16)
        yoff_ref[bi] = _dot_nt(c16, s0.astype(BF16))[:SUBLANES]
        x8 = xdw_ref[bi]
        b8 = bs_ref[bi]
        x16 = jnp.concatenate([x8, jnp.zeros_like(x8)], axis=0).astype(BF16)
        b16 = jnp.concatenate([b8, jnp.zeros_like(b8)], axis=0).astype(BF16)
        for g in range(B_GROUPS):
            add = _dot_tn(x16[:, g * gw:(g + 1) * gw], b16[:, g * B_STATE:(g + 1) * B_STATE])
            for hl in range(B_HEADS // B_GROUPS):
                hh = g * (B_HEADS // B_GROUPS) + hl
                rs = slice(hh * B_HEAD_DIM, (hh + 1) * B_HEAD_DIM)
                snew_ref[bi, rs, :] = s0[rs] * dec_ref[step * bb + bi, hh] + \
                    add[hl * B_HEAD_DIM:(hl + 1) * B_HEAD_DIM]
        return carry

    lax.fori_loop(0, bb, body, 0, unroll=2)


def _even_sample_back_kernel(hs_ref, ya_ref, ypart_ref, eacum_ref, z_ref, yoff_ref, normg_ref, wout_ref,
                             o_ref, *, steps, d, inner):
    bt = hs_ref.shape[0]
    half = inner // B_GROUPS
    mixes = []
    for t in range(steps):
        sl = slice(t * inner, (t + 1) * inner)
        yoff = jnp.concatenate(
            [yoff_ref[:, (g * steps + t) * inner + g * half:(g * steps + t) * inner + (g + 1) * half]
             for g in range(B_GROUPS)], axis=1)
        y = (ypart_ref[:, sl] + yoff * eacum_ref[:, sl]) * _silu(z_ref[:, sl])
        yn = []
        for g in range(B_GROUPS):
            yg = y[:, g * half:(g + 1) * half]
            yn.append(yg * lax.rsqrt(jnp.mean(yg * yg, axis=-1, keepdims=True) + EPS))
        yb = jnp.concatenate(yn, axis=1) * normg_ref[...]
        mixes.append(jnp.concatenate([ya_ref[:, t * d:(t + 1) * d], yb], axis=1))
    out = _dot(jnp.concatenate(mixes, axis=0).astype(BF16), wout_ref[...])
    for t in range(steps):
        o_ref[:, t * d:(t + 1) * d] = hs_ref[:, t * d:(t + 1) * d] + out[t * bt:(t + 1) * bt]


def _row_tiled(width, bt):
    return pl.BlockSpec((bt, width), lambda i: (i, 0))


def _even_sample(hs2, state_conv2, state_ssm3, p, steps, bt, bb):
    nb, _ = hs2.shape
    d = p["wa"].shape[0]
    width_a = p["wa"].shape[1] // 2
    inner = p["wz"].shape[1]
    conv_dim = p["wxbc"].shape[1]
    gn = B_GROUPS * B_STATE
    assert nb % bt == 0 and nb % bb == 0 and steps <= SUBLANES
    params = pltpu.CompilerParams(dimension_semantics=("arbitrary",), vmem_limit_bytes=VMEM_LIMIT)
    small = ["lng", "lnb", "wts", "bts", "convw", "convb", "dtb", "alog", "dskip"]
    widths = dict(v=steps * width_a, ya=steps * width_a, ypart=steps * inner, eacum=steps * inner,
                  z=steps * inner, conv=(B_CONV - 1) * conv_dim, cgt=SUBLANES * B_STATE,
                  xdw=SUBLANES * inner, bs=SUBLANES * gn, dec=LANES)
    front = pl.pallas_call(
        functools.partial(_even_sample_front_kernel, steps=steps, d=d, width_a=width_a, inner=inner,
                          conv_dim=conv_dim),
        out_shape=tuple(jax.ShapeDtypeStruct((nb, w), F32) for w in widths.values()),
        grid=(nb // bt,),
        in_specs=[_row_tiled(steps * d, bt), _resident((1, d))]
        + [_resident(p[k].shape) for k in ("wa", "wz", "wxbc", "wdt")]
        + [_resident(p[k].shape) for k in small]
        + [_row_tiled((B_CONV - 1) * conv_dim, bt)],
        out_specs=tuple(_row_tiled(w, bt) for w in widths.values()),
        compiler_params=params,
        name="even_sample_front",
    )(hs2, p["g"], p["wa"], p["wz"], p["wxbc"], p["wdt"], *[p[k] for k in small], state_conv2)
    v_rows, ya, ypart, eacum, z, new_conv, cgt, xdw, bs, dec = front

    hp = state_ssm3.shape[1]
    tile3 = lambda rows, width: pl.BlockSpec((bb, rows, width), lambda i: (i, 0, 0))
    yoff, new_ssm = pl.pallas_call(
        functools.partial(_even_sample_state_kernel, bb=bb),
        out_shape=(jax.ShapeDtypeStruct((nb, SUBLANES, hp), F32),
                   jax.ShapeDtypeStruct(state_ssm3.shape, F32)),
        grid=(nb // bb,),
        in_specs=[tile3(hp, B_STATE), tile3(SUBLANES, B_STATE), tile3(SUBLANES, inner), tile3(SUBLANES, gn),
                  pl.BlockSpec(memory_space=pltpu.SMEM)],
        out_specs=(tile3(SUBLANES, hp), tile3(hp, B_STATE)),
        compiler_params=params,
        name="even_sample_state",
    )(state_ssm3, cgt.reshape(nb, SUBLANES, B_STATE), xdw.reshape(nb, SUBLANES, inner),
      bs.reshape(nb, SUBLANES, gn), dec[:, :B_HEADS])

    out = pl.pallas_call(
        functools.partial(_even_sample_back_kernel, steps=steps, d=d, inner=inner),
        out_shape=jax.ShapeDtypeStruct(hs2.shape, F32),
        grid=(nb // bt,),
        in_specs=[_row_tiled(steps * d, bt), _row_tiled(steps * width_a, bt), _row_tiled(steps * inner, bt),
                  _row_tiled(steps * inner, bt), _row_tiled(steps * inner, bt), _row_tiled(SUBLANES * hp, bt),
                  _resident((1, inner)), _resident(p["wout"].shape)],
        out_specs=_row_tiled(steps * d, bt),
        compiler_params=params,
        name="even_sample_back",
    )(hs2, ya, ypart, eacum, z, yoff.reshape(nb, SUBLANES * hp), p["normg"], p["wout"])
    return out, v_rows, new_conv, new_ssm


def _odd_sample_front_kernel(hs_ref, g_ref, wc_ref, wq_ref, wk_ref, wv_ref, linw_ref, cscale_ref,
                             qn_ref, kn_ref, onesbd_ref, ps_ref,
                             yc_out, pool_out, q_out, knew_out, vnew_out, *, steps, d, width_c, past_len):
    bt = hs_ref.shape[0]
    xn = _rms(_stack_steps(hs_ref, steps, d), g_ref[...]).astype(BF16)
    blk = lambda a, t: a[t * bt:(t + 1) * bt]
    c_in = _dot(xn, wc_ref[...])
    n_state = max(C_WINDOWS) - 1
    ext = _steps(ps_ref[...], n_state, width_c) + [blk(c_in, t) for t in range(steps)]
    for j in range(n_state):
        pool_out[:, j * width_c:(j + 1) * width_c] = ext[len(ext) - n_state + j]
    gdim = width_c // len(C_WINDOWS)
    yc_cols = []
    for gi, win in enumerate(C_WINDOWS):
        sl = slice(gi * gdim, (gi + 1) * gdim)
        pooled = []
        for t in range(steps):
            hi = n_state + t
            lo = max(hi - win + 1, 0)
            acc = ext[lo][:, sl]
            for j in range(lo + 1, hi + 1):
                acc = acc + ext[j][:, sl]
            count = float(min(past_len + t + 1, win))
            pooled.append(acc / count - ext[hi][:, sl])
        yc_cols.append(_dot(jnp.concatenate(pooled, axis=0).astype(BF16), linw_ref[gi]))
    yc = jnp.concatenate(yc_cols, axis=1) * cscale_ref[...]
    for t in range(steps):
        yc_out[:, t * width_c:(t + 1) * width_c] = blk(yc, t)

    q = _dot(xn, wq_ref[...])
    k = _dot(xn, wk_ref[...])
    v = _dot(xn, wv_ref[...])
    ones_bd = onesbd_ref[...]
    inv_d = 1.0 / D_HEAD_DIM
    qn = q * lax.rsqrt(_head_sumsq(q, ones_bd) * inv_d + EPS) * qn_ref[...] * (D_HEAD_DIM ** -0.5)
    kn = k * lax.rsqrt(_head_sumsq(k, ones_bd) * inv_d + EPS) * kn_ref[...]
    qw = q.shape[1]
    kw = k.shape[1]
    pad = SUBLANES - steps
    q_out[:, steps * qw:] = jnp.zeros((bt, pad * qw), F32)
    knew_out[:, :pad * kw] = jnp.zeros((bt, pad * kw), F32)
    vnew_out[:, :pad * kw] = jnp.zeros((bt, pad * kw), F32)
    for t in range(steps):
        q_out[:, t * qw:(t + 1) * qw] = blk(qn, t)
        knew_out[:, (pad + t) * kw:(pad + t + 1) * kw] = blk(kn, t)
        vnew_out[:, (pad + t) * kw:(pad + t + 1) * kw] = blk(v, t)


SINK_BUCKET = REL_BUCKETS


def _odd_sample_attn_kernel(q_ref, knew_ref, vnew_ref, ck_ref, cv_ref, bucket_ref, sinks_ref, rel_ref,
                            o_ref, kout_ref, vout_ref, bias_ref, *, bb, steps, n_keys):
    win = ck_ref.shape[1]
    kvw = ck_ref.shape[2]
    tile16 = 2 * SUBLANES

    @pl.when(pl.program_id(0) == 0)
    def _():
        bucket = bucket_ref[...]
        for hh in range(D_Q_HEADS):
            acc = jnp.full(bucket.shape, NEG, F32)
            for bkt in range(REL_BUCKETS):
                acc = jnp.where(bucket == bkt, rel_ref[bkt, hh], acc)
            acc = jnp.where(bucket == SINK_BUCKET, sinks_ref[hh], acc)
            bias_ref[hh * SUBLANES:(hh + 1) * SUBLANES, :] = acc

    sub = lax.broadcasted_iota(jnp.int32, (SUBLANES, kvw), 0)
    new_rows = sub >= SUBLANES - steps
    lane_kv = lax.broadcasted_iota(jnp.int32, (SUBLANES, kvw), 1) // D_HEAD_DIM
    gw = D_KV_HEADS * D_HEAD_DIM
    zero_keys = jnp.zeros((n_keys - win - tile16, kvw), BF16)

    def extend(cache, new8):
        new16 = jnp.concatenate([new8, jnp.zeros_like(new8)], axis=0).astype(BF16)
        return jnp.concatenate([cache.astype(BF16), new16, zero_keys], axis=0)

    def shift_in(cache, new8, out_ref, bi):
        rolled = pltpu.roll(cache, win - steps, 0)
        out_ref[bi, 0:win - SUBLANES, :] = rolled[:win - SUBLANES]
        out_ref[bi, win - SUBLANES:, :] = jnp.where(new_rows, new8, rolled[win - SUBLANES:])

    def body(bi, carry):
        ck = ck_ref[bi]
        cv = cv_ref[bi]
        k8 = knew_ref[bi]
        v8 = vnew_ref[bi]
        shift_in(ck, k8, kout_ref, bi)
        shift_in(cv, v8, vout_ref, bi)
        q8 = q_ref[bi]
        pieces = []
        for kv in range(D_KV_HEADS):
            for grp in range(D_GROUP):
                qg = q8[:, grp * gw:(grp + 1) * gw]
                pieces.append(jnp.where(lane_kv == kv, qg, 0.0))
        lhs = jnp.concatenate(pieces, axis=0).astype(BF16)
        sc = _dot_nt(lhs, extend(ck, k8)) + bias_ref[...]
        m = jnp.max(sc, axis=-1, keepdims=True)
        pexp = jnp.exp(sc - m)
        probs = (pexp / jnp.sum(pexp, axis=-1, keepdims=True)).astype(BF16)
        ov = _dot(probs, extend(cv, v8))
        outs = []
        for grp in range(D_GROUP):
            r_last = ((D_KV_HEADS - 1) * D_GROUP + grp) * SUBLANES
            acc = ov[r_last:r_last + SUBLANES]
            for kv in range(D_KV_HEADS - 2, -1, -1):
                r0 = (kv * D_GROUP + grp) * SUBLANES
                acc = jnp.where(lane_kv == kv, ov[r0:r0 + SUBLANES], acc)
            outs.append(acc)
        o_ref[bi] = jnp.concatenate(outs, axis=1)
        return carry

    lax.fori_loop(0, bb, body, 0, unroll=SAMPLE_ATTN_UNROLL)


def _odd_sample_back_kernel(hs_ref, yc_ref, o_ref_in, wout_ref, out_ref, *, steps, d, width_c, qw):
    bt = hs_ref.shape[0]
    mix = jnp.concatenate(
        [jnp.concatenate([yc_ref[:, t * width_c:(t + 1) * width_c], o_ref_in[:, t * qw:(t + 1) * qw]], axis=1)
         for t in range(steps)], axis=0).astype(BF16)
    out = _dot(mix, wout_ref[...])
    for t in range(steps):
        out_ref[:, t * d:(t + 1) * d] = hs_ref[:, t * d:(t + 1) * d] + out[t * bt:(t + 1) * bt]


def _odd_sample(hs2, state_pool2, cache_k3, cache_v3, p, steps, past_len, bt, bb):
    nb, _ = hs2.shape
    d = p["wc"].shape[0]
    width_c = p["wc"].shape[1]
    qw = p["wq"].shape[1]
    kw = p["wk"].shape[1]
    win = cache_k3.shape[1]
    n_state = max(C_WINDOWS) - 1
    assert nb % bt == 0 and nb % bb == 0 and steps <= SUBLANES and win == CHUNK
    params = pltpu.CompilerParams(dimension_semantics=("arbitrary",), vmem_limit_bytes=VMEM_LIMIT)
    vm = ["linw", "cscale", "qn", "kn", "onesbd"]
    widths = dict(yc=steps * width_c, pool=n_state * width_c, q=SUBLANES * qw, knew=SUBLANES * kw,
                  vnew=SUBLANES * kw)
    yc, new_pool, q8, knew8, vnew8 = pl.pallas_call(
        functools.partial(_odd_sample_front_kernel, steps=steps, d=d, width_c=width_c, past_len=past_len),
        out_shape=tuple(jax.ShapeDtypeStruct((nb, w), F32) for w in widths.values()),
        grid=(nb // bt,),
        in_specs=[_row_tiled(steps * d, bt), _resident((1, d))]
        + [_resident(p[k].shape) for k in ("wc", "wq", "wk", "wv")]
        + [_resident(p[k].shape) for k in vm]
        + [_row_tiled(n_state * width_c, bt)],
        out_specs=tuple(_row_tiled(w, bt) for w in widths.values()),
        compiler_params=params,
        name="odd_sample_front",
    )(hs2, p["g"], p["wc"], p["wq"], p["wk"], p["wv"], *[p[k] for k in vm], state_pool2)

    n_keys = 2 * CHUNK
    pad = SUBLANES - steps
    bucket = np.full((SUBLANES, n_keys), -1, np.int32)
    for t in range(steps):
        q_pos = past_len + t
        k_pos = np.full(n_keys, -10 ** 9, np.int64)
        k_pos[:win] = past_len - win + np.arange(win)
        k_pos[win + pad:win + SUBLANES] = past_len + np.arange(steps)
        dist = q_pos - k_pos
        ok = (dist >= 0) & (dist < CHUNK) & (k_pos >= 0)
        bucket[t] = np.where(ok, _t5_bucket(np.where(ok, dist, 0)), -1)
    bucket[:, n_keys - 1] = SINK_BUCKET
    smem = pl.BlockSpec(memory_space=pltpu.SMEM)
    tile3 = lambda rows, width: pl.BlockSpec((bb, rows, width), lambda i: (i, 0, 0))
    o8, new_k, new_v = pl.pallas_call(
        functools.partial(_odd_sample_attn_kernel, bb=bb, steps=steps, n_keys=n_keys),
        out_shape=(jax.ShapeDtypeStruct((nb, SUBLANES, qw), F32),
                   jax.ShapeDtypeStruct(cache_k3.shape, F32),
                   jax.ShapeDtypeStruct(cache_v3.shape, F32)),
        grid=(nb // bb,),
        in_specs=[tile3(SUBLANES, qw), tile3(SUBLANES, kw), tile3(SUBLANES, kw), tile3(win, kw), tile3(win, kw),
                  _resident(bucket.shape), smem, smem],
        out_specs=(tile3(SUBLANES, qw), tile3(win, kw), tile3(win, kw)),
        scratch_shapes=[pltpu.VMEM((D_Q_HEADS * SUBLANES, n_keys), F32)],
        compiler_params=params,
        name="odd_sample_attn",
    )(q8.reshape(nb, SUBLANES, qw), knew8.reshape(nb, SUBLANES, kw), vnew8.reshape(nb, SUBLANES, kw),
      cache_k3, cache_v3, jnp.asarray(bucket), p["sinks"], p["rel"])

    out = pl.pallas_call(
        functools.partial(_odd_sample_back_kernel, steps=steps, d=d, width_c=width_c, qw=qw),
        out_shape=jax.ShapeDtypeStruct(hs2.shape, F32),
        grid=(nb // bt,),
        in_specs=[_row_tiled(steps * d, bt), _row_tiled(steps * width_c, bt), _row_tiled(SUBLANES * qw, bt),
                  _resident(p["wout"].shape)],
        out_specs=_row_tiled(steps * d, bt),
        compiler_params=params,
        name="odd_sample_back",
    )(hs2, yc, o8.reshape(nb, SUBLANES * qw), p["wout"])
    return out, new_pool, new_k, new_v


def _row(v):
    return v.reshape(1, -1).astype(F32)


def _pad_lanes(m, width=LANES):
    return jnp.pad(m, ((0, 0), (0, width - m.shape[1])))


def _head_expand_matrix():
    e = np.zeros((LANES, B_HEADS * B_HEAD_DIM), np.float32)
    for hh in range(B_HEADS):
        e[hh, hh * B_HEAD_DIM:(hh + 1) * B_HEAD_DIM] = 1.0
    return np.concatenate([e, e], axis=0)


def _column_select_matrix():
    e = np.zeros((LANES, B_HEADS * CHUNK), np.float32)
    for hh in range(B_HEADS):
        e[hh, hh * CHUNK:(hh + 1) * CHUNK] = 1.0
    return np.concatenate([e, e], axis=0)


def _prep_even(mix_norm, w_in, w_out, ln_g, ln_b, w_s, b_s, conv_w, conv_b, dt_bias, a_log, d_skip, norm_g):
    width_a = ln_g.shape[0]
    inner = norm_g.shape[0]
    conv_dim = conv_b.shape[0]
    o1 = 2 * width_a
    o2 = o1 + inner
    o3 = o2 + conv_dim
    return dict(
        g=_row(mix_norm),
        wa=w_in[:, :o1].astype(BF16),
        wz=w_in[:, o1:o2].astype(BF16),
        wxbc=w_in[:, o2:o3].astype(BF16),
        wdt=_pad_lanes(w_in[:, o3:]).astype(BF16),
        wout=w_out.astype(BF16),
        lng=_row(ln_g), lnb=_row(ln_b), ws=w_s,
        bsb=jnp.broadcast_to(b_s[:, :, None], b_s.shape + (width_a // A_HEADS,)),
        convw=conv_w, convb=_row(conv_b),
        dtb=_pad_lanes(_row(dt_bias)), alog=_pad_lanes(_row(a_log)),
        dskip=_row(jnp.repeat(d_skip, B_HEAD_DIM)), normg=_row(norm_g),
        expand=jnp.asarray(_head_expand_matrix(), BF16), colsel=jnp.asarray(_column_select_matrix(), BF16),
    )


def _prep_odd(mix_norm, w_in, w_out, lin_w, c_scale, q_norm, k_norm, sinks, rel_table):
    d = w_in.shape[0]
    width_c = c_scale.shape[0]
    qw = D_Q_HEADS * D_HEAD_DIM
    kw = D_KV_HEADS * D_HEAD_DIM
    wq = w_in[:, width_c:width_c + qw].reshape(d, D_KV_HEADS, D_GROUP, D_HEAD_DIM)
    wq = wq.transpose(0, 2, 1, 3).reshape(d, qw)
    wo_d = w_out[width_c:].reshape(D_KV_HEADS, D_GROUP, D_HEAD_DIM, -1).transpose(1, 0, 2, 3).reshape(qw, -1)
    ones_bd = np.kron(np.eye(256 // D_HEAD_DIM), np.ones((D_HEAD_DIM, D_HEAD_DIM))).astype(np.float32)
    return dict(
        g=_row(mix_norm),
        wc=w_in[:, :width_c].astype(BF16),
        wq=wq.astype(BF16),
        wk=w_in[:, width_c + qw:width_c + qw + kw].astype(BF16),
        wv=w_in[:, width_c + qw + kw:].astype(BF16),
        wout=jnp.concatenate([w_out[:width_c], wo_d], axis=0).astype(BF16),
        linw=lin_w.astype(BF16), cscale=_row(c_scale),
        qn=_row(jnp.tile(q_norm, D_Q_HEADS)), kn=_row(jnp.tile(k_norm, D_KV_HEADS)),
        onesbd=jnp.asarray(ones_bd, BF16),
        sinks=sinks.astype(F32), rel=rel_table.astype(F32),
    )


def _prep_even_sample(w_s, b_s, steps):
    head_w = CHUNK
    w = jnp.transpose(w_s[:, :steps, :steps], (1, 2, 0)).reshape(steps * steps, A_HEADS)
    b = b_s[:, :steps].T
    return dict(wts=jnp.repeat(w, head_w, axis=1), bts=jnp.repeat(b, head_w, axis=1))


PAST_LEN = 16384
FFN_TILE = 1024
FFN_STAGE_SLOTS = 8
FFN_STAGE_ROWS_GU = 32
FFN_STAGE_ROWS_D = 128
FFN_ROW_BLOCK = 256
FFN_BLOCK_TILES = 3
MIXER_TILE = 512
SAMPLE_ROW_TILE = 32
SAMPLE_SEQ_TILE = 8
SAMPLE_ATTN_UNROLL = 4


def kernel(x_prompt, x_sample, state_ssm, state_conv, state_pool, cache_k_win, cache_v_win,
           ffn1_norm, ffn1_w_gu, ffn1_w_down, mix_norm, ffn2_norm, ffn2_w_gu, ffn2_w_down,
           ev_w_in, ev_w_out, a_ln_g, a_ln_b, a_w_s, a_b_s, b_conv_w, b_conv_b, b_dt_bias, b_a_log,
           b_d_skip, b_norm_g, od_w_in, od_w_out, c_lin_w, c_scale, d_q_norm, d_k_norm, d_sinks,
           rel_bias_table):
    bp, seq, d = x_prompt.shape
    bs, steps, _ = x_sample.shape
    past_len = PAST_LEN
    hp = x_prompt
    hs = x_sample
    depth = ffn1_norm.shape[0]
    names = ("a_v_s", "ssm_p", "ssm_s", "conv_p", "conv_s", "pool_p", "pool_s", "k_p", "k_s", "v_p", "v_s")
    outs = {k: [] for k in names}

    def macaron(h_p, h_s, norm, w_gu_all, w_down_all, layer):
        o_p, o_s = _ffn(h_p.reshape(bp * seq, d), h_s.reshape(bs * steps, d), _row(norm), w_gu_all, w_down_all,
                        layer, FFN_TILE)
        return o_p.reshape(bp, seq, d), o_s.reshape(bs, steps, d)

    for layer in range(depth):
        i = layer // 2
        hp, hs = macaron(hp, hs, ffn1_norm[layer], ffn1_w_gu, ffn1_w_down, layer)
        hs2 = hs.reshape(bs, steps * d)
        if layer % 2 == 0:
            p = _prep_even(mix_norm[layer], ev_w_in[i], ev_w_out[i], a_ln_g[i], a_ln_b[i], a_w_s[i], a_b_s[i],
                           b_conv_w[i], b_conv_b[i], b_dt_bias[i], b_a_log[i], b_d_skip[i], b_norm_g[i])
            p.update(_prep_even_sample(a_w_s[i], a_b_s[i], steps))
            hp, conv_p, ssm_p = _even_prompt(hp, p, MIXER_TILE)
            hs2, v_rows, conv_s, ssm_s = _even_sample(
                hs2, state_conv[i].reshape(bs, -1), state_ssm[i].reshape(bs, B_HEADS * B_HEAD_DIM, B_STATE),
                p, steps, SAMPLE_ROW_TILE, SAMPLE_SEQ_TILE)
            outs["a_v_s"].append(v_rows.reshape(bs, steps, -1))
            outs["conv_p"].append(conv_p)
            outs["conv_s"].append(conv_s.reshape(state_conv[i].shape))
            outs["ssm_p"].append(ssm_p.reshape(bp, B_HEADS, B_HEAD_DIM, B_STATE))
            outs["ssm_s"].append(ssm_s.reshape(state_ssm[i].shape))
        else:
            p = _prep_odd(mix_norm[layer], od_w_in[i], od_w_out[i], c_lin_w[i], c_scale[i], d_q_norm[i],
                          d_k_norm[i], d_sinks[i], rel_bias_table)
            hp, pool_p, k_p, v_p = _odd_prompt(hp, p, MIXER_TILE)
            kv_shape = cache_k_win[i].shape
            hs2, pool_s, k_s, v_s = _odd_sample(
                hs2, state_pool[i].reshape(bs, -1), cache_k_win[i].reshape(bs, kv_shape[1], -1),
                cache_v_win[i].reshape(bs, kv_shape[1], -1), p, steps, past_len, SAMPLE_ROW_TILE, SAMPLE_SEQ_TILE)
            outs["pool_p"].append(pool_p)
            outs["pool_s"].append(pool_s.reshape(state_pool[i].shape))
            outs["k_p"].append(k_p.reshape(bp, CHUNK, D_KV_HEADS, D_HEAD_DIM))
            outs["v_p"].append(v_p.reshape(bp, CHUNK, D_KV_HEADS, D_HEAD_DIM))
            outs["k_s"].append(k_s.reshape(kv_shape))
            outs["v_s"].append(v_s.reshape(kv_shape))
        hs = hs2.reshape(bs, steps, d)
        hp, hs = macaron(hp, hs, ffn2_norm[layer], ffn2_w_gu, ffn2_w_down, layer)
    return (hp, hs) + tuple(jnp.stack(outs[k]) for k in names)
```

```python
import functools
import math

import numpy as np
import jax
import jax.numpy as jnp
from jax import lax
from jax.experimental import pallas as pl
from jax.experimental.pallas import tpu as pltpu

F32 = jnp.float32
BF16 = jnp.bfloat16

EPS = 1e-6
NEG = -1e30

LANES = 128
SUBLANES = 8
MXU_DIM = 256
VMEM_BYTES_V7X = 64 * 1024 * 1024
VMEM_LIMIT = VMEM_BYTES_V7X - 8 * 1024 * 1024
A_HEADS = 8
B_HEADS = 16
B_HEAD_DIM = 64
B_GROUPS = 2
B_STATE = 128
B_CONV = 4
CHUNK = 128
C_WINDOWS = (2, 4, 8, 16)
C_HALO = 16
D_Q_HEADS = 16
D_KV_HEADS = 4
D_HEAD_DIM = 64
D_GROUP = D_Q_HEADS // D_KV_HEADS
REL_BUCKETS = 32
REL_MAX_DIST = 128


def _rms(x, g):
    ms = jnp.mean(x * x, axis=-1, keepdims=True)
    return x * lax.rsqrt(ms + EPS) * g


def _sigmoid(x):
    return 1.0 / (1.0 + jnp.exp(-x))


def _silu(x):
    return x * _sigmoid(x)


def _gelu_tanh(x):
    c = math.sqrt(2.0 / math.pi)
    return x * (0.5 * (1.0 + jnp.tanh(c * (x + 0.044715 * (x * x * x)))))


def _softplus(x):
    return jnp.maximum(x, 0.0) + jnp.log1p(jnp.exp(-jnp.abs(x)))


def _split3(x):
    hi = x.astype(BF16)
    r1 = x - hi.astype(F32)
    mid = r1.astype(BF16)
    lo = (r1 - mid.astype(F32)).astype(BF16)
    return hi, mid, lo


def _split2_lanes(x):
    hi = x.astype(BF16)
    lo = (x - hi.astype(F32)).astype(BF16)
    return jnp.concatenate([hi, lo], axis=1)


def _dot(a, b):
    return jnp.dot(a, b, preferred_element_type=F32)


def _dot_nt(a, b):
    return lax.dot_general(a, b, (((1,), (1,)), ((), ())), preferred_element_type=F32)


def _dot_tn(a, b):
    return lax.dot_general(a, b, (((0,), (0,)), ((), ())), preferred_element_type=F32)


def _expand_heads(m, n_pairs):
    rows = m.shape[0]
    lane = lax.broadcasted_iota(jnp.int32, (rows, LANES), 1)
    first = lane < B_HEAD_DIM
    parts = []
    for p in range(n_pairs):
        a = jnp.broadcast_to(m[:, 2 * p:2 * p + 1], (rows, LANES))
        b = jnp.broadcast_to(m[:, 2 * p + 1:2 * p + 2], (rows, LANES))
        parts.append(jnp.where(first, a, b))
    return jnp.concatenate(parts, axis=1)


def _head_sumsq(x, ones_bd):
    xx = (x * x).astype(BF16)
    outs = [_dot(xx[:, c * 256:(c + 1) * 256], ones_bd) for c in range(x.shape[1] // 256)]
    return jnp.concatenate(outs, axis=1) if len(outs) > 1 else outs[0]


def _ff_blocks(d_ff):
    step = FFN_BLOCK_TILES * MXU_DIM
    return [(c0, min(c0 + step, d_ff)) for c0 in range(0, d_ff, step)]


def _load_cast_rows(src_hbm, dst_ref, stage_ref, sem_ref):
    slots, rows, _ = stage_ref.shape
    n = src_hbm.shape[0] // rows

    def copy(c):
        slot = c % slots
        return pltpu.make_async_copy(src_hbm.at[pl.ds(c * rows, rows), :], stage_ref.at[slot], sem_ref.at[slot])

    for c in range(min(slots, n)):
        copy(c).start(priority=c % 2)
    for c in range(n):
        copy(c).wait()
        dst_ref[c * rows:(c + 1) * rows, :] = stage_ref[c % slots].astype(BF16)
        if c + slots < n:
            copy(c + slots).start(priority=(c + slots) % 2)


def _ffn_rows(x_ref, g_ref, wgu_ref, wd_ref, o_ref, d_ff):
    tm = x_ref.shape[0]
    rows = [slice(r0, min(r0 + FFN_ROW_BLOCK, tm)) for r0 in range(0, tm, FFN_ROW_BLOCK)]
    xns = [_rms(x_ref[rs, :], g_ref[...]).astype(BF16) for rs in rows]
    for rs, xn in zip(rows, xns):
        y = None
        for c0, c1 in _ff_blocks(d_ff):
            gate = _dot(xn, wgu_ref[:, c0:c1])
            up = _dot(xn, wgu_ref[:, d_ff + c0:d_ff + c1])
            act = (_silu(gate) * up).astype(BF16)
            part = _dot(act, wd_ref[c0:c1, :])
            y = part if y is None else y + part
        o_ref[rs, :] = x_ref[rs, :] + 0.5 * y


def _ffn_kernel(xp_ref, xs_ref, g_ref, wgu_hbm, wd_hbm, op_ref, os_ref,
                wgu_ref, wd_ref, stage_gu_ref, stage_d_ref, sem_ref, *, d_ff, prompt_steps, layer):
    i = pl.program_id(0)

    @pl.when(i == 0)
    def _():
        _load_cast_rows(wgu_hbm.at[layer], wgu_ref, stage_gu_ref, sem_ref)
        _load_cast_rows(wd_hbm.at[layer], wd_ref, stage_d_ref, sem_ref)

    @pl.when(i < prompt_steps)
    def _():
        _ffn_rows(xp_ref, g_ref, wgu_ref, wd_ref, op_ref, d_ff)

    @pl.when(i == prompt_steps)
    def _():
        _ffn_rows(xs_ref, g_ref, wgu_ref, wd_ref, os_ref, d_ff)


def _resident(shape):
    nd = len(shape)
    return pl.BlockSpec(shape, lambda *_: (0,) * nd, pipeline_mode=pl.Buffered(1))


def _ffn(xp2d, xs2d, g, wgu_all, wd_all, layer, tm):
    m, d = xp2d.shape
    ms = xs2d.shape[0]
    wgu_shape, wd_shape = wgu_all.shape[1:], wd_all.shape[1:]
    d_ff = wd_shape[0]
    assert m % tm == 0 and d % FFN_STAGE_ROWS_GU == 0 and d_ff % FFN_STAGE_ROWS_D == 0
    steps = m // tm
    last = steps - 1
    whole = lambda shape: pl.BlockSpec(shape, lambda i: (0, 0))
    prompt_tile = pl.BlockSpec((tm, d), lambda i: (jnp.minimum(i, last), 0))
    hbm = pl.BlockSpec(memory_space=pl.ANY)
    return pl.pallas_call(
        functools.partial(_ffn_kernel, d_ff=d_ff, prompt_steps=steps, layer=layer),
        out_shape=(jax.ShapeDtypeStruct((m, d), F32), jax.ShapeDtypeStruct((ms, d), F32)),
        grid=(steps + 1,),
        in_specs=[prompt_tile, whole((ms, d)), _resident((1, d)), hbm, hbm],
        out_specs=(prompt_tile, whole((ms, d))),
        scratch_shapes=[pltpu.VMEM(wgu_shape, BF16), pltpu.VMEM(wd_shape, BF16),
                        pltpu.VMEM((FFN_STAGE_SLOTS, FFN_STAGE_ROWS_GU, wgu_shape[1]), F32),
                        pltpu.VMEM((FFN_STAGE_SLOTS, FFN_STAGE_ROWS_D, wd_shape[1]), F32),
                        pltpu.SemaphoreType.DMA((FFN_STAGE_SLOTS,))],
        compiler_params=pltpu.CompilerParams(
            dimension_semantics=("arbitrary",), vmem_limit_bytes=VMEM_LIMIT),
        name="ffn",
    )(xp2d, xs2d, g, wgu_all, wd_all)


def _zip_stages(first, second, lag=1):
    for i in range(max(len(first), len(second) + lag)):
        if i < len(first):
            first[i]()
        if 0 <= i - lag < len(second):
            second[i - lag]()


def _even_prompt_kernel(h_ref, g_ref, wa_ref, wz_ref, wxbc_ref, wdt_ref, wout_ref,
                        lng_ref, lnb_ref, ws_ref, bsb_ref, convw_ref, convb_ref,
                        dtb_ref, alog_ref, dskip_ref, normg_ref, expand_ref, colsel_ref,
                        o_ref, conv_out_ref, ssm_out_ref,
                        ext_ref, st_ref, *, tile, width_a, inner):
    s = pl.program_id(1)
    n_chunks = tile // CHUNK
    halo = SUBLANES

    @pl.when(s == 0)
    def _():
        ext_ref[0:halo, :] = jnp.zeros((halo, ext_ref.shape[1]), F32)
        st_ref[...] = jnp.zeros(st_ref.shape, F32)

    x = h_ref[...]
    xn = _rms(x, g_ref[...]).astype(BF16)

    row = lax.broadcasted_iota(jnp.int32, (CHUNK, CHUNK), 0)
    col = lax.broadcasted_iota(jnp.int32, (CHUNK, CHUNK), 1)
    causal = row >= col
    lane = lax.broadcasted_iota(jnp.int32, (CHUNK, LANES), 1)
    first_half = lane < B_HEAD_DIM

    blk = MXU_DIM
    col_blocks = lambda width: [slice(j * blk, (j + 1) * blk) for j in range(width // blk)]
    head_w = width_a // A_HEADS
    gn = B_GROUPS * B_STATE
    conv_dim = inner + 2 * gn
    n_pairs = B_HEADS // 2
    heads_per_group = B_HEADS // B_GROUPS
    gw = heads_per_group * B_HEAD_DIM
    chunks = [slice(c * CHUNK, (c + 1) * CHUNK) for c in range(n_chunks)]

    ga = {}
    a_cols = [slice(width_a + sl.start, width_a + sl.stop) for sl in col_blocks(width_a)] + col_blocks(width_a)
    pa = {}

    def proj_a(j):
        pa[j] = _dot(xn, wa_ref[:, a_cols[j]])

    def act_a(j):
        ga[j] = _gelu_tanh(pa.pop(j))

    n_a = len(a_cols)
    _zip_stages([functools.partial(proj_a, j) for j in range(n_a)],
                [functools.partial(act_a, j) for j in range(n_a)])
    nv = n_a // 2
    v = jnp.concatenate([ga[j] for j in range(nv)], axis=1)
    u_blocks = [ga[j] for j in range(nv, n_a)]

    raw = {}
    xbc_blocks = {}
    vb_box = []

    def proj_xbc(k):
        raw[k] = _dot(xn, wxbc_ref[:, k * blk:(k + 1) * blk])

    def layer_norm_v():
        mu = jnp.mean(v, axis=-1, keepdims=True)
        vc = v - mu
        var = jnp.mean(vc * vc, axis=-1, keepdims=True)
        vb_box.append((vc * lax.rsqrt(var + EPS) * lng_ref[...] + lnb_ref[...]).astype(BF16))

    def conv_block(k):
        cs = slice(k * blk, (k + 1) * blk)
        ext_ref[halo:halo + tile, cs] = raw.pop(k)
        ext = ext_ref[:, cs]
        ext1 = pltpu.roll(ext, 1, 0)
        pair = ext * convw_ref[1:2, cs] + ext1 * convw_ref[0:1, cs]
        conv = (convb_ref[:, cs] + ext * convw_ref[3:4, cs] + ext1 * convw_ref[2:3, cs]
                + pltpu.roll(pair, 2, 0))[halo:]
        tail = ext_ref[tile:tile + halo, cs]
        ext_ref[0:halo, cs] = tail
        conv_out_ref[:, cs] = tail
        xbc_blocks[k] = _silu(conv)

    n_x = conv_dim // blk
    _zip_stages([functools.partial(proj_xbc, k) for k in range(n_x)],
                [layer_norm_v] + [functools.partial(conv_block, k) for k in range(n_x)])
    vb = vb_box[0]

    z_blocks = {}
    gate_cols = {}

    def proj_z(j):
        z_blocks[j] = _dot(xn, wz_ref[:, j * blk:(j + 1) * blk])

    def mix_head(hh):
        w = jnp.where(causal, ws_ref[hh], 0.0).astype(BF16)
        rhs = jnp.concatenate([vb[rs, hh * head_w:(hh + 1) * head_w] for rs in chunks], axis=1)
        out = _dot(w, rhs)
        bias = bsb_ref[hh]
        gate_cols[hh] = jnp.concatenate(
            [out[:, c * head_w:(c + 1) * head_w] + bias for c in range(n_chunks)], axis=0)

    dt_raw = _dot(xn, wdt_ref[...])
    n_z = inner // blk
    heads_per_z = A_HEADS // n_z
    for j in range(n_z):
        proj_z(j)
        for hh in range(j * heads_per_z, (j + 1) * heads_per_z):
            mix_head(hh)
    heads_per_blk = blk // head_w
    ya_blocks = [(u_blocks[j] * jnp.concatenate(
        [gate_cols[j * heads_per_blk + i] for i in range(heads_per_blk)], axis=1)).astype(BF16)
        for j in range(len(u_blocks))]
    ya = jnp.concatenate(ya_blocks, axis=1)
    xs = jnp.concatenate([xbc_blocks[k] for k in range(inner // blk)], axis=1)
    bm = xbc_blocks[inner // blk]
    cm = xbc_blocks[inner // blk + 1]

    dt = _softplus(dt_raw + dtb_ref[...])
    a_neg = -jnp.exp(alog_ref[...])
    da = dt * a_neg
    tril_ones = jnp.where(causal, 1.0, 0.0).astype(BF16)
    acums = []
    for rs in chunks:
        d_hi, d_mid, d_lo = _split3(da[rs])
        acums.append(_dot(tril_ones, d_hi) + _dot(tril_ones, d_mid) + _dot(tril_ones, d_lo))
    acum = jnp.concatenate(acums, axis=0)
    decay = jnp.concatenate([jnp.exp(a[CHUNK - 1:CHUNK, :] - a) for a in acums], axis=0)
    expand = expand_ref[...]
    xd = xs * _dot(_split2_lanes(dt), expand)
    xdwb = (xs * _dot(_split2_lanes(dt * decay), expand)).astype(BF16)
    e_acum = _dot(_split2_lanes(jnp.exp(acum)), expand)
    bmb = bm.astype(BF16)
    cmb = cm.astype(BF16)

    y_rows = [None] * n_chunks
    out_a = {}

    def scan_chunk(c):
        rs = chunks[c]
        a_c = acums[c]
        acum_t = a_c.T
        cb = [_dot_nt(cmb[rs, g * B_STATE:(g + 1) * B_STATE], bmb[rs, g * B_STATE:(g + 1) * B_STATE])
              for g in range(B_GROUPS)]
        y_parts = []
        for p in range(n_pairs):
            g = (2 * p) // heads_per_group
            ms = []
            for hh in (2 * p, 2 * p + 1):
                seg = jnp.broadcast_to(a_c[:, hh:hh + 1], (CHUNK, CHUNK)) - \
                    jnp.broadcast_to(acum_t[hh:hh + 1, :], (CHUNK, CHUNK))
                lmat = jnp.where(causal, jnp.exp(seg), 0.0)
                ms.append((cb[g] * lmat).astype(BF16))
            lhs = jnp.concatenate(ms, axis=1)
            xd_p = xd[rs, p * LANES:(p + 1) * LANES]
            rhs = jnp.concatenate([jnp.where(first_half, xd_p, 0.0),
                                   jnp.where(first_half, 0.0, xd_p)], axis=0).astype(BF16)
            y_parts.append(_dot(lhs, rhs))
        y_rows[c] = jnp.concatenate(y_parts, axis=1)

    def proj_out_a(j):
        out_a[j] = _dot(ya, wout_ref[0:width_a, j * blk:(j + 1) * blk])

    n_o = o_ref.shape[1] // blk
    _zip_stages([functools.partial(scan_chunk, c) for c in range(n_chunks)],
                [functools.partial(proj_out_a, j) for j in range(n_o)], lag=0)

    for c, rs in enumerate(chunks):
        st_prev = st_ref[...]
        stb = st_prev.astype(BF16)
        y_off = jnp.concatenate(
            [_dot(cmb[rs, g * B_STATE:(g + 1) * B_STATE], stb[:, g * gw:(g + 1) * gw]) for g in range(B_GROUPS)],
            axis=1)
        st_add = jnp.concatenate(
            [_dot_tn(bmb[rs, g * B_STATE:(g + 1) * B_STATE], xdwb[rs, g * gw:(g + 1) * gw])
             for g in range(B_GROUPS)], axis=1)
        chunk_decay = e_acum[(c + 1) * CHUNK - 1:(c + 1) * CHUNK, :]
        st_ref[...] = st_prev * chunk_decay + st_add
        y_rows[c] = y_rows[c] + y_off * e_acum[rs]
    y = (jnp.concatenate(y_rows, axis=0) if n_chunks > 1 else y_rows[0]) + xs * dskip_ref[...]
    z = jnp.concatenate([z_blocks[j] for j in range(n_z)], axis=1)
    y = y * _silu(z)
    half = inner // B_GROUPS
    yn = []
    for g in range(B_GROUPS):
        yg = y[:, g * half:(g + 1) * half]
        yn.append(yg * lax.rsqrt(jnp.mean(yg * yg, axis=-1, keepdims=True) + EPS))
    yb = (jnp.concatenate(yn, axis=1) * normg_ref[...]).astype(BF16)

    for j in range(n_o):
        cs = slice(j * blk, (j + 1) * blk)
        o_ref[:, cs] = x[:, cs] + out_a[j] + _dot(yb, wout_ref[width_a:, cs])

    @pl.when(s == pl.num_programs(1) - 1)
    def _():
        ssm_out_ref[...] = st_ref[...].T


def _even_prompt(h, p, tile):
    b, seq, d = h.shape
    width_a = p["wa"].shape[1] // 2
    inner = p["wz"].shape[1]
    conv_dim = p["wxbc"].shape[1]
    assert seq % tile == 0 and tile % CHUNK == 0
    small = ["lng", "lnb", "ws", "bsb", "convw", "convb", "dtb", "alog", "dskip", "normg", "expand", "colsel"]
    out, conv_tail, ssm = pl.pallas_call(
        functools.partial(_even_prompt_kernel, tile=tile, width_a=width_a, inner=inner),
        out_shape=(jax.ShapeDtypeStruct((b, seq, d), F32),
                   jax.ShapeDtypeStruct((b, SUBLANES, conv_dim), F32),
                   jax.ShapeDtypeStruct((b, inner, B_STATE), F32)),
        grid=(b, seq // tile),
        in_specs=[pl.BlockSpec((None, tile, d), lambda i, j: (i, j, 0)),
                  _resident((1, d))]
        + [_resident(p[k].shape) for k in ("wa", "wz", "wxbc", "wdt", "wout")]
        + [_resident(p[k].shape) for k in small],
        out_specs=(pl.BlockSpec((None, tile, d), lambda i, j: (i, j, 0)),
                   pl.BlockSpec((None, SUBLANES, conv_dim), lambda i, j: (i, 0, 0)),
                   pl.BlockSpec((None, inner, B_STATE), lambda i, j: (i, 0, 0))),
        scratch_shapes=[pltpu.VMEM((tile + SUBLANES, conv_dim), F32),
                        pltpu.VMEM((B_STATE, inner), F32)],
        compiler_params=pltpu.CompilerParams(
            dimension_semantics=("arbitrary", "arbitrary"), vmem_limit_bytes=VMEM_LIMIT),
        name="even_prompt",
    )(h, p["g"], p["wa"], p["wz"], p["wxbc"], p["wdt"], p["wout"], *[p[k] for k in small])
    return out, conv_tail[:, SUBLANES - (B_CONV - 1):, :], ssm


def _t5_bucket(dist):
    n = np.maximum(dist, 0)
    max_exact = REL_BUCKETS // 2
    n_safe = np.maximum(n, 1).astype(np.float32)
    scale = np.float32((REL_BUCKETS - max_exact) / math.log(REL_MAX_DIST / max_exact))
    large = max_exact + (np.log(n_safe / max_exact) * scale).astype(np.int32)
    large = np.minimum(large, REL_BUCKETS - 1)
    return np.where(n < max_exact, n, large).astype(np.int32)


def _fill_bias(bias_ref, bucket_ref, rel_ref):
    bucket = bucket_ref[...]
    lq = bucket.shape[0]
    has_prev = lax.broadcasted_iota(jnp.int32, bucket.shape, 1) >= CHUNK
    for hh in range(D_Q_HEADS):
        kv, grp = divmod(hh, D_GROUP)
        acc = jnp.full(bucket.shape, NEG, F32)
        for bkt in range(REL_BUCKETS):
            acc = jnp.where(bucket == bkt, rel_ref[bkt, hh], acc)
        bias_ref[0, grp, kv * lq:(kv + 1) * lq, :] = acc
        bias_ref[1, grp, kv * lq:(kv + 1) * lq, :] = jnp.where(has_prev, acc, NEG)


def _kv_lane_ids(rows):
    return lax.broadcasted_iota(jnp.int32, (rows, D_KV_HEADS * D_HEAD_DIM), 1) // D_HEAD_DIM


def _attn_scores(qg, kk):
    lane_kv = _kv_lane_ids(qg.shape[0])
    zero = jnp.zeros_like(qg)
    lhs = jnp.concatenate([jnp.where(lane_kv == kv, qg, zero) for kv in range(D_KV_HEADS)], axis=0)
    return _dot_nt(lhs, kk)


def _attn_probs(sc, bias_ref, sinks_ref, grp, table):
    lq = sc.shape[0] // D_KV_HEADS
    probs = []
    for kv in range(D_KV_HEADS):
        rs = slice(kv * lq, (kv + 1) * lq)
        s_h = sc[rs] + bias_ref[table, grp, rs, :]
        sink = sinks_ref[kv * D_GROUP + grp]
        m = jnp.maximum(jnp.max(s_h, axis=-1, keepdims=True), sink)
        pexp = jnp.exp(s_h - m)
        denom = jnp.sum(pexp, axis=-1, keepdims=True) + jnp.exp(sink - m)
        probs.append((pexp / denom).astype(BF16))
    return jnp.concatenate(probs, axis=0)


def _attn_out(probs, vv):
    lq = probs.shape[0] // D_KV_HEADS
    lane_kv = _kv_lane_ids(lq)
    ov = _dot(probs, vv)
    out = ov[(D_KV_HEADS - 1) * lq:]
    for kv in range(D_KV_HEADS - 2, -1, -1):
        out = jnp.where(lane_kv == kv, ov[kv * lq:(kv + 1) * lq], out)
    return out


def _odd_prompt_kernel(h_ref, g_ref, wc_ref, wq_ref, wk_ref, wv_ref, wout_ref,
                       linw_ref, cscale_ref, qn_ref, kn_ref, onesbd_ref, bucket_ref,
                       sinks_ref, rel_ref,
                       o_ref, pool_out_ref, k_out_ref, v_out_ref,
                       extc_ref, kprev_ref, vprev_ref, bias_ref, *, tile, width_c):
    b = pl.program_id(0)
    s = pl.program_id(1)
    n_blocks = tile // CHUNK

    @pl.when((b == 0) & (s == 0))
    def _():
        _fill_bias(bias_ref, bucket_ref, rel_ref)

    @pl.when(s == 0)
    def _():
        extc_ref[0:C_HALO, :] = jnp.zeros((C_HALO, width_c), F32)
        kprev_ref[...] = jnp.zeros(kprev_ref.shape, F32)
        vprev_ref[...] = jnp.zeros(vprev_ref.shape, F32)

    x = h_ref[...]
    xn = _rms(x, g_ref[...]).astype(BF16)

    c_in = _dot(xn, wc_ref[...])
    extc_ref[C_HALO:C_HALO + tile, :] = c_in
    e = extc_ref[...]
    tail = extc_ref[tile:tile + C_HALO, :]
    extc_ref[0:C_HALO, :] = tail
    pool_out_ref[...] = tail
    pos = (s * tile + lax.broadcasted_iota(jnp.int32, (tile, 1), 0) + 1).astype(F32)
    gdim = width_c // len(C_WINDOWS)
    run = e
    shift = 1
    yc = []
    for gi, win in enumerate(C_WINDOWS):
        while shift < win:
            run = run + pltpu.roll(run, shift, 0)
            shift *= 2
        cnt = jnp.minimum(pos, float(win))
        pooled = run[C_HALO:, :gdim] / cnt - c_in[:, gi * gdim:(gi + 1) * gdim]
        yc.append(_dot(pooled.astype(BF16), linw_ref[gi]))
        if gi + 1 < len(C_WINDOWS):
            run = run[:, gdim:]
    yc = jnp.concatenate(yc, axis=1) * cscale_ref[...]

    q = _dot(xn, wq_ref[...])
    k = _dot(xn, wk_ref[...])
    v = _dot(xn, wv_ref[...])
    ones_bd = onesbd_ref[...]
    inv_d = 1.0 / D_HEAD_DIM
    qn = q * lax.rsqrt(_head_sumsq(q, ones_bd) * inv_d + EPS) * qn_ref[...]
    kn = k * lax.rsqrt(_head_sumsq(k, ones_bd) * inv_d + EPS) * kn_ref[...]
    qs = (qn * (D_HEAD_DIM ** -0.5)).astype(BF16)
    kb = kn.astype(BF16)
    vb = v.astype(BF16)
    first_table = jnp.where(s == 0, 1, 0)
    gw = D_KV_HEADS * D_HEAD_DIM
    keys, vals = [], []
    for blk in range(n_blocks):
        rs = slice(blk * CHUNK, (blk + 1) * CHUNK)
        if blk == 0:
            k_prev, v_prev = kprev_ref[...].astype(BF16), vprev_ref[...].astype(BF16)
        else:
            k_prev, v_prev = kb[(blk - 1) * CHUNK:blk * CHUNK], vb[(blk - 1) * CHUNK:blk * CHUNK]
        keys.append(jnp.concatenate([k_prev, kb[rs]], axis=0))
        vals.append(jnp.concatenate([v_prev, vb[rs]], axis=0))
    kprev_ref[...] = kn[tile - CHUNK:]
    vprev_ref[...] = v[tile - CHUNK:]
    items = [(blk, grp) for blk in range(n_blocks) for grp in range(D_GROUP)]
    ycb = yc.astype(BF16)
    n_o = o_ref.shape[1] // MXU_DIM
    out_c, sc, pr, og = {}, {}, {}, {}
    for i in range(len(items) + 2):
        if i < len(items):
            blk, grp = items[i]
            sc[i] = _attn_scores(qs[blk * CHUNK:(blk + 1) * CHUNK, grp * gw:(grp + 1) * gw], keys[blk])
        if 0 <= i - 1 < len(items):
            blk, grp = items[i - 1]
            pr[i - 1] = _attn_probs(sc.pop(i - 1), bias_ref, sinks_ref, grp, first_table if blk == 0 else 0)
        if i < n_o:
            out_c[i] = _dot(ycb, wout_ref[0:width_c, i * MXU_DIM:(i + 1) * MXU_DIM])
        if 0 <= i - 2 < len(items):
            blk, grp = items[i - 2]
            og[i - 2] = _attn_out(pr.pop(i - 2), vals[blk])
    yd = jnp.concatenate(
        [jnp.concatenate([og[blk * D_GROUP + grp] for grp in range(D_GROUP)], axis=1) for blk in range(n_blocks)],
        axis=0).astype(BF16)
    for j in range(n_o):
        cs = slice(j * MXU_DIM, (j + 1) * MXU_DIM)
        o_ref[:, cs] = x[:, cs] + out_c[j] + _dot(yd, wout_ref[width_c:, cs])

    @pl.when(s == pl.num_programs(1) - 1)
    def _():
        k_out_ref[...] = kn[tile - CHUNK:]
        v_out_ref[...] = v[tile - CHUNK:]


def _odd_prompt(h, p, tile):
    b, seq, d = h.shape
    width_c = p["wc"].shape[1]
    kvw = p["wk"].shape[1]
    assert seq % tile == 0 and tile % CHUNK == 0
    r = np.arange(CHUNK) + CHUNK
    c = np.arange(2 * CHUNK)
    dist = r[:, None] - c[None, :]
    bucket = np.where((dist >= 0) & (dist < CHUNK), _t5_bucket(dist), -1).astype(np.int32)
    vm = ["linw", "cscale", "qn", "kn", "onesbd"]
    smem = pl.BlockSpec(memory_space=pltpu.SMEM)
    out, pool_tail, k_win, v_win = pl.pallas_call(
        functools.partial(_odd_prompt_kernel, tile=tile, width_c=width_c),
        out_shape=(jax.ShapeDtypeStruct((b, seq, d), F32),
                   jax.ShapeDtypeStruct((b, C_HALO, width_c), F32),
                   jax.ShapeDtypeStruct((b, CHUNK, kvw), F32),
                   jax.ShapeDtypeStruct((b, CHUNK, kvw), F32)),
        grid=(b, seq // tile),
        in_specs=[pl.BlockSpec((None, tile, d), lambda i, j: (i, j, 0)),
                  _resident((1, d))]
        + [_resident(p[k].shape) for k in ("wc", "wq", "wk", "wv", "wout")]
        + [_resident(p[k].shape) for k in vm]
        + [_resident(bucket.shape), smem, smem],
        out_specs=(pl.BlockSpec((None, tile, d), lambda i, j: (i, j, 0)),
                   pl.BlockSpec((None, C_HALO, width_c), lambda i, j: (i, 0, 0)),
                   pl.BlockSpec((None, CHUNK, kvw), lambda i, j: (i, 0, 0)),
                   pl.BlockSpec((None, CHUNK, kvw), lambda i, j: (i, 0, 0))),
        scratch_shapes=[pltpu.VMEM((tile + C_HALO, width_c), F32),
                        pltpu.VMEM((CHUNK, kvw), F32),
                        pltpu.VMEM((CHUNK, kvw), F32),
                        pltpu.VMEM((2, D_GROUP, D_KV_HEADS * CHUNK, 2 * CHUNK), F32)],
        compiler_params=pltpu.CompilerParams(
            dimension_semantics=("arbitrary", "arbitrary"), vmem_limit_bytes=VMEM_LIMIT),
        name="odd_prompt",
    )(h, p["g"], p["wc"], p["wq"], p["wk"], p["wv"], p["wout"], *[p[k] for k in vm],
      jnp.asarray(bucket), p["sinks"], p["rel"])
    return out, pool_tail[:, C_HALO - (max(C_WINDOWS) - 1):, :], k_win, v_win


def _steps(x, n, width):
    return [x[:, t * width:(t + 1) * width] for t in range(n)]


def _stack_steps(ref, n, width):
    x = ref[...]
    return jnp.concatenate(_steps(x, n, width), axis=0)


def _even_sample_front_kernel(hs_ref, g_ref, wa_ref, wz_ref, wxbc_ref, wdt_ref,
                              lng_ref, lnb_ref, wts_ref, bts_ref, convw_ref, convb_ref,
                              dtb_ref, alog_ref, dskip_ref, cs_ref,
                              v_out, ya_out, ypart_out, eacum_out, z_out, conv_out,
                              cgt_out, xdw_out, bs_out, dec_out, *, steps, d, width_a, inner, conv_dim):
    bt = hs_ref.shape[0]
    xn = _rms(_stack_steps(hs_ref, steps, d), g_ref[...]).astype(BF16)
    blk = lambda a, t: a[t * bt:(t + 1) * bt]

    ga = _gelu_tanh(_dot(xn, wa_ref[...]))
    u = ga[:, :width_a]
    v = ga[:, width_a:]
    mu = jnp.mean(v, axis=-1, keepdims=True)
    vc = v - mu
    var = jnp.mean(vc * vc, axis=-1, keepdims=True)
    v = vc * lax.rsqrt(var + EPS) * lng_ref[...] + lnb_ref[...]
    for t in range(steps):
        v_out[:, t * width_a:(t + 1) * width_a] = blk(v, t)
        gate = bts_ref[t:t + 1, :]
        for s in range(t + 1):
            gate = gate + wts_ref[t * steps + s:t * steps + s + 1, :] * blk(v, s)
        ya_out[:, t * width_a:(t + 1) * width_a] = blk(u, t) * gate

    z = _dot(xn, wz_ref[...])
    for t in range(steps):
        z_out[:, t * inner:(t + 1) * inner] = blk(z, t)
    raw = _dot(xn, wxbc_ref[...])
    dt = _softplus(_dot(xn, wdt_ref[...]) + dtb_ref[...])
    ext = _steps(cs_ref[...], B_CONV - 1, conv_dim) + [blk(raw, t) for t in range(steps)]
    for k in range(B_CONV - 1):
        conv_out[:, k * conv_dim:(k + 1) * conv_dim] = ext[len(ext) - (B_CONV - 1) + k]
    gn = B_GROUPS * B_STATE
    n_pairs = B_HEADS // 2
    a_neg = -jnp.exp(alog_ref[...])
    xs, bm, cm, dts, acum = [], [], [], [], []
    for t in range(steps):
        conv = convb_ref[...]
        for tap in range(B_CONV):
            conv = conv + ext[t + tap] * convw_ref[tap:tap + 1, :]
        xbc = _silu(conv)
        xs.append(xbc[:, :inner])
        bm.append(xbc[:, inner:inner + gn])
        cm.append(xbc[:, inner + gn:])
        dts.append(blk(dt, t))
        da = dts[t] * a_neg
        acum.append(da if t == 0 else acum[t - 1] + da)
    lane = lax.broadcasted_iota(jnp.int32, (bt, LANES), 1)
    group0 = lane < (B_HEADS // B_GROUPS)
    dec_out[...] = jnp.exp(acum[steps - 1])
    pad_rows = SUBLANES - steps
    xdw_out[:, steps * inner:] = jnp.zeros((bt, pad_rows * inner), F32)
    bs_out[:, steps * gn:] = jnp.zeros((bt, pad_rows * gn), F32)
    xd = []
    for t in range(steps):
        xd.append(xs[t] * _expand_heads(dts[t], n_pairs))
        eacum_out[:, t * inner:(t + 1) * inner] = _expand_heads(jnp.exp(acum[t]), n_pairs)
        xdw_out[:, t * inner:(t + 1) * inner] = xs[t] * _expand_heads(
            dts[t] * jnp.exp(acum[steps - 1] - acum[t]), n_pairs)
        bs_out[:, t * gn:(t + 1) * gn] = bm[t]
        for g in range(B_GROUPS):
            r = g * steps + t
            cgt_out[:, r * B_STATE:(r + 1) * B_STATE] = cm[t][:, g * B_STATE:(g + 1) * B_STATE]
    for t in range(steps):
        y = xs[t] * dskip_ref[...]
        for s in range(t + 1):
            cb = [jnp.sum(cm[t][:, g * B_STATE:(g + 1) * B_STATE] * bm[s][:, g * B_STATE:(g + 1) * B_STATE],
                          axis=-1, keepdims=True) for g in range(B_GROUPS)]
            coef = jnp.where(group0, cb[0], cb[1]) * jnp.exp(acum[t] - acum[s])
            y = y + _expand_heads(coef, n_pairs) * xd[s]
        ypart_out[:, t * inner:(t + 1) * inner] = y


def _even_sample_state_kernel(s0_ref, cgt_ref, xdw_ref, bs_ref, dec_ref, yoff_ref, snew_ref, *, bb):
    step = pl.program_id(0)
    gw = (B_HEADS // B_GROUPS) * B_HEAD_DIM

    def body(bi, carry):
        s0 = s0_ref[bi]
        c8 = cgt_ref[bi]
        c16 = jnp.concatenate([c8, jnp.zeros_like(c8)], axis=0).astype(BF16)
        yoff_ref[bi] = _dot_nt(c16, s0.astype(BF16))[:SUBLANES]
        x8 = xdw_ref[bi]
        b8 = bs_ref[bi]
        x16 = jnp.concatenate([x8, jnp.zeros_like(x8)], axis=0).astype(BF16)
        b16 = jnp.concatenate([b8, jnp.zeros_like(b8)], axis=0).astype(BF16)
        for g in range(B_GROUPS):
            add = _dot_tn(x16[:, g * gw:(g + 1) * gw], b16[:, g * B_STATE:(g + 1) * B_STATE])
            for hl in range(B_HEADS // B_GROUPS):
                hh = g * (B_HEADS // B_GROUPS) + hl
                rs = slice(hh * B_HEAD_DIM, (hh + 1) * B_HEAD_DIM)
                snew_ref[bi, rs, :] = s0[rs] * dec_ref[step * bb + bi, hh] + \
                    add[hl * B_HEAD_DIM:(hl + 1) * B_HEAD_DIM]
        return carry

    lax.fori_loop(0, bb, body, 0, unroll=2)


def _even_sample_back_kernel(hs_ref, ya_ref, ypart_ref, eacum_ref, z_ref, yoff_ref, normg_ref, wout_ref,
                             o_ref, *, steps, d, inner):
    bt = hs_ref.shape[0]
    half = inner // B_GROUPS
    mixes = []
    for t in range(steps):
        sl = slice(t * inner, (t + 1) * inner)
        yoff = jnp.concatenate(
            [yoff_ref[:, (g * steps + t) * inner + g * half:(g * steps + t) * inner + (g + 1) * half]
             for g in range(B_GROUPS)], axis=1)
        y = (ypart_ref[:, sl] + yoff * eacum_ref[:, sl]) * _silu(z_ref[:, sl])
        yn = []
        for g in range(B_GROUPS):
            yg = y[:, g * half:(g + 1) * half]
            yn.append(yg * lax.rsqrt(jnp.mean(yg * yg, axis=-1, keepdims=True) + EPS))
        yb = jnp.concatenate(yn, axis=1) * normg_ref[...]
        mixes.append(jnp.concatenate([ya_ref[:, t * d:(t + 1) * d], yb], axis=1))
    out = _dot(jnp.concatenate(mixes, axis=0).astype(BF16), wout_ref[...])
    for t in range(steps):
        o_ref[:, t * d:(t + 1) * d] = hs_ref[:, t * d:(t + 1) * d] + out[t * bt:(t + 1) * bt]


def _row_tiled(width, bt):
    return pl.BlockSpec((bt, width), lambda i: (i, 0))


def _even_sample(hs2, state_conv2, state_ssm3, p, steps, bt, bb):
    nb, _ = hs2.shape
    d = p["wa"].shape[0]
    width_a = p["wa"].shape[1] // 2
    inner = p["wz"].shape[1]
    conv_dim = p["wxbc"].shape[1]
    gn = B_GROUPS * B_STATE
    assert nb % bt == 0 and nb % bb == 0 and steps <= SUBLANES
    params = pltpu.CompilerParams(dimension_semantics=("arbitrary",), vmem_limit_bytes=VMEM_LIMIT)
    small = ["lng", "lnb", "wts", "bts", "convw", "convb", "dtb", "alog", "dskip"]
    widths = dict(v=steps * width_a, ya=steps * width_a, ypart=steps * inner, eacum=steps * inner,
                  z=steps * inner, conv=(B_CONV - 1) * conv_dim, cgt=SUBLANES * B_STATE,
                  xdw=SUBLANES * inner, bs=SUBLANES * gn, dec=LANES)
    front = pl.pallas_call(
        functools.partial(_even_sample_front_kernel, steps=steps, d=d, width_a=width_a, inner=inner,
                          conv_dim=conv_dim),
        out_shape=tuple(jax.ShapeDtypeStruct((nb, w), F32) for w in widths.values()),
        grid=(nb // bt,),
        in_specs=[_row_tiled(steps * d, bt), _resident((1, d))]
        + [_resident(p[k].shape) for k in ("wa", "wz", "wxbc", "wdt")]
        + [_resident(p[k].shape) for k in small]
        + [_row_tiled((B_CONV - 1) * conv_dim, bt)],
        out_specs=tuple(_row_tiled(w, bt) for w in widths.values()),
        compiler_params=params,
        name="even_sample_front",
    )(hs2, p["g"], p["wa"], p["wz"], p["wxbc"], p["wdt"], *[p[k] for k in small], state_conv2)
    v_rows, ya, ypart, eacum, z, new_conv, cgt, xdw, bs, dec = front

    hp = state_ssm3.shape[1]
    tile3 = lambda rows, width: pl.BlockSpec((bb, rows, width), lambda i: (i, 0, 0))
    yoff, new_ssm = pl.pallas_call(
        functools.partial(_even_sample_state_kernel, bb=bb),
        out_shape=(jax.ShapeDtypeStruct((nb, SUBLANES, hp), F32),
                   jax.ShapeDtypeStruct(state_ssm3.shape, F32)),
        grid=(nb // bb,),
        in_specs=[tile3(hp, B_STATE), tile3(SUBLANES, B_STATE), tile3(SUBLANES, inner), tile3(SUBLANES, gn),
                  pl.BlockSpec(memory_space=pltpu.SMEM)],
        out_specs=(tile3(SUBLANES, hp), tile3(hp, B_STATE)),
        compiler_params=params,
        name="even_sample_state",
    )(state_ssm3, cgt.reshape(nb, SUBLANES, B_STATE), xdw.reshape(nb, SUBLANES, inner),
      bs.reshape(nb, SUBLANES, gn), dec[:, :B_HEADS])

    out = pl.pallas_call(
        functools.partial(_even_sample_back_kernel, steps=steps, d=d, inner=inner),
        out_shape=jax.ShapeDtypeStruct(hs2.shape, F32),
        grid=(nb // bt,),
        in_specs=[_row_tiled(steps * d, bt), _row_tiled(steps * width_a, bt), _row_tiled(steps * inner, bt),
                  _row_tiled(steps * inner, bt), _row_tiled(steps * inner, bt), _row_tiled(SUBLANES * hp, bt),
                  _resident((1, inner)), _resident(p["wout"].shape)],
        out_specs=_row_tiled(steps * d, bt),
        compiler_params=params,
        name="even_sample_back",
    )(hs2, ya, ypart, eacum, z, yoff.reshape(nb, SUBLANES * hp), p["normg"], p["wout"])
    return out, v_rows, new_conv, new_ssm


def _odd_sample_front_kernel(hs_ref, g_ref, wc_ref, wq_ref, wk_ref, wv_ref, linw_ref, cscale_ref,
                             qn_ref, kn_ref, onesbd_ref, ps_ref,
                             yc_out, pool_out, q_out, knew_out, vnew_out, *, steps, d, width_c, past_len):
    bt = hs_ref.shape[0]
    xn = _rms(_stack_steps(hs_ref, steps, d), g_ref[...]).astype(BF16)
    blk = lambda a, t: a[t * bt:(t + 1) * bt]
    c_in = _dot(xn, wc_ref[...])
    n_state = max(C_WINDOWS) - 1
    ext = _steps(ps_ref[...], n_state, width_c) + [blk(c_in, t) for t in range(steps)]
    for j in range(n_state):
        pool_out[:, j * width_c:(j + 1) * width_c] = ext[len(ext) - n_state + j]
    gdim = width_c // len(C_WINDOWS)
    yc_cols = []
    for gi, win in enumerate(C_WINDOWS):
        sl = slice(gi * gdim, (gi + 1) * gdim)
        pooled = []
        for t in range(steps):
            hi = n_state + t
            lo = max(hi - win + 1, 0)
            acc = ext[lo][:, sl]
            for j in range(lo + 1, hi + 1):
                acc = acc + ext[j][:, sl]
            count = float(min(past_len + t + 1, win))
            pooled.append(acc / count - ext[hi][:, sl])
        yc_cols.append(_dot(jnp.concatenate(pooled, axis=0).astype(BF16), linw_ref[gi]))
    yc = jnp.concatenate(yc_cols, axis=1) * cscale_ref[...]
    for t in range(steps):
        yc_out[:, t * width_c:(t + 1) * width_c] = blk(yc, t)

    q = _dot(xn, wq_ref[...])
    k = _dot(xn, wk_ref[...])
    v = _dot(xn, wv_ref[...])
    ones_bd = onesbd_ref[...]
    inv_d = 1.0 / D_HEAD_DIM
    qn = q * lax.rsqrt(_head_sumsq(q, ones_bd) * inv_d + EPS) * qn_ref[...] * (D_HEAD_DIM ** -0.5)
    kn = k * lax.rsqrt(_head_sumsq(k, ones_bd) * inv_d + EPS) * kn_ref[...]
    qw = q.shape[1]
    kw = k.shape[1]
    pad = SUBLANES - steps
    q_out[:, steps * qw:] = jnp.zeros((bt, pad * qw), F32)
    knew_out[:, :pad * kw] = jnp.zeros((bt, pad * kw), F32)
    vnew_out[:, :pad * kw] = jnp.zeros((bt, pad * kw), F32)
    for t in range(steps):
        q_out[:, t * qw:(t + 1) * qw] = blk(qn, t)
        knew_out[:, (pad + t) * kw:(pad + t + 1) * kw] = blk(kn, t)
        vnew_out[:, (pad + t) * kw:(pad + t + 1) * kw] = blk(v, t)


SINK_BUCKET = REL_BUCKETS


def _odd_sample_attn_kernel(q_ref, knew_ref, vnew_ref, ck_ref, cv_ref, bucket_ref, sinks_ref, rel_ref,
                            o_ref, kout_ref, vout_ref, bias_ref, *, bb, steps, n_keys):
    win = ck_ref.shape[1]
    kvw = ck_ref.shape[2]
    tile16 = 2 * SUBLANES

    @pl.when(pl.program_id(0) == 0)
    def _():
        bucket = bucket_ref[...]
        for hh in range(D_Q_HEADS):
            acc = jnp.full(bucket.shape, NEG, F32)
            for bkt in range(REL_BUCKETS):
                acc = jnp.where(bucket == bkt, rel_ref[bkt, hh], acc)
            acc = jnp.where(bucket == SINK_BUCKET, sinks_ref[hh], acc)
            bias_ref[hh * SUBLANES:(hh + 1) * SUBLANES, :] = acc

    sub = lax.broadcasted_iota(jnp.int32, (SUBLANES, kvw), 0)
    new_rows = sub >= SUBLANES - steps
    lane_kv = lax.broadcasted_iota(jnp.int32, (SUBLANES, kvw), 1) // D_HEAD_DIM
    gw = D_KV_HEADS * D_HEAD_DIM
    zero_keys = jnp.zeros((n_keys - win - tile16, kvw), BF16)

    def extend(cache, new8):
        new16 = jnp.concatenate([new8, jnp.zeros_like(new8)], axis=0).astype(BF16)
        return jnp.concatenate([cache.astype(BF16), new16, zero_keys], axis=0)

    def shift_in(cache, new8, out_ref, bi):
        rolled = pltpu.roll(cache, win - steps, 0)
        out_ref[bi, 0:win - SUBLANES, :] = rolled[:win - SUBLANES]
        out_ref[bi, win - SUBLANES:, :] = jnp.where(new_rows, new8, rolled[win - SUBLANES:])

    def body(bi, carry):
        ck = ck_ref[bi]
        cv = cv_ref[bi]
        k8 = knew_ref[bi]
        v8 = vnew_ref[bi]
        shift_in(ck, k8, kout_ref, bi)
        shift_in(cv, v8, vout_ref, bi)
        q8 = q_ref[bi]
        pieces = []
        for kv in range(D_KV_HEADS):
            for grp in range(D_GROUP):
                qg = q8[:, grp * gw:(grp + 1) * gw]
                pieces.append(jnp.where(lane_kv == kv, qg, 0.0))
        lhs = jnp.concatenate(pieces, axis=0).astype(BF16)
        sc = _dot_nt(lhs, extend(ck, k8)) + bias_ref[...]
        m = jnp.max(sc, axis=-1, keepdims=True)
        pexp = jnp.exp(sc - m)
        probs = (pexp / jnp.sum(pexp, axis=-1, keepdims=True)).astype(BF16)
        ov = _dot(probs, extend(cv, v8))
        outs = []
        for grp in range(D_GROUP):
            r_last = ((D_KV_HEADS - 1) * D_GROUP + grp) * SUBLANES
            acc = ov[r_last:r_last + SUBLANES]
            for kv in range(D_KV_HEADS - 2, -1, -1):
                r0 = (kv * D_GROUP + grp) * SUBLANES
                acc = jnp.where(lane_kv == kv, ov[r0:r0 + SUBLANES], acc)
            outs.append(acc)
        o_ref[bi] = jnp.concatenate(outs, axis=1)
        return carry

    lax.fori_loop(0, bb, body, 0, unroll=SAMPLE_ATTN_UNROLL)


def _odd_sample_back_kernel(hs_ref, yc_ref, o_ref_in, wout_ref, out_ref, *, steps, d, width_c, qw):
    bt = hs_ref.shape[0]
    mix = jnp.concatenate(
        [jnp.concatenate([yc_ref[:, t * width_c:(t + 1) * width_c], o_ref_in[:, t * qw:(t + 1) * qw]], axis=1)
         for t in range(steps)], axis=0).astype(BF16)
    out = _dot(mix, wout_ref[...])
    for t in range(steps):
        out_ref[:, t * d:(t + 1) * d] = hs_ref[:, t * d:(t + 1) * d] + out[t * bt:(t + 1) * bt]


def _odd_sample(hs2, state_pool2, cache_k3, cache_v3, p, steps, past_len, bt, bb):
    nb, _ = hs2.shape
    d = p["wc"].shape[0]
    width_c = p["wc"].shape[1]
    qw = p["wq"].shape[1]
    kw = p["wk"].shape[1]
    win = cache_k3.shape[1]
    n_state = max(C_WINDOWS) - 1
    assert nb % bt == 0 and nb % bb == 0 and steps <= SUBLANES and win == CHUNK
    params = pltpu.CompilerParams(dimension_semantics=("arbitrary",), vmem_limit_bytes=VMEM_LIMIT)
    vm = ["linw", "cscale", "qn", "kn", "onesbd"]
    widths = dict(yc=steps * width_c, pool=n_state * width_c, q=SUBLANES * qw, knew=SUBLANES * kw,
                  vnew=SUBLANES * kw)
    yc, new_pool, q8, knew8, vnew8 = pl.pallas_call(
        functools.partial(_odd_sample_front_kernel, steps=steps, d=d, width_c=width_c, past_len=past_len),
        out_shape=tuple(jax.ShapeDtypeStruct((nb, w), F32) for w in widths.values()),
        grid=(nb // bt,),
        in_specs=[_row_tiled(steps * d, bt), _resident((1, d))]
        + [_resident(p[k].shape) for k in ("wc", "wq", "wk", "wv")]
        + [_resident(p[k].shape) for k in vm]
        + [_row_tiled(n_state * width_c, bt)],
        out_specs=tuple(_row_tiled(w, bt) for w in widths.values()),
        compiler_params=params,
        name="odd_sample_front",
    )(hs2, p["g"], p["wc"], p["wq"], p["wk"], p["wv"], *[p[k] for k in vm], state_pool2)

    n_keys = 2 * CHUNK
    pad = SUBLANES - steps
    bucket = np.full((SUBLANES, n_keys), -1, np.int32)
    for t in range(steps):
        q_pos = past_len + t
        k_pos = np.full(n_keys, -10 ** 9, np.int64)
        k_pos[:win] = past_len - win + np.arange(win)
        k_pos[win + pad:win + SUBLANES] = past_len + np.arange(steps)
        dist = q_pos - k_pos
        ok = (dist >= 0) & (dist < CHUNK) & (k_pos >= 0)
        bucket[t] = np.where(ok, _t5_bucket(np.where(ok, dist, 0)), -1)
    bucket[:, n_keys - 1] = SINK_BUCKET
    smem = pl.BlockSpec(memory_space=pltpu.SMEM)
    tile3 = lambda rows, width: pl.BlockSpec((bb, rows, width), lambda i: (i, 0, 0))
    o8, new_k, new_v = pl.pallas_call(
        functools.partial(_odd_sample_attn_kernel, bb=bb, steps=steps, n_keys=n_keys),
        out_shape=(jax.ShapeDtypeStruct((nb, SUBLANES, qw), F32),
                   jax.ShapeDtypeStruct(cache_k3.shape, F32),
                   jax.ShapeDtypeStruct(cache_v3.shape, F32)),
        grid=(nb // bb,),
        in_specs=[tile3(SUBLANES, qw), tile3(SUBLANES, kw), tile3(SUBLANES, kw), tile3(win, kw), tile3(win, kw),
                  _resident(bucket.shape), smem, smem],
        out_specs=(tile3(SUBLANES, qw), tile3(win, kw), tile3(win, kw)),
        scratch_shapes=[pltpu.VMEM((D_Q_HEADS * SUBLANES, n_keys), F32)],
        compiler_params=params,
        name="odd_sample_attn",
    )(q8.reshape(nb, SUBLANES, qw), knew8.reshape(nb, SUBLANES, kw), vnew8.reshape(nb, SUBLANES, kw),
      cache_k3, cache_v3, jnp.asarray(bucket), p["sinks"], p["rel"])

    out = pl.pallas_call(
        functools.partial(_odd_sample_back_kernel, steps=steps, d=d, width_c=width_c, qw=qw),
        out_shape=jax.ShapeDtypeStruct(hs2.shape, F32),
        grid=(nb // bt,),
        in_specs=[_row_tiled(steps * d, bt), _row_tiled(steps * width_c, bt), _row_tiled(SUBLANES * qw, bt),
                  _resident(p["wout"].shape)],
        out_specs=_row_tiled(steps * d, bt),
        compiler_params=params,
        name="odd_sample_back",
    )(hs2, yc, o8.reshape(nb, SUBLANES * qw), p["wout"])
    return out, new_pool, new_k, new_v


def _row(v):
    return v.reshape(1, -1).astype(F32)


def _pad_lanes(m, width=LANES):
    return jnp.pad(m, ((0, 0), (0, width - m.shape[1])))


def _head_expand_matrix():
    e = np.zeros((LANES, B_HEADS * B_HEAD_DIM), np.float32)
    for hh in range(B_HEADS):
        e[hh, hh * B_HEAD_DIM:(hh + 1) * B_HEAD_DIM] = 1.0
    return np.concatenate([e, e], axis=0)


def _column_select_matrix():
    e = np.zeros((LANES, B_HEADS * CHUNK), np.float32)
    for hh in range(B_HEADS):
        e[hh, hh * CHUNK:(hh + 1) * CHUNK] = 1.0
    return np.concatenate([e, e], axis=0)


def _prep_even(mix_norm, w_in, w_out, ln_g, ln_b, w_s, b_s, conv_w, conv_b, dt_bias, a_log, d_skip, norm_g):
    width_a = ln_g.shape[0]
    inner = norm_g.shape[0]
    conv_dim = conv_b.shape[0]
    o1 = 2 * width_a
    o2 = o1 + inner
    o3 = o2 + conv_dim
    return dict(
        g=_row(mix_norm),
        wa=w_in[:, :o1].astype(BF16),
        wz=w_in[:, o1:o2].astype(BF16),
        wxbc=w_in[:, o2:o3].astype(BF16),
        wdt=_pad_lanes(w_in[:, o3:]).astype(BF16),
        wout=w_out.astype(BF16),
        lng=_row(ln_g), lnb=_row(ln_b), ws=w_s,
        bsb=jnp.broadcast_to(b_s[:, :, None], b_s.shape + (width_a // A_HEADS,)),
        convw=conv_w, convb=_row(conv_b),
        dtb=_pad_lanes(_row(dt_bias)), alog=_pad_lanes(_row(a_log)),
        dskip=_row(jnp.repeat(d_skip, B_HEAD_DIM)), normg=_row(norm_g),
        expand=jnp.asarray(_head_expand_matrix(), BF16), colsel=jnp.asarray(_column_select_matrix(), BF16),
    )


def _prep_odd(mix_norm, w_in, w_out, lin_w, c_scale, q_norm, k_norm, sinks, rel_table):
    d = w_in.shape[0]
    width_c = c_scale.shape[0]
    qw = D_Q_HEADS * D_HEAD_DIM
    kw = D_KV_HEADS * D_HEAD_DIM
    wq = w_in[:, width_c:width_c + qw].reshape(d, D_KV_HEADS, D_GROUP, D_HEAD_DIM)
    wq = wq.transpose(0, 2, 1, 3).reshape(d, qw)
    wo_d = w_out[width_c:].reshape(D_KV_HEADS, D_GROUP, D_HEAD_DIM, -1).transpose(1, 0, 2, 3).reshape(qw, -1)
    ones_bd = np.kron(np.eye(256 // D_HEAD_DIM), np.ones((D_HEAD_DIM, D_HEAD_DIM))).astype(np.float32)
    return dict(
        g=_row(mix_norm),
        wc=w_in[:, :width_c].astype(BF16),
        wq=wq.astype(BF16),
        wk=w_in[:, width_c + qw:width_c + qw + kw].astype(BF16),
        wv=w_in[:, width_c + qw + kw:].astype(BF16),
        wout=jnp.concatenate([w_out[:width_c], wo_d], axis=0).astype(BF16),
        linw=lin_w.astype(BF16), cscale=_row(c_scale),
        qn=_row(jnp.tile(q_norm, D_Q_HEADS)), kn=_row(jnp.tile(k_norm, D_KV_HEADS)),
        onesbd=jnp.asarray(ones_bd, BF16),
        sinks=sinks.astype(F32), rel=rel_table.astype(F32),
    )


def _prep_even_sample(w_s, b_s, steps):
    head_w = CHUNK
    w = jnp.transpose(w_s[:, :steps, :steps], (1, 2, 0)).reshape(steps * steps, A_HEADS)
    b = b_s[:, :steps].T
    return dict(wts=jnp.repeat(w, head_w, axis=1), bts=jnp.repeat(b, head_w, axis=1))


PAST_LEN = 16384
FFN_TILE = 1024
FFN_STAGE_SLOTS = 8
FFN_STAGE_ROWS_GU = 32
FFN_STAGE_ROWS_D = 128
FFN_ROW_BLOCK = 256
FFN_BLOCK_TILES = 3
MIXER_TILE = 512
SAMPLE_ROW_TILE = 32
SAMPLE_SEQ_TILE = 8
SAMPLE_ATTN_UNROLL = 4


def kernel(x_prompt, x_sample, state_ssm, state_conv, state_pool, cache_k_win, cache_v_win,
           ffn1_norm, ffn1_w_gu, ffn1_w_down, mix_norm, ffn2_norm, ffn2_w_gu, ffn2_w_down,
           ev_w_in, ev_w_out, a_ln_g, a_ln_b, a_w_s, a_b_s, b_conv_w, b_conv_b, b_dt_bias, b_a_log,
           b_d_skip, b_norm_g, od_w_in, od_w_out, c_lin_w, c_scale, d_q_norm, d_k_norm, d_sinks,
           rel_bias_table):
    bp, seq, d = x_prompt.shape
    bs, steps, _ = x_sample.shape
    past_len = PAST_LEN
    hp = x_prompt
    hs = x_sample
    depth = ffn1_norm.shape[0]
    names = ("a_v_s", "ssm_p", "ssm_s", "conv_p", "conv_s", "pool_p", "pool_s", "k_p", "k_s", "v_p", "v_s")
    outs = {k: [] for k in names}

    def macaron(h_p, h_s, norm, w_gu_all, w_down_all, layer):
        o_p, o_s = _ffn(h_p.reshape(bp * seq, d), h_s.reshape(bs * steps, d), _row(norm), w_gu_all, w_down_all,
                        layer, FFN_TILE)
        return o_p.reshape(bp, seq, d), o_s.reshape(bs, steps, d)

    for layer in range(depth):
        i = layer // 2
        hp, hs = macaron(hp, hs, ffn1_norm[layer], ffn1_w_gu, ffn1_w_down, layer)
        hs2 = hs.reshape(bs, steps * d)
        if layer % 2 == 0:
            p = _prep_even(mix_norm[layer], ev_w_in[i], ev_w_out[i], a_ln_g[i], a_ln_b[i], a_w_s[i], a_b_s[i],
                           b_conv_w[i], b_conv_b[i], b_dt_bias[i], b_a_log[i], b_d_skip[i], b_norm_g[i])
            p.update(_prep_even_sample(a_w_s[i], a_b_s[i], steps))
            hp, conv_p, ssm_p = _even_prompt(hp, p, MIXER_TILE)
            hs2, v_rows, conv_s, ssm_s = _even_sample(
                hs2, state_conv[i].reshape(bs, -1), state_ssm[i].reshape(bs, B_HEADS * B_HEAD_DIM, B_STATE),
                p, steps, SAMPLE_ROW_TILE, SAMPLE_SEQ_TILE)
            outs["a_v_s"].append(v_rows.reshape(bs, steps, -1))
            outs["conv_p"].append(conv_p)
            outs["conv_s"].append(conv_s.reshape(state_conv[i].shape))
            outs["ssm_p"].append(ssm_p.reshape(bp, B_HEADS, B_HEAD_DIM, B_STATE))
            outs["ssm_s"].append(ssm_s.reshape(state_ssm[i].shape))
        else:
            p = _prep_odd(mix_norm[layer], od_w_in[i], od_w_out[i], c_lin_w[i], c_scale[i], d_q_norm[i],
                          d_k_norm[i], d_sinks[i], rel_bias_table)
            hp, pool_p, k_p, v_p = _odd_prompt(hp, p, MIXER_TILE)
            kv_shape = cache_k_win[i].shape
            hs2, pool_s, k_s, v_s = _odd_sample(
                hs2, state_pool[i].reshape(bs, -1), cache_k_win[i].reshape(bs, kv_shape[1], -1),
                cache_v_win[i].reshape(bs, kv_shape[1], -1), p, steps, past_len, SAMPLE_ROW_TILE, SAMPLE_SEQ_TILE)
            outs["pool_p"].append(pool_p)
            outs["pool_s"].append(pool_s.reshape(state_pool[i].shape))
            outs["k_p"].append(k_p.reshape(bp, CHUNK, D_KV_HEADS, D_HEAD_DIM))
            outs["v_p"].append(v_p.reshape(bp, CHUNK, D_KV_HEADS, D_HEAD_DIM))
            outs["k_s"].append(k_s.reshape(kv_shape))
            outs["v_s"].append(v_s.reshape(kv_shape))
        hs = hs2.reshape(bs, steps, d)
        hp, hs = macaron(hp, hs, ffn2_norm[layer], ffn2_w_gu, ffn2_w_down, layer)
    return (hp, hs) + tuple(jnp.stack(outs[k]) for k in names)
```

```python
import functools
import math

import numpy as np
import jax
import jax.numpy as jnp
from jax import lax
from jax.experimental import pallas as pl
from jax.experimental.pallas import tpu as pltpu

F32 = jnp.float32
BF16 = jnp.bfloat16

EPS = 1e-6
NEG = -1e30

LANES = 128
SUBLANES = 8
MXU_DIM = 256
VMEM_BYTES_V7X = 64 * 1024 * 1024
VMEM_LIMIT = VMEM_BYTES_V7X - 8 * 1024 * 1024
A_HEADS = 8
B_HEADS = 16
B_HEAD_DIM = 64
B_GROUPS = 2
B_STATE = 128
B_CONV = 4
CHUNK = 128
C_WINDOWS = (2, 4, 8, 16)
C_HALO = 16
D_Q_HEADS = 16
D_KV_HEADS = 4
D_HEAD_DIM = 64
D_GROUP = D_Q_HEADS // D_KV_HEADS
REL_BUCKETS = 32
REL_MAX_DIST = 128


def _rms(x, g):
    ms = jnp.mean(x * x, axis=-1, keepdims=True)
    return x * lax.rsqrt(ms + EPS) * g


def _sigmoid(x):
    return 1.0 / (1.0 + jnp.exp(-x))


def _silu(x):
    return x * _sigmoid(x)


def _gelu_tanh(x):
    c = math.sqrt(2.0 / math.pi)
    return x * (0.5 * (1.0 + jnp.tanh(c * (x + 0.044715 * (x * x * x)))))


def _softplus(x):
    return jnp.maximum(x, 0.0) + jnp.log1p(jnp.exp(-jnp.abs(x)))


def _split3(x):
    hi = x.astype(BF16)
    r1 = x - hi.astype(F32)
    mid = r1.astype(BF16)
    lo = (r1 - mid.astype(F32)).astype(BF16)
    return hi, mid, lo


def _split2_lanes(x):
    hi = x.astype(BF16)
    lo = (x - hi.astype(F32)).astype(BF16)
    return jnp.concatenate([hi, lo], axis=1)


def _dot(a, b):
    return jnp.dot(a, b, preferred_element_type=F32)


def _dot_nt(a, b):
    return lax.dot_general(a, b, (((1,), (1,)), ((), ())), preferred_element_type=F32)


def _dot_tn(a, b):
    return lax.dot_general(a, b, (((0,), (0,)), ((), ())), preferred_element_type=F32)


def _expand_heads(m, n_pairs):
    rows = m.shape[0]
    lane = lax.broadcasted_iota(jnp.int32, (rows, LANES), 1)
    first = lane < B_HEAD_DIM
    parts = []
    for p in range(n_pairs):
        a = jnp.broadcast_to(m[:, 2 * p:2 * p + 1], (rows, LANES))
        b = jnp.broadcast_to(m[:, 2 * p + 1:2 * p + 2], (rows, LANES))
        parts.append(jnp.where(first, a, b))
    return jnp.concatenate(parts, axis=1)


def _head_sumsq(x, ones_bd):
    xx = (x * x).astype(BF16)
    blk = ones_bd.shape[0]
    outs = [_dot(xx[:, c * blk:(c + 1) * blk], ones_bd) for c in range(x.shape[1] // blk)]
    return jnp.concatenate(outs, axis=1) if len(outs) > 1 else outs[0]


def _ff_blocks(d_ff):
    step = FFN_BLOCK_TILES * MXU_DIM
    return [(c0, min(c0 + step, d_ff)) for c0 in range(0, d_ff, step)]


def _load_cast_rows(src_hbm, dst_ref, stage_ref, sem_ref):
    slots, rows, _ = stage_ref.shape
    n = src_hbm.shape[0] // rows

    def copy(c):
        slot = c % slots
        return pltpu.make_async_copy(src_hbm.at[pl.ds(c * rows, rows), :], stage_ref.at[slot], sem_ref.at[slot])

    for c in range(min(slots, n)):
        copy(c).start()
    for c in range(n):
        copy(c).wait()
        dst_ref[c * rows:(c + 1) * rows, :] = stage_ref[c % slots].astype(BF16)
        if c + slots < n:
            copy(c + slots).start()


def _ffn_rows(x_ref, g_ref, wgu_ref, wd_ref, o_ref, d_ff):
    tm = x_ref.shape[0]
    rows = [slice(r0, min(r0 + FFN_ROW_BLOCK, tm)) for r0 in range(0, tm, FFN_ROW_BLOCK)]
    xns = [_rms(x_ref[rs, :], g_ref[...]).astype(BF16) for rs in rows]
    for rs, xn in zip(rows, xns):
        y = None
        for c0, c1 in _ff_blocks(d_ff):
            gate = _dot(xn, wgu_ref[:, c0:c1])
            up = _dot(xn, wgu_ref[:, d_ff + c0:d_ff + c1])
            act = (_silu(gate) * up).astype(BF16)
            part = _dot(act, wd_ref[c0:c1, :])
            y = part if y is None else y + part
        o_ref[rs, :] = x_ref[rs, :] + 0.5 * y


def _ffn_kernel(xp_ref, xs_ref, g_ref, wgu_hbm, wd_hbm, op_ref, os_ref,
                wgu_ref, wd_ref, stage_gu_ref, stage_d_ref, sem_ref, *, d_ff, prompt_steps, layer):
    i = pl.program_id(0)

    @pl.when(i == 0)
    def _():
        _load_cast_rows(wgu_hbm.at[layer], wgu_ref, stage_gu_ref, sem_ref)
        _load_cast_rows(wd_hbm.at[layer], wd_ref, stage_d_ref, sem_ref)

    @pl.when(i < prompt_steps)
    def _():
        _ffn_rows(xp_ref, g_ref, wgu_ref, wd_ref, op_ref, d_ff)

    @pl.when(i == prompt_steps)
    def _():
        _ffn_rows(xs_ref, g_ref, wgu_ref, wd_ref, os_ref, d_ff)


def _resident(shape):
    nd = len(shape)
    return pl.BlockSpec(shape, lambda *_: (0,) * nd, pipeline_mode=pl.Buffered(1))


def _ffn(xp2d, xs2d, g, wgu_all, wd_all, layer, tm):
    m, d = xp2d.shape
    ms = xs2d.shape[0]
    wgu_shape, wd_shape = wgu_all.shape[1:], wd_all.shape[1:]
    d_ff = wd_shape[0]
    assert m % tm == 0 and d % FFN_STAGE_ROWS_GU == 0 and d_ff % FFN_STAGE_ROWS_D == 0
    steps = m // tm
    last = steps - 1
    whole = lambda shape: pl.BlockSpec(shape, lambda i: (0, 0))
    prompt_tile = pl.BlockSpec((tm, d), lambda i: (jnp.minimum(i, last), 0))
    hbm = pl.BlockSpec(memory_space=pl.ANY)
    return pl.pallas_call(
        functools.partial(_ffn_kernel, d_ff=d_ff, prompt_steps=steps, layer=layer),
        out_shape=(jax.ShapeDtypeStruct((m, d), F32), jax.ShapeDtypeStruct((ms, d), F32)),
        grid=(steps + 1,),
        in_specs=[prompt_tile, whole((ms, d)), _resident((1, d)), hbm, hbm],
        out_specs=(prompt_tile, whole((ms, d))),
        scratch_shapes=[pltpu.VMEM(wgu_shape, BF16), pltpu.VMEM(wd_shape, BF16),
                        pltpu.VMEM((FFN_STAGE_SLOTS, FFN_STAGE_ROWS_GU, wgu_shape[1]), F32),
                        pltpu.VMEM((FFN_STAGE_SLOTS, FFN_STAGE_ROWS_D, wd_shape[1]), F32),
                        pltpu.SemaphoreType.DMA((FFN_STAGE_SLOTS,))],
        compiler_params=pltpu.CompilerParams(
            dimension_semantics=("arbitrary",), vmem_limit_bytes=VMEM_LIMIT),
        name="ffn",
    )(xp2d, xs2d, g, wgu_all, wd_all)


def _zip_stages(first, second, lag=1):
    for i in range(max(len(first), len(second) + lag)):
        if i < len(first):
            first[i]()
        if 0 <= i - lag < len(second):
            second[i - lag]()


def _even_prompt_kernel(h_ref, g_ref, wa_ref, wz_ref, wxbc_ref, wdt_ref, wout_ref,
                        lng_ref, lnb_ref, ws_ref, bsb_ref, convw_ref, convb_ref,
                        dtb_ref, alog_ref, dskip_ref, normg_ref, expand_ref,
                        o_ref, conv_out_ref, ssm_out_ref,
                        ext_ref, st_ref, *, tile, width_a, inner):
    s = pl.program_id(1)
    n_chunks = tile // CHUNK
    halo = SUBLANES

    @pl.when(s == 0)
    def _():
        ext_ref[0:halo, :] = jnp.zeros((halo, ext_ref.shape[1]), F32)
        st_ref[...] = jnp.zeros(st_ref.shape, F32)

    x = h_ref[...]
    xn = _rms(x, g_ref[...]).astype(BF16)

    row = lax.broadcasted_iota(jnp.int32, (CHUNK, CHUNK), 0)
    col = lax.broadcasted_iota(jnp.int32, (CHUNK, CHUNK), 1)
    causal = row >= col
    lane = lax.broadcasted_iota(jnp.int32, (CHUNK, LANES), 1)
    first_half = lane < B_HEAD_DIM

    blk = MXU_DIM
    col_blocks = lambda width: [slice(j * blk, (j + 1) * blk) for j in range(width // blk)]
    head_w = width_a // A_HEADS
    gn = B_GROUPS * B_STATE
    conv_dim = inner + 2 * gn
    n_pairs = B_HEADS // 2
    heads_per_group = B_HEADS // B_GROUPS
    gw = heads_per_group * B_HEAD_DIM
    chunks = [slice(c * CHUNK, (c + 1) * CHUNK) for c in range(n_chunks)]

    ga = {}
    a_cols = [slice(width_a + sl.start, width_a + sl.stop) for sl in col_blocks(width_a)] + col_blocks(width_a)
    pa = {}

    def proj_a(j):
        pa[j] = _dot(xn, wa_ref[:, a_cols[j]])

    def act_a(j):
        ga[j] = _gelu_tanh(pa.pop(j))

    n_a = len(a_cols)
    _zip_stages([functools.partial(proj_a, j) for j in range(n_a)],
                [functools.partial(act_a, j) for j in range(n_a)])
    nv = n_a // 2
    v = jnp.concatenate([ga[j] for j in range(nv)], axis=1)
    u_blocks = [ga[j] for j in range(nv, n_a)]

    raw = {}
    xbc_blocks = {}
    vb_box = []

    def proj_xbc(k):
        raw[k] = _dot(xn, wxbc_ref[:, k * blk:(k + 1) * blk])

    def layer_norm_v():
        mu = jnp.mean(v, axis=-1, keepdims=True)
        vc = v - mu
        var = jnp.mean(vc * vc, axis=-1, keepdims=True)
        vb_box.append((vc * lax.rsqrt(var + EPS) * lng_ref[...] + lnb_ref[...]).astype(BF16))

    def conv_block(k):
        cs = slice(k * blk, (k + 1) * blk)
        ext_ref[halo:halo + tile, cs] = raw.pop(k)
        ext = ext_ref[:, cs]
        ext1 = pltpu.roll(ext, 1, 0)
        pair = ext * convw_ref[1:2, cs] + ext1 * convw_ref[0:1, cs]
        conv = (convb_ref[:, cs] + ext * convw_ref[3:4, cs] + ext1 * convw_ref[2:3, cs]
                + pltpu.roll(pair, 2, 0))[halo:]
        tail = ext_ref[tile:tile + halo, cs]
        ext_ref[0:halo, cs] = tail
        conv_out_ref[:, cs] = tail
        xbc_blocks[k] = _silu(conv)

    n_x = conv_dim // blk
    _zip_stages([functools.partial(proj_xbc, k) for k in range(n_x)],
                [layer_norm_v] + [functools.partial(conv_block, k) for k in range(n_x)])
    vb = vb_box[0]

    z_blocks = {}
    gate_cols = {}

    def proj_z(j):
        z_blocks[j] = _dot(xn, wz_ref[:, j * blk:(j + 1) * blk])

    def mix_head(hh):
        w = jnp.where(causal, ws_ref[hh], 0.0).astype(BF16)
        rhs = jnp.concatenate([vb[rs, hh * head_w:(hh + 1) * head_w] for rs in chunks], axis=1)
        out = _dot(w, rhs)
        bias = bsb_ref[hh]
        gate_cols[hh] = jnp.concatenate(
            [out[:, c * head_w:(c + 1) * head_w] + bias for c in range(n_chunks)], axis=0)

    dt_raw = _dot(xn, wdt_ref[...])
    n_z = inner // blk
    heads_per_z = A_HEADS // n_z
    for j in range(n_z):
        proj_z(j)
        for hh in range(j * heads_per_z, (j + 1) * heads_per_z):
            mix_head(hh)
    heads_per_blk = blk // head_w
    ya_blocks = [(u_blocks[j] * jnp.concatenate(
        [gate_cols[j * heads_per_blk + i] for i in range(heads_per_blk)], axis=1)).astype(BF16)
        for j in range(len(u_blocks))]
    ya = jnp.concatenate(ya_blocks, axis=1)
    xs = jnp.concatenate([xbc_blocks[k] for k in range(inner // blk)], axis=1)
    bm = xbc_blocks[inner // blk]
    cm = xbc_blocks[inner // blk + 1]

    dt = _softplus(dt_raw + dtb_ref[...])
    a_neg = -jnp.exp(alog_ref[...])
    da = dt * a_neg
    tril_ones = jnp.where(causal, 1.0, 0.0).astype(BF16)
    acums = []
    for rs in chunks:
        d_hi, d_mid, d_lo = _split3(da[rs])
        acums.append(_dot(tril_ones, d_hi) + _dot(tril_ones, d_mid) + _dot(tril_ones, d_lo))
    acum = jnp.concatenate(acums, axis=0)
    decay = jnp.concatenate([jnp.exp(a[CHUNK - 1:CHUNK, :] - a) for a in acums], axis=0)
    expand = expand_ref[...]
    xd = xs * _dot(_split2_lanes(dt), expand)
    xdwb = (xs * _dot(_split2_lanes(dt * decay), expand)).astype(BF16)
    e_acum = _dot(_split2_lanes(jnp.exp(acum)), expand)
    bmb = bm.astype(BF16)
    cmb = cm.astype(BF16)

    y_rows = [None] * n_chunks
    out_a = {}

    def scan_chunk(c):
        rs = chunks[c]
        a_c = acums[c]
        acum_t = a_c.T
        cb = [_dot_nt(cmb[rs, g * B_STATE:(g + 1) * B_STATE], bmb[rs, g * B_STATE:(g + 1) * B_STATE])
              for g in range(B_GROUPS)]
        y_parts = []
        for p in range(n_pairs):
            g = (2 * p) // heads_per_group
            ms = []
            for hh in (2 * p, 2 * p + 1):
                seg = jnp.broadcast_to(a_c[:, hh:hh + 1], (CHUNK, CHUNK)) - \
                    jnp.broadcast_to(acum_t[hh:hh + 1, :], (CHUNK, CHUNK))
                lmat = jnp.where(causal, jnp.exp(seg), 0.0)
                ms.append((cb[g] * lmat).astype(BF16))
            lhs = jnp.concatenate(ms, axis=1)
            xd_p = xd[rs, p * LANES:(p + 1) * LANES]
            rhs = jnp.concatenate([jnp.where(first_half, xd_p, 0.0),
                                   jnp.where(first_half, 0.0, xd_p)], axis=0).astype(BF16)
            y_parts.append(_dot(lhs, rhs))
        y_rows[c] = jnp.concatenate(y_parts, axis=1)

    def proj_out_a(j):
        out_a[j] = _dot(ya, wout_ref[0:width_a, j * blk:(j + 1) * blk])

    n_o = o_ref.shape[1] // blk
    _zip_stages([functools.partial(scan_chunk, c) for c in range(n_chunks)],
                [functools.partial(proj_out_a, j) for j in range(n_o)], lag=0)

    for c, rs in enumerate(chunks):
        st_prev = st_ref[...]
        stb = st_prev.astype(BF16)
        y_off = jnp.concatenate(
            [_dot(cmb[rs, g * B_STATE:(g + 1) * B_STATE], stb[:, g * gw:(g + 1) * gw]) for g in range(B_GROUPS)],
            axis=1)
        st_add = jnp.concatenate(
            [_dot_tn(bmb[rs, g * B_STATE:(g + 1) * B_STATE], xdwb[rs, g * gw:(g + 1) * gw])
             for g in range(B_GROUPS)], axis=1)
        chunk_decay = e_acum[(c + 1) * CHUNK - 1:(c + 1) * CHUNK, :]
        st_ref[...] = st_prev * chunk_decay + st_add
        y_rows[c] = y_rows[c] + y_off * e_acum[rs]
    y = (jnp.concatenate(y_rows, axis=0) if n_chunks > 1 else y_rows[0]) + xs * dskip_ref[...]
    z = jnp.concatenate([z_blocks[j] for j in range(n_z)], axis=1)
    y = y * _silu(z)
    half = inner // B_GROUPS
    yn = []
    for g in range(B_GROUPS):
        yg = y[:, g * half:(g + 1) * half]
        yn.append(yg * lax.rsqrt(jnp.mean(yg * yg, axis=-1, keepdims=True) + EPS))
    yb = (jnp.concatenate(yn, axis=1) * normg_ref[...]).astype(BF16)

    for j in range(n_o):
        cs = slice(j * blk, (j + 1) * blk)
        o_ref[:, cs] = x[:, cs] + out_a[j] + _dot(yb, wout_ref[width_a:, cs])

    @pl.when(s == pl.num_programs(1) - 1)
    def _():
        ssm_out_ref[...] = st_ref[...].T


def _even_prompt(h, p, tile):
    b, seq, d = h.shape
    width_a = p["wa"].shape[1] // 2
    inner = p["wz"].shape[1]
    conv_dim = p["wxbc"].shape[1]
    assert seq % tile == 0 and tile % CHUNK == 0
    small = ["lng", "lnb", "ws", "bsb", "convw", "convb", "dtb", "alog", "dskip", "normg", "expand"]
    out, conv_tail, ssm = pl.pallas_call(
        functools.partial(_even_prompt_kernel, tile=tile, width_a=width_a, inner=inner),
        out_shape=(jax.ShapeDtypeStruct((b, seq, d), F32),
                   jax.ShapeDtypeStruct((b, SUBLANES, conv_dim), F32),
                   jax.ShapeDtypeStruct((b, inner, B_STATE), F32)),
        grid=(b, seq // tile),
        in_specs=[pl.BlockSpec((None, tile, d), lambda i, j: (i, j, 0)),
                  _resident((1, d))]
        + [_resident(p[k].shape) for k in ("wa", "wz", "wxbc", "wdt", "wout")]
        + [_resident(p[k].shape) for k in small],
        out_specs=(pl.BlockSpec((None, tile, d), lambda i, j: (i, j, 0)),
                   pl.BlockSpec((None, SUBLANES, conv_dim), lambda i, j: (i, 0, 0)),
                   pl.BlockSpec((None, inner, B_STATE), lambda i, j: (i, 0, 0))),
        scratch_shapes=[pltpu.VMEM((tile + SUBLANES, conv_dim), F32),
                        pltpu.VMEM((B_STATE, inner), F32)],
        compiler_params=pltpu.CompilerParams(
            dimension_semantics=("arbitrary", "arbitrary"), vmem_limit_bytes=VMEM_LIMIT),
        name="even_prompt",
    )(h, p["g"], p["wa"], p["wz"], p["wxbc"], p["wdt"], p["wout"], *[p[k] for k in small])
    return out, conv_tail[:, SUBLANES - (B_CONV - 1):, :], ssm


def _t5_bucket(dist):
    n = np.maximum(dist, 0)
    max_exact = REL_BUCKETS // 2
    n_safe = np.maximum(n, 1).astype(np.float32)
    scale = np.float32((REL_BUCKETS - max_exact) / math.log(REL_MAX_DIST / max_exact))
    large = max_exact + (np.log(n_safe / max_exact) * scale).astype(np.int32)
    large = np.minimum(large, REL_BUCKETS - 1)
    return np.where(n < max_exact, n, large).astype(np.int32)


def _fill_bias(bias_ref, bucket_ref, rel_ref):
    bucket = bucket_ref[...]
    lq = bucket.shape[0]
    has_prev = lax.broadcasted_iota(jnp.int32, bucket.shape, 1) >= CHUNK
    for hh in range(D_Q_HEADS):
        kv, grp = divmod(hh, D_GROUP)
        acc = jnp.full(bucket.shape, NEG, F32)
        for bkt in range(REL_BUCKETS):
            acc = jnp.where(bucket == bkt, rel_ref[bkt, hh], acc)
        bias_ref[0, grp, kv * lq:(kv + 1) * lq, :] = acc
        bias_ref[1, grp, kv * lq:(kv + 1) * lq, :] = jnp.where(has_prev, acc, NEG)


def _kv_lane_ids(rows):
    return lax.broadcasted_iota(jnp.int32, (rows, D_KV_HEADS * D_HEAD_DIM), 1) // D_HEAD_DIM


def _attn_scores(qg, kk):
    lane_kv = _kv_lane_ids(qg.shape[0])
    zero = jnp.zeros_like(qg)
    lhs = jnp.concatenate([jnp.where(lane_kv == kv, qg, zero) for kv in range(D_KV_HEADS)], axis=0)
    return _dot_nt(lhs, kk)


def _attn_probs(sc, bias_ref, sinks_ref, grp, table):
    lq = sc.shape[0] // D_KV_HEADS
    probs = []
    for kv in range(D_KV_HEADS):
        rs = slice(kv * lq, (kv + 1) * lq)
        s_h = sc[rs] + bias_ref[table, grp, rs, :]
        sink = sinks_ref[kv * D_GROUP + grp]
        m = jnp.maximum(jnp.max(s_h, axis=-1, keepdims=True), sink)
        pexp = jnp.exp(s_h - m)
        denom = jnp.sum(pexp, axis=-1, keepdims=True) + jnp.exp(sink - m)
        probs.append((pexp / denom).astype(BF16))
    return jnp.concatenate(probs, axis=0)


def _attn_out(probs, vv):
    lq = probs.shape[0] // D_KV_HEADS
    lane_kv = _kv_lane_ids(lq)
    ov = _dot(probs, vv)
    out = ov[(D_KV_HEADS - 1) * lq:]
    for kv in range(D_KV_HEADS - 2, -1, -1):
        out = jnp.where(lane_kv == kv, ov[kv * lq:(kv + 1) * lq], out)
    return out


def _odd_prompt_kernel(h_ref, g_ref, wc_ref, wq_ref, wk_ref, wv_ref, wout_ref,
                       linw_ref, cscale_ref, qn_ref, kn_ref, onesbd_ref, bucket_ref,
                       sinks_ref, rel_ref,
                       o_ref, pool_out_ref, k_out_ref, v_out_ref,
                       extc_ref, kprev_ref, vprev_ref, bias_ref, *, tile, width_c):
    b = pl.program_id(0)
    s = pl.program_id(1)
    n_blocks = tile // CHUNK

    @pl.when((b == 0) & (s == 0))
    def _():
        _fill_bias(bias_ref, bucket_ref, rel_ref)

    @pl.when(s == 0)
    def _():
        extc_ref[0:C_HALO, :] = jnp.zeros((C_HALO, width_c), F32)
        kprev_ref[...] = jnp.zeros(kprev_ref.shape, F32)
        vprev_ref[...] = jnp.zeros(vprev_ref.shape, F32)

    x = h_ref[...]
    xn = _rms(x, g_ref[...]).astype(BF16)

    c_in = _dot(xn, wc_ref[...])
    extc_ref[C_HALO:C_HALO + tile, :] = c_in
    e = extc_ref[...]
    tail = extc_ref[tile:tile + C_HALO, :]
    extc_ref[0:C_HALO, :] = tail
    pool_out_ref[...] = tail
    pos = (s * tile + lax.broadcasted_iota(jnp.int32, (tile, 1), 0) + 1).astype(F32)
    gdim = width_c // len(C_WINDOWS)
    run = e
    shift = 1
    yc = []
    for gi, win in enumerate(C_WINDOWS):
        while shift < win:
            run = run + pltpu.roll(run, shift, 0)
            shift *= 2
        cnt = jnp.minimum(pos, float(win))
        pooled = run[C_HALO:, :gdim] / cnt - c_in[:, gi * gdim:(gi + 1) * gdim]
        yc.append(_dot(pooled.astype(BF16), linw_ref[gi]))
        if gi + 1 < len(C_WINDOWS):
            run = run[:, gdim:]
    ycb = (jnp.concatenate(yc, axis=1) * cscale_ref[...]).astype(BF16)

    q = _dot(xn, wq_ref[...])
    k = _dot(xn, wk_ref[...])
    v = _dot(xn, wv_ref[...])
    ones_bd = onesbd_ref[...]
    inv_d = 1.0 / D_HEAD_DIM
    qn = q * lax.rsqrt(_head_sumsq(q, ones_bd) * inv_d + EPS) * qn_ref[...]
    kn = k * lax.rsqrt(_head_sumsq(k, ones_bd) * inv_d + EPS) * kn_ref[...]
    qs = (qn * (D_HEAD_DIM ** -0.5)).astype(BF16)
    kb = kn.astype(BF16)
    vb = v.astype(BF16)
    first_table = jnp.where(s == 0, 1, 0)
    gw = D_KV_HEADS * D_HEAD_DIM
    keys, vals = [], []
    for blk in range(n_blocks):
        rs = slice(blk * CHUNK, (blk + 1) * CHUNK)
        if blk == 0:
            k_prev, v_prev = kprev_ref[...].astype(BF16), vprev_ref[...].astype(BF16)
        else:
            k_prev, v_prev = kb[(blk - 1) * CHUNK:blk * CHUNK], vb[(blk - 1) * CHUNK:blk * CHUNK]
        keys.append(jnp.concatenate([k_prev, kb[rs]], axis=0))
        vals.append(jnp.concatenate([v_prev, vb[rs]], axis=0))
    kprev_ref[...] = kn[tile - CHUNK:]
    vprev_ref[...] = v[tile - CHUNK:]
    items = [(blk, grp) for blk in range(n_blocks) for grp in range(D_GROUP)]
    n_o = o_ref.shape[1] // MXU_DIM
    out_c, sc, pr, og = {}, {}, {}, {}
    for i in range(len(items) + 2):
        if i < len(items):
            blk, grp = items[i]
            sc[i] = _attn_scores(qs[blk * CHUNK:(blk + 1) * CHUNK, grp * gw:(grp + 1) * gw], keys[blk])
        if 0 <= i - 1 < len(items):
            blk, grp = items[i - 1]
            pr[i - 1] = _attn_probs(sc.pop(i - 1), bias_ref, sinks_ref, grp, first_table if blk == 0 else 0)
        if i < n_o:
            out_c[i] = _dot(ycb, wout_ref[0:width_c, i * MXU_DIM:(i + 1) * MXU_DIM])
        if 0 <= i - 2 < len(items):
            blk, grp = items[i - 2]
            og[i - 2] = _attn_out(pr.pop(i - 2), vals[blk])
    yd = jnp.concatenate(
        [jnp.concatenate([og[blk * D_GROUP + grp] for grp in range(D_GROUP)], axis=1) for blk in range(n_blocks)],
        axis=0).astype(BF16)
    for j in range(n_o):
        cs = slice(j * MXU_DIM, (j + 1) * MXU_DIM)
        o_ref[:, cs] = x[:, cs] + out_c[j] + _dot(yd, wout_ref[width_c:, cs])

    @pl.when(s == pl.num_programs(1) - 1)
    def _():
        k_out_ref[...] = kn[tile - CHUNK:]
        v_out_ref[...] = v[tile - CHUNK:]


def _odd_prompt(h, p, tile):
    b, seq, d = h.shape
    width_c = p["wc"].shape[1]
    kvw = p["wk"].shape[1]
    assert seq % tile == 0 and tile % CHUNK == 0
    r = np.arange(CHUNK) + CHUNK
    c = np.arange(2 * CHUNK)
    dist = r[:, None] - c[None, :]
    bucket = np.where((dist >= 0) & (dist < CHUNK), _t5_bucket(dist), -1).astype(np.int32)
    vm = ["linw", "cscale", "qn", "kn", "onesbd"]
    smem = pl.BlockSpec(memory_space=pltpu.SMEM)
    out, pool_tail, k_win, v_win = pl.pallas_call(
        functools.partial(_odd_prompt_kernel, tile=tile, width_c=width_c),
        out_shape=(jax.ShapeDtypeStruct((b, seq, d), F32),
                   jax.ShapeDtypeStruct((b, C_HALO, width_c), F32),
                   jax.ShapeDtypeStruct((b, CHUNK, kvw), F32),
                   jax.ShapeDtypeStruct((b, CHUNK, kvw), F32)),
        grid=(b, seq // tile),
        in_specs=[pl.BlockSpec((None, tile, d), lambda i, j: (i, j, 0)),
                  _resident((1, d))]
        + [_resident(p[k].shape) for k in ("wc", "wq", "wk", "wv", "wout")]
        + [_resident(p[k].shape) for k in vm]
        + [_resident(bucket.shape), smem, smem],
        out_specs=(pl.BlockSpec((None, tile, d), lambda i, j: (i, j, 0)),
                   pl.BlockSpec((None, C_HALO, width_c), lambda i, j: (i, 0, 0)),
                   pl.BlockSpec((None, CHUNK, kvw), lambda i, j: (i, 0, 0)),
                   pl.BlockSpec((None, CHUNK, kvw), lambda i, j: (i, 0, 0))),
        scratch_shapes=[pltpu.VMEM((tile + C_HALO, width_c), F32),
                        pltpu.VMEM((CHUNK, kvw), F32),
                        pltpu.VMEM((CHUNK, kvw), F32),
                        pltpu.VMEM((2, D_GROUP, D_KV_HEADS * CHUNK, 2 * CHUNK), F32)],
        compiler_params=pltpu.CompilerParams(
            dimension_semantics=("arbitrary", "arbitrary"), vmem_limit_bytes=VMEM_LIMIT),
        name="odd_prompt",
    )(h, p["g"], p["wc"], p["wq"], p["wk"], p["wv"], p["wout"], *[p[k] for k in vm],
      jnp.asarray(bucket), p["sinks"], p["rel"])
    return out, pool_tail[:, C_HALO - (max(C_WINDOWS) - 1):, :], k_win, v_win


def _steps(x, n, width):
    return [x[:, t * width:(t + 1) * width] for t in range(n)]


def _stack_steps(ref, n, width):
    x = ref[...]
    return jnp.concatenate(_steps(x, n, width), axis=0)


def _even_sample_front_kernel(hs_ref, g_ref, wa_ref, wz_ref, wxbc_ref, wdt_ref,
                              lng_ref, lnb_ref, wts_ref, bts_ref, convw_ref, convb_ref,
                              dtb_ref, alog_ref, dskip_ref, cs_ref,
                              v_out, ya_out, ypart_out, eacum_out, z_out, conv_out,
                              cgt_out, xdw_out, bs_out, dec_out, *, steps, d, width_a, inner, conv_dim):
    bt = hs_ref.shape[0]
    xn = _rms(_stack_steps(hs_ref, steps, d), g_ref[...]).astype(BF16)
    blk = lambda a, t: a[t * bt:(t + 1) * bt]

    ga = _gelu_tanh(_dot(xn, wa_ref[...]))
    u = ga[:, :width_a]
    v = ga[:, width_a:]
    mu = jnp.mean(v, axis=-1, keepdims=True)
    vc = v - mu
    var = jnp.mean(vc * vc, axis=-1, keepdims=True)
    v = vc * lax.rsqrt(var + EPS) * lng_ref[...] + lnb_ref[...]
    for t in range(steps):
        v_out[:, t * width_a:(t + 1) * width_a] = blk(v, t)
        gate = bts_ref[t:t + 1, :]
        for s in range(t + 1):
            gate = gate + wts_ref[t * steps + s:t * steps + s + 1, :] * blk(v, s)
        ya_out[:, t * width_a:(t + 1) * width_a] = blk(u, t) * gate

    z = _dot(xn, wz_ref[...])
    for t in range(steps):
        z_out[:, t * inner:(t + 1) * inner] = blk(z, t)
    raw = _dot(xn, wxbc_ref[...])
    dt = _softplus(_dot(xn, wdt_ref[...]) + dtb_ref[...])
    ext = _steps(cs_ref[...], B_CONV - 1, conv_dim) + [blk(raw, t) for t in range(steps)]
    for k in range(B_CONV - 1):
        conv_out[:, k * conv_dim:(k + 1) * conv_dim] = ext[len(ext) - (B_CONV - 1) + k]
    gn = B_GROUPS * B_STATE
    n_pairs = B_HEADS // 2
    a_neg = -jnp.exp(alog_ref[...])
    xs, bm, cm, dts, acum = [], [], [], [], []
    for t in range(steps):
        conv = convb_ref[...]
        for tap in range(B_CONV):
            conv = conv + ext[t + tap] * convw_ref[tap:tap + 1, :]
        xbc = _silu(conv)
        xs.append(xbc[:, :inner])
        bm.append(xbc[:, inner:inner + gn])
        cm.append(xbc[:, inner + gn:])
        dts.append(blk(dt, t))
        da = dts[t] * a_neg
        acum.append(da if t == 0 else acum[t - 1] + da)
    lane = lax.broadcasted_iota(jnp.int32, (bt, LANES), 1)
    group0 = lane < (B_HEADS // B_GROUPS)
    dec_out[...] = jnp.exp(acum[steps - 1])
    pad_rows = SUBLANES - steps
    xdw_out[:, steps * inner:] = jnp.zeros((bt, pad_rows * inner), F32)
    bs_out[:, steps * gn:] = jnp.zeros((bt, pad_rows * gn), F32)
    xd = []
    for t in range(steps):
        xd.append(xs[t] * _expand_heads(dts[t], n_pairs))
        eacum_out[:, t * inner:(t + 1) * inner] = _expand_heads(jnp.exp(acum[t]), n_pairs)
        xdw_out[:, t * inner:(t + 1) * inner] = xs[t] * _expand_heads(
            dts[t] * jnp.exp(acum[steps - 1] - acum[t]), n_pairs)
        bs_out[:, t * gn:(t + 1) * gn] = bm[t]
        for g in range(B_GROUPS):
            r = g * steps + t
            cgt_out[:, r * B_STATE:(r + 1) * B_STATE] = cm[t][:, g * B_STATE:(g + 1) * B_STATE]
    for t in range(steps):
        y = xs[t] * dskip_ref[...]
        for s in range(t + 1):
            cb = [jnp.sum(cm[t][:, g * B_STATE:(g + 1) * B_STATE] * bm[s][:, g * B_STATE:(g + 1) * B_STATE],
                          axis=-1, keepdims=True) for g in range(B_GROUPS)]
            coef = jnp.where(group0, cb[0], cb[1]) * jnp.exp(acum[t] - acum[s])
            y = y + _expand_heads(coef, n_pairs) * xd[s]
        ypart_out[:, t * inner:(t + 1) * inner] = y


def _even_sample_state_kernel(s0_ref, cgt_ref, xdw_ref, bs_ref, dec_ref, yoff_ref, snew_ref, *, bb):
    step = pl.program_id(0)
    gw = (B_HEADS // B_GROUPS) * B_HEAD_DIM

    def body(bi, carry):
        s0 = s0_ref[bi]
        c8 = cgt_ref[bi]
        c16 = jnp.concatenate([c8, jnp.zeros_like(c8)], axis=0).astype(BF16)
        yoff_ref[bi] = _dot_nt(c16, s0.astype(BF16))[:SUBLANES]
        x8 = xdw_ref[bi]
        b8 = bs_ref[bi]
        x16 = jnp.concatenate([x8, jnp.zeros_like(x8)], axis=0).astype(BF16)
        b16 = jnp.concatenate([b8, jnp.zeros_like(b8)], axis=0).astype(BF16)
        for g in range(B_GROUPS):
            add = _dot_tn(x16[:, g * gw:(g + 1) * gw], b16[:, g * B_STATE:(g + 1) * B_STATE])
            for hl in range(B_HEADS // B_GROUPS):
                hh = g * (B_HEADS // B_GROUPS) + hl
                rs = slice(hh * B_HEAD_DIM, (hh + 1) * B_HEAD_DIM)
                snew_ref[bi, rs, :] = s0[rs] * dec_ref[step * bb + bi, hh] + \
                    add[hl * B_HEAD_DIM:(hl + 1) * B_HEAD_DIM]
        return carry

    lax.fori_loop(0, bb, body, 0, unroll=2)


def _even_sample_back_kernel(hs_ref, ya_ref, ypart_ref, eacum_ref, z_ref, yoff_ref, normg_ref, wout_ref,
                             o_ref, *, steps, d, inner):
    bt = hs_ref.shape[0]
    half = inner // B_GROUPS
    mixes = []
    for t in range(steps):
        sl = slice(t * inner, (t + 1) * inner)
        yoff = jnp.concatenate(
            [yoff_ref[:, (g * steps + t) * inner + g * half:(g * steps + t) * inner + (g + 1) * half]
             for g in range(B_GROUPS)], axis=1)
        y = (ypart_ref[:, sl] + yoff * eacum_ref[:, sl]) * _silu(z_ref[:, sl])
        yn = []
        for g in range(B_GROUPS):
            yg = y[:, g * half:(g + 1) * half]
            yn.append(yg * lax.rsqrt(jnp.mean(yg * yg, axis=-1, keepdims=True) + EPS))
        yb = jnp.concatenate(yn, axis=1) * normg_ref[...]
        mixes.append(jnp.concatenate([ya_ref[:, t * d:(t + 1) * d], yb], axis=1))
    out = _dot(jnp.concatenate(mixes, axis=0).astype(BF16), wout_ref[...])
    for t in range(steps):
        o_ref[:, t * d:(t + 1) * d] = hs_ref[:, t * d:(t + 1) * d] + out[t * bt:(t + 1) * bt]


def _row_tiled(width, bt):
    return pl.BlockSpec((bt, width), lambda i: (i, 0))


def _even_sample(hs2, state_conv2, state_ssm3, p, steps, bt, bb):
    nb, _ = hs2.shape
    d = p["wa"].shape[0]
    width_a = p["wa"].shape[1] // 2
    inner = p["wz"].shape[1]
    conv_dim = p["wxbc"].shape[1]
    gn = B_GROUPS * B_STATE
    assert nb % bt == 0 and nb % bb == 0 and steps <= SUBLANES
    params = pltpu.CompilerParams(dimension_semantics=("arbitrary",), vmem_limit_bytes=VMEM_LIMIT)
    small = ["lng", "lnb", "wts", "bts", "convw", "convb", "dtb", "alog", "dskip"]
    widths = dict(v=steps * width_a, ya=steps * width_a, ypart=steps * inner, eacum=steps * inner,
                  z=steps * inner, conv=(B_CONV - 1) * conv_dim, cgt=SUBLANES * B_STATE,
                  xdw=SUBLANES * inner, bs=SUBLANES * gn, dec=LANES)
    front = pl.pallas_call(
        functools.partial(_even_sample_front_kernel, steps=steps, d=d, width_a=width_a, inner=inner,
                          conv_dim=conv_dim),
        out_shape=tuple(jax.ShapeDtypeStruct((nb, w), F32) for w in widths.values()),
        grid=(nb // bt,),
        in_specs=[_row_tiled(steps * d, bt), _resident((1, d))]
        + [_resident(p[k].shape) for k in ("wa", "wz", "wxbc", "wdt")]
        + [_resident(p[k].shape) for k in small]
        + [_row_tiled((B_CONV - 1) * conv_dim, bt)],
        out_specs=tuple(_row_tiled(w, bt) for w in widths.values()),
        compiler_params=params,
        name="even_sample_front",
    )(hs2, p["g"], p["wa"], p["wz"], p["wxbc"], p["wdt"], *[p[k] for k in small], state_conv2)
    v_rows, ya, ypart, eacum, z, new_conv, cgt, xdw, bs, dec = front

    hp = state_ssm3.shape[1]
    tile3 = lambda rows, width: pl.BlockSpec((bb, rows, width), lambda i: (i, 0, 0))
    yoff, new_ssm = pl.pallas_call(
        functools.partial(_even_sample_state_kernel, bb=bb),
        out_shape=(jax.ShapeDtypeStruct((nb, SUBLANES, hp), F32),
                   jax.ShapeDtypeStruct(state_ssm3.shape, F32)),
        grid=(nb // bb,),
        in_specs=[tile3(hp, B_STATE), tile3(SUBLANES, B_STATE), tile3(SUBLANES, inner), tile3(SUBLANES, gn),
                  pl.BlockSpec(memory_space=pltpu.SMEM)],
        out_specs=(tile3(SUBLANES, hp), tile3(hp, B_STATE)),
        compiler_params=params,
        name="even_sample_state",
    )(state_ssm3, cgt.reshape(nb, SUBLANES, B_STATE), xdw.reshape(nb, SUBLANES, inner),
      bs.reshape(nb, SUBLANES, gn), dec[:, :B_HEADS])

    out = pl.pallas_call(
        functools.partial(_even_sample_back_kernel, steps=steps, d=d, inner=inner),
        out_shape=jax.ShapeDtypeStruct(hs2.shape, F32),
        grid=(nb // bt,),
        in_specs=[_row_tiled(steps * d, bt), _row_tiled(steps * width_a, bt), _row_tiled(steps * inner, bt),
                  _row_tiled(steps * inner, bt), _row_tiled(steps * inner, bt), _row_tiled(SUBLANES * hp, bt),
                  _resident((1, inner)), _resident(p["wout"].shape)],
        out_specs=_row_tiled(steps * d, bt),
        compiler_params=params,
        name="even_sample_back",
    )(hs2, ya, ypart, eacum, z, yoff.reshape(nb, SUBLANES * hp), p["normg"], p["wout"])
    return out, v_rows, new_conv, new_ssm


def _odd_sample_front_kernel(hs_ref, g_ref, wc_ref, wq_ref, wk_ref, wv_ref, linw_ref, cscale_ref,
                             qn_ref, kn_ref, onesbd_ref, ps_ref,
                             yc_out, pool_out, q_out, knew_out, vnew_out, *, steps, d, width_c, past_len):
    bt = hs_ref.shape[0]
    xn = _rms(_stack_steps(hs_ref, steps, d), g_ref[...]).astype(BF16)
    blk = lambda a, t: a[t * bt:(t + 1) * bt]
    c_in = _dot(xn, wc_ref[...])
    n_state = max(C_WINDOWS) - 1
    ext = _steps(ps_ref[...], n_state, width_c) + [blk(c_in, t) for t in range(steps)]
    for j in range(n_state):
        pool_out[:, j * width_c:(j + 1) * width_c] = ext[len(ext) - n_state + j]
    gdim = width_c // len(C_WINDOWS)
    yc_cols = []
    for gi, win in enumerate(C_WINDOWS):
        sl = slice(gi * gdim, (gi + 1) * gdim)
        pooled = []
        for t in range(steps):
            hi = n_state + t
            lo = max(hi - win + 1, 0)
            acc = ext[lo][:, sl]
            for j in range(lo + 1, hi + 1):
                acc = acc + ext[j][:, sl]
            count = float(min(past_len + t + 1, win))
            pooled.append(acc / count - ext[hi][:, sl])
        yc_cols.append(_dot(jnp.concatenate(pooled, axis=0).astype(BF16), linw_ref[gi]))
    yc = jnp.concatenate(yc_cols, axis=1) * cscale_ref[...]
    for t in range(steps):
        yc_out[:, t * width_c:(t + 1) * width_c] = blk(yc, t)

    q = _dot(xn, wq_ref[...])
    k = _dot(xn, wk_ref[...])
    v = _dot(xn, wv_ref[...])
    ones_bd = onesbd_ref[...]
    inv_d = 1.0 / D_HEAD_DIM
    qn = q * lax.rsqrt(_head_sumsq(q, ones_bd) * inv_d + EPS) * qn_ref[...] * (D_HEAD_DIM ** -0.5)
    kn = k * lax.rsqrt(_head_sumsq(k, ones_bd) * inv_d + EPS) * kn_ref[...]
    qw = q.shape[1]
    kw = k.shape[1]
    pad = SUBLANES - steps
    q_out[:, steps * qw:] = jnp.zeros((bt, pad * qw), F32)
    knew_out[:, :pad * kw] = jnp.zeros((bt, pad * kw), F32)
    vnew_out[:, :pad * kw] = jnp.zeros((bt, pad * kw), F32)
    for t in range(steps):
        q_out[:, t * qw:(t + 1) * qw] = blk(qn, t)
        knew_out[:, (pad + t) * kw:(pad + t + 1) * kw] = blk(kn, t)
        vnew_out[:, (pad + t) * kw:(pad + t + 1) * kw] = blk(v, t)


SINK_BUCKET = REL_BUCKETS


def _odd_sample_attn_kernel(q_ref, knew_ref, vnew_ref, ck_ref, cv_ref, bucket_ref, sinks_ref, rel_ref,
                            o_ref, kout_ref, vout_ref, bias_ref, *, bb, steps, n_keys):
    win = ck_ref.shape[1]
    kvw = ck_ref.shape[2]
    tile16 = 2 * SUBLANES

    @pl.when(pl.program_id(0) == 0)
    def _():
        bucket = bucket_ref[...]
        for hh in range(D_Q_HEADS):
            acc = jnp.full(bucket.shape, NEG, F32)
            for bkt in range(REL_BUCKETS):
                acc = jnp.where(bucket == bkt, rel_ref[bkt, hh], acc)
            acc = jnp.where(bucket == SINK_BUCKET, sinks_ref[hh], acc)
            bias_ref[hh * SUBLANES:(hh + 1) * SUBLANES, :] = acc

    sub = lax.broadcasted_iota(jnp.int32, (SUBLANES, kvw), 0)
    new_rows = sub >= SUBLANES - steps
    lane_kv = lax.broadcasted_iota(jnp.int32, (SUBLANES, kvw), 1) // D_HEAD_DIM
    gw = D_KV_HEADS * D_HEAD_DIM
    zero_keys = jnp.zeros((n_keys - win - tile16, kvw), BF16)

    def extend(cache, new8):
        new16 = jnp.concatenate([new8, jnp.zeros_like(new8)], axis=0).astype(BF16)
        return jnp.concatenate([cache.astype(BF16), new16, zero_keys], axis=0)

    def shift_in(cache, new8, out_ref, bi):
        rolled = pltpu.roll(cache, win - steps, 0)
        out_ref[bi, 0:win - SUBLANES, :] = rolled[:win - SUBLANES]
        out_ref[bi, win - SUBLANES:, :] = jnp.where(new_rows, new8, rolled[win - SUBLANES:])

    def body(bi, carry):
        ck = ck_ref[bi]
        cv = cv_ref[bi]
        k8 = knew_ref[bi]
        v8 = vnew_ref[bi]
        shift_in(ck, k8, kout_ref, bi)
        shift_in(cv, v8, vout_ref, bi)
        q8 = q_ref[bi]
        pieces = []
        for kv in range(D_KV_HEADS):
            for grp in range(D_GROUP):
                qg = q8[:, grp * gw:(grp + 1) * gw]
                pieces.append(jnp.where(lane_kv == kv, qg, 0.0))
        lhs = jnp.concatenate(pieces, axis=0).astype(BF16)
        sc = _dot_nt(lhs, extend(ck, k8)) + bias_ref[...]
        m = jnp.max(sc, axis=-1, keepdims=True)
        pexp = jnp.exp(sc - m)
        probs = (pexp / jnp.sum(pexp, axis=-1, keepdims=True)).astype(BF16)
        ov = _dot(probs, extend(cv, v8))
        outs = []
        for grp in range(D_GROUP):
            r_last = ((D_KV_HEADS - 1) * D_GROUP + grp) * SUBLANES
            acc = ov[r_last:r_last + SUBLANES]
            for kv in range(D_KV_HEADS - 2, -1, -1):
                r0 = (kv * D_GROUP + grp) * SUBLANES
                acc = jnp.where(lane_kv == kv, ov[r0:r0 + SUBLANES], acc)
            outs.append(acc)
        o_ref[bi] = jnp.concatenate(outs, axis=1)
        return carry

    lax.fori_loop(0, bb, body, 0, unroll=SAMPLE_ATTN_UNROLL)


def _odd_sample_back_kernel(hs_ref, yc_ref, o_ref_in, wout_ref, out_ref, *, steps, d, width_c, qw):
    bt = hs_ref.shape[0]
    mix = jnp.concatenate(
        [jnp.concatenate([yc_ref[:, t * width_c:(t + 1) * width_c], o_ref_in[:, t * qw:(t + 1) * qw]], axis=1)
         for t in range(steps)], axis=0).astype(BF16)
    out = _dot(mix, wout_ref[...])
    for t in range(steps):
        out_ref[:, t * d:(t + 1) * d] = hs_ref[:, t * d:(t + 1) * d] + out[t * bt:(t + 1) * bt]


def _odd_sample(hs2, state_pool2, cache_k3, cache_v3, p, steps, past_len, bt, bb):
    nb, _ = hs2.shape
    d = p["wc"].shape[0]
    width_c = p["wc"].shape[1]
    qw = p["wq"].shape[1]
    kw = p["wk"].shape[1]
    win = cache_k3.shape[1]
    n_state = max(C_WINDOWS) - 1
    assert nb % bt == 0 and nb % bb == 0 and steps <= SUBLANES and win == CHUNK
    params = pltpu.CompilerParams(dimension_semantics=("arbitrary",), vmem_limit_bytes=VMEM_LIMIT)
    vm = ["linw", "cscale", "qn", "kn", "onesbd"]
    widths = dict(yc=steps * width_c, pool=n_state * width_c, q=SUBLANES * qw, knew=SUBLANES * kw,
                  vnew=SUBLANES * kw)
    yc, new_pool, q8, knew8, vnew8 = pl.pallas_call(
        functools.partial(_odd_sample_front_kernel, steps=steps, d=d, width_c=width_c, past_len=past_len),
        out_shape=tuple(jax.ShapeDtypeStruct((nb, w), F32) for w in widths.values()),
        grid=(nb // bt,),
        in_specs=[_row_tiled(steps * d, bt), _resident((1, d))]
        + [_resident(p[k].shape) for k in ("wc", "wq", "wk", "wv")]
        + [_resident(p[k].shape) for k in vm]
        + [_row_tiled(n_state * width_c, bt)],
        out_specs=tuple(_row_tiled(w, bt) for w in widths.values()),
        compiler_params=params,
        name="odd_sample_front",
    )(hs2, p["g"], p["wc"], p["wq"], p["wk"], p["wv"], *[p[k] for k in vm], state_pool2)

    n_keys = 2 * CHUNK
    pad = SUBLANES - steps
    bucket = np.full((SUBLANES, n_keys), -1, np.int32)
    for t in range(steps):
        q_pos = past_len + t
        k_pos = np.full(n_keys, -10 ** 9, np.int64)
        k_pos[:win] = past_len - win + np.arange(win)
        k_pos[win + pad:win + SUBLANES] = past_len + np.arange(steps)
        dist = q_pos - k_pos
        ok = (dist >= 0) & (dist < CHUNK) & (k_pos >= 0)
        bucket[t] = np.where(ok, _t5_bucket(np.where(ok, dist, 0)), -1)
    bucket[:, n_keys - 1] = SINK_BUCKET
    smem = pl.BlockSpec(memory_space=pltpu.SMEM)
    tile3 = lambda rows, width: pl.BlockSpec((bb, rows, width), lambda i: (i, 0, 0))
    o8, new_k, new_v = pl.pallas_call(
        functools.partial(_odd_sample_attn_kernel, bb=bb, steps=steps, n_keys=n_keys),
        out_shape=(jax.ShapeDtypeStruct((nb, SUBLANES, qw), F32),
                   jax.ShapeDtypeStruct(cache_k3.shape, F32),
                   jax.ShapeDtypeStruct(cache_v3.shape, F32)),
        grid=(nb // bb,),
        in_specs=[tile3(SUBLANES, qw), tile3(SUBLANES, kw), tile3(SUBLANES, kw), tile3(win, kw), tile3(win, kw),
                  _resident(bucket.shape), smem, smem],
        out_specs=(tile3(SUBLANES, qw), tile3(win, kw), tile3(win, kw)),
        scratch_shapes=[pltpu.VMEM((D_Q_HEADS * SUBLANES, n_keys), F32)],
        compiler_params=params,
        name="odd_sample_attn",
    )(q8.reshape(nb, SUBLANES, qw), knew8.reshape(nb, SUBLANES, kw), vnew8.reshape(nb, SUBLANES, kw),
      cache_k3, cache_v3, jnp.asarray(bucket), p["sinks"], p["rel"])

    out = pl.pallas_call(
        functools.partial(_odd_sample_back_kernel, steps=steps, d=d, width_c=width_c, qw=qw),
        out_shape=jax.ShapeDtypeStruct(hs2.shape, F32),
        grid=(nb // bt,),
        in_specs=[_row_tiled(steps * d, bt), _row_tiled(steps * width_c, bt), _row_tiled(SUBLANES * qw, bt),
                  _resident(p["wout"].shape)],
        out_specs=_row_tiled(steps * d, bt),
        compiler_params=params,
        name="odd_sample_back",
    )(hs2, yc, o8.reshape(nb, SUBLANES * qw), p["wout"])
    return out, new_pool, new_k, new_v


def _row(v):
    return v.reshape(1, -1).astype(F32)


def _pad_lanes(m, width=LANES):
    return jnp.pad(m, ((0, 0), (0, width - m.shape[1])))


def _head_expand_matrix():
    e = np.zeros((LANES, B_HEADS * B_HEAD_DIM), np.float32)
    for hh in range(B_HEADS):
        e[hh, hh * B_HEAD_DIM:(hh + 1) * B_HEAD_DIM] = 1.0
    return np.concatenate([e, e], axis=0)


def _prep_even(mix_norm, w_in, w_out, ln_g, ln_b, w_s, b_s, conv_w, conv_b, dt_bias, a_log, d_skip, norm_g):
    width_a = ln_g.shape[0]
    inner = norm_g.shape[0]
    conv_dim = conv_b.shape[0]
    o1 = 2 * width_a
    o2 = o1 + inner
    o3 = o2 + conv_dim
    return dict(
        g=_row(mix_norm),
        wa=w_in[:, :o1].astype(BF16),
        wz=w_in[:, o1:o2].astype(BF16),
        wxbc=w_in[:, o2:o3].astype(BF16),
        wdt=_pad_lanes(w_in[:, o3:]).astype(BF16),
        wout=w_out.astype(BF16),
        lng=_row(ln_g), lnb=_row(ln_b), ws=w_s,
        bsb=jnp.broadcast_to(b_s[:, :, None], b_s.shape + (width_a // A_HEADS,)),
        convw=conv_w, convb=_row(conv_b),
        dtb=_pad_lanes(_row(dt_bias)), alog=_pad_lanes(_row(a_log)),
        dskip=_row(jnp.repeat(d_skip, B_HEAD_DIM)), normg=_row(norm_g),
        expand=jnp.asarray(_head_expand_matrix(), BF16),
    )


def _prep_odd(mix_norm, w_in, w_out, lin_w, c_scale, q_norm, k_norm, sinks, rel_table):
    d = w_in.shape[0]
    width_c = c_scale.shape[0]
    qw = D_Q_HEADS * D_HEAD_DIM
    kw = D_KV_HEADS * D_HEAD_DIM
    wq = w_in[:, width_c:width_c + qw].reshape(d, D_KV_HEADS, D_GROUP, D_HEAD_DIM)
    wq = wq.transpose(0, 2, 1, 3).reshape(d, qw)
    wo_d = w_out[width_c:].reshape(D_KV_HEADS, D_GROUP, D_HEAD_DIM, -1).transpose(1, 0, 2, 3).reshape(qw, -1)
    ones_bd = np.kron(np.eye(MXU_DIM // D_HEAD_DIM), np.ones((D_HEAD_DIM, D_HEAD_DIM))).astype(np.float32)
    return dict(
        g=_row(mix_norm),
        wc=w_in[:, :width_c].astype(BF16),
        wq=wq.astype(BF16),
        wk=w_in[:, width_c + qw:width_c + qw + kw].astype(BF16),
        wv=w_in[:, width_c + qw + kw:].astype(BF16),
        wout=jnp.concatenate([w_out[:width_c], wo_d], axis=0).astype(BF16),
        linw=lin_w.astype(BF16), cscale=_row(c_scale),
        qn=_row(jnp.tile(q_norm, D_Q_HEADS)), kn=_row(jnp.tile(k_norm, D_KV_HEADS)),
        onesbd=jnp.asarray(ones_bd, BF16),
        sinks=sinks.astype(F32), rel=rel_table.astype(F32),
    )


def _prep_even_sample(w_s, b_s, steps):
    head_w = CHUNK
    w = jnp.transpose(w_s[:, :steps, :steps], (1, 2, 0)).reshape(steps * steps, A_HEADS)
    b = b_s[:, :steps].T
    return dict(wts=jnp.repeat(w, head_w, axis=1), bts=jnp.repeat(b, head_w, axis=1))


PAST_LEN = 16384
FFN_TILE = 1024
FFN_STAGE_SLOTS = 8
FFN_STAGE_ROWS_GU = 32
FFN_STAGE_ROWS_D = 128
FFN_ROW_BLOCK = 256
FFN_BLOCK_TILES = 3
EVEN_MIXER_TILE = 512
ODD_MIXER_TILE = 512
SAMPLE_ROW_TILE = 32
SAMPLE_SEQ_TILE = 8
SAMPLE_ATTN_UNROLL = 4


def kernel(x_prompt, x_sample, state_ssm, state_conv, state_pool, cache_k_win, cache_v_win,
           ffn1_norm, ffn1_w_gu, ffn1_w_down, mix_norm, ffn2_norm, ffn2_w_gu, ffn2_w_down,
           ev_w_in, ev_w_out, a_ln_g, a_ln_b, a_w_s, a_b_s, b_conv_w, b_conv_b, b_dt_bias, b_a_log,
           b_d_skip, b_norm_g, od_w_in, od_w_out, c_lin_w, c_scale, d_q_norm, d_k_norm, d_sinks,
           rel_bias_table):
    bp, seq, d = x_prompt.shape
    bs, steps, _ = x_sample.shape
    past_len = PAST_LEN
    hp = x_prompt
    hs = x_sample
    depth = ffn1_norm.shape[0]
    names = ("a_v_s", "ssm_p", "ssm_s", "conv_p", "conv_s", "pool_p", "pool_s", "k_p", "k_s", "v_p", "v_s")
    outs = {k: [] for k in names}

    def macaron(h_p, h_s, norm, w_gu_all, w_down_all, layer):
        o_p, o_s = _ffn(h_p.reshape(bp * seq, d), h_s.reshape(bs * steps, d), _row(norm), w_gu_all, w_down_all,
                        layer, FFN_TILE)
        return o_p.reshape(bp, seq, d), o_s.reshape(bs, steps, d)

    for layer in range(depth):
        i = layer // 2
        hp, hs = macaron(hp, hs, ffn1_norm[layer], ffn1_w_gu, ffn1_w_down, layer)
        hs2 = hs.reshape(bs, steps * d)
        if layer % 2 == 0:
            p = _prep_even(mix_norm[layer], ev_w_in[i], ev_w_out[i], a_ln_g[i], a_ln_b[i], a_w_s[i], a_b_s[i],
                           b_conv_w[i], b_conv_b[i], b_dt_bias[i], b_a_log[i], b_d_skip[i], b_norm_g[i])
            p.update(_prep_even_sample(a_w_s[i], a_b_s[i], steps))
            hp, conv_p, ssm_p = _even_prompt(hp, p, EVEN_MIXER_TILE)
            hs2, v_rows, conv_s, ssm_s = _even_sample(
                hs2, state_conv[i].reshape(bs, -1), state_ssm[i].reshape(bs, B_HEADS * B_HEAD_DIM, B_STATE),
                p, steps, SAMPLE_ROW_TILE, SAMPLE_SEQ_TILE)
            outs["a_v_s"].append(v_rows.reshape(bs, steps, -1))
            outs["conv_p"].append(conv_p)
            outs["conv_s"].append(conv_s.reshape(state_conv[i].shape))
            outs["ssm_p"].append(ssm_p.reshape(bp, B_HEADS, B_HEAD_DIM, B_STATE))
            outs["ssm_s"].append(ssm_s.reshape(state_ssm[i].shape))
        else:
            p = _prep_odd(mix_norm[layer], od_w_in[i], od_w_out[i], c_lin_w[i], c_scale[i], d_q_norm[i],
                          d_k_norm[i], d_sinks[i], rel_bias_table)
            hp, pool_p, k_p, v_p = _odd_prompt(hp, p, ODD_MIXER_TILE)
            kv_shape = cache_k_win[i].shape
            hs2, pool_s, k_s, v_s = _odd_sample(
                hs2, state_pool[i].reshape(bs, -1), cache_k_win[i].reshape(bs, kv_shape[1], -1),
                cache_v_win[i].reshape(bs, kv_shape[1], -1), p, steps, past_len, SAMPLE_ROW_TILE, SAMPLE_SEQ_TILE)
            outs["pool_p"].append(pool_p)
            outs["pool_s"].append(pool_s.reshape(state_pool[i].shape))
            outs["k_p"].append(k_p.reshape(bp, CHUNK, D_KV_HEADS, D_HEAD_DIM))
            outs["v_p"].append(v_p.reshape(bp, CHUNK, D_KV_HEADS, D_HEAD_DIM))
            outs["k_s"].append(k_s.reshape(kv_shape))
            outs["v_s"].append(v_s.reshape(kv_shape))
        hs = hs2.reshape(bs, steps, d)
        hp, hs = macaron(hp, hs, ffn2_norm[layer], ffn2_w_gu, ffn2_w_down, layer)
    return (hp, hs) + tuple(jnp.stack(outs[k]) for k in names)
```

```python
import functools
import math

import numpy as np
import jax
import jax.numpy as jnp
from jax import lax
from jax.experimental import pallas as pl
from jax.experimental.pallas import tpu as pltpu

F32 = jnp.float32
BF16 = jnp.bfloat16

EPS = 1e-6
NEG = -1e30

LANES = 128
SUBLANES = 8
MXU_DIM = 256
VMEM_BYTES_V7X = 64 * 1024 * 1024
VMEM_LIMIT = VMEM_BYTES_V7X - 8 * 1024 * 1024
A_HEADS = 8
B_HEADS = 16
B_HEAD_DIM = 64
B_GROUPS = 2
B_STATE = 128
B_CONV = 4
CHUNK = 128
C_WINDOWS = (2, 4, 8, 16)
C_HALO = 16
D_Q_HEADS = 16
D_KV_HEADS = 4
D_HEAD_DIM = 64
D_GROUP = D_Q_HEADS // D_KV_HEADS
REL_BUCKETS = 32
REL_MAX_DIST = 128


def _rms(x, g):
    ms = jnp.mean(x * x, axis=-1, keepdims=True)
    return x * lax.rsqrt(ms + EPS) * g


def _sigmoid(x):
    return 1.0 / (1.0 + jnp.exp(-x))


def _silu(x):
    return x * _sigmoid(x)


def _gelu_tanh(x):
    c = math.sqrt(2.0 / math.pi)
    return x * (0.5 * (1.0 + jnp.tanh(c * (x + 0.044715 * (x * x * x)))))


def _softplus(x):
    return jnp.maximum(x, 0.0) + jnp.log1p(jnp.exp(-jnp.abs(x)))


def _split3(x):
    hi = x.astype(BF16)
    r1 = x - hi.astype(F32)
    mid = r1.astype(BF16)
    lo = (r1 - mid.astype(F32)).astype(BF16)
    return hi, mid, lo


def _split2_lanes(x):
    hi = x.astype(BF16)
    lo = (x - hi.astype(F32)).astype(BF16)
    return jnp.concatenate([hi, lo], axis=1)


def _dot(a, b):
    return jnp.dot(a, b, preferred_element_type=F32)


def _dot_nt(a, b):
    return lax.dot_general(a, b, (((1,), (1,)), ((), ())), preferred_element_type=F32)


def _dot_tn(a, b):
    return lax.dot_general(a, b, (((0,), (0,)), ((), ())), preferred_element_type=F32)


def _expand_heads(m, n_pairs):
    rows = m.shape[0]
    lane = lax.broadcasted_iota(jnp.int32, (rows, LANES), 1)
    first = lane < B_HEAD_DIM
    parts = []
    for p in range(n_pairs):
        a = jnp.broadcast_to(m[:, 2 * p:2 * p + 1], (rows, LANES))
        b = jnp.broadcast_to(m[:, 2 * p + 1:2 * p + 2], (rows, LANES))
        parts.append(jnp.where(first, a, b))
    return jnp.concatenate(parts, axis=1)


def _head_sumsq(x, ones_bd):
    xx = (x * x).astype(BF16)
    blk = ones_bd.shape[0]
    outs = [_dot(xx[:, c * blk:(c + 1) * blk], ones_bd) for c in range(x.shape[1] // blk)]
    return jnp.concatenate(outs, axis=1) if len(outs) > 1 else outs[0]


def _ff_blocks(d_ff):
    step = FFN_BLOCK_TILES * MXU_DIM
    return [(c0, min(c0 + step, d_ff)) for c0 in range(0, d_ff, step)]


def _load_cast_rows(src_hbm, dst_ref, stage_ref, sem_ref):
    slots, rows, _ = stage_ref.shape
    n = src_hbm.shape[0] // rows

    def copy(c):
        slot = c % slots
        return pltpu.make_async_copy(src_hbm.at[pl.ds(c * rows, rows), :], stage_ref.at[slot], sem_ref.at[slot])

    for c in range(min(slots, n)):
        copy(c).start()
    for c in range(n):
        copy(c).wait()
        dst_ref[c * rows:(c + 1) * rows, :] = stage_ref[c % slots].astype(BF16)
        if c + slots < n:
            copy(c + slots).start()


def _ffn_rows(x_ref, g_ref, wgu_ref, wd_ref, o_ref, d_ff):
    tm = x_ref.shape[0]
    rows = [slice(r0, min(r0 + FFN_ROW_BLOCK, tm)) for r0 in range(0, tm, FFN_ROW_BLOCK)]
    xns = [_rms(x_ref[rs, :], g_ref[...]).astype(BF16) for rs in rows]
    for rs, xn in zip(rows, xns):
        y = None
        for c0, c1 in _ff_blocks(d_ff):
            gate = _dot(xn, wgu_ref[:, c0:c1])
            up = _dot(xn, wgu_ref[:, d_ff + c0:d_ff + c1])
            act = (_silu(gate) * up).astype(BF16)
            part = _dot(act, wd_ref[c0:c1, :])
            y = part if y is None else y + part
        o_ref[rs, :] = x_ref[rs, :] + 0.5 * y


def _ffn_kernel(xp_ref, xs_ref, g_ref, wgu_hbm, wd_hbm, op_ref, os_ref,
                wgu_ref, wd_ref, stage_gu_ref, stage_d_ref, sem_ref, *, d_ff, prompt_steps, layer):
    i = pl.program_id(0)

    @pl.when(i == 0)
    def _():
        _load_cast_rows(wgu_hbm.at[layer], wgu_ref, stage_gu_ref, sem_ref)
        _load_cast_rows(wd_hbm.at[layer], wd_ref, stage_d_ref, sem_ref)

    @pl.when(i < prompt_steps)
    def _():
        _ffn_rows(xp_ref, g_ref, wgu_ref, wd_ref, op_ref, d_ff)

    @pl.when(i == prompt_steps)
    def _():
        _ffn_rows(xs_ref, g_ref, wgu_ref, wd_ref, os_ref, d_ff)


def _resident(shape):
    nd = len(shape)
    return pl.BlockSpec(shape, lambda *_: (0,) * nd, pipeline_mode=pl.Buffered(1))


def _ffn(xp2d, xs2d, g, wgu_all, wd_all, layer, tm):
    m, d = xp2d.shape
    ms = xs2d.shape[0]
    wgu_shape, wd_shape = wgu_all.shape[1:], wd_all.shape[1:]
    d_ff = wd_shape[0]
    assert m % tm == 0 and d % FFN_STAGE_ROWS_GU == 0 and d_ff % FFN_STAGE_ROWS_D == 0
    steps = m // tm
    last = steps - 1
    whole = lambda shape: pl.BlockSpec(shape, lambda i: (0, 0))
    prompt_tile = pl.BlockSpec((tm, d), lambda i: (jnp.minimum(i, last), 0))
    hbm = pl.BlockSpec(memory_space=pl.ANY)
    return pl.pallas_call(
        functools.partial(_ffn_kernel, d_ff=d_ff, prompt_steps=steps, layer=layer),
        out_shape=(jax.ShapeDtypeStruct((m, d), F32), jax.ShapeDtypeStruct((ms, d), F32)),
        grid=(steps + 1,),
        in_specs=[prompt_tile, whole((ms, d)), _resident((1, d)), hbm, hbm],
        out_specs=(prompt_tile, whole((ms, d))),
        scratch_shapes=[pltpu.VMEM(wgu_shape, BF16), pltpu.VMEM(wd_shape, BF16),
                        pltpu.VMEM((FFN_STAGE_SLOTS, FFN_STAGE_ROWS_GU, wgu_shape[1]), F32),
                        pltpu.VMEM((FFN_STAGE_SLOTS, FFN_STAGE_ROWS_D, wd_shape[1]), F32),
                        pltpu.SemaphoreType.DMA((FFN_STAGE_SLOTS,))],
        compiler_params=pltpu.CompilerParams(
            dimension_semantics=("arbitrary",), vmem_limit_bytes=VMEM_LIMIT),
        name="ffn",
    )(xp2d, xs2d, g, wgu_all, wd_all)


def _zip_stages(first, second, lag=1):
    for i in range(max(len(first), len(second) + lag)):
        if i < len(first):
            first[i]()
        if 0 <= i - lag < len(second):
            second[i - lag]()


def _even_prompt_kernel(h_ref, g_ref, wa_ref, wz_ref, wxbc_ref, wdt_ref, wout_ref,
                        lng_ref, lnb_ref, ws_ref, bsb_ref, convw_ref, convb_ref,
                        dtb_ref, alog_ref, dskip_ref, normg_ref, expand_ref,
                        o_ref, conv_out_ref, ssm_out_ref,
                        ext_ref, st_ref, *, tile, width_a, inner):
    s = pl.program_id(1)
    n_chunks = tile // CHUNK
    halo = SUBLANES

    @pl.when(s == 0)
    def _():
        ext_ref[0:halo, :] = jnp.zeros((halo, ext_ref.shape[1]), F32)
        st_ref[...] = jnp.zeros(st_ref.shape, F32)

    x = h_ref[...]
    xn = _rms(x, g_ref[...]).astype(BF16)

    row = lax.broadcasted_iota(jnp.int32, (CHUNK, CHUNK), 0)
    col = lax.broadcasted_iota(jnp.int32, (CHUNK, CHUNK), 1)
    causal = row >= col
    lane = lax.broadcasted_iota(jnp.int32, (CHUNK, LANES), 1)
    first_half = lane < B_HEAD_DIM

    blk = MXU_DIM
    col_blocks = lambda width: [slice(j * blk, (j + 1) * blk) for j in range(width // blk)]
    head_w = width_a // A_HEADS
    gn = B_GROUPS * B_STATE
    conv_dim = inner + 2 * gn
    n_pairs = B_HEADS // 2
    heads_per_group = B_HEADS // B_GROUPS
    gw = heads_per_group * B_HEAD_DIM
    chunks = [slice(c * CHUNK, (c + 1) * CHUNK) for c in range(n_chunks)]

    ga = {}
    a_cols = [slice(width_a + sl.start, width_a + sl.stop) for sl in col_blocks(width_a)] + col_blocks(width_a)
    pa = {}

    def proj_a(j):
        pa[j] = _dot(xn, wa_ref[:, a_cols[j]])

    def act_a(j):
        ga[j] = _gelu_tanh(pa.pop(j))

    n_a = len(a_cols)
    _zip_stages([functools.partial(proj_a, j) for j in range(n_a)],
                [functools.partial(act_a, j) for j in range(n_a)])
    nv = n_a // 2
    v = jnp.concatenate([ga[j] for j in range(nv)], axis=1)
    u_blocks = [ga[j] for j in range(nv, n_a)]

    raw = {}
    xbc_blocks = {}
    vb_box = []

    def proj_xbc(k):
        raw[k] = _dot(xn, wxbc_ref[:, k * blk:(k + 1) * blk])

    def layer_norm_v():
        mu = jnp.mean(v, axis=-1, keepdims=True)
        vc = v - mu
        var = jnp.mean(vc * vc, axis=-1, keepdims=True)
        vb_box.append((vc * lax.rsqrt(var + EPS) * lng_ref[...] + lnb_ref[...]).astype(BF16))

    def conv_block(k):
        cs = slice(k * blk, (k + 1) * blk)
        ext_ref[halo:halo + tile, cs] = raw.pop(k)
        ext = ext_ref[:, cs]
        ext1 = pltpu.roll(ext, 1, 0)
        pair = ext * convw_ref[1:2, cs] + ext1 * convw_ref[0:1, cs]
        conv = (convb_ref[:, cs] + ext * convw_ref[3:4, cs] + ext1 * convw_ref[2:3, cs]
                + pltpu.roll(pair, 2, 0))[halo:]
        tail = ext_ref[tile:tile + halo, cs]
        ext_ref[0:halo, cs] = tail
        conv_out_ref[:, cs] = tail
        xbc_blocks[k] = _silu(conv)

    n_x = conv_dim // blk
    _zip_stages([functools.partial(proj_xbc, k) for k in range(n_x)],
                [layer_norm_v] + [functools.partial(conv_block, k) for k in range(n_x)])
    vb = vb_box[0]

    z_blocks = {}
    gate_cols = {}

    def proj_z(j):
        z_blocks[j] = _dot(xn, wz_ref[:, j * blk:(j + 1) * blk])

    def mix_head(hh):
        w = jnp.where(causal, ws_ref[hh], 0.0).astype(BF16)
        rhs = jnp.concatenate([vb[rs, hh * head_w:(hh + 1) * head_w] for rs in chunks], axis=1)
        out = _dot(w, rhs)
        bias = bsb_ref[hh]
        gate_cols[hh] = jnp.concatenate(
            [out[:, c * head_w:(c + 1) * head_w] + bias for c in range(n_chunks)], axis=0)

    dt_raw = _dot(xn, wdt_ref[...])
    n_z = inner // blk
    heads_per_z = A_HEADS // n_z
    for j in range(n_z):
        proj_z(j)
        for hh in range(j * heads_per_z, (j + 1) * heads_per_z):
            mix_head(hh)
    heads_per_blk = blk // head_w
    ya_blocks = [(u_blocks[j] * jnp.concatenate(
        [gate_cols[j * heads_per_blk + i] for i in range(heads_per_blk)], axis=1)).astype(BF16)
        for j in range(len(u_blocks))]
    ya = jnp.concatenate(ya_blocks, axis=1)
    xs = jnp.concatenate([xbc_blocks[k] for k in range(inner // blk)], axis=1)
    bm = xbc_blocks[inner // blk]
    cm = xbc_blocks[inner // blk + 1]

    dt = _softplus(dt_raw + dtb_ref[...])
    a_neg = -jnp.exp(alog_ref[...])
    da = dt * a_neg
    tril_ones = jnp.where(causal, 1.0, 0.0).astype(BF16)
    acums = []
    for rs in chunks:
        d_hi, d_mid, d_lo = _split3(da[rs])
        acums.append(_dot(tril_ones, d_hi) + _dot(tril_ones, d_mid) + _dot(tril_ones, d_lo))
    acum = jnp.concatenate(acums, axis=0)
    decay = jnp.concatenate([jnp.exp(a[CHUNK - 1:CHUNK, :] - a) for a in acums], axis=0)
    expand = expand_ref[...]
    xd = xs * _dot(_split2_lanes(dt), expand)
    xdwb = (xs * _dot(_split2_lanes(dt * decay), expand)).astype(BF16)
    e_acum = _dot(_split2_lanes(jnp.exp(acum)), expand)
    bmb = bm.astype(BF16)
    cmb = cm.astype(BF16)

    y_rows = [None] * n_chunks
    out_a = {}

    def scan_chunk(c):
        rs = chunks[c]
        a_c = acums[c]
        acum_t = a_c.T
        cb = [_dot_nt(cmb[rs, g * B_STATE:(g + 1) * B_STATE], bmb[rs, g * B_STATE:(g + 1) * B_STATE])
              for g in range(B_GROUPS)]
        y_parts = []
        for p in range(n_pairs):
            g = (2 * p) // heads_per_group
            ms = []
            for hh in (2 * p, 2 * p + 1):
                seg = jnp.broadcast_to(a_c[:, hh:hh + 1], (CHUNK, CHUNK)) - \
                    jnp.broadcast_to(acum_t[hh:hh + 1, :], (CHUNK, CHUNK))
                lmat = jnp.where(causal, jnp.exp(seg), 0.0)
                ms.append((cb[g] * lmat).astype(BF16))
            lhs = jnp.concatenate(ms, axis=1)
            xd_p = xd[rs, p * LANES:(p + 1) * LANES]
            rhs = jnp.concatenate([jnp.where(first_half, xd_p, 0.0),
                                   jnp.where(first_half, 0.0, xd_p)], axis=0).astype(BF16)
            y_parts.append(_dot(lhs, rhs))
        y_rows[c] = jnp.concatenate(y_parts, axis=1)

    def proj_out_a(j):
        out_a[j] = _dot(ya, wout_ref[0:width_a, j * blk:(j + 1) * blk])

    n_o = o_ref.shape[1] // blk
    _zip_stages([functools.partial(scan_chunk, c) for c in range(n_chunks)],
                [functools.partial(proj_out_a, j) for j in range(n_o)], lag=0)

    for c, rs in enumerate(chunks):
        st_prev = st_ref[...]
        stb = st_prev.astype(BF16)
        y_off = jnp.concatenate(
            [_dot(cmb[rs, g * B_STATE:(g + 1) * B_STATE], stb[:, g * gw:(g + 1) * gw]) for g in range(B_GROUPS)],
            axis=1)
        st_add = jnp.concatenate(
            [_dot_tn(bmb[rs, g * B_STATE:(g + 1) * B_STATE], xdwb[rs, g * gw:(g + 1) * gw])
             for g in range(B_GROUPS)], axis=1)
        chunk_decay = e_acum[(c + 1) * CHUNK - 1:(c + 1) * CHUNK, :]
        st_ref[...] = st_prev * chunk_decay + st_add
        y_rows[c] = y_rows[c] + y_off * e_acum[rs]
    y = (jnp.concatenate(y_rows, axis=0) if n_chunks > 1 else y_rows[0]) + xs * dskip_ref[...]
    z = jnp.concatenate([z_blocks[j] for j in range(n_z)], axis=1)
    y = y * _silu(z)
    half = inner // B_GROUPS
    yn = []
    for g in range(B_GROUPS):
        yg = y[:, g * half:(g + 1) * half]
        yn.append(yg * lax.rsqrt(jnp.mean(yg * yg, axis=-1, keepdims=True) + EPS))
    yb = (jnp.concatenate(yn, axis=1) * normg_ref[...]).astype(BF16)

    for j in range(n_o):
        cs = slice(j * blk, (j + 1) * blk)
        o_ref[:, cs] = x[:, cs] + out_a[j] + _dot(yb, wout_ref[width_a:, cs])

    @pl.when(s == pl.num_programs(1) - 1)
    def _():
        ssm_out_ref[...] = st_ref[...].T


def _even_prompt(h, p, tile):
    b, seq, d = h.shape
    width_a = p["wa"].shape[1] // 2
    inner = p["wz"].shape[1]
    conv_dim = p["wxbc"].shape[1]
    assert seq % tile == 0 and tile % CHUNK == 0
    small = ["lng", "lnb", "ws", "bsb", "convw", "convb", "dtb", "alog", "dskip", "normg", "expand"]
    out, conv_tail, ssm = pl.pallas_call(
        functools.partial(_even_prompt_kernel, tile=tile, width_a=width_a, inner=inner),
        out_shape=(jax.ShapeDtypeStruct((b, seq, d), F32),
                   jax.ShapeDtypeStruct((b, SUBLANES, conv_dim), F32),
                   jax.ShapeDtypeStruct((b, inner, B_STATE), F32)),
        grid=(b, seq // tile),
        in_specs=[pl.BlockSpec((None, tile, d), lambda i, j: (i, j, 0)),
                  _resident((1, d))]
        + [_resident(p[k].shape) for k in ("wa", "wz", "wxbc", "wdt", "wout")]
        + [_resident(p[k].shape) for k in small],
        out_specs=(pl.BlockSpec((None, tile, d), lambda i, j: (i, j, 0)),
                   pl.BlockSpec((None, SUBLANES, conv_dim), lambda i, j: (i, 0, 0)),
                   pl.BlockSpec((None, inner, B_STATE), lambda i, j: (i, 0, 0))),
        scratch_shapes=[pltpu.VMEM((tile + SUBLANES, conv_dim), F32),
                        pltpu.VMEM((B_STATE, inner), F32)],
        compiler_params=pltpu.CompilerParams(
            dimension_semantics=("arbitrary", "arbitrary"), vmem_limit_bytes=VMEM_LIMIT),
        name="even_prompt",
    )(h, p["g"], p["wa"], p["wz"], p["wxbc"], p["wdt"], p["wout"], *[p[k] for k in small])
    return out, conv_tail[:, SUBLANES - (B_CONV - 1):, :], ssm


def _t5_bucket(dist):
    n = np.maximum(dist, 0)
    max_exact = REL_BUCKETS // 2
    n_safe = np.maximum(n, 1).astype(np.float32)
    scale = np.float32((REL_BUCKETS - max_exact) / math.log(REL_MAX_DIST / max_exact))
    large = max_exact + (np.log(n_safe / max_exact) * scale).astype(np.int32)
    large = np.minimum(large, REL_BUCKETS - 1)
    return np.where(n < max_exact, n, large).astype(np.int32)


def _fill_bias(bias_ref, bucket_ref, rel_ref):
    bucket = bucket_ref[...]
    lq = bucket.shape[0]
    has_prev = lax.broadcasted_iota(jnp.int32, bucket.shape, 1) >= CHUNK
    for hh in range(D_Q_HEADS):
        kv, grp = divmod(hh, D_GROUP)
        acc = jnp.full(bucket.shape, NEG, F32)
        for bkt in range(REL_BUCKETS):
            acc = jnp.where(bucket == bkt, rel_ref[bkt, hh], acc)
        bias_ref[0, grp, kv * lq:(kv + 1) * lq, :] = acc
        bias_ref[1, grp, kv * lq:(kv + 1) * lq, :] = jnp.where(has_prev, acc, NEG)


def _kv_lane_ids(rows):
    return lax.broadcasted_iota(jnp.int32, (rows, D_KV_HEADS * D_HEAD_DIM), 1) // D_HEAD_DIM


def _attn_scores(qg, kk):
    lane_kv = _kv_lane_ids(qg.shape[0])
    zero = jnp.zeros_like(qg)
    lhs = jnp.concatenate([jnp.where(lane_kv == kv, qg, zero) for kv in range(D_KV_HEADS)], axis=0)
    return _dot_nt(lhs, kk)


def _attn_probs(sc, bias_ref, sinks_ref, grp, table):
    lq = sc.shape[0] // D_KV_HEADS
    probs = []
    for kv in range(D_KV_HEADS):
        rs = slice(kv * lq, (kv + 1) * lq)
        s_h = sc[rs] + bias_ref[table, grp, rs, :]
        sink = sinks_ref[kv * D_GROUP + grp]
        m = jnp.maximum(jnp.max(s_h, axis=-1, keepdims=True), sink)
        pexp = jnp.exp(s_h - m)
        denom = jnp.sum(pexp, axis=-1, keepdims=True) + jnp.exp(sink - m)
        probs.append((pexp / denom).astype(BF16))
    return jnp.concatenate(probs, axis=0)


def _attn_out(probs, vv):
    lq = probs.shape[0] // D_KV_HEADS
    lane_kv = _kv_lane_ids(lq)
    ov = _dot(probs, vv)
    out = ov[(D_KV_HEADS - 1) * lq:]
    for kv in range(D_KV_HEADS - 2, -1, -1):
        out = jnp.where(lane_kv == kv, ov[kv * lq:(kv + 1) * lq], out)
    return out


def _odd_prompt_kernel(h_ref, g_ref, wc_ref, wq_ref, wk_ref, wv_ref, wout_ref,
                       linw_ref, cscale_ref, qn_ref, kn_ref, onesbd_ref, bucket_ref,
                       sinks_ref, rel_ref,
                       o_ref, pool_out_ref, k_out_ref, v_out_ref,
                       extc_ref, kprev_ref, vprev_ref, bias_ref, *, tile, width_c):
    b = pl.program_id(0)
    s = pl.program_id(1)
    n_blocks = tile // CHUNK

    @pl.when((b == 0) & (s == 0))
    def _():
        _fill_bias(bias_ref, bucket_ref, rel_ref)

    @pl.when(s == 0)
    def _():
        extc_ref[0:C_HALO, :] = jnp.zeros((C_HALO, width_c), F32)
        kprev_ref[...] = jnp.zeros(kprev_ref.shape, F32)
        vprev_ref[...] = jnp.zeros(vprev_ref.shape, F32)

    x = h_ref[...]
    xn = _rms(x, g_ref[...]).astype(BF16)

    c_in = _dot(xn, wc_ref[...])
    q = _dot(xn, wq_ref[...])
    k = _dot(xn, wk_ref[...])
    v = _dot(xn, wv_ref[...])

    extc_ref[C_HALO:C_HALO + tile, :] = c_in
    e = extc_ref[...]
    tail = extc_ref[tile:tile + C_HALO, :]
    extc_ref[0:C_HALO, :] = tail
    pool_out_ref[...] = tail
    pos = (s * tile + lax.broadcasted_iota(jnp.int32, (tile, 1), 0) + 1).astype(F32)
    gdim = width_c // len(C_WINDOWS)
    run = e
    shift = 1
    yc = []
    for gi, win in enumerate(C_WINDOWS):
        while shift < win:
            run = run + pltpu.roll(run, shift, 0)
            shift *= 2
        cnt = jnp.minimum(pos, float(win))
        pooled = run[C_HALO:, :gdim] / cnt - c_in[:, gi * gdim:(gi + 1) * gdim]
        yc.append(_dot(pooled.astype(BF16), linw_ref[gi]))
        if gi + 1 < len(C_WINDOWS):
            run = run[:, gdim:]
    ycb = (jnp.concatenate(yc, axis=1) * cscale_ref[...]).astype(BF16)

    ones_bd = onesbd_ref[...]
    inv_d = 1.0 / D_HEAD_DIM
    qn = q * lax.rsqrt(_head_sumsq(q, ones_bd) * inv_d + EPS) * qn_ref[...]
    kn = k * lax.rsqrt(_head_sumsq(k, ones_bd) * inv_d + EPS) * kn_ref[...]
    qs = (qn * (D_HEAD_DIM ** -0.5)).astype(BF16)
    kb = kn.astype(BF16)
    vb = v.astype(BF16)
    first_table = jnp.where(s == 0, 1, 0)
    gw = D_KV_HEADS * D_HEAD_DIM
    keys, vals = [], []
    for blk in range(n_blocks):
        rs = slice(blk * CHUNK, (blk + 1) * CHUNK)
        if blk == 0:
            k_prev, v_prev = kprev_ref[...].astype(BF16), vprev_ref[...].astype(BF16)
        else:
            k_prev, v_prev = kb[(blk - 1) * CHUNK:blk * CHUNK], vb[(blk - 1) * CHUNK:blk * CHUNK]
        keys.append(jnp.concatenate([k_prev, kb[rs]], axis=0))
        vals.append(jnp.concatenate([v_prev, vb[rs]], axis=0))
    kprev_ref[...] = kn[tile - CHUNK:]
    vprev_ref[...] = v[tile - CHUNK:]
    items = [(blk, grp) for blk in range(n_blocks) for grp in range(D_GROUP)]
    n_o = o_ref.shape[1] // MXU_DIM
    out_c, sc, pr, og = {}, {}, {}, {}
    for i in range(len(items) + 2):
        if i < len(items):
            blk, grp = items[i]
            sc[i] = _attn_scores(qs[blk * CHUNK:(blk + 1) * CHUNK, grp * gw:(grp + 1) * gw], keys[blk])
        if 0 <= i - 1 < len(items):
            blk, grp = items[i - 1]
            pr[i - 1] = _attn_probs(sc.pop(i - 1), bias_ref, sinks_ref, grp, first_table if blk == 0 else 0)
        if i < n_o:
            out_c[i] = _dot(ycb, wout_ref[0:width_c, i * MXU_DIM:(i + 1) * MXU_DIM])
        if 0 <= i - 2 < len(items):
            blk, grp = items[i - 2]
            og[i - 2] = _attn_out(pr.pop(i - 2), vals[blk])
    yd = jnp.concatenate(
        [jnp.concatenate([og[blk * D_GROUP + grp] for grp in range(D_GROUP)], axis=1) for blk in range(n_blocks)],
        axis=0).astype(BF16)
    for j in range(n_o):
        cs = slice(j * MXU_DIM, (j + 1) * MXU_DIM)
        o_ref[:, cs] = x[:, cs] + out_c[j] + _dot(yd, wout_ref[width_c:, cs])

    @pl.when(s == pl.num_programs(1) - 1)
    def _():
        k_out_ref[...] = kn[tile - CHUNK:]
        v_out_ref[...] = v[tile - CHUNK:]


def _odd_prompt(h, p, tile):
    b, seq, d = h.shape
    width_c = p["wc"].shape[1]
    kvw = p["wk"].shape[1]
    assert seq % tile == 0 and tile % CHUNK == 0
    r = np.arange(CHUNK) + CHUNK
    c = np.arange(2 * CHUNK)
    dist = r[:, None] - c[None, :]
    bucket = np.where((dist >= 0) & (dist < CHUNK), _t5_bucket(dist), -1).astype(np.int32)
    vm = ["linw", "cscale", "qn", "kn", "onesbd"]
    smem = pl.BlockSpec(memory_space=pltpu.SMEM)
    out, pool_tail, k_win, v_win = pl.pallas_call(
        functools.partial(_odd_prompt_kernel, tile=tile, width_c=width_c),
        out_shape=(jax.ShapeDtypeStruct((b, seq, d), F32),
                   jax.ShapeDtypeStruct((b, C_HALO, width_c), F32),
                   jax.ShapeDtypeStruct((b, CHUNK, kvw), F32),
                   jax.ShapeDtypeStruct((b, CHUNK, kvw), F32)),
        grid=(b, seq // tile),
        in_specs=[pl.BlockSpec((None, tile, d), lambda i, j: (i, j, 0)),
                  _resident((1, d))]
        + [_resident(p[k].shape) for k in ("wc", "wq", "wk", "wv", "wout")]
        + [_resident(p[k].shape) for k in vm]
        + [_resident(bucket.shape), smem, smem],
        out_specs=(pl.BlockSpec((None, tile, d), lambda i, j: (i, j, 0)),
                   pl.BlockSpec((None, C_HALO, width_c), lambda i, j: (i, 0, 0)),
                   pl.BlockSpec((None, CHUNK, kvw), lambda i, j: (i, 0, 0)),
                   pl.BlockSpec((None, CHUNK, kvw), lambda i, j: (i, 0, 0))),
        scratch_shapes=[pltpu.VMEM((tile + C_HALO, width_c), F32),
                        pltpu.VMEM((CHUNK, kvw), F32),
                        pltpu.VMEM((CHUNK, kvw), F32),
                        pltpu.VMEM((2, D_GROUP, D_KV_HEADS * CHUNK, 2 * CHUNK), F32)],
        compiler_params=pltpu.CompilerParams(
            dimension_semantics=("arbitrary", "arbitrary"), vmem_limit_bytes=VMEM_LIMIT),
        name="odd_prompt",
    )(h, p["g"], p["wc"], p["wq"], p["wk"], p["wv"], p["wout"], *[p[k] for k in vm],
      jnp.asarray(bucket), p["sinks"], p["rel"])
    return out, pool_tail[:, C_HALO - (max(C_WINDOWS) - 1):, :], k_win, v_win


def _steps(x, n, width):
    return [x[:, t * width:(t + 1) * width] for t in range(n)]


def _stack_steps(ref, n, width):
    x = ref[...]
    return jnp.concatenate(_steps(x, n, width), axis=0)


def _even_sample_front_kernel(hs_ref, g_ref, wa_ref, wz_ref, wxbc_ref, wdt_ref,
                              lng_ref, lnb_ref, wts_ref, bts_ref, convw_ref, convb_ref,
                              dtb_ref, alog_ref, dskip_ref, cs_ref,
                              v_out, ya_out, ypart_out, eacum_out, z_out, conv_out,
                              cgt_out, xdw_out, bs_out, dec_out, *, steps, d, width_a, inner, conv_dim):
    bt = hs_ref.shape[0]
    xn = _rms(_stack_steps(hs_ref, steps, d), g_ref[...]).astype(BF16)
    blk = lambda a, t: a[t * bt:(t + 1) * bt]

    ga = _gelu_tanh(_dot(xn, wa_ref[...]))
    u = ga[:, :width_a]
    v = ga[:, width_a:]
    mu = jnp.mean(v, axis=-1, keepdims=True)
    vc = v - mu
    var = jnp.mean(vc * vc, axis=-1, keepdims=True)
    v = vc * lax.rsqrt(var + EPS) * lng_ref[...] + lnb_ref[...]
    for t in range(steps):
        v_out[:, t * width_a:(t + 1) * width_a] = blk(v, t)
        gate = bts_ref[t:t + 1, :]
        for s in range(t + 1):
            gate = gate + wts_ref[t * steps + s:t * steps + s + 1, :] * blk(v, s)
        ya_out[:, t * width_a:(t + 1) * width_a] = blk(u, t) * gate

    z = _dot(xn, wz_ref[...])
    for t in range(steps):
        z_out[:, t * inner:(t + 1) * inner] = blk(z, t)
    raw = _dot(xn, wxbc_ref[...])
    dt = _softplus(_dot(xn, wdt_ref[...]) + dtb_ref[...])
    ext = _steps(cs_ref[...], B_CONV - 1, conv_dim) + [blk(raw, t) for t in range(steps)]
    for k in range(B_CONV - 1):
        conv_out[:, k * conv_dim:(k + 1) * conv_dim] = ext[len(ext) - (B_CONV - 1) + k]
    gn = B_GROUPS * B_STATE
    n_pairs = B_HEADS // 2
    a_neg = -jnp.exp(alog_ref[...])
    xs, bm, cm, dts, acum = [], [], [], [], []
    for t in range(steps):
        conv = convb_ref[...]
        for tap in range(B_CONV):
            conv = conv + ext[t + tap] * convw_ref[tap:tap + 1, :]
        xbc = _silu(conv)
        xs.append(xbc[:, :inner])
        bm.append(xbc[:, inner:inner + gn])
        cm.append(xbc[:, inner + gn:])
        dts.append(blk(dt, t))
        da = dts[t] * a_neg
        acum.append(da if t == 0 else acum[t - 1] + da)
    lane = lax.broadcasted_iota(jnp.int32, (bt, LANES), 1)
    group0 = lane < (B_HEADS // B_GROUPS)
    dec_out[...] = jnp.exp(acum[steps - 1])
    pad_rows = SUBLANES - steps
    xdw_out[:, steps * inner:] = jnp.zeros((bt, pad_rows * inner), F32)
    bs_out[:, steps * gn:] = jnp.zeros((bt, pad_rows * gn), F32)
    xd = []
    for t in range(steps):
        xd.append(xs[t] * _expand_heads(dts[t], n_pairs))
        eacum_out[:, t * inner:(t + 1) * inner] = _expand_heads(jnp.exp(acum[t]), n_pairs)
        xdw_out[:, t * inner:(t + 1) * inner] = xs[t] * _expand_heads(
            dts[t] * jnp.exp(acum[steps - 1] - acum[t]), n_pairs)
        bs_out[:, t * gn:(t + 1) * gn] = bm[t]
        for g in range(B_GROUPS):
            r = g * steps + t
            cgt_out[:, r * B_STATE:(r + 1) * B_STATE] = cm[t][:, g * B_STATE:(g + 1) * B_STATE]
    for t in range(steps):
        y = xs[t] * dskip_ref[...]
        for s in range(t + 1):
            cb = [jnp.sum(cm[t][:, g * B_STATE:(g + 1) * B_STATE] * bm[s][:, g * B_STATE:(g + 1) * B_STATE],
                          axis=-1, keepdims=True) for g in range(B_GROUPS)]
            coef = jnp.where(group0, cb[0], cb[1]) * jnp.exp(acum[t] - acum[s])
            y = y + _expand_heads(coef, n_pairs) * xd[s]
        ypart_out[:, t * inner:(t + 1) * inner] = y


def _even_sample_state_kernel(s0_ref, cgt_ref, xdw_ref, bs_ref, dec_ref, yoff_ref, snew_ref, *, bb):
    step = pl.program_id(0)
    gw = (B_HEADS // B_GROUPS) * B_HEAD_DIM

    def body(bi, carry):
        s0 = s0_ref[bi]
        c8 = cgt_ref[bi]
        c16 = jnp.concatenate([c8, jnp.zeros_like(c8)], axis=0).astype(BF16)
        yoff_ref[bi] = _dot_nt(c16, s0.astype(BF16))[:SUBLANES]
        x8 = xdw_ref[bi]
        b8 = bs_ref[bi]
        x16 = jnp.concatenate([x8, jnp.zeros_like(x8)], axis=0).astype(BF16)
        b16 = jnp.concatenate([b8, jnp.zeros_like(b8)], axis=0).astype(BF16)
        for g in range(B_GROUPS):
            add = _dot_tn(x16[:, g * gw:(g + 1) * gw], b16[:, g * B_STATE:(g + 1) * B_STATE])
            for hl in range(B_HEADS // B_GROUPS):
                hh = g * (B_HEADS // B_GROUPS) + hl
                rs = slice(hh * B_HEAD_DIM, (hh + 1) * B_HEAD_DIM)
                snew_ref[bi, rs, :] = s0[rs] * dec_ref[step * bb + bi, hh] + \
                    add[hl * B_HEAD_DIM:(hl + 1) * B_HEAD_DIM]
        return carry

    lax.fori_loop(0, bb, body, 0, unroll=2)


def _even_sample_back_kernel(hs_ref, ya_ref, ypart_ref, eacum_ref, z_ref, yoff_ref, normg_ref, wout_ref,
                             o_ref, *, steps, d, inner):
    bt = hs_ref.shape[0]
    half = inner // B_GROUPS
    mixes = []
    for t in range(steps):
        sl = slice(t * inner, (t + 1) * inner)
        yoff = jnp.concatenate(
            [yoff_ref[:, (g * steps + t) * inner + g * half:(g * steps + t) * inner + (g + 1) * half]
             for g in range(B_GROUPS)], axis=1)
        y = (ypart_ref[:, sl] + yoff * eacum_ref[:, sl]) * _silu(z_ref[:, sl])
        yn = []
        for g in range(B_GROUPS):
            yg = y[:, g * half:(g + 1) * half]
            yn.append(yg * lax.rsqrt(jnp.mean(yg * yg, axis=-1, keepdims=True) + EPS))
        yb = jnp.concatenate(yn, axis=1) * normg_ref[...]
        mixes.append(jnp.concatenate([ya_ref[:, t * d:(t + 1) * d], yb], axis=1))
    out = _dot(jnp.concatenate(mixes, axis=0).astype(BF16), wout_ref[...])
    for t in range(steps):
        o_ref[:, t * d:(t + 1) * d] = hs_ref[:, t * d:(t + 1) * d] + out[t * bt:(t + 1) * bt]


def _row_tiled(width, bt):
    return pl.BlockSpec((bt, width), lambda i: (i, 0))


def _even_sample(hs2, state_conv2, state_ssm3, p, steps, bt, bb):
    nb, _ = hs2.shape
    d = p["wa"].shape[0]
    width_a = p["wa"].shape[1] // 2
    inner = p["wz"].shape[1]
    conv_dim = p["wxbc"].shape[1]
    gn = B_GROUPS * B_STATE
    assert nb % bt == 0 and nb % bb == 0 and steps <= SUBLANES
    params = pltpu.CompilerParams(dimension_semantics=("arbitrary",), vmem_limit_bytes=VMEM_LIMIT)
    small = ["lng", "lnb", "wts", "bts", "convw", "convb", "dtb", "alog", "dskip"]
    widths = dict(v=steps * width_a, ya=steps * width_a, ypart=steps * inner, eacum=steps * inner,
                  z=steps * inner, conv=(B_CONV - 1) * conv_dim, cgt=SUBLANES * B_STATE,
                  xdw=SUBLANES * inner, bs=SUBLANES * gn, dec=LANES)
    front = pl.pallas_call(
        functools.partial(_even_sample_front_kernel, steps=steps, d=d, width_a=width_a, inner=inner,
                          conv_dim=conv_dim),
        out_shape=tuple(jax.ShapeDtypeStruct((nb, w), F32) for w in widths.values()),
        grid=(nb // bt,),
        in_specs=[_row_tiled(steps * d, bt), _resident((1, d))]
        + [_resident(p[k].shape) for k in ("wa", "wz", "wxbc", "wdt")]
        + [_resident(p[k].shape) for k in small]
        + [_row_tiled((B_CONV - 1) * conv_dim, bt)],
        out_specs=tuple(_row_tiled(w, bt) for w in widths.values()),
        compiler_params=params,
        name="even_sample_front",
    )(hs2, p["g"], p["wa"], p["wz"], p["wxbc"], p["wdt"], *[p[k] for k in small], state_conv2)
    v_rows, ya, ypart, eacum, z, new_conv, cgt, xdw, bs, dec = front

    hp = state_ssm3.shape[1]
    tile3 = lambda rows, width: pl.BlockSpec((bb, rows, width), lambda i: (i, 0, 0))
    yoff, new_ssm = pl.pallas_call(
        functools.partial(_even_sample_state_kernel, bb=bb),
        out_shape=(jax.ShapeDtypeStruct((nb, SUBLANES, hp), F32),
                   jax.ShapeDtypeStruct(state_ssm3.shape, F32)),
        grid=(nb // bb,),
        in_specs=[tile3(hp, B_STATE), tile3(SUBLANES, B_STATE), tile3(SUBLANES, inner), tile3(SUBLANES, gn),
                  pl.BlockSpec(memory_space=pltpu.SMEM)],
        out_specs=(tile3(SUBLANES, hp), tile3(hp, B_STATE)),
        compiler_params=params,
        name="even_sample_state",
    )(state_ssm3, cgt.reshape(nb, SUBLANES, B_STATE), xdw.reshape(nb, SUBLANES, inner),
      bs.reshape(nb, SUBLANES, gn), dec[:, :B_HEADS])

    out = pl.pallas_call(
        functools.partial(_even_sample_back_kernel, steps=steps, d=d, inner=inner),
        out_shape=jax.ShapeDtypeStruct(hs2.shape, F32),
        grid=(nb // bt,),
        in_specs=[_row_tiled(steps * d, bt), _row_tiled(steps * width_a, bt), _row_tiled(steps * inner, bt),
                  _row_tiled(steps * inner, bt), _row_tiled(steps * inner, bt), _row_tiled(SUBLANES * hp, bt),
                  _resident((1, inner)), _resident(p["wout"].shape)],
        out_specs=_row_tiled(steps * d, bt),
        compiler_params=params,
        name="even_sample_back",
    )(hs2, ya, ypart, eacum, z, yoff.reshape(nb, SUBLANES * hp), p["normg"], p["wout"])
    return out, v_rows, new_conv, new_ssm


def _odd_sample_front_kernel(hs_ref, g_ref, wc_ref, wq_ref, wk_ref, wv_ref, linw_ref, cscale_ref,
                             qn_ref, kn_ref, onesbd_ref, ps_ref,
                             yc_out, pool_out, q_out, knew_out, vnew_out, *, steps, d, width_c, past_len):
    bt = hs_ref.shape[0]
    xn = _rms(_stack_steps(hs_ref, steps, d), g_ref[...]).astype(BF16)
    blk = lambda a, t: a[t * bt:(t + 1) * bt]
    c_in = _dot(xn, wc_ref[...])
    n_state = max(C_WINDOWS) - 1
    ext = _steps(ps_ref[...], n_state, width_c) + [blk(c_in, t) for t in range(steps)]
    for j in range(n_state):
        pool_out[:, j * width_c:(j + 1) * width_c] = ext[len(ext) - n_state + j]
    gdim = width_c // len(C_WINDOWS)
    yc_cols = []
    for gi, win in enumerate(C_WINDOWS):
        sl = slice(gi * gdim, (gi + 1) * gdim)
        pooled = []
        for t in range(steps):
            hi = n_state + t
            lo = max(hi - win + 1, 0)
            acc = ext[lo][:, sl]
            for j in range(lo + 1, hi + 1):
                acc = acc + ext[j][:, sl]
            count = float(min(past_len + t + 1, win))
            pooled.append(acc / count - ext[hi][:, sl])
        yc_cols.append(_dot(jnp.concatenate(pooled, axis=0).astype(BF16), linw_ref[gi]))
    yc = jnp.concatenate(yc_cols, axis=1) * cscale_ref[...]
    for t in range(steps):
        yc_out[:, t * width_c:(t + 1) * width_c] = blk(yc, t)

    q = _dot(xn, wq_ref[...])
    k = _dot(xn, wk_ref[...])
    v = _dot(xn, wv_ref[...])
    ones_bd = onesbd_ref[...]
    inv_d = 1.0 / D_HEAD_DIM
    qn = q * lax.rsqrt(_head_sumsq(q, ones_bd) * inv_d + EPS) * qn_ref[...] * (D_HEAD_DIM ** -0.5)
    kn = k * lax.rsqrt(_head_sumsq(k, ones_bd) * inv_d + EPS) * kn_ref[...]
    qw = q.shape[1]
    kw = k.shape[1]
    pad = SUBLANES - steps
    q_out[:, steps * qw:] = jnp.zeros((bt, pad * qw), F32)
    knew_out[:, :pad * kw] = jnp.zeros((bt, pad * kw), F32)
    vnew_out[:, :pad * kw] = jnp.zeros((bt, pad * kw), F32)
    for t in range(steps):
        q_out[:, t * qw:(t + 1) * qw] = blk(qn, t)
        knew_out[:, (pad + t) * kw:(pad + t + 1) * kw] = blk(kn, t)
        vnew_out[:, (pad + t) * kw:(pad + t + 1) * kw] = blk(v, t)


SINK_BUCKET = REL_BUCKETS


def _odd_sample_attn_kernel(q_ref, knew_ref, vnew_ref, ck_ref, cv_ref, bucket_ref, sinks_ref, rel_ref,
                            o_ref, kout_ref, vout_ref, bias_ref, *, bb, steps, n_keys):
    win = ck_ref.shape[1]
    kvw = ck_ref.shape[2]
    tile16 = 2 * SUBLANES

    @pl.when(pl.program_id(0) == 0)
    def _():
        bucket = bucket_ref[...]
        for hh in range(D_Q_HEADS):
            acc = jnp.full(bucket.shape, NEG, F32)
            for bkt in range(REL_BUCKETS):
                acc = jnp.where(bucket == bkt, rel_ref[bkt, hh], acc)
            acc = jnp.where(bucket == SINK_BUCKET, sinks_ref[hh], acc)
            bias_ref[hh * SUBLANES:(hh + 1) * SUBLANES, :] = acc

    sub = lax.broadcasted_iota(jnp.int32, (SUBLANES, kvw), 0)
    new_rows = sub >= SUBLANES - steps
    lane_kv = lax.broadcasted_iota(jnp.int32, (SUBLANES, kvw), 1) // D_HEAD_DIM
    gw = D_KV_HEADS * D_HEAD_DIM
    zero_keys = jnp.zeros((n_keys - win - tile16, kvw), BF16)

    def extend(cache, new8):
        new16 = jnp.concatenate([new8, jnp.zeros_like(new8)], axis=0).astype(BF16)
        return jnp.concatenate([cache.astype(BF16), new16, zero_keys], axis=0)

    def shift_in(cache, new8, out_ref, bi):
        rolled = pltpu.roll(cache, win - steps, 0)
        out_ref[bi, 0:win - SUBLANES, :] = rolled[:win - SUBLANES]
        out_ref[bi, win - SUBLANES:, :] = jnp.where(new_rows, new8, rolled[win - SUBLANES:])

    def body(bi, carry):
        ck = ck_ref[bi]
        cv = cv_ref[bi]
        k8 = knew_ref[bi]
        v8 = vnew_ref[bi]
        shift_in(ck, k8, kout_ref, bi)
        shift_in(cv, v8, vout_ref, bi)
        q8 = q_ref[bi]
        pieces = []
        for kv in range(D_KV_HEADS):
            for grp in range(D_GROUP):
                qg = q8[:, grp * gw:(grp + 1) * gw]
                pieces.append(jnp.where(lane_kv == kv, qg, 0.0))
        lhs = jnp.concatenate(pieces, axis=0).astype(BF16)
        sc = _dot_nt(lhs, extend(ck, k8)) + bias_ref[...]
        m = jnp.max(sc, axis=-1, keepdims=True)
        pexp = jnp.exp(sc - m)
        probs = (pexp / jnp.sum(pexp, axis=-1, keepdims=True)).astype(BF16)
        ov = _dot(probs, extend(cv, v8))
        outs = []
        for grp in range(D_GROUP):
            r_last = ((D_KV_HEADS - 1) * D_GROUP + grp) * SUBLANES
            acc = ov[r_last:r_last + SUBLANES]
            for kv in range(D_KV_HEADS - 2, -1, -1):
                r0 = (kv * D_GROUP + grp) * SUBLANES
                acc = jnp.where(lane_kv == kv, ov[r0:r0 + SUBLANES], acc)
            outs.append(acc)
        o_ref[bi] = jnp.concatenate(outs, axis=1)
        return carry

    lax.fori_loop(0, bb, body, 0, unroll=SAMPLE_ATTN_UNROLL)


def _odd_sample_back_kernel(hs_ref, yc_ref, o_ref_in, wout_ref, out_ref, *, steps, d, width_c, qw):
    bt = hs_ref.shape[0]
    mix = jnp.concatenate(
        [jnp.concatenate([yc_ref[:, t * width_c:(t + 1) * width_c], o_ref_in[:, t * qw:(t + 1) * qw]], axis=1)
         for t in range(steps)], axis=0).astype(BF16)
    out = _dot(mix, wout_ref[...])
    for t in range(steps):
        out_ref[:, t * d:(t + 1) * d] = hs_ref[:, t * d:(t + 1) * d] + out[t * bt:(t + 1) * bt]


def _odd_sample(hs2, state_pool2, cache_k3, cache_v3, p, steps, past_len, bt, bb):
    nb, _ = hs2.shape
    d = p["wc"].shape[0]
    width_c = p["wc"].shape[1]
    qw = p["wq"].shape[1]
    kw = p["wk"].shape[1]
    win = cache_k3.shape[1]
    n_state = max(C_WINDOWS) - 1
    assert nb % bt == 0 and nb % bb == 0 and steps <= SUBLANES and win == CHUNK
    params = pltpu.CompilerParams(dimension_semantics=("arbitrary",), vmem_limit_bytes=VMEM_LIMIT)
    vm = ["linw", "cscale", "qn", "kn", "onesbd"]
    widths = dict(yc=steps * width_c, pool=n_state * width_c, q=SUBLANES * qw, knew=SUBLANES * kw,
                  vnew=SUBLANES * kw)
    yc, new_pool, q8, knew8, vnew8 = pl.pallas_call(
        functools.partial(_odd_sample_front_kernel, steps=steps, d=d, width_c=width_c, past_len=past_len),
        out_shape=tuple(jax.ShapeDtypeStruct((nb, w), F32) for w in widths.values()),
        grid=(nb // bt,),
        in_specs=[_row_tiled(steps * d, bt), _resident((1, d))]
        + [_resident(p[k].shape) for k in ("wc", "wq", "wk", "wv")]
        + [_resident(p[k].shape) for k in vm]
        + [_row_tiled(n_state * width_c, bt)],
        out_specs=tuple(_row_tiled(w, bt) for w in widths.values()),
        compiler_params=params,
        name="odd_sample_front",
    )(hs2, p["g"], p["wc"], p["wq"], p["wk"], p["wv"], *[p[k] for k in vm], state_pool2)

    n_keys = 2 * CHUNK
    pad = SUBLANES - steps
    bucket = np.full((SUBLANES, n_keys), -1, np.int32)
    for t in range(steps):
        q_pos = past_len + t
        k_pos = np.full(n_keys, -10 ** 9, np.int64)
        k_pos[:win] = past_len - win + np.arange(win)
        k_pos[win + pad:win + SUBLANES] = past_len + np.arange(steps)
        dist = q_pos - k_pos
        ok = (dist >= 0) & (dist < CHUNK) & (k_pos >= 0)
        bucket[t] = np.where(ok, _t5_bucket(np.where(ok, dist, 0)), -1)
    bucket[:, n_keys - 1] = SINK_BUCKET
    smem = pl.BlockSpec(memory_space=pltpu.SMEM)
    tile3 = lambda rows, width: pl.BlockSpec((bb, rows, width), lambda i: (i, 0, 0))
    o8, new_k, new_v = pl.pallas_call(
        functools.partial(_odd_sample_attn_kernel, bb=bb, steps=steps, n_keys=n_keys),
        out_shape=(jax.ShapeDtypeStruct((nb, SUBLANES, qw), F32),
                   jax.ShapeDtypeStruct(cache_k3.shape, F32),
                   jax.ShapeDtypeStruct(cache_v3.shape, F32)),
        grid=(nb // bb,),
        in_specs=[tile3(SUBLANES, qw), tile3(SUBLANES, kw), tile3(SUBLANES, kw), tile3(win, kw), tile3(win, kw),
                  _resident(bucket.shape), smem, smem],
        out_specs=(tile3(SUBLANES, qw), tile3(win, kw), tile3(win, kw)),
        scratch_shapes=[pltpu.VMEM((D_Q_HEADS * SUBLANES, n_keys), F32)],
        compiler_params=params,
        name="odd_sample_attn",
    )(q8.reshape(nb, SUBLANES, qw), knew8.reshape(nb, SUBLANES, kw), vnew8.reshape(nb, SUBLANES, kw),
      cache_k3, cache_v3, jnp.asarray(bucket), p["sinks"], p["rel"])

    out = pl.pallas_call(
        functools.partial(_odd_sample_back_kernel, steps=steps, d=d, width_c=width_c, qw=qw),
        out_shape=jax.ShapeDtypeStruct(hs2.shape, F32),
        grid=(nb // bt,),
        in_specs=[_row_tiled(steps * d, bt), _row_tiled(steps * width_c, bt), _row_tiled(SUBLANES * qw, bt),
                  _resident(p["wout"].shape)],
        out_specs=_row_tiled(steps * d, bt),
        compiler_params=params,
        name="odd_sample_back",
    )(hs2, yc, o8.reshape(nb, SUBLANES * qw), p["wout"])
    return out, new_pool, new_k, new_v


def _row(v):
    return v.reshape(1, -1).astype(F32)


def _pad_lanes(m, width=LANES):
    return jnp.pad(m, ((0, 0), (0, width - m.shape[1])))


def _head_expand_matrix():
    e = np.zeros((LANES, B_HEADS * B_HEAD_DIM), np.float32)
    for hh in range(B_HEADS):
        e[hh, hh * B_HEAD_DIM:(hh + 1) * B_HEAD_DIM] = 1.0
    return np.concatenate([e, e], axis=0)


def _prep_even(mix_norm, w_in, w_out, ln_g, ln_b, w_s, b_s, conv_w, conv_b, dt_bias, a_log, d_skip, norm_g):
    width_a = ln_g.shape[0]
    inner = norm_g.shape[0]
    conv_dim = conv_b.shape[0]
    o1 = 2 * width_a
    o2 = o1 + inner
    o3 = o2 + conv_dim
    return dict(
        g=_row(mix_norm),
        wa=w_in[:, :o1].astype(BF16),
        wz=w_in[:, o1:o2].astype(BF16),
        wxbc=w_in[:, o2:o3].astype(BF16),
        wdt=_pad_lanes(w_in[:, o3:]).astype(BF16),
        wout=w_out.astype(BF16),
        lng=_row(ln_g), lnb=_row(ln_b), ws=w_s,
        bsb=jnp.broadcast_to(b_s[:, :, None], b_s.shape + (width_a // A_HEADS,)),
        convw=conv_w, convb=_row(conv_b),
        dtb=_pad_lanes(_row(dt_bias)), alog=_pad_lanes(_row(a_log)),
        dskip=_row(jnp.repeat(d_skip, B_HEAD_DIM)), normg=_row(norm_g),
        expand=jnp.asarray(_head_expand_matrix(), BF16),
    )


def _prep_odd(mix_norm, w_in, w_out, lin_w, c_scale, q_norm, k_norm, sinks, rel_table):
    d = w_in.shape[0]
    width_c = c_scale.shape[0]
    qw = D_Q_HEADS * D_HEAD_DIM
    kw = D_KV_HEADS * D_HEAD_DIM
    wq = w_in[:, width_c:width_c + qw].reshape(d, D_KV_HEADS, D_GROUP, D_HEAD_DIM)
    wq = wq.transpose(0, 2, 1, 3).reshape(d, qw)
    wo_d = w_out[width_c:].reshape(D_KV_HEADS, D_GROUP, D_HEAD_DIM, -1).transpose(1, 0, 2, 3).reshape(qw, -1)
    ones_bd = np.kron(np.eye(MXU_DIM // D_HEAD_DIM), np.ones((D_HEAD_DIM, D_HEAD_DIM))).astype(np.float32)
    return dict(
        g=_row(mix_norm),
        wc=w_in[:, :width_c].astype(BF16),
        wq=wq.astype(BF16),
        wk=w_in[:, width_c + qw:width_c + qw + kw].astype(BF16),
        wv=w_in[:, width_c + qw + kw:].astype(BF16),
        wout=jnp.concatenate([w_out[:width_c], wo_d], axis=0).astype(BF16),
        linw=lin_w.astype(BF16), cscale=_row(c_scale),
        qn=_row(jnp.tile(q_norm, D_Q_HEADS)), kn=_row(jnp.tile(k_norm, D_KV_HEADS)),
        onesbd=jnp.asarray(ones_bd, BF16),
        sinks=sinks.astype(F32), rel=rel_table.astype(F32),
    )


def _prep_even_sample(w_s, b_s, steps):
    head_w = CHUNK
    w = jnp.transpose(w_s[:, :steps, :steps], (1, 2, 0)).reshape(steps * steps, A_HEADS)
    b = b_s[:, :steps].T
    return dict(wts=jnp.repeat(w, head_w, axis=1), bts=jnp.repeat(b, head_w, axis=1))


PAST_LEN = 16384
FFN_TILE = 1024
FFN_STAGE_SLOTS = 8
FFN_STAGE_ROWS_GU = 32
FFN_STAGE_ROWS_D = 128
FFN_ROW_BLOCK = 256
FFN_BLOCK_TILES = 3
EVEN_MIXER_TILE = 512
ODD_MIXER_TILE = 512
SAMPLE_ROW_TILE = 64
SAMPLE_SEQ_TILE = 16
SAMPLE_ATTN_UNROLL = 4


def kernel(x_prompt, x_sample, state_ssm, state_conv, state_pool, cache_k_win, cache_v_win,
           ffn1_norm, ffn1_w_gu, ffn1_w_down, mix_norm, ffn2_norm, ffn2_w_gu, ffn2_w_down,
           ev_w_in, ev_w_out, a_ln_g, a_ln_b, a_w_s, a_b_s, b_conv_w, b_conv_b, b_dt_bias, b_a_log,
           b_d_skip, b_norm_g, od_w_in, od_w_out, c_lin_w, c_scale, d_q_norm, d_k_norm, d_sinks,
           rel_bias_table):
    bp, seq, d = x_prompt.shape
    bs, steps, _ = x_sample.shape
    past_len = PAST_LEN
    hp = x_prompt
    hs = x_sample
    depth = ffn1_norm.shape[0]
    names = ("a_v_s", "ssm_p", "ssm_s", "conv_p", "conv_s", "pool_p", "pool_s", "k_p", "k_s", "v_p", "v_s")
    outs = {k: [] for k in names}

    def macaron(h_p, h_s, norm, w_gu_all, w_down_all, layer):
        o_p, o_s = _ffn(h_p.reshape(bp * seq, d), h_s.reshape(bs * steps, d), _row(norm), w_gu_all, w_down_all,
                        layer, FFN_TILE)
        return o_p.reshape(bp, seq, d), o_s.reshape(bs, steps, d)

    for layer in range(depth):
        i = layer // 2
        hp, hs = macaron(hp, hs, ffn1_norm[layer], ffn1_w_gu, ffn1_w_down, layer)
        hs2 = hs.reshape(bs, steps * d)
        if layer % 2 == 0:
            p = _prep_even(mix_norm[layer], ev_w_in[i], ev_w_out[i], a_ln_g[i], a_ln_b[i], a_w_s[i], a_b_s[i],
                           b_conv_w[i], b_conv_b[i], b_dt_bias[i], b_a_log[i], b_d_skip[i], b_norm_g[i])
            p.update(_prep_even_sample(a_w_s[i], a_b_s[i], steps))
            hp, conv_p, ssm_p = _even_prompt(hp, p, EVEN_MIXER_TILE)
            hs2, v_rows, conv_s, ssm_s = _even_sample(
                hs2, state_conv[i].reshape(bs, -1), state_ssm[i].reshape(bs, B_HEADS * B_HEAD_DIM, B_STATE),
                p, steps, SAMPLE_ROW_TILE, SAMPLE_SEQ_TILE)
            outs["a_v_s"].append(v_rows.reshape(bs, steps, -1))
            outs["conv_p"].append(conv_p)
            outs["conv_s"].append(conv_s.reshape(state_conv[i].shape))
            outs["ssm_p"].append(ssm_p.reshape(bp, B_HEADS, B_HEAD_DIM, B_STATE))
            outs["ssm_s"].append(ssm_s.reshape(state_ssm[i].shape))
        else:
            p = _prep_odd(mix_norm[layer], od_w_in[i], od_w_out[i], c_lin_w[i], c_scale[i], d_q_norm[i],
                          d_k_norm[i], d_sinks[i], rel_bias_table)
            hp, pool_p, k_p, v_p = _odd_prompt(hp, p, ODD_MIXER_TILE)
            kv_shape = cache_k_win[i].shape
            hs2, pool_s, k_s, v_s = _odd_sample(
                hs2, state_pool[i].reshape(bs, -1), cache_k_win[i].reshape(bs, kv_shape[1], -1),
                cache_v_win[i].reshape(bs, kv_shape[1], -1), p, steps, past_len, SAMPLE_ROW_TILE, SAMPLE_SEQ_TILE)
            outs["pool_p"].append(pool_p)
            outs["pool_s"].append(pool_s.reshape(state_pool[i].shape))
            outs["k_p"].append(k_p.reshape(bp, CHUNK, D_KV_HEADS, D_HEAD_DIM))
            outs["v_p"].append(v_p.reshape(bp, CHUNK, D_KV_HEADS, D_HEAD_DIM))
            outs["k_s"].append(k_s.reshape(kv_shape))
            outs["v_s"].append(v_s.reshape(kv_shape))
        hs = hs2.reshape(bs, steps, d)
        hp, hs = macaron(hp, hs, ffn2_norm[layer], ffn2_w_gu, ffn2_w_down, layer)
    return (hp, hs) + tuple(jnp.stack(outs[k]) for k in names)
```

```python
import functools
import math

import numpy as np
import jax
import jax.numpy as jnp
from jax import lax
from jax.experimental import pallas as pl
from jax.experimental.pallas import tpu as pltpu

F32 = jnp.float32
BF16 = jnp.bfloat16

EPS = 1e-6
NEG = -1e30

LANES = 128
SUBLANES = 8
MXU_DIM = 256
VMEM_BYTES_V7X = 64 * 1024 * 1024
VMEM_LIMIT = VMEM_BYTES_V7X - 8 * 1024 * 1024
A_HEADS = 8
B_HEADS = 16
B_HEAD_DIM = 64
B_GROUPS = 2
B_STATE = 128
B_CONV = 4
CHUNK = 128
C_WINDOWS = (2, 4, 8, 16)
C_HALO = 16
D_Q_HEADS = 16
D_KV_HEADS = 4
D_HEAD_DIM = 64
D_GROUP = D_Q_HEADS // D_KV_HEADS
REL_BUCKETS = 32
REL_MAX_DIST = 128


def _rms(x, g):
    ms = jnp.mean(x * x, axis=-1, keepdims=True)
    return x * lax.rsqrt(ms + EPS) * g


def _sigmoid(x):
    return 1.0 / (1.0 + jnp.exp(-x))


def _silu(x):
    return x * _sigmoid(x)


def _gelu_tanh(x):
    c = math.sqrt(2.0 / math.pi)
    return x * (0.5 * (1.0 + jnp.tanh(c * (x + 0.044715 * (x * x * x)))))


def _softplus(x):
    return jnp.maximum(x, 0.0) + jnp.log1p(jnp.exp(-jnp.abs(x)))


def _split3(x):
    hi = x.astype(BF16)
    r1 = x - hi.astype(F32)
    mid = r1.astype(BF16)
    lo = (r1 - mid.astype(F32)).astype(BF16)
    return hi, mid, lo


def _split2_lanes(x):
    hi = x.astype(BF16)
    lo = (x - hi.astype(F32)).astype(BF16)
    return jnp.concatenate([hi, lo], axis=1)


def _dot(a, b):
    return jnp.dot(a, b, preferred_element_type=F32)


def _dot_nt(a, b):
    return lax.dot_general(a, b, (((1,), (1,)), ((), ())), preferred_element_type=F32)


def _dot_tn(a, b):
    return lax.dot_general(a, b, (((0,), (0,)), ((), ())), preferred_element_type=F32)


def _expand_heads(m, n_pairs):
    rows = m.shape[0]
    lane = lax.broadcasted_iota(jnp.int32, (rows, LANES), 1)
    first = lane < B_HEAD_DIM
    parts = []
    for p in range(n_pairs):
        a = jnp.broadcast_to(m[:, 2 * p:2 * p + 1], (rows, LANES))
        b = jnp.broadcast_to(m[:, 2 * p + 1:2 * p + 2], (rows, LANES))
        parts.append(jnp.where(first, a, b))
    return jnp.concatenate(parts, axis=1)


def _head_sumsq(x, ones_bd):
    xx = (x * x).astype(BF16)
    blk = ones_bd.shape[0]
    outs = [_dot(xx[:, c * blk:(c + 1) * blk], ones_bd) for c in range(x.shape[1] // blk)]
    return jnp.concatenate(outs, axis=1) if len(outs) > 1 else outs[0]


def _ff_blocks(d_ff):
    step = FFN_BLOCK_TILES * MXU_DIM
    return [(c0, min(c0 + step, d_ff)) for c0 in range(0, d_ff, step)]


def _load_cast_rows(src_hbm, dst_ref, stage_ref, sem_ref):
    slots, rows, _ = stage_ref.shape
    n = src_hbm.shape[0] // rows

    def copy(c):
        slot = c % slots
        return pltpu.make_async_copy(src_hbm.at[pl.ds(c * rows, rows), :], stage_ref.at[slot], sem_ref.at[slot])

    for c in range(min(slots, n)):
        copy(c).start()
    for c in range(n):
        copy(c).wait()
        dst_ref[c * rows:(c + 1) * rows, :] = stage_ref[c % slots].astype(BF16)
        if c + slots < n:
            copy(c + slots).start()


def _ffn_rows(x_ref, g_ref, wgu_ref, wd_ref, o_ref, d_ff):
    tm = x_ref.shape[0]
    rows = [slice(r0, min(r0 + FFN_ROW_BLOCK, tm)) for r0 in range(0, tm, FFN_ROW_BLOCK)]
    xns = [_rms(x_ref[rs, :], g_ref[...]).astype(BF16) for rs in rows]
    for rs, xn in zip(rows, xns):
        y = None
        for c0, c1 in _ff_blocks(d_ff):
            gate = _dot(xn, wgu_ref[:, c0:c1])
            up = _dot(xn, wgu_ref[:, d_ff + c0:d_ff + c1])
            act = (_silu(gate) * up).astype(BF16)
            part = _dot(act, wd_ref[c0:c1, :])
            y = part if y is None else y + part
        o_ref[rs, :] = x_ref[rs, :] + 0.5 * y


def _ffn_kernel(xp_ref, xs_ref, g_ref, wgu_hbm, wd_hbm, op_ref, os_ref,
                wgu_ref, wd_ref, stage_gu_ref, stage_d_ref, sem_ref, *, d_ff, prompt_steps, layer):
    i = pl.program_id(0)

    @pl.when(i == 0)
    def _():
        _load_cast_rows(wgu_hbm.at[layer], wgu_ref, stage_gu_ref, sem_ref)
        _load_cast_rows(wd_hbm.at[layer], wd_ref, stage_d_ref, sem_ref)

    @pl.when(i < prompt_steps)
    def _():
        _ffn_rows(xp_ref, g_ref, wgu_ref, wd_ref, op_ref, d_ff)

    @pl.when(i == prompt_steps)
    def _():
        _ffn_rows(xs_ref, g_ref, wgu_ref, wd_ref, os_ref, d_ff)


def _resident(shape):
    nd = len(shape)
    return pl.BlockSpec(shape, lambda *_: (0,) * nd, pipeline_mode=pl.Buffered(1))


def _ffn(xp2d, xs2d, g, wgu_all, wd_all, layer, tm):
    m, d = xp2d.shape
    ms = xs2d.shape[0]
    wgu_shape, wd_shape = wgu_all.shape[1:], wd_all.shape[1:]
    d_ff = wd_shape[0]
    assert m % tm == 0 and d % FFN_STAGE_ROWS_GU == 0 and d_ff % FFN_STAGE_ROWS_D == 0
    steps = m // tm
    last = steps - 1
    whole = lambda shape: pl.BlockSpec(shape, lambda i: (0, 0))
    prompt_tile = pl.BlockSpec((tm, d), lambda i: (jnp.minimum(i, last), 0))
    hbm = pl.BlockSpec(memory_space=pl.ANY)
    return pl.pallas_call(
        functools.partial(_ffn_kernel, d_ff=d_ff, prompt_steps=steps, layer=layer),
        out_shape=(jax.ShapeDtypeStruct((m, d), F32), jax.ShapeDtypeStruct((ms, d), F32)),
        grid=(steps + 1,),
        in_specs=[prompt_tile, whole((ms, d)), _resident((1, d)), hbm, hbm],
        out_specs=(prompt_tile, whole((ms, d))),
        scratch_shapes=[pltpu.VMEM(wgu_shape, BF16), pltpu.VMEM(wd_shape, BF16),
                        pltpu.VMEM((FFN_STAGE_SLOTS, FFN_STAGE_ROWS_GU, wgu_shape[1]), F32),
                        pltpu.VMEM((FFN_STAGE_SLOTS, FFN_STAGE_ROWS_D, wd_shape[1]), F32),
                        pltpu.SemaphoreType.DMA((FFN_STAGE_SLOTS,))],
        compiler_params=pltpu.CompilerParams(
            dimension_semantics=("arbitrary",), vmem_limit_bytes=VMEM_LIMIT),
        name="ffn",
    )(xp2d, xs2d, g, wgu_all, wd_all)


def _zip_stages(first, second, lag=1):
    for i in range(max(len(first), len(second) + lag)):
        if i < len(first):
            first[i]()
        if 0 <= i - lag < len(second):
            second[i - lag]()


def _even_prompt_kernel(h_ref, g_ref, wa_ref, wz_ref, wxbc_ref, wdt_ref, wout_ref,
                        lng_ref, lnb_ref, ws_ref, bsb_ref, convw_ref, convb_ref,
                        dtb_ref, alog_ref, dskip_ref, normg_ref, expand_ref,
                        o_ref, conv_out_ref, ssm_out_ref,
                        ext_ref, st_ref, *, tile, width_a, inner):
    s = pl.program_id(1)
    n_chunks = tile // CHUNK
    halo = SUBLANES

    @pl.when(s == 0)
    def _():
        ext_ref[0:halo, :] = jnp.zeros((halo, ext_ref.shape[1]), F32)
        st_ref[...] = jnp.zeros(st_ref.shape, F32)

    x = h_ref[...]
    xn = _rms(x, g_ref[...]).astype(BF16)

    row = lax.broadcasted_iota(jnp.int32, (CHUNK, CHUNK), 0)
    col = lax.broadcasted_iota(jnp.int32, (CHUNK, CHUNK), 1)
    causal = row >= col
    lane = lax.broadcasted_iota(jnp.int32, (CHUNK, LANES), 1)
    first_half = lane < B_HEAD_DIM

    blk = MXU_DIM
    col_blocks = lambda width: [slice(j * blk, (j + 1) * blk) for j in range(width // blk)]
    head_w = width_a // A_HEADS
    gn = B_GROUPS * B_STATE
    conv_dim = inner + 2 * gn
    n_pairs = B_HEADS // 2
    heads_per_group = B_HEADS // B_GROUPS
    gw = heads_per_group * B_HEAD_DIM
    chunks = [slice(c * CHUNK, (c + 1) * CHUNK) for c in range(n_chunks)]

    ga = {}
    a_cols = [slice(width_a + sl.start, width_a + sl.stop) for sl in col_blocks(width_a)] + col_blocks(width_a)
    pa = {}

    def proj_a(j):
        pa[j] = _dot(xn, wa_ref[:, a_cols[j]])

    def act_a(j):
        ga[j] = _gelu_tanh(pa.pop(j))

    n_a = len(a_cols)
    _zip_stages([functools.partial(proj_a, j) for j in range(n_a)],
                [functools.partial(act_a, j) for j in range(n_a)])
    nv = n_a // 2
    v = jnp.concatenate([ga[j] for j in range(nv)], axis=1)
    u_blocks = [ga[j] for j in range(nv, n_a)]

    raw = {}
    xbc_blocks = {}
    vb_box = []

    def proj_xbc(k):
        raw[k] = _dot(xn, wxbc_ref[:, k * blk:(k + 1) * blk])

    def layer_norm_v():
        mu = jnp.mean(v, axis=-1, keepdims=True)
        vc = v - mu
        var = jnp.mean(vc * vc, axis=-1, keepdims=True)
        vb_box.append((vc * lax.rsqrt(var + EPS) * lng_ref[...] + lnb_ref[...]).astype(BF16))

    def conv_block(k):
        cs = slice(k * blk, (k + 1) * blk)
        ext_ref[halo:halo + tile, cs] = raw.pop(k)
        ext = ext_ref[:, cs]
        ext1 = pltpu.roll(ext, 1, 0)
        pair = ext * convw_ref[1:2, cs] + ext1 * convw_ref[0:1, cs]
        conv = (convb_ref[:, cs] + ext * convw_ref[3:4, cs] + ext1 * convw_ref[2:3, cs]
                + pltpu.roll(pair, 2, 0))[halo:]
        tail = ext_ref[tile:tile + halo, cs]
        ext_ref[0:halo, cs] = tail
        conv_out_ref[:, cs] = tail
        xbc_blocks[k] = _silu(conv)

    n_x = conv_dim // blk
    _zip_stages([functools.partial(proj_xbc, k) for k in range(n_x)],
                [layer_norm_v] + [functools.partial(conv_block, k) for k in range(n_x)])
    vb = vb_box[0]

    z_blocks = {}
    gate_cols = {}

    def proj_z(j):
        z_blocks[j] = _dot(xn, wz_ref[:, j * blk:(j + 1) * blk])

    def mix_head(hh):
        w = jnp.where(causal, ws_ref[hh], 0.0).astype(BF16)
        rhs = jnp.concatenate([vb[rs, hh * head_w:(hh + 1) * head_w] for rs in chunks], axis=1)
        out = _dot(w, rhs)
        bias = bsb_ref[hh]
        gate_cols[hh] = jnp.concatenate(
            [out[:, c * head_w:(c + 1) * head_w] + bias for c in range(n_chunks)], axis=0)

    dt_raw = _dot(xn, wdt_ref[...])
    n_z = inner // blk
    heads_per_z = A_HEADS // n_z
    for j in range(n_z):
        proj_z(j)
        for hh in range(j * heads_per_z, (j + 1) * heads_per_z):
            mix_head(hh)
    heads_per_blk = blk // head_w
    ya_blocks = [(u_blocks[j] * jnp.concatenate(
        [gate_cols[j * heads_per_blk + i] for i in range(heads_per_blk)], axis=1)).astype(BF16)
        for j in range(len(u_blocks))]
    ya = jnp.concatenate(ya_blocks, axis=1)
    xs = jnp.concatenate([xbc_blocks[k] for k in range(inner // blk)], axis=1)
    bm = xbc_blocks[inner // blk]
    cm = xbc_blocks[inner // blk + 1]

    dt = _softplus(dt_raw + dtb_ref[...])
    a_neg = -jnp.exp(alog_ref[...])
    da = dt * a_neg
    tril_ones = jnp.where(causal, 1.0, 0.0).astype(BF16)
    acums = []
    for rs in chunks:
        d_hi, d_mid, d_lo = _split3(da[rs])
        acums.append(_dot(tril_ones, d_hi) + _dot(tril_ones, d_mid) + _dot(tril_ones, d_lo))
    acum = jnp.concatenate(acums, axis=0)
    decay = jnp.concatenate([jnp.exp(a[CHUNK - 1:CHUNK, :] - a) for a in acums], axis=0)
    expand = expand_ref[...]
    xd = xs * _dot(_split2_lanes(dt), expand)
    xdwb = (xs * _dot(_split2_lanes(dt * decay), expand)).astype(BF16)
    e_acum = _dot(_split2_lanes(jnp.exp(acum)), expand)
    bmb = bm.astype(BF16)
    cmb = cm.astype(BF16)

    y_rows = [None] * n_chunks
    out_a = {}

    def scan_chunk(c):
        rs = chunks[c]
        a_c = acums[c]
        acum_t = a_c.T
        cb = [_dot_nt(cmb[rs, g * B_STATE:(g + 1) * B_STATE], bmb[rs, g * B_STATE:(g + 1) * B_STATE])
              for g in range(B_GROUPS)]
        y_parts = []
        for p in range(n_pairs):
            g = (2 * p) // heads_per_group
            ms = []
            for hh in (2 * p, 2 * p + 1):
                seg = jnp.broadcast_to(a_c[:, hh:hh + 1], (CHUNK, CHUNK)) - \
                    jnp.broadcast_to(acum_t[hh:hh + 1, :], (CHUNK, CHUNK))
                lmat = jnp.where(causal, jnp.exp(seg), 0.0)
                ms.append((cb[g] * lmat).astype(BF16))
            lhs = jnp.concatenate(ms, axis=1)
            xd_p = xd[rs, p * LANES:(p + 1) * LANES]
            rhs = jnp.concatenate([jnp.where(first_half, xd_p, 0.0),
                                   jnp.where(first_half, 0.0, xd_p)], axis=0).astype(BF16)
            y_parts.append(_dot(lhs, rhs))
        y_rows[c] = jnp.concatenate(y_parts, axis=1)

    def proj_out_a(j):
        out_a[j] = _dot(ya, wout_ref[0:width_a, j * blk:(j + 1) * blk])

    n_o = o_ref.shape[1] // blk
    _zip_stages([functools.partial(scan_chunk, c) for c in range(n_chunks)],
                [functools.partial(proj_out_a, j) for j in range(n_o)], lag=0)

    for c, rs in enumerate(chunks):
        st_prev = st_ref[...]
        stb = st_prev.astype(BF16)
        y_off = jnp.concatenate(
            [_dot(cmb[rs, g * B_STATE:(g + 1) * B_STATE], stb[:, g * gw:(g + 1) * gw]) for g in range(B_GROUPS)],
            axis=1)
        st_add = jnp.concatenate(
            [_dot_tn(bmb[rs, g * B_STATE:(g + 1) * B_STATE], xdwb[rs, g * gw:(g + 1) * gw])
             for g in range(B_GROUPS)], axis=1)
        chunk_decay = e_acum[(c + 1) * CHUNK - 1:(c + 1) * CHUNK, :]
        st_ref[...] = st_prev * chunk_decay + st_add
        y_rows[c] = y_rows[c] + y_off * e_acum[rs]
    y = (jnp.concatenate(y_rows, axis=0) if n_chunks > 1 else y_rows[0]) + xs * dskip_ref[...]
    z = jnp.concatenate([z_blocks[j] for j in range(n_z)], axis=1)
    y = y * _silu(z)
    half = inner // B_GROUPS
    yn = []
    for g in range(B_GROUPS):
        yg = y[:, g * half:(g + 1) * half]
        yn.append(yg * lax.rsqrt(jnp.mean(yg * yg, axis=-1, keepdims=True) + EPS))
    yb = (jnp.concatenate(yn, axis=1) * normg_ref[...]).astype(BF16)

    for j in range(n_o):
        cs = slice(j * blk, (j + 1) * blk)
        o_ref[:, cs] = x[:, cs] + out_a[j] + _dot(yb, wout_ref[width_a:, cs])

    @pl.when(s == pl.num_programs(1) - 1)
    def _():
        ssm_out_ref[...] = st_ref[...].T


def _even_prompt(h, p, tile):
    b, seq, d = h.shape
    width_a = p["wa"].shape[1] // 2
    inner = p["wz"].shape[1]
    conv_dim = p["wxbc"].shape[1]
    assert seq % tile == 0 and tile % CHUNK == 0
    small = ["lng", "lnb", "ws", "bsb", "convw", "convb", "dtb", "alog", "dskip", "normg", "expand"]
    out, conv_tail, ssm = pl.pallas_call(
        functools.partial(_even_prompt_kernel, tile=tile, width_a=width_a, inner=inner),
        out_shape=(jax.ShapeDtypeStruct((b, seq, d), F32),
                   jax.ShapeDtypeStruct((b, SUBLANES, conv_dim), F32),
                   jax.ShapeDtypeStruct((b, inner, B_STATE), F32)),
        grid=(b, seq // tile),
        in_specs=[pl.BlockSpec((None, tile, d), lambda i, j: (i, j, 0)),
                  _resident((1, d))]
        + [_resident(p[k].shape) for k in ("wa", "wz", "wxbc", "wdt", "wout")]
        + [_resident(p[k].shape) for k in small],
        out_specs=(pl.BlockSpec((None, tile, d), lambda i, j: (i, j, 0)),
                   pl.BlockSpec((None, SUBLANES, conv_dim), lambda i, j: (i, 0, 0)),
                   pl.BlockSpec((None, inner, B_STATE), lambda i, j: (i, 0, 0))),
        scratch_shapes=[pltpu.VMEM((tile + SUBLANES, conv_dim), F32),
                        pltpu.VMEM((B_STATE, inner), F32)],
        compiler_params=pltpu.CompilerParams(
            dimension_semantics=("arbitrary", "arbitrary"), vmem_limit_bytes=VMEM_LIMIT),
        name="even_prompt",
    )(h, p["g"], p["wa"], p["wz"], p["wxbc"], p["wdt"], p["wout"], *[p[k] for k in small])
    return out, conv_tail[:, SUBLANES - (B_CONV - 1):, :], ssm


def _t5_bucket(dist):
    n = np.maximum(dist, 0)
    max_exact = REL_BUCKETS // 2
    n_safe = np.maximum(n, 1).astype(np.float32)
    scale = np.float32((REL_BUCKETS - max_exact) / math.log(REL_MAX_DIST / max_exact))
    large = max_exact + (np.log(n_safe / max_exact) * scale).astype(np.int32)
    large = np.minimum(large, REL_BUCKETS - 1)
    return np.where(n < max_exact, n, large).astype(np.int32)


def _fill_bias(bias_ref, bucket_ref, rel_ref):
    bucket = bucket_ref[...]
    lq = bucket.shape[0]
    has_prev = lax.broadcasted_iota(jnp.int32, bucket.shape, 1) >= CHUNK
    for hh in range(D_Q_HEADS):
        kv, grp = divmod(hh, D_GROUP)
        acc = jnp.full(bucket.shape, NEG, F32)
        for bkt in range(REL_BUCKETS):
            acc = jnp.where(bucket == bkt, rel_ref[bkt, hh], acc)
        bias_ref[0, grp, kv * lq:(kv + 1) * lq, :] = acc
        bias_ref[1, grp, kv * lq:(kv + 1) * lq, :] = jnp.where(has_prev, acc, NEG)


def _kv_lane_ids(rows):
    return lax.broadcasted_iota(jnp.int32, (rows, D_KV_HEADS * D_HEAD_DIM), 1) // D_HEAD_DIM


def _attn_scores(qg, kk):
    lane_kv = _kv_lane_ids(qg.shape[0])
    zero = jnp.zeros_like(qg)
    lhs = jnp.concatenate([jnp.where(lane_kv == kv, qg, zero) for kv in range(D_KV_HEADS)], axis=0)
    return _dot_nt(lhs, kk)


def _attn_probs(sc, bias_ref, sinks_ref, grp, table):
    lq = sc.shape[0] // D_KV_HEADS
    probs = []
    for kv in range(D_KV_HEADS):
        rs = slice(kv * lq, (kv + 1) * lq)
        s_h = sc[rs] + bias_ref[table, grp, rs, :]
        sink = sinks_ref[kv * D_GROUP + grp]
        m = jnp.maximum(jnp.max(s_h, axis=-1, keepdims=True), sink)
        pexp = jnp.exp(s_h - m)
        denom = jnp.sum(pexp, axis=-1, keepdims=True) + jnp.exp(sink - m)
        probs.append((pexp / denom).astype(BF16))
    return jnp.concatenate(probs, axis=0)


def _attn_out(probs, vv):
    lq = probs.shape[0] // D_KV_HEADS
    lane_kv = _kv_lane_ids(lq)
    ov = _dot(probs, vv)
    out = ov[(D_KV_HEADS - 1) * lq:]
    for kv in range(D_KV_HEADS - 2, -1, -1):
        out = jnp.where(lane_kv == kv, ov[kv * lq:(kv + 1) * lq], out)
    return out


def _odd_prompt_kernel(h_ref, g_ref, wc_ref, wq_ref, wk_ref, wv_ref, wout_ref,
                       linw_ref, cscale_ref, qn_ref, kn_ref, onesbd_ref, bucket_ref,
                       sinks_ref, rel_ref,
                       o_ref, pool_out_ref, k_out_ref, v_out_ref,
                       extc_ref, kprev_ref, vprev_ref, bias_ref, *, tile, width_c):
    b = pl.program_id(0)
    s = pl.program_id(1)
    n_blocks = tile // CHUNK

    @pl.when((b == 0) & (s == 0))
    def _():
        _fill_bias(bias_ref, bucket_ref, rel_ref)

    @pl.when(s == 0)
    def _():
        extc_ref[0:C_HALO, :] = jnp.zeros((C_HALO, width_c), F32)
        kprev_ref[...] = jnp.zeros(kprev_ref.shape, F32)
        vprev_ref[...] = jnp.zeros(vprev_ref.shape, F32)

    x = h_ref[...]
    xn = _rms(x, g_ref[...]).astype(BF16)

    c_in = _dot(xn, wc_ref[...])
    q = _dot(xn, wq_ref[...])
    k = _dot(xn, wk_ref[...])
    v = _dot(xn, wv_ref[...])

    extc_ref[C_HALO:C_HALO + tile, :] = c_in
    e = extc_ref[...]
    tail = extc_ref[tile:tile + C_HALO, :]
    extc_ref[0:C_HALO, :] = tail
    pool_out_ref[...] = tail
    pos = (s * tile + lax.broadcasted_iota(jnp.int32, (tile, 1), 0) + 1).astype(F32)
    gdim = width_c // len(C_WINDOWS)
    run = e
    shift = 1
    yc = []
    for gi, win in enumerate(C_WINDOWS):
        while shift < win:
            run = run + pltpu.roll(run, shift, 0)
            shift *= 2
        cnt = jnp.minimum(pos, float(win))
        pooled = run[C_HALO:, :gdim] / cnt - c_in[:, gi * gdim:(gi + 1) * gdim]
        yc.append(_dot(pooled.astype(BF16), linw_ref[gi]))
        if gi + 1 < len(C_WINDOWS):
            run = run[:, gdim:]
    ycb = (jnp.concatenate(yc, axis=1) * cscale_ref[...]).astype(BF16)

    ones_bd = onesbd_ref[...]
    inv_d = 1.0 / D_HEAD_DIM
    qn = q * lax.rsqrt(_head_sumsq(q, ones_bd) * inv_d + EPS) * qn_ref[...]
    kn = k * lax.rsqrt(_head_sumsq(k, ones_bd) * inv_d + EPS) * kn_ref[...]
    qs = (qn * (D_HEAD_DIM ** -0.5)).astype(BF16)
    kb = kn.astype(BF16)
    vb = v.astype(BF16)
    first_table = jnp.where(s == 0, 1, 0)
    gw = D_KV_HEADS * D_HEAD_DIM
    keys, vals = [], []
    for blk in range(n_blocks):
        rs = slice(blk * CHUNK, (blk + 1) * CHUNK)
        if blk == 0:
            k_prev, v_prev = kprev_ref[...].astype(BF16), vprev_ref[...].astype(BF16)
        else:
            k_prev, v_prev = kb[(blk - 1) * CHUNK:blk * CHUNK], vb[(blk - 1) * CHUNK:blk * CHUNK]
        keys.append(jnp.concatenate([k_prev, kb[rs]], axis=0))
        vals.append(jnp.concatenate([v_prev, vb[rs]], axis=0))
    kprev_ref[...] = kn[tile - CHUNK:]
    vprev_ref[...] = v[tile - CHUNK:]
    items = [(blk, grp) for blk in range(n_blocks) for grp in range(D_GROUP)]
    n_o = o_ref.shape[1] // MXU_DIM
    out_c, sc, pr, og = {}, {}, {}, {}
    for i in range(len(items) + 2):
        if i < len(items):
            blk, grp = items[i]
            sc[i] = _attn_scores(qs[blk * CHUNK:(blk + 1) * CHUNK, grp * gw:(grp + 1) * gw], keys[blk])
        if 0 <= i - 1 < len(items):
            blk, grp = items[i - 1]
            pr[i - 1] = _attn_probs(sc.pop(i - 1), bias_ref, sinks_ref, grp, first_table if blk == 0 else 0)
        if i < n_o:
            out_c[i] = _dot(ycb, wout_ref[0:width_c, i * MXU_DIM:(i + 1) * MXU_DIM])
        if 0 <= i - 2 < len(items):
            blk, grp = items[i - 2]
            og[i - 2] = _attn_out(pr.pop(i - 2), vals[blk])
    yd = jnp.concatenate(
        [jnp.concatenate([og[blk * D_GROUP + grp] for grp in range(D_GROUP)], axis=1) for blk in range(n_blocks)],
        axis=0).astype(BF16)
    for j in range(n_o):
        cs = slice(j * MXU_DIM, (j + 1) * MXU_DIM)
        o_ref[:, cs] = x[:, cs] + out_c[j] + _dot(yd, wout_ref[width_c:, cs])

    @pl.when(s == pl.num_programs(1) - 1)
    def _():
        k_out_ref[...] = kn[tile - CHUNK:]
        v_out_ref[...] = v[tile - CHUNK:]


def _odd_prompt(h, p, tile):
    b, seq, d = h.shape
    width_c = p["wc"].shape[1]
    kvw = p["wk"].shape[1]
    assert seq % tile == 0 and tile % CHUNK == 0
    r = np.arange(CHUNK) + CHUNK
    c = np.arange(2 * CHUNK)
    dist = r[:, None] - c[None, :]
    bucket = np.where((dist >= 0) & (dist < CHUNK), _t5_bucket(dist), -1).astype(np.int32)
    vm = ["linw", "cscale", "qn", "kn", "onesbd"]
    smem = pl.BlockSpec(memory_space=pltpu.SMEM)
    out, pool_tail, k_win, v_win = pl.pallas_call(
        functools.partial(_odd_prompt_kernel, tile=tile, width_c=width_c),
        out_shape=(jax.ShapeDtypeStruct((b, seq, d), F32),
                   jax.ShapeDtypeStruct((b, C_HALO, width_c), F32),
                   jax.ShapeDtypeStruct((b, CHUNK, kvw), F32),
                   jax.ShapeDtypeStruct((b, CHUNK, kvw), F32)),
        grid=(b, seq // tile),
        in_specs=[pl.BlockSpec((None, tile, d), lambda i, j: (i, j, 0)),
                  _resident((1, d))]
        + [_resident(p[k].shape) for k in ("wc", "wq", "wk", "wv", "wout")]
        + [_resident(p[k].shape) for k in vm]
        + [_resident(bucket.shape), smem, smem],
        out_specs=(pl.BlockSpec((None, tile, d), lambda i, j: (i, j, 0)),
                   pl.BlockSpec((None, C_HALO, width_c), lambda i, j: (i, 0, 0)),
                   pl.BlockSpec((None, CHUNK, kvw), lambda i, j: (i, 0, 0)),
                   pl.BlockSpec((None, CHUNK, kvw), lambda i, j: (i, 0, 0))),
        scratch_shapes=[pltpu.VMEM((tile + C_HALO, width_c), F32),
                        pltpu.VMEM((CHUNK, kvw), F32),
                        pltpu.VMEM((CHUNK, kvw), F32),
                        pltpu.VMEM((2, D_GROUP, D_KV_HEADS * CHUNK, 2 * CHUNK), F32)],
        compiler_params=pltpu.CompilerParams(
            dimension_semantics=("arbitrary", "arbitrary"), vmem_limit_bytes=VMEM_LIMIT),
        name="odd_prompt",
    )(h, p["g"], p["wc"], p["wq"], p["wk"], p["wv"], p["wout"], *[p[k] for k in vm],
      jnp.asarray(bucket), p["sinks"], p["rel"])
    return out, pool_tail[:, C_HALO - (max(C_WINDOWS) - 1):, :], k_win, v_win


def _steps(x, n, width):
    return [x[:, t * width:(t + 1) * width] for t in range(n)]


def _stack_steps(ref, n, width):
    x = ref[...]
    return jnp.concatenate(_steps(x, n, width), axis=0)


def _even_sample_front_kernel(hs_ref, g_ref, wa_ref, wz_ref, wxbc_ref, wdt_ref,
                              lng_ref, lnb_ref, wts_ref, bts_ref, convw_ref, convb_ref,
                              dtb_ref, alog_ref, dskip_ref, cs_ref,
                              v_out, ya_out, ypart_out, eacum_out, z_out, conv_out,
                              cgt_out, xdw_out, bs_out, dec_out, *, steps, d, width_a, inner, conv_dim):
    bt = hs_ref.shape[0]
    xn = _rms(_stack_steps(hs_ref, steps, d), g_ref[...]).astype(BF16)
    blk = lambda a, t: a[t * bt:(t + 1) * bt]

    ga = _gelu_tanh(_dot(xn, wa_ref[...]))
    u = ga[:, :width_a]
    v = ga[:, width_a:]
    mu = jnp.mean(v, axis=-1, keepdims=True)
    vc = v - mu
    var = jnp.mean(vc * vc, axis=-1, keepdims=True)
    v = vc * lax.rsqrt(var + EPS) * lng_ref[...] + lnb_ref[...]
    for t in range(steps):
        v_out[:, t * width_a:(t + 1) * width_a] = blk(v, t)
        gate = bts_ref[t:t + 1, :]
        for s in range(t + 1):
            gate = gate + wts_ref[t * steps + s:t * steps + s + 1, :] * blk(v, s)
        ya_out[:, t * width_a:(t + 1) * width_a] = blk(u, t) * gate

    z = _dot(xn, wz_ref[...])
    for t in range(steps):
        z_out[:, t * inner:(t + 1) * inner] = blk(z, t)
    raw = _dot(xn, wxbc_ref[...])
    dt = _softplus(_dot(xn, wdt_ref[...]) + dtb_ref[...])
    ext = _steps(cs_ref[...], B_CONV - 1, conv_dim) + [blk(raw, t) for t in range(steps)]
    for k in range(B_CONV - 1):
        conv_out[:, k * conv_dim:(k + 1) * conv_dim] = ext[len(ext) - (B_CONV - 1) + k]
    gn = B_GROUPS * B_STATE
    n_pairs = B_HEADS // 2
    a_neg = -jnp.exp(alog_ref[...])
    xs, bm, cm, dts, acum = [], [], [], [], []
    for t in range(steps):
        conv = convb_ref[...]
        for tap in range(B_CONV):
            conv = conv + ext[t + tap] * convw_ref[tap:tap + 1, :]
        xbc = _silu(conv)
        xs.append(xbc[:, :inner])
        bm.append(xbc[:, inner:inner + gn])
        cm.append(xbc[:, inner + gn:])
        dts.append(blk(dt, t))
        da = dts[t] * a_neg
        acum.append(da if t == 0 else acum[t - 1] + da)
    lane = lax.broadcasted_iota(jnp.int32, (bt, LANES), 1)
    group0 = lane < (B_HEADS // B_GROUPS)
    dec_out[...] = jnp.exp(acum[steps - 1])
    pad_rows = SUBLANES - steps
    xdw_out[:, steps * inner:] = jnp.zeros((bt, pad_rows * inner), F32)
    bs_out[:, steps * gn:] = jnp.zeros((bt, pad_rows * gn), F32)
    xd = []
    for t in range(steps):
        xd.append(xs[t] * _expand_heads(dts[t], n_pairs))
        eacum_out[:, t * inner:(t + 1) * inner] = _expand_heads(jnp.exp(acum[t]), n_pairs)
        xdw_out[:, t * inner:(t + 1) * inner] = xs[t] * _expand_heads(
            dts[t] * jnp.exp(acum[steps - 1] - acum[t]), n_pairs)
        bs_out[:, t * gn:(t + 1) * gn] = bm[t]
        for g in range(B_GROUPS):
            r = g * steps + t
            cgt_out[:, r * B_STATE:(r + 1) * B_STATE] = cm[t][:, g * B_STATE:(g + 1) * B_STATE]
    for t in range(steps):
        y = xs[t] * dskip_ref[...]
        for s in range(t + 1):
            cb = [jnp.sum(cm[t][:, g * B_STATE:(g + 1) * B_STATE] * bm[s][:, g * B_STATE:(g + 1) * B_STATE],
                          axis=-1, keepdims=True) for g in range(B_GROUPS)]
            coef = jnp.where(group0, cb[0], cb[1]) * jnp.exp(acum[t] - acum[s])
            y = y + _expand_heads(coef, n_pairs) * xd[s]
        ypart_out[:, t * inner:(t + 1) * inner] = y


def _even_sample_state_kernel(s0_ref, cgt_ref, xdw_ref, bs_ref, dec_ref, yoff_ref, snew_ref, *, bb):
    step = pl.program_id(0)
    gw = (B_HEADS // B_GROUPS) * B_HEAD_DIM

    def body(bi, carry):
        s0 = s0_ref[bi]
        c8 = cgt_ref[bi]
        c16 = jnp.concatenate([c8, jnp.zeros_like(c8)], axis=0).astype(BF16)
        yoff_ref[bi] = _dot_nt(c16, s0.astype(BF16))[:SUBLANES]
        x8 = xdw_ref[bi]
        b8 = bs_ref[bi]
        x16 = jnp.concatenate([x8, jnp.zeros_like(x8)], axis=0).astype(BF16)
        b16 = jnp.concatenate([b8, jnp.zeros_like(b8)], axis=0).astype(BF16)
        for g in range(B_GROUPS):
            add = _dot_tn(x16[:, g * gw:(g + 1) * gw], b16[:, g * B_STATE:(g + 1) * B_STATE])
            for hl in range(B_HEADS // B_GROUPS):
                hh = g * (B_HEADS // B_GROUPS) + hl
                rs = slice(hh * B_HEAD_DIM, (hh + 1) * B_HEAD_DIM)
                snew_ref[bi, rs, :] = s0[rs] * dec_ref[step * bb + bi, hh] + \
                    add[hl * B_HEAD_DIM:(hl + 1) * B_HEAD_DIM]
        return carry

    lax.fori_loop(0, bb, body, 0, unroll=2)


def _even_sample_back_kernel(hs_ref, ya_ref, ypart_ref, eacum_ref, z_ref, yoff_ref, normg_ref, wout_ref,
                             o_ref, *, steps, d, inner):
    bt = hs_ref.shape[0]
    half = inner // B_GROUPS
    mixes = []
    for t in range(steps):
        sl = slice(t * inner, (t + 1) * inner)
        yoff = jnp.concatenate(
            [yoff_ref[:, (g * steps + t) * inner + g * half:(g * steps + t) * inner + (g + 1) * half]
             for g in range(B_GROUPS)], axis=1)
        y = (ypart_ref[:, sl] + yoff * eacum_ref[:, sl]) * _silu(z_ref[:, sl])
        yn = []
        for g in range(B_GROUPS):
            yg = y[:, g * half:(g + 1) * half]
            yn.append(yg * lax.rsqrt(jnp.mean(yg * yg, axis=-1, keepdims=True) + EPS))
        yb = jnp.concatenate(yn, axis=1) * normg_ref[...]
        mixes.append(jnp.concatenate([ya_ref[:, t * d:(t + 1) * d], yb], axis=1))
    out = _dot(jnp.concatenate(mixes, axis=0).astype(BF16), wout_ref[...])
    for t in range(steps):
        o_ref[:, t * d:(t + 1) * d] = hs_ref[:, t * d:(t + 1) * d] + out[t * bt:(t + 1) * bt]


def _row_tiled(width, bt):
    return pl.BlockSpec((bt, width), lambda i: (i, 0))


def _even_sample(hs2, state_conv2, state_ssm3, p, steps, bt, bb):
    nb, _ = hs2.shape
    d = p["wa"].shape[0]
    width_a = p["wa"].shape[1] // 2
    inner = p["wz"].shape[1]
    conv_dim = p["wxbc"].shape[1]
    gn = B_GROUPS * B_STATE
    assert nb % bt == 0 and nb % bb == 0 and steps <= SUBLANES
    params = pltpu.CompilerParams(dimension_semantics=("arbitrary",), vmem_limit_bytes=VMEM_LIMIT)
    small = ["lng", "lnb", "wts", "bts", "convw", "convb", "dtb", "alog", "dskip"]
    widths = dict(v=steps * width_a, ya=steps * width_a, ypart=steps * inner, eacum=steps * inner,
                  z=steps * inner, conv=(B_CONV - 1) * conv_dim, cgt=SUBLANES * B_STATE,
                  xdw=SUBLANES * inner, bs=SUBLANES * gn, dec=LANES)
    front = pl.pallas_call(
        functools.partial(_even_sample_front_kernel, steps=steps, d=d, width_a=width_a, inner=inner,
                          conv_dim=conv_dim),
        out_shape=tuple(jax.ShapeDtypeStruct((nb, w), F32) for w in widths.values()),
        grid=(nb // bt,),
        in_specs=[_row_tiled(steps * d, bt), _resident((1, d))]
        + [_resident(p[k].shape) for k in ("wa", "wz", "wxbc", "wdt")]
        + [_resident(p[k].shape) for k in small]
        + [_row_tiled((B_CONV - 1) * conv_dim, bt)],
        out_specs=tuple(_row_tiled(w, bt) for w in widths.values()),
        compiler_params=params,
        name="even_sample_front",
    )(hs2, p["g"], p["wa"], p["wz"], p["wxbc"], p["wdt"], *[p[k] for k in small], state_conv2)
    v_rows, ya, ypart, eacum, z, new_conv, cgt, xdw, bs, dec = front

    hp = state_ssm3.shape[1]
    tile3 = lambda rows, width: pl.BlockSpec((bb, rows, width), lambda i: (i, 0, 0))
    yoff, new_ssm = pl.pallas_call(
        functools.partial(_even_sample_state_kernel, bb=bb),
        out_shape=(jax.ShapeDtypeStruct((nb, SUBLANES, hp), F32),
                   jax.ShapeDtypeStruct(state_ssm3.shape, F32)),
        grid=(nb // bb,),
        in_specs=[tile3(hp, B_STATE), tile3(SUBLANES, B_STATE), tile3(SUBLANES, inner), tile3(SUBLANES, gn),
                  pl.BlockSpec(memory_space=pltpu.SMEM)],
        out_specs=(tile3(SUBLANES, hp), tile3(hp, B_STATE)),
        compiler_params=params,
        name="even_sample_state",
    )(state_ssm3, cgt.reshape(nb, SUBLANES, B_STATE), xdw.reshape(nb, SUBLANES, inner),
      bs.reshape(nb, SUBLANES, gn), dec[:, :B_HEADS])

    out = pl.pallas_call(
        functools.partial(_even_sample_back_kernel, steps=steps, d=d, inner=inner),
        out_shape=jax.ShapeDtypeStruct(hs2.shape, F32),
        grid=(nb // bt,),
        in_specs=[_row_tiled(steps * d, bt), _row_tiled(steps * width_a, bt), _row_tiled(steps * inner, bt),
                  _row_tiled(steps * inner, bt), _row_tiled(steps * inner, bt), _row_tiled(SUBLANES * hp, bt),
                  _resident((1, inner)), _resident(p["wout"].shape)],
        out_specs=_row_tiled(steps * d, bt),
        compiler_params=params,
        name="even_sample_back",
    )(hs2, ya, ypart, eacum, z, yoff.reshape(nb, SUBLANES * hp), p["normg"], p["wout"])
    return out, v_rows, new_conv, new_ssm


def _odd_sample_front_kernel(hs_ref, g_ref, wc_ref, wq_ref, wk_ref, wv_ref, linw_ref, cscale_ref,
                             qn_ref, kn_ref, onesbd_ref, ps_ref,
                             yc_out, pool_out, q_out, knew_out, vnew_out, *, steps, d, width_c, past_len):
    bt = hs_ref.shape[0]
    xn = _rms(_stack_steps(hs_ref, steps, d), g_ref[...]).astype(BF16)
    blk = lambda a, t: a[t * bt:(t + 1) * bt]
    c_in = _dot(xn, wc_ref[...])
    n_state = max(C_WINDOWS) - 1
    ext = _steps(ps_ref[...], n_state, width_c) + [blk(c_in, t) for t in range(steps)]
    for j in range(n_state):
        pool_out[:, j * width_c:(j + 1) * width_c] = ext[len(ext) - n_state + j]
    gdim = width_c // len(C_WINDOWS)
    yc_cols = []
    for gi, win in enumerate(C_WINDOWS):
        sl = slice(gi * gdim, (gi + 1) * gdim)
        pooled = []
        for t in range(steps):
            hi = n_state + t
            lo = max(hi - win + 1, 0)
            acc = ext[lo][:, sl]
            for j in range(lo + 1, hi + 1):
                acc = acc + ext[j][:, sl]
            count = float(min(past_len + t + 1, win))
            pooled.append(acc / count - ext[hi][:, sl])
        yc_cols.append(_dot(jnp.concatenate(pooled, axis=0).astype(BF16), linw_ref[gi]))
    yc = jnp.concatenate(yc_cols, axis=1) * cscale_ref[...]
    for t in range(steps):
        yc_out[:, t * width_c:(t + 1) * width_c] = blk(yc, t)

    q = _dot(xn, wq_ref[...])
    k = _dot(xn, wk_ref[...])
    v = _dot(xn, wv_ref[...])
    ones_bd = onesbd_ref[...]
    inv_d = 1.0 / D_HEAD_DIM
    qn = q * lax.rsqrt(_head_sumsq(q, ones_bd) * inv_d + EPS) * qn_ref[...] * (D_HEAD_DIM ** -0.5)
    kn = k * lax.rsqrt(_head_sumsq(k, ones_bd) * inv_d + EPS) * kn_ref[...]
    qw = q.shape[1]
    kw = k.shape[1]
    pad = SUBLANES - steps
    q_out[:, steps * qw:] = jnp.zeros((bt, pad * qw), F32)
    knew_out[:, :pad * kw] = jnp.zeros((bt, pad * kw), F32)
    vnew_out[:, :pad * kw] = jnp.zeros((bt, pad * kw), F32)
    for t in range(steps):
        q_out[:, t * qw:(t + 1) * qw] = blk(qn, t)
        knew_out[:, (pad + t) * kw:(pad + t + 1) * kw] = blk(kn, t)
        vnew_out[:, (pad + t) * kw:(pad + t + 1) * kw] = blk(v, t)


SINK_BUCKET = REL_BUCKETS


def _odd_sample_attn_kernel(q_ref, knew_ref, vnew_ref, ck_ref, cv_ref, bucket_ref, sinks_ref, rel_ref,
                            o_ref, kout_ref, vout_ref, bias_ref, *, bb, steps, n_keys):
    win = ck_ref.shape[1]
    kvw = ck_ref.shape[2]
    tile16 = 2 * SUBLANES

    @pl.when(pl.program_id(0) == 0)
    def _():
        bucket = bucket_ref[...]
        for hh in range(D_Q_HEADS):
            acc = jnp.full(bucket.shape, NEG, F32)
            for bkt in range(REL_BUCKETS):
                acc = jnp.where(bucket == bkt, rel_ref[bkt, hh], acc)
            acc = jnp.where(bucket == SINK_BUCKET, sinks_ref[hh], acc)
            bias_ref[hh * SUBLANES:(hh + 1) * SUBLANES, :] = acc

    sub = lax.broadcasted_iota(jnp.int32, (SUBLANES, kvw), 0)
    new_rows = sub >= SUBLANES - steps
    lane_kv = lax.broadcasted_iota(jnp.int32, (SUBLANES, kvw), 1) // D_HEAD_DIM
    gw = D_KV_HEADS * D_HEAD_DIM
    zero_keys = jnp.zeros((n_keys - win - tile16, kvw), BF16)

    def extend(cache, new8):
        new16 = jnp.concatenate([new8, jnp.zeros_like(new8)], axis=0).astype(BF16)
        return jnp.concatenate([cache.astype(BF16), new16, zero_keys], axis=0)

    def shift_in(cache, new8, out_ref, bi):
        rolled = pltpu.roll(cache, win - steps, 0)
        out_ref[bi, 0:win - SUBLANES, :] = rolled[:win - SUBLANES]
        out_ref[bi, win - SUBLANES:, :] = jnp.where(new_rows, new8, rolled[win - SUBLANES:])

    def body(bi, carry):
        ck = ck_ref[bi]
        cv = cv_ref[bi]
        k8 = knew_ref[bi]
        v8 = vnew_ref[bi]
        shift_in(ck, k8, kout_ref, bi)
        shift_in(cv, v8, vout_ref, bi)
        q8 = q_ref[bi]
        pieces = []
        for kv in range(D_KV_HEADS):
            for grp in range(D_GROUP):
                qg = q8[:, grp * gw:(grp + 1) * gw]
                pieces.append(jnp.where(lane_kv == kv, qg, 0.0))
        lhs = jnp.concatenate(pieces, axis=0).astype(BF16)
        sc = _dot_nt(lhs, extend(ck, k8)) + bias_ref[...]
        m = jnp.max(sc, axis=-1, keepdims=True)
        pexp = jnp.exp(sc - m)
        probs = (pexp / jnp.sum(pexp, axis=-1, keepdims=True)).astype(BF16)
        ov = _dot(probs, extend(cv, v8))
        outs = []
        for grp in range(D_GROUP):
            r_last = ((D_KV_HEADS - 1) * D_GROUP + grp) * SUBLANES
            acc = ov[r_last:r_last + SUBLANES]
            for kv in range(D_KV_HEADS - 2, -1, -1):
                r0 = (kv * D_GROUP + grp) * SUBLANES
                acc = jnp.where(lane_kv == kv, ov[r0:r0 + SUBLANES], acc)
            outs.append(acc)
        o_ref[bi] = jnp.concatenate(outs, axis=1)
        return carry

    lax.fori_loop(0, bb, body, 0, unroll=SAMPLE_ATTN_UNROLL)


def _odd_sample_back_kernel(hs_ref, yc_ref, o_ref_in, wout_ref, out_ref, *, steps, d, width_c, qw):
    bt = hs_ref.shape[0]
    mix = jnp.concatenate(
        [jnp.concatenate([yc_ref[:, t * width_c:(t + 1) * width_c], o_ref_in[:, t * qw:(t + 1) * qw]], axis=1)
         for t in range(steps)], axis=0).astype(BF16)
    out = _dot(mix, wout_ref[...])
    for t in range(steps):
        out_ref[:, t * d:(t + 1) * d] = hs_ref[:, t * d:(t + 1) * d] + out[t * bt:(t + 1) * bt]


def _odd_sample(hs2, state_pool2, cache_k3, cache_v3, p, steps, past_len, bt, bb):
    nb, _ = hs2.shape
    d = p["wc"].shape[0]
    width_c = p["wc"].shape[1]
    qw = p["wq"].shape[1]
    kw = p["wk"].shape[1]
    win = cache_k3.shape[1]
    n_state = max(C_WINDOWS) - 1
    assert nb % bt == 0 and nb % bb == 0 and steps <= SUBLANES and win == CHUNK
    params = pltpu.CompilerParams(dimension_semantics=("arbitrary",), vmem_limit_bytes=VMEM_LIMIT)
    vm = ["linw", "cscale", "qn", "kn", "onesbd"]
    widths = dict(yc=steps * width_c, pool=n_state * width_c, q=SUBLANES * qw, knew=SUBLANES * kw,
                  vnew=SUBLANES * kw)
    yc, new_pool, q8, knew8, vnew8 = pl.pallas_call(
        functools.partial(_odd_sample_front_kernel, steps=steps, d=d, width_c=width_c, past_len=past_len),
        out_shape=tuple(jax.ShapeDtypeStruct((nb, w), F32) for w in widths.values()),
        grid=(nb // bt,),
        in_specs=[_row_tiled(steps * d, bt), _resident((1, d))]
        + [_resident(p[k].shape) for k in ("wc", "wq", "wk", "wv")]
        + [_resident(p[k].shape) for k in vm]
        + [_row_tiled(n_state * width_c, bt)],
        out_specs=tuple(_row_tiled(w, bt) for w in widths.values()),
        compiler_params=params,
        name="odd_sample_front",
    )(hs2, p["g"], p["wc"], p["wq"], p["wk"], p["wv"], *[p[k] for k in vm], state_pool2)

    n_keys = 2 * CHUNK
    pad = SUBLANES - steps
    bucket = np.full((SUBLANES, n_keys), -1, np.int32)
    for t in range(steps):
        q_pos = past_len + t
        k_pos = np.full(n_keys, -10 ** 9, np.int64)
        k_pos[:win] = past_len - win + np.arange(win)
        k_pos[win + pad:win + SUBLANES] = past_len + np.arange(steps)
        dist = q_pos - k_pos
        ok = (dist >= 0) & (dist < CHUNK) & (k_pos >= 0)
        bucket[t] = np.where(ok, _t5_bucket(np.where(ok, dist, 0)), -1)
    bucket[:, n_keys - 1] = SINK_BUCKET
    smem = pl.BlockSpec(memory_space=pltpu.SMEM)
    tile3 = lambda rows, width: pl.BlockSpec((bb, rows, width), lambda i: (i, 0, 0))
    o8, new_k, new_v = pl.pallas_call(
        functools.partial(_odd_sample_attn_kernel, bb=bb, steps=steps, n_keys=n_keys),
        out_shape=(jax.ShapeDtypeStruct((nb, SUBLANES, qw), F32),
                   jax.ShapeDtypeStruct(cache_k3.shape, F32),
                   jax.ShapeDtypeStruct(cache_v3.shape, F32)),
        grid=(nb // bb,),
        in_specs=[tile3(SUBLANES, qw), tile3(SUBLANES, kw), tile3(SUBLANES, kw), tile3(win, kw), tile3(win, kw),
                  _resident(bucket.shape), smem, smem],
        out_specs=(tile3(SUBLANES, qw), tile3(win, kw), tile3(win, kw)),
        scratch_shapes=[pltpu.VMEM((D_Q_HEADS * SUBLANES, n_keys), F32)],
        compiler_params=params,
        name="odd_sample_attn",
    )(q8.reshape(nb, SUBLANES, qw), knew8.reshape(nb, SUBLANES, kw), vnew8.reshape(nb, SUBLANES, kw),
      cache_k3, cache_v3, jnp.asarray(bucket), p["sinks"], p["rel"])

    out = pl.pallas_call(
        functools.partial(_odd_sample_back_kernel, steps=steps, d=d, width_c=width_c, qw=qw),
        out_shape=jax.ShapeDtypeStruct(hs2.shape, F32),
        grid=(nb // bt,),
        in_specs=[_row_tiled(steps * d, bt), _row_tiled(steps * width_c, bt), _row_tiled(SUBLANES * qw, bt),
                  _resident(p["wout"].shape)],
        out_specs=_row_tiled(steps * d, bt),
        compiler_params=params,
        name="odd_sample_back",
    )(hs2, yc, o8.reshape(nb, SUBLANES * qw), p["wout"])
    return out, new_pool, new_k, new_v


def _row(v):
    return v.reshape(1, -1).astype(F32)


def _pad_lanes(m, width=LANES):
    return jnp.pad(m, ((0, 0), (0, width - m.shape[1])))


def _head_expand_matrix():
    e = np.zeros((LANES, B_HEADS * B_HEAD_DIM), np.float32)
    for hh in range(B_HEADS):
        e[hh, hh * B_HEAD_DIM:(hh + 1) * B_HEAD_DIM] = 1.0
    return np.concatenate([e, e], axis=0)


def _prep_even(mix_norm, w_in, w_out, ln_g, ln_b, w_s, b_s, conv_w, conv_b, dt_bias, a_log, d_skip, norm_g):
    width_a = ln_g.shape[0]
    inner = norm_g.shape[0]
    conv_dim = conv_b.shape[0]
    o1 = 2 * width_a
    o2 = o1 + inner
    o3 = o2 + conv_dim
    return dict(
        g=_row(mix_norm),
        wa=w_in[:, :o1].astype(BF16),
        wz=w_in[:, o1:o2].astype(BF16),
        wxbc=w_in[:, o2:o3].astype(BF16),
        wdt=_pad_lanes(w_in[:, o3:]).astype(BF16),
        wout=w_out.astype(BF16),
        lng=_row(ln_g), lnb=_row(ln_b), ws=w_s,
        bsb=jnp.broadcast_to(b_s[:, :, None], b_s.shape + (width_a // A_HEADS,)),
        convw=conv_w, convb=_row(conv_b),
        dtb=_pad_lanes(_row(dt_bias)), alog=_pad_lanes(_row(a_log)),
        dskip=_row(jnp.repeat(d_skip, B_HEAD_DIM)), normg=_row(norm_g),
        expand=jnp.asarray(_head_expand_matrix(), BF16),
    )


def _prep_odd(mix_norm, w_in, w_out, lin_w, c_scale, q_norm, k_norm, sinks, rel_table):
    d = w_in.shape[0]
    width_c = c_scale.shape[0]
    qw = D_Q_HEADS * D_HEAD_DIM
    kw = D_KV_HEADS * D_HEAD_DIM
    wq = w_in[:, width_c:width_c + qw].reshape(d, D_KV_HEADS, D_GROUP, D_HEAD_DIM)
    wq = wq.transpose(0, 2, 1, 3).reshape(d, qw)
    wo_d = w_out[width_c:].reshape(D_KV_HEADS, D_GROUP, D_HEAD_DIM, -1).transpose(1, 0, 2, 3).reshape(qw, -1)
    ones_bd = np.kron(np.eye(MXU_DIM // D_HEAD_DIM), np.ones((D_HEAD_DIM, D_HEAD_DIM))).astype(np.float32)
    return dict(
        g=_row(mix_norm),
        wc=w_in[:, :width_c].astype(BF16),
        wq=wq.astype(BF16),
        wk=w_in[:, width_c + qw:width_c + qw + kw].astype(BF16),
        wv=w_in[:, width_c + qw + kw:].astype(BF16),
        wout=jnp.concatenate([w_out[:width_c], wo_d], axis=0).astype(BF16),
        linw=lin_w.astype(BF16), cscale=_row(c_scale),
        qn=_row(jnp.tile(q_norm, D_Q_HEADS)), kn=_row(jnp.tile(k_norm, D_KV_HEADS)),
        onesbd=jnp.asarray(ones_bd, BF16),
        sinks=sinks.astype(F32), rel=rel_table.astype(F32),
    )


def _prep_even_sample(w_s, b_s, steps):
    head_w = CHUNK
    w = jnp.transpose(w_s[:, :steps, :steps], (1, 2, 0)).reshape(steps * steps, A_HEADS)
    b = b_s[:, :steps].T
    return dict(wts=jnp.repeat(w, head_w, axis=1), bts=jnp.repeat(b, head_w, axis=1))


PAST_LEN = 16384
FFN_TILE = 1024
FFN_STAGE_SLOTS = 8
FFN_STAGE_ROWS_GU = 32
FFN_STAGE_ROWS_D = 128
FFN_ROW_BLOCK = 256
FFN_BLOCK_TILES = 6
EVEN_MIXER_TILE = 512
ODD_MIXER_TILE = 512
SAMPLE_ROW_TILE = 64
SAMPLE_SEQ_TILE = 16
SAMPLE_ATTN_UNROLL = 4


def kernel(x_prompt, x_sample, state_ssm, state_conv, state_pool, cache_k_win, cache_v_win,
           ffn1_norm, ffn1_w_gu, ffn1_w_down, mix_norm, ffn2_norm, ffn2_w_gu, ffn2_w_down,
           ev_w_in, ev_w_out, a_ln_g, a_ln_b, a_w_s, a_b_s, b_conv_w, b_conv_b, b_dt_bias, b_a_log,
           b_d_skip, b_norm_g, od_w_in, od_w_out, c_lin_w, c_scale, d_q_norm, d_k_norm, d_sinks,
           rel_bias_table):
    bp, seq, d = x_prompt.shape
    bs, steps, _ = x_sample.shape
    past_len = PAST_LEN
    hp = x_prompt
    hs = x_sample
    depth = ffn1_norm.shape[0]
    names = ("a_v_s", "ssm_p", "ssm_s", "conv_p", "conv_s", "pool_p", "pool_s", "k_p", "k_s", "v_p", "v_s")
    outs = {k: [] for k in names}

    def macaron(h_p, h_s, norm, w_gu_all, w_down_all, layer):
        o_p, o_s = _ffn(h_p.reshape(bp * seq, d), h_s.reshape(bs * steps, d), _row(norm), w_gu_all, w_down_all,
                        layer, FFN_TILE)
        return o_p.reshape(bp, seq, d), o_s.reshape(bs, steps, d)

    for layer in range(depth):
        i = layer // 2
        hp, hs = macaron(hp, hs, ffn1_norm[layer], ffn1_w_gu, ffn1_w_down, layer)
        hs2 = hs.reshape(bs, steps * d)
        if layer % 2 == 0:
            p = _prep_even(mix_norm[layer], ev_w_in[i], ev_w_out[i], a_ln_g[i], a_ln_b[i], a_w_s[i], a_b_s[i],
                           b_conv_w[i], b_conv_b[i], b_dt_bias[i], b_a_log[i], b_d_skip[i], b_norm_g[i])
            p.update(_prep_even_sample(a_w_s[i], a_b_s[i], steps))
            hp, conv_p, ssm_p = _even_prompt(hp, p, EVEN_MIXER_TILE)
            hs2, v_rows, conv_s, ssm_s = _even_sample(
                hs2, state_conv[i].reshape(bs, -1), state_ssm[i].reshape(bs, B_HEADS * B_HEAD_DIM, B_STATE),
                p, steps, SAMPLE_ROW_TILE, SAMPLE_SEQ_TILE)
            outs["a_v_s"].append(v_rows.reshape(bs, steps, -1))
            outs["conv_p"].append(conv_p)
            outs["conv_s"].append(conv_s.reshape(state_conv[i].shape))
            outs["ssm_p"].append(ssm_p.reshape(bp, B_HEADS, B_HEAD_DIM, B_STATE))
            outs["ssm_s"].append(ssm_s.reshape(state_ssm[i].shape))
        else:
            p = _prep_odd(mix_norm[layer], od_w_in[i], od_w_out[i], c_lin_w[i], c_scale[i], d_q_norm[i],
                          d_k_norm[i], d_sinks[i], rel_bias_table)
            hp, pool_p, k_p, v_p = _odd_prompt(hp, p, ODD_MIXER_TILE)
            kv_shape = cache_k_win[i].shape
            hs2, pool_s, k_s, v_s = _odd_sample(
                hs2, state_pool[i].reshape(bs, -1), cache_k_win[i].reshape(bs, kv_shape[1], -1),
                cache_v_win[i].reshape(bs, kv_shape[1], -1), p, steps, past_len, SAMPLE_ROW_TILE, SAMPLE_SEQ_TILE)
            outs["pool_p"].append(pool_p)
            outs["pool_s"].append(pool_s.reshape(state_pool[i].shape))
            outs["k_p"].append(k_p.reshape(bp, CHUNK, D_KV_HEADS, D_HEAD_DIM))
            outs["v_p"].append(v_p.reshape(bp, CHUNK, D_KV_HEADS, D_HEAD_DIM))
            outs["k_s"].append(k_s.reshape(kv_shape))
            outs["v_s"].append(v_s.reshape(kv_shape))
        hs = hs2.reshape(bs, steps, d)
        hp, hs = macaron(hp, hs, ffn2_norm[layer], ffn2_w_gu, ffn2_w_down, layer)
    return (hp, hs) + tuple(jnp.stack(outs[k]) for k in names)
```

```python
import functools
import math

import numpy as np
import jax
import jax.numpy as jnp
from jax import lax
from jax.experimental import pallas as pl
from jax.experimental.pallas import tpu as pltpu

F32 = jnp.float32
BF16 = jnp.bfloat16

EPS = 1e-6
NEG = -1e30

LANES = 128
SUBLANES = 8
MXU_DIM = 256
VMEM_BYTES_V7X = 64 * 1024 * 1024
VMEM_LIMIT = VMEM_BYTES_V7X - 8 * 1024 * 1024
A_HEADS = 8
B_HEADS = 16
B_HEAD_DIM = 64
B_GROUPS = 2
B_STATE = 128
B_CONV = 4
CHUNK = 128
C_WINDOWS = (2, 4, 8, 16)
C_HALO = 16
D_Q_HEADS = 16
D_KV_HEADS = 4
D_HEAD_DIM = 64
D_GROUP = D_Q_HEADS // D_KV_HEADS
REL_BUCKETS = 32
REL_MAX_DIST = 128


def _rms(x, g):
    ms = jnp.mean(x * x, axis=-1, keepdims=True)
    return x * lax.rsqrt(ms + EPS) * g


def _sigmoid(x):
    return 1.0 / (1.0 + jnp.exp(-x))


def _silu(x):
    return x * _sigmoid(x)


def _gelu_tanh(x):
    c = math.sqrt(2.0 / math.pi)
    return x * (0.5 * (1.0 + jnp.tanh(c * (x + 0.044715 * (x * x * x)))))


def _softplus(x):
    return jnp.maximum(x, 0.0) + jnp.log1p(jnp.exp(-jnp.abs(x)))


def _split3(x):
    hi = x.astype(BF16)
    r1 = x - hi.astype(F32)
    mid = r1.astype(BF16)
    lo = (r1 - mid.astype(F32)).astype(BF16)
    return hi, mid, lo


def _split2_lanes(x):
    hi = x.astype(BF16)
    lo = (x - hi.astype(F32)).astype(BF16)
    return jnp.concatenate([hi, lo], axis=1)


def _dot(a, b):
    return jnp.dot(a, b, preferred_element_type=F32)


def _dot_nt(a, b):
    return lax.dot_general(a, b, (((1,), (1,)), ((), ())), preferred_element_type=F32)


def _dot_tn(a, b):
    return lax.dot_general(a, b, (((0,), (0,)), ((), ())), preferred_element_type=F32)


def _expand_heads(m, n_pairs):
    rows = m.shape[0]
    lane = lax.broadcasted_iota(jnp.int32, (rows, LANES), 1)
    first = lane < B_HEAD_DIM
    parts = []
    for p in range(n_pairs):
        a = jnp.broadcast_to(m[:, 2 * p:2 * p + 1], (rows, LANES))
        b = jnp.broadcast_to(m[:, 2 * p + 1:2 * p + 2], (rows, LANES))
        parts.append(jnp.where(first, a, b))
    return jnp.concatenate(parts, axis=1)


def _head_sumsq(x, ones_bd):
    xx = (x * x).astype(BF16)
    blk = ones_bd.shape[0]
    outs = [_dot(xx[:, c * blk:(c + 1) * blk], ones_bd) for c in range(x.shape[1] // blk)]
    return jnp.concatenate(outs, axis=1) if len(outs) > 1 else outs[0]


def _ff_blocks(d_ff):
    step = FFN_BLOCK_TILES * MXU_DIM
    return [(c0, min(c0 + step, d_ff)) for c0 in range(0, d_ff, step)]


def _load_cast_rows(src_hbm, dst_ref, stage_ref, sem_ref):
    slots, rows, _ = stage_ref.shape
    n = src_hbm.shape[0] // rows

    def copy(c):
        slot = c % slots
        return pltpu.make_async_copy(src_hbm.at[pl.ds(c * rows, rows), :], stage_ref.at[slot], sem_ref.at[slot])

    for c in range(min(slots, n)):
        copy(c).start()
    for c in range(n):
        copy(c).wait()
        dst_ref[c * rows:(c + 1) * rows, :] = stage_ref[c % slots].astype(BF16)
        if c + slots < n:
            copy(c + slots).start()


def _ffn_rows(x_ref, g_ref, wgu_ref, wd_ref, o_ref, d_ff, xn_ref=None, slot=None, next_ref=None):
    tm = x_ref.shape[0]
    rows = [slice(r0, min(r0 + FFN_ROW_BLOCK, tm)) for r0 in range(0, tm, FFN_ROW_BLOCK)]
    if xn_ref is None:
        xns = [_rms(x_ref[rs, :], g_ref[...]).astype(BF16) for rs in rows]
    else:
        xns = [xn_ref[slot, rs, :] for rs in rows]
    for rs, xn in zip(rows, xns):
        y = None
        for c0, c1 in _ff_blocks(d_ff):
            gate = _dot(xn, wgu_ref[:, c0:c1])
            up = _dot(xn, wgu_ref[:, d_ff + c0:d_ff + c1])
            act = (_silu(gate) * up).astype(BF16)
            part = _dot(act, wd_ref[c0:c1, :])
            y = part if y is None else y + part
        o_ref[rs, :] = x_ref[rs, :] + 0.5 * y
        if xn_ref is not None:
            xn_ref[1 - slot, rs, :] = _rms(next_ref[rs, :], g_ref[...]).astype(BF16)


def _ffn_kernel(xp_ref, xnext_ref, xs_ref, g_ref, wgu_hbm, wd_hbm, op_ref, os_ref,
                wgu_ref, wd_ref, stage_gu_ref, stage_d_ref, sem_ref, xn_ref, *, d_ff, prompt_steps, layer):
    i = pl.program_id(0)

    @pl.when(i == 0)
    def _():
        xn_ref[0] = _rms(xp_ref[...], g_ref[...]).astype(BF16)
        _load_cast_rows(wgu_hbm.at[layer], wgu_ref, stage_gu_ref, sem_ref)
        _load_cast_rows(wd_hbm.at[layer], wd_ref, stage_d_ref, sem_ref)

    @pl.when(i < prompt_steps)
    def _():
        _ffn_rows(xp_ref, g_ref, wgu_ref, wd_ref, op_ref, d_ff, xn_ref, i % 2, xnext_ref)

    @pl.when(i == prompt_steps)
    def _():
        _ffn_rows(xs_ref, g_ref, wgu_ref, wd_ref, os_ref, d_ff)


def _resident(shape):
    nd = len(shape)
    return pl.BlockSpec(shape, lambda *_: (0,) * nd, pipeline_mode=pl.Buffered(1))


def _ffn(xp2d, xs2d, g, wgu_all, wd_all, layer, tm):
    m, d = xp2d.shape
    ms = xs2d.shape[0]
    wgu_shape, wd_shape = wgu_all.shape[1:], wd_all.shape[1:]
    d_ff = wd_shape[0]
    assert m % tm == 0 and d % FFN_STAGE_ROWS_GU == 0 and d_ff % FFN_STAGE_ROWS_D == 0
    steps = m // tm
    last = steps - 1
    whole = lambda shape: pl.BlockSpec(shape, lambda i: (0, 0))
    prompt_tile = pl.BlockSpec((tm, d), lambda i: (jnp.minimum(i, last), 0))
    next_tile = pl.BlockSpec((tm, d), lambda i: (jnp.minimum(i + 1, last), 0))
    hbm = pl.BlockSpec(memory_space=pl.ANY)
    return pl.pallas_call(
        functools.partial(_ffn_kernel, d_ff=d_ff, prompt_steps=steps, layer=layer),
        out_shape=(jax.ShapeDtypeStruct((m, d), F32), jax.ShapeDtypeStruct((ms, d), F32)),
        grid=(steps + 1,),
        in_specs=[prompt_tile, next_tile, _resident((ms, d)), _resident((1, d)), hbm, hbm],
        out_specs=(prompt_tile, whole((ms, d))),
        scratch_shapes=[pltpu.VMEM(wgu_shape, BF16), pltpu.VMEM(wd_shape, BF16),
                        pltpu.VMEM((FFN_STAGE_SLOTS, FFN_STAGE_ROWS_GU, wgu_shape[1]), F32),
                        pltpu.VMEM((FFN_STAGE_SLOTS, FFN_STAGE_ROWS_D, wd_shape[1]), F32),
                        pltpu.SemaphoreType.DMA((FFN_STAGE_SLOTS,)),
                        pltpu.VMEM((2, tm, d), BF16)],
        compiler_params=pltpu.CompilerParams(
            dimension_semantics=("arbitrary",), vmem_limit_bytes=VMEM_LIMIT),
        name="ffn",
    )(xp2d, xp2d, xs2d, g, wgu_all, wd_all)


def _zip_stages(first, second, lag=1):
    for i in range(max(len(first), len(second) + lag)):
        if i < len(first):
            first[i]()
        if 0 <= i - lag < len(second):
            second[i - lag]()


def _even_prompt_kernel(h_ref, g_ref, wa_ref, wz_ref, wxbc_ref, wdt_ref, wout_ref,
                        lng_ref, lnb_ref, ws_ref, bsb_ref, convw_ref, convb_ref,
                        dtb_ref, alog_ref, dskip_ref, normg_ref, expand_ref,
                        o_ref, conv_out_ref, ssm_out_ref,
                        ext_ref, st_ref, *, tile, width_a, inner):
    s = pl.program_id(1)
    n_chunks = tile // CHUNK
    halo = SUBLANES

    @pl.when(s == 0)
    def _():
        ext_ref[0:halo, :] = jnp.zeros((halo, ext_ref.shape[1]), F32)
        st_ref[...] = jnp.zeros(st_ref.shape, F32)

    x = h_ref[...]
    xn = _rms(x, g_ref[...]).astype(BF16)

    row = lax.broadcasted_iota(jnp.int32, (CHUNK, CHUNK), 0)
    col = lax.broadcasted_iota(jnp.int32, (CHUNK, CHUNK), 1)
    causal = row >= col
    lane = lax.broadcasted_iota(jnp.int32, (CHUNK, LANES), 1)
    first_half = lane < B_HEAD_DIM

    blk = MXU_DIM
    col_blocks = lambda width: [slice(j * blk, (j + 1) * blk) for j in range(width // blk)]
    head_w = width_a // A_HEADS
    gn = B_GROUPS * B_STATE
    conv_dim = inner + 2 * gn
    n_pairs = B_HEADS // 2
    heads_per_group = B_HEADS // B_GROUPS
    gw = heads_per_group * B_HEAD_DIM
    chunks = [slice(c * CHUNK, (c + 1) * CHUNK) for c in range(n_chunks)]

    ga = {}
    a_cols = [slice(width_a + sl.start, width_a + sl.stop) for sl in col_blocks(width_a)] + col_blocks(width_a)
    pa = {}

    def proj_a(j):
        pa[j] = _dot(xn, wa_ref[:, a_cols[j]])

    def act_a(j):
        ga[j] = _gelu_tanh(pa.pop(j))

    n_a = len(a_cols)
    _zip_stages([functools.partial(proj_a, j) for j in range(n_a)],
                [functools.partial(act_a, j) for j in range(n_a)])
    nv = n_a // 2
    v = jnp.concatenate([ga[j] for j in range(nv)], axis=1)
    u_blocks = [ga[j] for j in range(nv, n_a)]

    raw = {}
    xbc_blocks = {}
    vb_box = []

    def proj_xbc(k):
        raw[k] = _dot(xn, wxbc_ref[:, k * blk:(k + 1) * blk])

    def layer_norm_v():
        mu = jnp.mean(v, axis=-1, keepdims=True)
        vc = v - mu
        var = jnp.mean(vc * vc, axis=-1, keepdims=True)
        vb_box.append((vc * lax.rsqrt(var + EPS) * lng_ref[...] + lnb_ref[...]).astype(BF16))

    def conv_block(k):
        cs = slice(k * blk, (k + 1) * blk)
        ext_ref[halo:halo + tile, cs] = raw.pop(k)
        ext = ext_ref[:, cs]
        ext1 = pltpu.roll(ext, 1, 0)
        pair = ext * convw_ref[1:2, cs] + ext1 * convw_ref[0:1, cs]
        conv = (convb_ref[:, cs] + ext * convw_ref[3:4, cs] + ext1 * convw_ref[2:3, cs]
                + pltpu.roll(pair, 2, 0))[halo:]
        tail = ext_ref[tile:tile + halo, cs]
        ext_ref[0:halo, cs] = tail
        conv_out_ref[:, cs] = tail
        xbc_blocks[k] = _silu(conv)

    n_x = conv_dim // blk
    _zip_stages([functools.partial(proj_xbc, k) for k in range(n_x)],
                [layer_norm_v] + [functools.partial(conv_block, k) for k in range(n_x)])
    vb = vb_box[0]

    z_blocks = {}
    gate_cols = {}

    def proj_z(j):
        z_blocks[j] = _dot(xn, wz_ref[:, j * blk:(j + 1) * blk])

    def mix_head(hh):
        w = jnp.where(causal, ws_ref[hh], 0.0).astype(BF16)
        rhs = jnp.concatenate([vb[rs, hh * head_w:(hh + 1) * head_w] for rs in chunks], axis=1)
        out = _dot(w, rhs)
        bias = bsb_ref[hh]
        gate_cols[hh] = jnp.concatenate(
            [out[:, c * head_w:(c + 1) * head_w] + bias for c in range(n_chunks)], axis=0)

    dt_raw = _dot(xn, wdt_ref[...])
    n_z = inner // blk
    heads_per_z = A_HEADS // n_z
    for j in range(n_z):
        proj_z(j)
        for hh in range(j * heads_per_z, (j + 1) * heads_per_z):
            mix_head(hh)
    heads_per_blk = blk // head_w
    ya_blocks = [(u_blocks[j] * jnp.concatenate(
        [gate_cols[j * heads_per_blk + i] for i in range(heads_per_blk)], axis=1)).astype(BF16)
        for j in range(len(u_blocks))]
    ya = jnp.concatenate(ya_blocks, axis=1)
    xs = jnp.concatenate([xbc_blocks[k] for k in range(inner // blk)], axis=1)
    bm = xbc_blocks[inner // blk]
    cm = xbc_blocks[inner // blk + 1]

    dt = _softplus(dt_raw + dtb_ref[...])
    a_neg = -jnp.exp(alog_ref[...])
    da = dt * a_neg
    tril_ones = jnp.where(causal, 1.0, 0.0).astype(BF16)
    acums = []
    for rs in chunks:
        d_hi, d_mid, d_lo = _split3(da[rs])
        acums.append(_dot(tril_ones, d_hi) + _dot(tril_ones, d_mid) + _dot(tril_ones, d_lo))
    acum = jnp.concatenate(acums, axis=0)
    decay = jnp.concatenate([jnp.exp(a[CHUNK - 1:CHUNK, :] - a) for a in acums], axis=0)
    expand = expand_ref[...]
    xd = xs * _dot(_split2_lanes(dt), expand)
    xdwb = (xs * _dot(_split2_lanes(dt * decay), expand)).astype(BF16)
    e_acum = _dot(_split2_lanes(jnp.exp(acum)), expand)
    bmb = bm.astype(BF16)
    cmb = cm.astype(BF16)

    y_rows = [None] * n_chunks
    out_a = {}

    def scan_chunk(c):
        rs = chunks[c]
        a_c = acums[c]
        acum_t = a_c.T
        cb = [_dot_nt(cmb[rs, g * B_STATE:(g + 1) * B_STATE], bmb[rs, g * B_STATE:(g + 1) * B_STATE])
              for g in range(B_GROUPS)]
        y_parts = []
        for p in range(n_pairs):
            g = (2 * p) // heads_per_group
            ms = []
            for hh in (2 * p, 2 * p + 1):
                seg = jnp.broadcast_to(a_c[:, hh:hh + 1], (CHUNK, CHUNK)) - \
                    jnp.broadcast_to(acum_t[hh:hh + 1, :], (CHUNK, CHUNK))
                lmat = jnp.where(causal, jnp.exp(seg), 0.0)
                ms.append((cb[g] * lmat).astype(BF16))
            lhs = jnp.concatenate(ms, axis=1)
            xd_p = xd[rs, p * LANES:(p + 1) * LANES]
            rhs = jnp.concatenate([jnp.where(first_half, xd_p, 0.0),
                                   jnp.where(first_half, 0.0, xd_p)], axis=0).astype(BF16)
            y_parts.append(_dot(lhs, rhs))
        y_rows[c] = jnp.concatenate(y_parts, axis=1)

    def proj_out_a(j):
        out_a[j] = _dot(ya, wout_ref[0:width_a, j * blk:(j + 1) * blk])

    n_o = o_ref.shape[1] // blk
    _zip_stages([functools.partial(scan_chunk, c) for c in range(n_chunks)],
                [functools.partial(proj_out_a, j) for j in range(n_o)], lag=0)

    for c, rs in enumerate(chunks):
        st_prev = st_ref[...]
        stb = st_prev.astype(BF16)
        y_off = jnp.concatenate(
            [_dot(cmb[rs, g * B_STATE:(g + 1) * B_STATE], stb[:, g * gw:(g + 1) * gw]) for g in range(B_GROUPS)],
            axis=1)
        st_add = jnp.concatenate(
            [_dot_tn(bmb[rs, g * B_STATE:(g + 1) * B_STATE], xdwb[rs, g * gw:(g + 1) * gw])
             for g in range(B_GROUPS)], axis=1)
        chunk_decay = e_acum[(c + 1) * CHUNK - 1:(c + 1) * CHUNK, :]
        st_ref[...] = st_prev * chunk_decay + st_add
        y_rows[c] = y_rows[c] + y_off * e_acum[rs]
    y = (jnp.concatenate(y_rows, axis=0) if n_chunks > 1 else y_rows[0]) + xs * dskip_ref[...]
    z = jnp.concatenate([z_blocks[j] for j in range(n_z)], axis=1)
    y = y * _silu(z)
    half = inner // B_GROUPS
    yn = []
    for g in range(B_GROUPS):
        yg = y[:, g * half:(g + 1) * half]
        yn.append(yg * lax.rsqrt(jnp.mean(yg * yg, axis=-1, keepdims=True) + EPS))
    yb = (jnp.concatenate(yn, axis=1) * normg_ref[...]).astype(BF16)

    for j in range(n_o):
        cs = slice(j * blk, (j + 1) * blk)
        o_ref[:, cs] = x[:, cs] + out_a[j] + _dot(yb, wout_ref[width_a:, cs])

    @pl.when(s == pl.num_programs(1) - 1)
    def _():
        ssm_out_ref[...] = st_ref[...].T


def _even_prompt(h, p, tile):
    b, seq, d = h.shape
    width_a = p["wa"].shape[1] // 2
    inner = p["wz"].shape[1]
    conv_dim = p["wxbc"].shape[1]
    assert seq % tile == 0 and tile % CHUNK == 0
    small = ["lng", "lnb", "ws", "bsb", "convw", "convb", "dtb", "alog", "dskip", "normg", "expand"]
    out, conv_tail, ssm = pl.pallas_call(
        functools.partial(_even_prompt_kernel, tile=tile, width_a=width_a, inner=inner),
        out_shape=(jax.ShapeDtypeStruct((b, seq, d), F32),
                   jax.ShapeDtypeStruct((b, SUBLANES, conv_dim), F32),
                   jax.ShapeDtypeStruct((b, inner, B_STATE), F32)),
        grid=(b, seq // tile),
        in_specs=[pl.BlockSpec((None, tile, d), lambda i, j: (i, j, 0)),
                  _resident((1, d))]
        + [_resident(p[k].shape) for k in ("wa", "wz", "wxbc", "wdt", "wout")]
        + [_resident(p[k].shape) for k in small],
        out_specs=(pl.BlockSpec((None, tile, d), lambda i, j: (i, j, 0)),
                   pl.BlockSpec((None, SUBLANES, conv_dim), lambda i, j: (i, 0, 0)),
                   pl.BlockSpec((None, inner, B_STATE), lambda i, j: (i, 0, 0))),
        scratch_shapes=[pltpu.VMEM((tile + SUBLANES, conv_dim), F32),
                        pltpu.VMEM((B_STATE, inner), F32)],
        compiler_params=pltpu.CompilerParams(
            dimension_semantics=("arbitrary", "arbitrary"), vmem_limit_bytes=VMEM_LIMIT),
        name="even_prompt",
    )(h, p["g"], p["wa"], p["wz"], p["wxbc"], p["wdt"], p["wout"], *[p[k] for k in small])
    return out, conv_tail[:, SUBLANES - (B_CONV - 1):, :], ssm


def _t5_bucket(dist):
    n = np.maximum(dist, 0)
    max_exact = REL_BUCKETS // 2
    n_safe = np.maximum(n, 1).astype(np.float32)
    scale = np.float32((REL_BUCKETS - max_exact) / math.log(REL_MAX_DIST / max_exact))
    large = max_exact + (np.log(n_safe / max_exact) * scale).astype(np.int32)
    large = np.minimum(large, REL_BUCKETS - 1)
    return np.where(n < max_exact, n, large).astype(np.int32)


def _fill_bias(bias_ref, bucket_ref, rel_ref):
    bucket = bucket_ref[...]
    lq = bucket.shape[0]
    has_prev = lax.broadcasted_iota(jnp.int32, bucket.shape, 1) >= CHUNK
    for hh in range(D_Q_HEADS):
        kv, grp = divmod(hh, D_GROUP)
        acc = jnp.full(bucket.shape, NEG, F32)
        for bkt in range(REL_BUCKETS):
            acc = jnp.where(bucket == bkt, rel_ref[bkt, hh], acc)
        bias_ref[0, grp, kv * lq:(kv + 1) * lq, :] = acc
        bias_ref[1, grp, kv * lq:(kv + 1) * lq, :] = jnp.where(has_prev, acc, NEG)


def _kv_lane_ids(rows):
    return lax.broadcasted_iota(jnp.int32, (rows, D_KV_HEADS * D_HEAD_DIM), 1) // D_HEAD_DIM


def _attn_scores(qg, kk):
    lane_kv = _kv_lane_ids(qg.shape[0])
    zero = jnp.zeros_like(qg)
    lhs = jnp.concatenate([jnp.where(lane_kv == kv, qg, zero) for kv in range(D_KV_HEADS)], axis=0)
    return _dot_nt(lhs, kk)


def _attn_probs(sc, bias_ref, sinks_ref, grp, table):
    lq = sc.shape[0] // D_KV_HEADS
    probs = []
    for kv in range(D_KV_HEADS):
        rs = slice(kv * lq, (kv + 1) * lq)
        s_h = sc[rs] + bias_ref[table, grp, rs, :]
        sink = sinks_ref[kv * D_GROUP + grp]
        m = jnp.maximum(jnp.max(s_h, axis=-1, keepdims=True), sink)
        pexp = jnp.exp(s_h - m)
        denom = jnp.sum(pexp, axis=-1, keepdims=True) + jnp.exp(sink - m)
        probs.append((pexp / denom).astype(BF16))
    return jnp.concatenate(probs, axis=0)


def _attn_out(probs, vv):
    lq = probs.shape[0] // D_KV_HEADS
    lane_kv = _kv_lane_ids(lq)
    ov = _dot(probs, vv)
    out = ov[(D_KV_HEADS - 1) * lq:]
    for kv in range(D_KV_HEADS - 2, -1, -1):
        out = jnp.where(lane_kv == kv, ov[kv * lq:(kv + 1) * lq], out)
    return out


def _odd_prompt_kernel(h_ref, g_ref, wc_ref, wq_ref, wk_ref, wv_ref, wout_ref,
                       linw_ref, cscale_ref, qn_ref, kn_ref, onesbd_ref, bucket_ref,
                       sinks_ref, rel_ref,
                       o_ref, pool_out_ref, k_out_ref, v_out_ref,
                       extc_ref, kprev_ref, vprev_ref, bias_ref, *, tile, width_c):
    b = pl.program_id(0)
    s = pl.program_id(1)
    n_blocks = tile // CHUNK

    @pl.when((b == 0) & (s == 0))
    def _():
        _fill_bias(bias_ref, bucket_ref, rel_ref)

    @pl.when(s == 0)
    def _():
        extc_ref[0:C_HALO, :] = jnp.zeros((C_HALO, width_c), F32)
        kprev_ref[...] = jnp.zeros(kprev_ref.shape, F32)
        vprev_ref[...] = jnp.zeros(vprev_ref.shape, F32)

    x = h_ref[...]
    xn = _rms(x, g_ref[...]).astype(BF16)

    c_in = _dot(xn, wc_ref[...])
    q = _dot(xn, wq_ref[...])
    k = _dot(xn, wk_ref[...])
    v = _dot(xn, wv_ref[...])

    extc_ref[C_HALO:C_HALO + tile, :] = c_in
    e = extc_ref[...]
    tail = extc_ref[tile:tile + C_HALO, :]
    extc_ref[0:C_HALO, :] = tail
    pool_out_ref[...] = tail
    pos = (s * tile + lax.broadcasted_iota(jnp.int32, (tile, 1), 0) + 1).astype(F32)
    gdim = width_c // len(C_WINDOWS)
    run = e
    shift = 1
    yc = []
    for gi, win in enumerate(C_WINDOWS):
        while shift < win:
            run = run + pltpu.roll(run, shift, 0)
            shift *= 2
        cnt = jnp.minimum(pos, float(win))
        pooled = run[C_HALO:, :gdim] / cnt - c_in[:, gi * gdim:(gi + 1) * gdim]
        yc.append(_dot(pooled.astype(BF16), linw_ref[gi]))
        if gi + 1 < len(C_WINDOWS):
            run = run[:, gdim:]
    ycb = (jnp.concatenate(yc, axis=1) * cscale_ref[...]).astype(BF16)

    ones_bd = onesbd_ref[...]
    inv_d = 1.0 / D_HEAD_DIM
    qn = q * lax.rsqrt(_head_sumsq(q, ones_bd) * inv_d + EPS) * qn_ref[...]
    kn = k * lax.rsqrt(_head_sumsq(k, ones_bd) * inv_d + EPS) * kn_ref[...]
    qs = (qn * (D_HEAD_DIM ** -0.5)).astype(BF16)
    kb = kn.astype(BF16)
    vb = v.astype(BF16)
    first_table = jnp.where(s == 0, 1, 0)
    gw = D_KV_HEADS * D_HEAD_DIM
    keys, vals = [], []
    for blk in range(n_blocks):
        rs = slice(blk * CHUNK, (blk + 1) * CHUNK)
        if blk == 0:
            k_prev, v_prev = kprev_ref[...].astype(BF16), vprev_ref[...].astype(BF16)
        else:
            k_prev, v_prev = kb[(blk - 1) * CHUNK:blk * CHUNK], vb[(blk - 1) * CHUNK:blk * CHUNK]
        keys.append(jnp.concatenate([k_prev, kb[rs]], axis=0))
        vals.append(jnp.concatenate([v_prev, vb[rs]], axis=0))
    kprev_ref[...] = kn[tile - CHUNK:]
    vprev_ref[...] = v[tile - CHUNK:]
    items = [(blk, grp) for blk in range(n_blocks) for grp in range(D_GROUP)]
    n_o = o_ref.shape[1] // MXU_DIM
    out_c, sc, pr, og = {}, {}, {}, {}
    for i in range(len(items) + 2):
        if i < len(items):
            blk, grp = items[i]
            sc[i] = _attn_scores(qs[blk * CHUNK:(blk + 1) * CHUNK, grp * gw:(grp + 1) * gw], keys[blk])
        if 0 <= i - 1 < len(items):
            blk, grp = items[i - 1]
            pr[i - 1] = _attn_probs(sc.pop(i - 1), bias_ref, sinks_ref, grp, first_table if blk == 0 else 0)
        if i < n_o:
            out_c[i] = _dot(ycb, wout_ref[0:width_c, i * MXU_DIM:(i + 1) * MXU_DIM])
        if 0 <= i - 2 < len(items):
            blk, grp = items[i - 2]
            og[i - 2] = _attn_out(pr.pop(i - 2), vals[blk])
    yd = jnp.concatenate(
        [jnp.concatenate([og[blk * D_GROUP + grp] for grp in range(D_GROUP)], axis=1) for blk in range(n_blocks)],
        axis=0).astype(BF16)
    for j in range(n_o):
        cs = slice(j * MXU_DIM, (j + 1) * MXU_DIM)
        o_ref[:, cs] = x[:, cs] + out_c[j] + _dot(yd, wout_ref[width_c:, cs])

    @pl.when(s == pl.num_programs(1) - 1)
    def _():
        k_out_ref[...] = kn[tile - CHUNK:]
        v_out_ref[...] = v[tile - CHUNK:]


def _odd_prompt(h, p, tile):
    b, seq, d = h.shape
    width_c = p["wc"].shape[1]
    kvw = p["wk"].shape[1]
    assert seq % tile == 0 and tile % CHUNK == 0
    r = np.arange(CHUNK) + CHUNK
    c = np.arange(2 * CHUNK)
    dist = r[:, None] - c[None, :]
    bucket = np.where((dist >= 0) & (dist < CHUNK), _t5_bucket(dist), -1).astype(np.int32)
    vm = ["linw", "cscale", "qn", "kn", "onesbd"]
    smem = pl.BlockSpec(memory_space=pltpu.SMEM)
    out, pool_tail, k_win, v_win = pl.pallas_call(
        functools.partial(_odd_prompt_kernel, tile=tile, width_c=width_c),
        out_shape=(jax.ShapeDtypeStruct((b, seq, d), F32),
                   jax.ShapeDtypeStruct((b, C_HALO, width_c), F32),
                   jax.ShapeDtypeStruct((b, CHUNK, kvw), F32),
                   jax.ShapeDtypeStruct((b, CHUNK, kvw), F32)),
        grid=(b, seq // tile),
        in_specs=[pl.BlockSpec((None, tile, d), lambda i, j: (i, j, 0)),
                  _resident((1, d))]
        + [_resident(p[k].shape) for k in ("wc", "wq", "wk", "wv", "wout")]
        + [_resident(p[k].shape) for k in vm]
        + [_resident(bucket.shape), smem, smem],
        out_specs=(pl.BlockSpec((None, tile, d), lambda i, j: (i, j, 0)),
                   pl.BlockSpec((None, C_HALO, width_c), lambda i, j: (i, 0, 0)),
                   pl.BlockSpec((None, CHUNK, kvw), lambda i, j: (i, 0, 0)),
                   pl.BlockSpec((None, CHUNK, kvw), lambda i, j: (i, 0, 0))),
        scratch_shapes=[pltpu.VMEM((tile + C_HALO, width_c), F32),
                        pltpu.VMEM((CHUNK, kvw), F32),
                        pltpu.VMEM((CHUNK, kvw), F32),
                        pltpu.VMEM((2, D_GROUP, D_KV_HEADS * CHUNK, 2 * CHUNK), F32)],
        compiler_params=pltpu.CompilerParams(
            dimension_semantics=("arbitrary", "arbitrary"), vmem_limit_bytes=VMEM_LIMIT),
        name="odd_prompt",
    )(h, p["g"], p["wc"], p["wq"], p["wk"], p["wv"], p["wout"], *[p[k] for k in vm],
      jnp.asarray(bucket), p["sinks"], p["rel"])
    return out, pool_tail[:, C_HALO - (max(C_WINDOWS) - 1):, :], k_win, v_win


def _steps(x, n, width):
    return [x[:, t * width:(t + 1) * width] for t in range(n)]


def _stack_steps(ref, n, width):
    x = ref[...]
    return jnp.concatenate(_steps(x, n, width), axis=0)


def _even_sample_front_kernel(hs_ref, g_ref, wa_ref, wz_ref, wxbc_ref, wdt_ref,
                              lng_ref, lnb_ref, wts_ref, bts_ref, convw_ref, convb_ref,
                              dtb_ref, alog_ref, dskip_ref, cs_ref,
                              v_out, ya_out, ypart_out, eacum_out, z_out, conv_out,
                              cgt_out, xdw_out, bs_out, dec_out, *, steps, d, width_a, inner, conv_dim):
    bt = hs_ref.shape[0]
    xn = _rms(_stack_steps(hs_ref, steps, d), g_ref[...]).astype(BF16)
    blk = lambda a, t: a[t * bt:(t + 1) * bt]

    ga = _gelu_tanh(_dot(xn, wa_ref[...]))
    u = ga[:, :width_a]
    v = ga[:, width_a:]
    mu = jnp.mean(v, axis=-1, keepdims=True)
    vc = v - mu
    var = jnp.mean(vc * vc, axis=-1, keepdims=True)
    v = vc * lax.rsqrt(var + EPS) * lng_ref[...] + lnb_ref[...]
    for t in range(steps):
        v_out[:, t * width_a:(t + 1) * width_a] = blk(v, t)
        gate = bts_ref[t:t + 1, :]
        for s in range(t + 1):
            gate = gate + wts_ref[t * steps + s:t * steps + s + 1, :] * blk(v, s)
        ya_out[:, t * width_a:(t + 1) * width_a] = blk(u, t) * gate

    z = _dot(xn, wz_ref[...])
    for t in range(steps):
        z_out[:, t * inner:(t + 1) * inner] = blk(z, t)
    raw = _dot(xn, wxbc_ref[...])
    dt = _softplus(_dot(xn, wdt_ref[...]) + dtb_ref[...])
    ext = _steps(cs_ref[...], B_CONV - 1, conv_dim) + [blk(raw, t) for t in range(steps)]
    for k in range(B_CONV - 1):
        conv_out[:, k * conv_dim:(k + 1) * conv_dim] = ext[len(ext) - (B_CONV - 1) + k]
    gn = B_GROUPS * B_STATE
    n_pairs = B_HEADS // 2
    a_neg = -jnp.exp(alog_ref[...])
    xs, bm, cm, dts, acum = [], [], [], [], []
    for t in range(steps):
        conv = convb_ref[...]
        for tap in range(B_CONV):
            conv = conv + ext[t + tap] * convw_ref[tap:tap + 1, :]
        xbc = _silu(conv)
        xs.append(xbc[:, :inner])
        bm.append(xbc[:, inner:inner + gn])
        cm.append(xbc[:, inner + gn:])
        dts.append(blk(dt, t))
        da = dts[t] * a_neg
        acum.append(da if t == 0 else acum[t - 1] + da)
    lane = lax.broadcasted_iota(jnp.int32, (bt, LANES), 1)
    group0 = lane < (B_HEADS // B_GROUPS)
    dec_out[...] = jnp.exp(acum[steps - 1])
    pad_rows = SUBLANES - steps
    xdw_out[:, steps * inner:] = jnp.zeros((bt, pad_rows * inner), F32)
    bs_out[:, steps * gn:] = jnp.zeros((bt, pad_rows * gn), F32)
    xd = []
    for t in range(steps):
        xd.append(xs[t] * _expand_heads(dts[t], n_pairs))
        eacum_out[:, t * inner:(t + 1) * inner] = _expand_heads(jnp.exp(acum[t]), n_pairs)
        xdw_out[:, t * inner:(t + 1) * inner] = xs[t] * _expand_heads(
            dts[t] * jnp.exp(acum[steps - 1] - acum[t]), n_pairs)
        bs_out[:, t * gn:(t + 1) * gn] = bm[t]
        for g in range(B_GROUPS):
            r = g * steps + t
            cgt_out[:, r * B_STATE:(r + 1) * B_STATE] = cm[t][:, g * B_STATE:(g + 1) * B_STATE]
    for t in range(steps):
        y = xs[t] * dskip_ref[...]
        for s in range(t + 1):
            cb = [jnp.sum(cm[t][:, g * B_STATE:(g + 1) * B_STATE] * bm[s][:, g * B_STATE:(g + 1) * B_STATE],
                          axis=-1, keepdims=True) for g in range(B_GROUPS)]
            coef = jnp.where(group0, cb[0], cb[1]) * jnp.exp(acum[t] - acum[s])
            y = y + _expand_heads(coef, n_pairs) * xd[s]
        ypart_out[:, t * inner:(t + 1) * inner] = y


def _even_sample_state_kernel(s0_ref, cgt_ref, xdw_ref, bs_ref, dec_ref, yoff_ref, snew_ref, *, bb):
    step = pl.program_id(0)
    gw = (B_HEADS // B_GROUPS) * B_HEAD_DIM

    def body(bi, carry):
        s0 = s0_ref[bi]
        c8 = cgt_ref[bi]
        c16 = jnp.concatenate([c8, jnp.zeros_like(c8)], axis=0).astype(BF16)
        yoff_ref[bi] = _dot_nt(c16, s0.astype(BF16))[:SUBLANES]
        x8 = xdw_ref[bi]
        b8 = bs_ref[bi]
        x16 = jnp.concatenate([x8, jnp.zeros_like(x8)], axis=0).astype(BF16)
        b16 = jnp.concatenate([b8, jnp.zeros_like(b8)], axis=0).astype(BF16)
        for g in range(B_GROUPS):
            add = _dot_tn(x16[:, g * gw:(g + 1) * gw], b16[:, g * B_STATE:(g + 1) * B_STATE])
            for hl in range(B_HEADS // B_GROUPS):
                hh = g * (B_HEADS // B_GROUPS) + hl
                rs = slice(hh * B_HEAD_DIM, (hh + 1) * B_HEAD_DIM)
                snew_ref[bi, rs, :] = s0[rs] * dec_ref[step * bb + bi, hh] + \
                    add[hl * B_HEAD_DIM:(hl + 1) * B_HEAD_DIM]
        return carry

    lax.fori_loop(0, bb, body, 0, unroll=2)


def _even_sample_back_kernel(hs_ref, ya_ref, ypart_ref, eacum_ref, z_ref, yoff_ref, normg_ref, wout_ref,
                             o_ref, *, steps, d, inner):
    bt = hs_ref.shape[0]
    half = inner // B_GROUPS
    mixes = []
    for t in range(steps):
        sl = slice(t * inner, (t + 1) * inner)
        yoff = jnp.concatenate(
            [yoff_ref[:, (g * steps + t) * inner + g * half:(g * steps + t) * inner + (g + 1) * half]
             for g in range(B_GROUPS)], axis=1)
        y = (ypart_ref[:, sl] + yoff * eacum_ref[:, sl]) * _silu(z_ref[:, sl])
        yn = []
        for g in range(B_GROUPS):
            yg = y[:, g * half:(g + 1) * half]
            yn.append(yg * lax.rsqrt(jnp.mean(yg * yg, axis=-1, keepdims=True) + EPS))
        yb = jnp.concatenate(yn, axis=1) * normg_ref[...]
        mixes.append(jnp.concatenate([ya_ref[:, t * d:(t + 1) * d], yb], axis=1))
    out = _dot(jnp.concatenate(mixes, axis=0).astype(BF16), wout_ref[...])
    for t in range(steps):
        o_ref[:, t * d:(t + 1) * d] = hs_ref[:, t * d:(t + 1) * d] + out[t * bt:(t + 1) * bt]


def _row_tiled(width, bt):
    return pl.BlockSpec((bt, width), lambda i: (i, 0))


def _even_sample(hs2, state_conv2, state_ssm3, p, steps, bt, bb):
    nb, _ = hs2.shape
    d = p["wa"].shape[0]
    width_a = p["wa"].shape[1] // 2
    inner = p["wz"].shape[1]
    conv_dim = p["wxbc"].shape[1]
    gn = B_GROUPS * B_STATE
    assert nb % bt == 0 and nb % bb == 0 and steps <= SUBLANES
    params = pltpu.CompilerParams(dimension_semantics=("arbitrary",), vmem_limit_bytes=VMEM_LIMIT)
    small = ["lng", "lnb", "wts", "bts", "convw", "convb", "dtb", "alog", "dskip"]
    widths = dict(v=steps * width_a, ya=steps * width_a, ypart=steps * inner, eacum=steps * inner,
                  z=steps * inner, conv=(B_CONV - 1) * conv_dim, cgt=SUBLANES * B_STATE,
                  xdw=SUBLANES * inner, bs=SUBLANES * gn, dec=LANES)
    front = pl.pallas_call(
        functools.partial(_even_sample_front_kernel, steps=steps, d=d, width_a=width_a, inner=inner,
                          conv_dim=conv_dim),
        out_shape=tuple(jax.ShapeDtypeStruct((nb, w), F32) for w in widths.values()),
        grid=(nb // bt,),
        in_specs=[_row_tiled(steps * d, bt), _resident((1, d))]
        + [_resident(p[k].shape) for k in ("wa", "wz", "wxbc", "wdt")]
        + [_resident(p[k].shape) for k in small]
        + [_row_tiled((B_CONV - 1) * conv_dim, bt)],
        out_specs=tuple(_row_tiled(w, bt) for w in widths.values()),
        compiler_params=params,
        name="even_sample_front",
    )(hs2, p["g"], p["wa"], p["wz"], p["wxbc"], p["wdt"], *[p[k] for k in small], state_conv2)
    v_rows, ya, ypart, eacum, z, new_conv, cgt, xdw, bs, dec = front

    hp = state_ssm3.shape[1]
    tile3 = lambda rows, width: pl.BlockSpec((bb, rows, width), lambda i: (i, 0, 0))
    yoff, new_ssm = pl.pallas_call(
        functools.partial(_even_sample_state_kernel, bb=bb),
        out_shape=(jax.ShapeDtypeStruct((nb, SUBLANES, hp), F32),
                   jax.ShapeDtypeStruct(state_ssm3.shape, F32)),
        grid=(nb // bb,),
        in_specs=[tile3(hp, B_STATE), tile3(SUBLANES, B_STATE), tile3(SUBLANES, inner), tile3(SUBLANES, gn),
                  pl.BlockSpec(memory_space=pltpu.SMEM)],
        out_specs=(tile3(SUBLANES, hp), tile3(hp, B_STATE)),
        compiler_params=params,
        name="even_sample_state",
    )(state_ssm3, cgt.reshape(nb, SUBLANES, B_STATE), xdw.reshape(nb, SUBLANES, inner),
      bs.reshape(nb, SUBLANES, gn), dec[:, :B_HEADS])

    out = pl.pallas_call(
        functools.partial(_even_sample_back_kernel, steps=steps, d=d, inner=inner),
        out_shape=jax.ShapeDtypeStruct(hs2.shape, F32),
        grid=(nb // bt,),
        in_specs=[_row_tiled(steps * d, bt), _row_tiled(steps * width_a, bt), _row_tiled(steps * inner, bt),
                  _row_tiled(steps * inner, bt), _row_tiled(steps * inner, bt), _row_tiled(SUBLANES * hp, bt),
                  _resident((1, inner)), _resident(p["wout"].shape)],
        out_specs=_row_tiled(steps * d, bt),
        compiler_params=params,
        name="even_sample_back",
    )(hs2, ya, ypart, eacum, z, yoff.reshape(nb, SUBLANES * hp), p["normg"], p["wout"])
    return out, v_rows, new_conv, new_ssm


def _odd_sample_front_kernel(hs_ref, g_ref, wc_ref, wq_ref, wk_ref, wv_ref, linw_ref, cscale_ref,
                             qn_ref, kn_ref, onesbd_ref, ps_ref,
                             yc_out, pool_out, q_out, knew_out, vnew_out, *, steps, d, width_c, past_len):
    bt = hs_ref.shape[0]
    xn = _rms(_stack_steps(hs_ref, steps, d), g_ref[...]).astype(BF16)
    blk = lambda a, t: a[t * bt:(t + 1) * bt]
    c_in = _dot(xn, wc_ref[...])
    n_state = max(C_WINDOWS) - 1
    ext = _steps(ps_ref[...], n_state, width_c) + [blk(c_in, t) for t in range(steps)]
    for j in range(n_state):
        pool_out[:, j * width_c:(j + 1) * width_c] = ext[len(ext) - n_state + j]
    gdim = width_c // len(C_WINDOWS)
    yc_cols = []
    for gi, win in enumerate(C_WINDOWS):
        sl = slice(gi * gdim, (gi + 1) * gdim)
        pooled = []
        for t in range(steps):
            hi = n_state + t
            lo = max(hi - win + 1, 0)
            acc = ext[lo][:, sl]
            for j in range(lo + 1, hi + 1):
                acc = acc + ext[j][:, sl]
            count = float(min(past_len + t + 1, win))
            pooled.append(acc / count - ext[hi][:, sl])
        yc_cols.append(_dot(jnp.concatenate(pooled, axis=0).astype(BF16), linw_ref[gi]))
    yc = jnp.concatenate(yc_cols, axis=1) * cscale_ref[...]
    for t in range(steps):
        yc_out[:, t * width_c:(t + 1) * width_c] = blk(yc, t)

    q = _dot(xn, wq_ref[...])
    k = _dot(xn, wk_ref[...])
    v = _dot(xn, wv_ref[...])
    ones_bd = onesbd_ref[...]
    inv_d = 1.0 / D_HEAD_DIM
    qn = q * lax.rsqrt(_head_sumsq(q, ones_bd) * inv_d + EPS) * qn_ref[...] * (D_HEAD_DIM ** -0.5)
    kn = k * lax.rsqrt(_head_sumsq(k, ones_bd) * inv_d + EPS) * kn_ref[...]
    qw = q.shape[1]
    kw = k.shape[1]
    pad = SUBLANES - steps
    q_out[:, steps * qw:] = jnp.zeros((bt, pad * qw), F32)
    knew_out[:, :pad * kw] = jnp.zeros((bt, pad * kw), F32)
    vnew_out[:, :pad * kw] = jnp.zeros((bt, pad * kw), F32)
    for t in range(steps):
        q_out[:, t * qw:(t + 1) * qw] = blk(qn, t)
        knew_out[:, (pad + t) * kw:(pad + t + 1) * kw] = blk(kn, t)
        vnew_out[:, (pad + t) * kw:(pad + t + 1) * kw] = blk(v, t)


SINK_BUCKET = REL_BUCKETS


def _odd_sample_attn_kernel(q_ref, knew_ref, vnew_ref, ck_ref, cv_ref, bucket_ref, sinks_ref, rel_ref,
                            o_ref, kout_ref, vout_ref, bias_ref, *, bb, steps, n_keys):
    win = ck_ref.shape[1]
    kvw = ck_ref.shape[2]
    tile16 = 2 * SUBLANES

    @pl.when(pl.program_id(0) == 0)
    def _():
        bucket = bucket_ref[...]
        for hh in range(D_Q_HEADS):
            acc = jnp.full(bucket.shape, NEG, F32)
            for bkt in range(REL_BUCKETS):
                acc = jnp.where(bucket == bkt, rel_ref[bkt, hh], acc)
            acc = jnp.where(bucket == SINK_BUCKET, sinks_ref[hh], acc)
            bias_ref[hh * SUBLANES:(hh + 1) * SUBLANES, :] = acc

    sub = lax.broadcasted_iota(jnp.int32, (SUBLANES, kvw), 0)
    new_rows = sub >= SUBLANES - steps
    lane_kv = lax.broadcasted_iota(jnp.int32, (SUBLANES, kvw), 1) // D_HEAD_DIM
    gw = D_KV_HEADS * D_HEAD_DIM
    zero_keys = jnp.zeros((n_keys - win - tile16, kvw), BF16)

    def extend(cache, new8):
        new16 = jnp.concatenate([new8, jnp.zeros_like(new8)], axis=0).astype(BF16)
        return jnp.concatenate([cache.astype(BF16), new16, zero_keys], axis=0)

    def shift_in(cache, new8, out_ref, bi):
        rolled = pltpu.roll(cache, win - steps, 0)
        out_ref[bi, 0:win - SUBLANES, :] = rolled[:win - SUBLANES]
        out_ref[bi, win - SUBLANES:, :] = jnp.where(new_rows, new8, rolled[win - SUBLANES:])

    def body(bi, carry):
        ck = ck_ref[bi]
        cv = cv_ref[bi]
        k8 = knew_ref[bi]
        v8 = vnew_ref[bi]
        shift_in(ck, k8, kout_ref, bi)
        shift_in(cv, v8, vout_ref, bi)
        q8 = q_ref[bi]
        pieces = []
        for kv in range(D_KV_HEADS):
            for grp in range(D_GROUP):
                qg = q8[:, grp * gw:(grp + 1) * gw]
                pieces.append(jnp.where(lane_kv == kv, qg, 0.0))
        lhs = jnp.concatenate(pieces, axis=0).astype(BF16)
        sc = _dot_nt(lhs, extend(ck, k8)) + bias_ref[...]
        m = jnp.max(sc, axis=-1, keepdims=True)
        pexp = jnp.exp(sc - m)
        probs = (pexp / jnp.sum(pexp, axis=-1, keepdims=True)).astype(BF16)
        ov = _dot(probs, extend(cv, v8))
        outs = []
        for grp in range(D_GROUP):
            r_last = ((D_KV_HEADS - 1) * D_GROUP + grp) * SUBLANES
            acc = ov[r_last:r_last + SUBLANES]
            for kv in range(D_KV_HEADS - 2, -1, -1):
                r0 = (kv * D_GROUP + grp) * SUBLANES
                acc = jnp.where(lane_kv == kv, ov[r0:r0 + SUBLANES], acc)
            outs.append(acc)
        o_ref[bi] = jnp.concatenate(outs, axis=1)
        return carry

    lax.fori_loop(0, bb, body, 0, unroll=SAMPLE_ATTN_UNROLL)


def _odd_sample_back_kernel(hs_ref, yc_ref, o_ref_in, wout_ref, out_ref, *, steps, d, width_c, qw):
    bt = hs_ref.shape[0]
    mix = jnp.concatenate(
        [jnp.concatenate([yc_ref[:, t * width_c:(t + 1) * width_c], o_ref_in[:, t * qw:(t + 1) * qw]], axis=1)
         for t in range(steps)], axis=0).astype(BF16)
    out = _dot(mix, wout_ref[...])
    for t in range(steps):
        out_ref[:, t * d:(t + 1) * d] = hs_ref[:, t * d:(t + 1) * d] + out[t * bt:(t + 1) * bt]


def _odd_sample(hs2, state_pool2, cache_k3, cache_v3, p, steps, past_len, bt, bb):
    nb, _ = hs2.shape
    d = p["wc"].shape[0]
    width_c = p["wc"].shape[1]
    qw = p["wq"].shape[1]
    kw = p["wk"].shape[1]
    win = cache_k3.shape[1]
    n_state = max(C_WINDOWS) - 1
    assert nb % bt == 0 and nb % bb == 0 and steps <= SUBLANES and win == CHUNK
    params = pltpu.CompilerParams(dimension_semantics=("arbitrary",), vmem_limit_bytes=VMEM_LIMIT)
    vm = ["linw", "cscale", "qn", "kn", "onesbd"]
    widths = dict(yc=steps * width_c, pool=n_state * width_c, q=SUBLANES * qw, knew=SUBLANES * kw,
                  vnew=SUBLANES * kw)
    yc, new_pool, q8, knew8, vnew8 = pl.pallas_call(
        functools.partial(_odd_sample_front_kernel, steps=steps, d=d, width_c=width_c, past_len=past_len),
        out_shape=tuple(jax.ShapeDtypeStruct((nb, w), F32) for w in widths.values()),
        grid=(nb // bt,),
        in_specs=[_row_tiled(steps * d, bt), _resident((1, d))]
        + [_resident(p[k].shape) for k in ("wc", "wq", "wk", "wv")]
        + [_resident(p[k].shape) for k in vm]
        + [_row_tiled(n_state * width_c, bt)],
        out_specs=tuple(_row_tiled(w, bt) for w in widths.values()),
        compiler_params=params,
        name="odd_sample_front",
    )(hs2, p["g"], p["wc"], p["wq"], p["wk"], p["wv"], *[p[k] for k in vm], state_pool2)

    n_keys = 2 * CHUNK
    pad = SUBLANES - steps
    bucket = np.full((SUBLANES, n_keys), -1, np.int32)
    for t in range(steps):
        q_pos = past_len + t
        k_pos = np.full(n_keys, -10 ** 9, np.int64)
        k_pos[:win] = past_len - win + np.arange(win)
        k_pos[win + pad:win + SUBLANES] = past_len + np.arange(steps)
        dist = q_pos - k_pos
        ok = (dist >= 0) & (dist < CHUNK) & (k_pos >= 0)
        bucket[t] = np.where(ok, _t5_bucket(np.where(ok, dist, 0)), -1)
    bucket[:, n_keys - 1] = SINK_BUCKET
    smem = pl.BlockSpec(memory_space=pltpu.SMEM)
    tile3 = lambda rows, width: pl.BlockSpec((bb, rows, width), lambda i: (i, 0, 0))
    o8, new_k, new_v = pl.pallas_call(
        functools.partial(_odd_sample_attn_kernel, bb=bb, steps=steps, n_keys=n_keys),
        out_shape=(jax.ShapeDtypeStruct((nb, SUBLANES, qw), F32),
                   jax.ShapeDtypeStruct(cache_k3.shape, F32),
                   jax.ShapeDtypeStruct(cache_v3.shape, F32)),
        grid=(nb // bb,),
        in_specs=[tile3(SUBLANES, qw), tile3(SUBLANES, kw), tile3(SUBLANES, kw), tile3(win, kw), tile3(win, kw),
                  _resident(bucket.shape), smem, smem],
        out_specs=(tile3(SUBLANES, qw), tile3(win, kw), tile3(win, kw)),
        scratch_shapes=[pltpu.VMEM((D_Q_HEADS * SUBLANES, n_keys), F32)],
        compiler_params=params,
        name="odd_sample_attn",
    )(q8.reshape(nb, SUBLANES, qw), knew8.reshape(nb, SUBLANES, kw), vnew8.reshape(nb, SUBLANES, kw),
      cache_k3, cache_v3, jnp.asarray(bucket), p["sinks"], p["rel"])

    out = pl.pallas_call(
        functools.partial(_odd_sample_back_kernel, steps=steps, d=d, width_c=width_c, qw=qw),
        out_shape=jax.ShapeDtypeStruct(hs2.shape, F32),
        grid=(nb // bt,),
        in_specs=[_row_tiled(steps * d, bt), _row_tiled(steps * width_c, bt), _row_tiled(SUBLANES * qw, bt),
                  _resident(p["wout"].shape)],
        out_specs=_row_tiled(steps * d, bt),
        compiler_params=params,
        name="odd_sample_back",
    )(hs2, yc, o8.reshape(nb, SUBLANES * qw), p["wout"])
    return out, new_pool, new_k, new_v


def _row(v):
    return v.reshape(1, -1).astype(F32)


def _pad_lanes(m, width=LANES):
    return jnp.pad(m, ((0, 0), (0, width - m.shape[1])))


def _head_expand_matrix():
    e = np.zeros((LANES, B_HEADS * B_HEAD_DIM), np.float32)
    for hh in range(B_HEADS):
        e[hh, hh * B_HEAD_DIM:(hh + 1) * B_HEAD_DIM] = 1.0
    return np.concatenate([e, e], axis=0)


def _prep_even(mix_norm, w_in, w_out, ln_g, ln_b, w_s, b_s, conv_w, conv_b, dt_bias, a_log, d_skip, norm_g):
    width_a = ln_g.shape[0]
    inner = norm_g.shape[0]
    conv_dim = conv_b.shape[0]
    o1 = 2 * width_a
    o2 = o1 + inner
    o3 = o2 + conv_dim
    return dict(
        g=_row(mix_norm),
        wa=w_in[:, :o1].astype(BF16),
        wz=w_in[:, o1:o2].astype(BF16),
        wxbc=w_in[:, o2:o3].astype(BF16),
        wdt=_pad_lanes(w_in[:, o3:]).astype(BF16),
        wout=w_out.astype(BF16),
        lng=_row(ln_g), lnb=_row(ln_b), ws=w_s,
        bsb=jnp.broadcast_to(b_s[:, :, None], b_s.shape + (width_a // A_HEADS,)),
        convw=conv_w, convb=_row(conv_b),
        dtb=_pad_lanes(_row(dt_bias)), alog=_pad_lanes(_row(a_log)),
        dskip=_row(jnp.repeat(d_skip, B_HEAD_DIM)), normg=_row(norm_g),
        expand=jnp.asarray(_head_expand_matrix(), BF16),
    )


def _prep_odd(mix_norm, w_in, w_out, lin_w, c_scale, q_norm, k_norm, sinks, rel_table):
    d = w_in.shape[0]
    width_c = c_scale.shape[0]
    qw = D_Q_HEADS * D_HEAD_DIM
    kw = D_KV_HEADS * D_HEAD_DIM
    wq = w_in[:, width_c:width_c + qw].reshape(d, D_KV_HEADS, D_GROUP, D_HEAD_DIM)
    wq = wq.transpose(0, 2, 1, 3).reshape(d, qw)
    wo_d = w_out[width_c:].reshape(D_KV_HEADS, D_GROUP, D_HEAD_DIM, -1).transpose(1, 0, 2, 3).reshape(qw, -1)
    ones_bd = np.kron(np.eye(MXU_DIM // D_HEAD_DIM), np.ones((D_HEAD_DIM, D_HEAD_DIM))).astype(np.float32)
    return dict(
        g=_row(mix_norm),
        wc=w_in[:, :width_c].astype(BF16),
        wq=wq.astype(BF16),
        wk=w_in[:, width_c + qw:width_c + qw + kw].astype(BF16),
        wv=w_in[:, width_c + qw + kw:].astype(BF16),
        wout=jnp.concatenate([w_out[:width_c], wo_d], axis=0).astype(BF16),
        linw=lin_w.astype(BF16), cscale=_row(c_scale),
        qn=_row(jnp.tile(q_norm, D_Q_HEADS)), kn=_row(jnp.tile(k_norm, D_KV_HEADS)),
        onesbd=jnp.asarray(ones_bd, BF16),
        sinks=sinks.astype(F32), rel=rel_table.astype(F32),
    )


def _prep_even_sample(w_s, b_s, steps):
    head_w = CHUNK
    w = jnp.transpose(w_s[:, :steps, :steps], (1, 2, 0)).reshape(steps * steps, A_HEADS)
    b = b_s[:, :steps].T
    return dict(wts=jnp.repeat(w, head_w, axis=1), bts=jnp.repeat(b, head_w, axis=1))


PAST_LEN = 16384
FFN_TILE = 1024
FFN_STAGE_SLOTS = 2
FFN_STAGE_ROWS_GU = 32
FFN_STAGE_ROWS_D = 128
FFN_ROW_BLOCK = 256
FFN_BLOCK_TILES = 6
EVEN_MIXER_TILE = 512
ODD_MIXER_TILE = 512
SAMPLE_ROW_TILE = 64
SAMPLE_SEQ_TILE = 16
SAMPLE_ATTN_UNROLL = 4


def kernel(x_prompt, x_sample, state_ssm, state_conv, state_pool, cache_k_win, cache_v_win,
           ffn1_norm, ffn1_w_gu, ffn1_w_down, mix_norm, ffn2_norm, ffn2_w_gu, ffn2_w_down,
           ev_w_in, ev_w_out, a_ln_g, a_ln_b, a_w_s, a_b_s, b_conv_w, b_conv_b, b_dt_bias, b_a_log,
           b_d_skip, b_norm_g, od_w_in, od_w_out, c_lin_w, c_scale, d_q_norm, d_k_norm, d_sinks,
           rel_bias_table):
    bp, seq, d = x_prompt.shape
    bs, steps, _ = x_sample.shape
    past_len = PAST_LEN
    hp = x_prompt
    hs = x_sample
    depth = ffn1_norm.shape[0]
    names = ("a_v_s", "ssm_p", "ssm_s", "conv_p", "conv_s", "pool_p", "pool_s", "k_p", "k_s", "v_p", "v_s")
    outs = {k: [] for k in names}

    def macaron(h_p, h_s, norm, w_gu_all, w_down_all, layer):
        o_p, o_s = _ffn(h_p.reshape(bp * seq, d), h_s.reshape(bs * steps, d), _row(norm), w_gu_all, w_down_all,
                        layer, FFN_TILE)
        return o_p.reshape(bp, seq, d), o_s.reshape(bs, steps, d)

    for layer in range(depth):
        i = layer // 2
        hp, hs = macaron(hp, hs, ffn1_norm[layer], ffn1_w_gu, ffn1_w_down, layer)
        hs2 = hs.reshape(bs, steps * d)
        if layer % 2 == 0:
            p = _prep_even(mix_norm[layer], ev_w_in[i], ev_w_out[i], a_ln_g[i], a_ln_b[i], a_w_s[i], a_b_s[i],
                           b_conv_w[i], b_conv_b[i], b_dt_bias[i], b_a_log[i], b_d_skip[i], b_norm_g[i])
            p.update(_prep_even_sample(a_w_s[i], a_b_s[i], steps))
            hp, conv_p, ssm_p = _even_prompt(hp, p, EVEN_MIXER_TILE)
            hs2, v_rows, conv_s, ssm_s = _even_sample(
                hs2, state_conv[i].reshape(bs, -1), state_ssm[i].reshape(bs, B_HEADS * B_HEAD_DIM, B_STATE),
                p, steps, SAMPLE_ROW_TILE, SAMPLE_SEQ_TILE)
            outs["a_v_s"].append(v_rows.reshape(bs, steps, -1))
            outs["conv_p"].append(conv_p)
            outs["conv_s"].append(conv_s.reshape(state_conv[i].shape))
            outs["ssm_p"].append(ssm_p.reshape(bp, B_HEADS, B_HEAD_DIM, B_STATE))
            outs["ssm_s"].append(ssm_s.reshape(state_ssm[i].shape))
        else:
            p = _prep_odd(mix_norm[layer], od_w_in[i], od_w_out[i], c_lin_w[i], c_scale[i], d_q_norm[i],
                          d_k_norm[i], d_sinks[i], rel_bias_table)
            hp, pool_p, k_p, v_p = _odd_prompt(hp, p, ODD_MIXER_TILE)
            kv_shape = cache_k_win[i].shape
            hs2, pool_s, k_s, v_s = _odd_sample(
                hs2, state_pool[i].reshape(bs, -1), cache_k_win[i].reshape(bs, kv_shape[1], -1),
                cache_v_win[i].reshape(bs, kv_shape[1], -1), p, steps, past_len, SAMPLE_ROW_TILE, SAMPLE_SEQ_TILE)
            outs["pool_p"].append(pool_p)
            outs["pool_s"].append(pool_s.reshape(state_pool[i].shape))
            outs["k_p"].append(k_p.reshape(bp, CHUNK, D_KV_HEADS, D_HEAD_DIM))
            outs["v_p"].append(v_p.reshape(bp, CHUNK, D_KV_HEADS, D_HEAD_DIM))
            outs["k_s"].append(k_s.reshape(kv_shape))
            outs["v_s"].append(v_s.reshape(kv_shape))
        hs = hs2.reshape(bs, steps, d)
        hp, hs = macaron(hp, hs, ffn2_norm[layer], ffn2_w_gu, ffn2_w_down, layer)
    return (hp, hs) + tuple(jnp.stack(outs[k]) for k in names)
```
